```python
import jax, jax.numpy as jnp
from jax import lax
import numpy as np

D_MODEL = 1024
BATCH = 4
SEQ = 8192
DEPTH = 4

D_MIX = D_MODEL
SSD_WIDTH = D_MIX // 2
SSD_HEAD_DIM = 64
SSD_HEADS = SSD_WIDTH // SSD_HEAD_DIM
SSD_GROUPS = 2
SSD_STATE = 128
SSD_CONV = 4
SSD_CHUNK = 128
ATTN_WIDTH = D_MIX // 4
ATTN_HEAD_DIM = 64
ATTN_HEADS = ATTN_WIDTH // ATTN_HEAD_DIM
ATTN_BLOCK = 128
CONV_WIDTH = D_MIX - SSD_WIDTH - ATTN_WIDTH
CONV_KERNEL = 31
XBC_WIDTH = SSD_WIDTH + 2 * SSD_GROUPS * SSD_STATE
D_IN = SSD_WIDTH + XBC_WIDTH + SSD_HEADS + 3 * ATTN_WIDTH + ATTN_HEADS + 2 * CONV_WIDTH
N_EXPERT_GROUPS = 4
EXPERTS_PER_GROUP = 8
N_EXPERTS = N_EXPERT_GROUPS * EXPERTS_PER_GROUP
TOP_K = 2
D_EXPERT = 512
MOE_BLOCK = 256
NORM_EPS = 1e-6

kernel_name = 'hymba_ssd_fox_conformer_hmoe_adaln'


def _rms(x):
    xf = x.astype(jnp.float32)
    return xf * lax.rsqrt(jnp.mean(xf * xf, axis=-1, keepdims=True) + NORM_EPS)


def rms_norm(x, g):
    return (_rms(x) * g.astype(jnp.float32)).astype(x.dtype)


def layer_norm(x, g, b):
    xf = x.astype(jnp.float32)
    mu = jnp.mean(xf, axis=-1, keepdims=True)
    var = jnp.mean(jnp.square(xf - mu), axis=-1, keepdims=True)
    y = (xf - mu) * lax.rsqrt(var + NORM_EPS) * g.astype(jnp.float32) + b.astype(jnp.float32)
    return y.astype(x.dtype)


def causal_depthwise_conv(x, w, b):
    k = w.shape[0]
    y = lax.conv_general_dilated(
        x, w[:, None, :].astype(x.dtype), window_strides=(1,), padding=[(k - 1, 0)],
        dimension_numbers=('NWC', 'WIO', 'NWC'), feature_group_count=x.shape[-1])
    return y + b.astype(x.dtype)


def ssd_mixer(z, xbc, dt_raw, conv_w, conv_b, dt_bias, a_log, d_skip, norm_g):
    f32 = jnp.float32
    bsz, seq, _ = z.shape
    nc = seq // SSD_CHUNK
    r = SSD_HEADS // SSD_GROUPS
    gn = SSD_GROUPS * SSD_STATE
    xbc = jax.nn.silu(causal_depthwise_conv(xbc, conv_w, conv_b))
    xs, bmat, cmat = jnp.split(xbc, [SSD_WIDTH, SSD_WIDTH + gn], axis=-1)
    dt = jax.nn.softplus(dt_raw.astype(f32) + dt_bias.astype(f32))
    a = -jnp.exp(a_log.astype(f32))
    x_h = xs.reshape(bsz, nc, SSD_CHUNK, SSD_GROUPS, r, SSD_HEAD_DIM).astype(f32)
    dt_c = dt.reshape(bsz, nc, SSD_CHUNK, SSD_GROUPS, r)
    xdt = x_h * dt_c[..., None]
    bm = bmat.reshape(bsz, nc, SSD_CHUNK, SSD_GROUPS, SSD_STATE).astype(f32)
    cm = cmat.reshape(bsz, nc, SSD_CHUNK, SSD_GROUPS, SSD_STATE).astype(f32)
    acs = jnp.cumsum(dt_c * a.reshape(SSD_GROUPS, r), axis=2)
    acs = acs.transpose(0, 1, 3, 4, 2)
    causal = jnp.tril(jnp.ones((SSD_CHUNK, SSD_CHUNK), dtype=bool))
    decay_in = jnp.exp(jnp.where(causal, acs[..., :, None] - acs[..., None, :], -jnp.inf))
    cb = jnp.einsum('bclgn,bcsgn->bcgls', cm, bm)
    y_diag = jnp.einsum('bcgls,bcgrls,bcsgrp->bclgrp', cb, decay_in, xdt)
    decay_to_end = jnp.exp(acs[..., -1:] - acs)
    states = jnp.einsum('bclgn,bcgrl,bclgrp->bcgrpn', bm, decay_to_end, xdt)
    chunk_decay = jnp.exp(acs[..., -1])

    def step(h, inp):
        st, dec = inp
        return h * dec[..., None, None] + st, h

    h0 = jnp.zeros_like(states[:, 0])
    _, prev = lax.scan(step, h0, (states.swapaxes(0, 1), chunk_decay.swapaxes(0, 1)))
    prev = prev.swapaxes(0, 1)
    y_off = jnp.einsum('bclgn,bcgrpn,bcgrl->bclgrp', cm, prev, jnp.exp(acs))
    y = y_diag + y_off + x_h * d_skip.astype(f32).reshape(SSD_GROUPS, r, 1)
    y = y.reshape(bsz, seq, SSD_GROUPS, r * SSD_HEAD_DIM)
    y = y * jax.nn.silu(z.astype(f32)).reshape(bsz, seq, SSD_GROUPS, r * SSD_HEAD_DIM)
    y = _rms(y).reshape(bsz, seq, SSD_WIDTH) * norm_g.astype(f32)
    return y.astype(z.dtype)


def forgetting_attention(q, k, v, f_logit, f_bias, norm_g):
    f32 = jnp.float32
    bsz, seq, _ = q.shape
    nb = seq // ATTN_BLOCK
    scale = ATTN_HEAD_DIM ** -0.5
    qh = q.reshape(bsz, seq, ATTN_HEADS, ATTN_HEAD_DIM).transpose(0, 2, 1, 3)
    kh = k.reshape(bsz, seq, ATTN_HEADS, ATTN_HEAD_DIM).transpose(0, 2, 1, 3)
    vh = v.reshape(bsz, seq, ATTN_HEADS, ATTN_HEAD_DIM).transpose(0, 2, 1, 3)
    log_f = jax.nn.log_sigmoid(f_logit.astype(f32) + f_bias.astype(f32))
    cum = jnp.cumsum(log_f, axis=1).transpose(0, 2, 1)
    qb = qh.reshape(bsz, ATTN_HEADS, nb, ATTN_BLOCK, ATTN_HEAD_DIM).transpose(2, 0, 1, 3, 4)
    cb = cum.reshape(bsz, ATTN_HEADS, nb, ATTN_BLOCK).transpose(2, 0, 1, 3)
    k_pos = jnp.arange(seq)

    def block(args):
        qi, ci, i = args
        s = jnp.einsum('bhqd,bhkd->bhqk', qi, kh).astype(f32) * scale
        s = s + ci[..., :, None] - cum[:, :, None, :]
        q_pos = i * ATTN_BLOCK + jnp.arange(ATTN_BLOCK)
        s = jnp.where(k_pos[None, :] <= q_pos[:, None], s, -jnp.inf)
        p = jax.nn.softmax(s, axis=-1)
        return jnp.einsum('bhqk,bhkd->bhqd', p.astype(vh.dtype), vh)

    o = lax.map(block, (qb, cb, jnp.arange(nb)))
    o = o.transpose(1, 0, 3, 2, 4).reshape(bsz, seq, ATTN_WIDTH)
    return rms_norm(o, norm_g)


def conformer_conv(glu_a, glu_b, conv_w, conv_b, ln_g, ln_b):
    u = glu_a * jax.nn.sigmoid(glu_b)
    u = causal_depthwise_conv(u, conv_w, conv_b)
    u = layer_norm(u, ln_g, ln_b)
    return jax.nn.silu(u)


def hybrid_mixer(h, w_in, ssd_conv_w, ssd_conv_b, ssd_dt_bias, ssd_a_log, ssd_d, ssd_norm_g,
                 fox_f_bias, fox_norm_g, cm_conv_w, cm_conv_b, cm_ln_g, cm_ln_b, w_out):
    sizes = (SSD_WIDTH, XBC_WIDTH, SSD_HEADS, ATTN_WIDTH, ATTN_WIDTH, ATTN_WIDTH,
             ATTN_HEADS, CONV_WIDTH, CONV_WIDTH)
    splits = np.cumsum(sizes)[:-1].tolist()
    proj = h @ w_in
    z, xbc, dt_raw, q, k, v, f_logit, glu_a, glu_b = jnp.split(proj, splits, axis=-1)
    y_ssd = ssd_mixer(z, xbc, dt_raw, ssd_conv_w, ssd_conv_b, ssd_dt_bias, ssd_a_log, ssd_d, ssd_norm_g)
    y_att = forgetting_attention(q, k, v, f_logit, fox_f_bias, fox_norm_g)
    y_cnv = conformer_conv(glu_a, glu_b, cm_conv_w, cm_conv_b, cm_ln_g, cm_ln_b)
    y = jnp.concatenate([y_ssd.astype(h.dtype), y_att.astype(h.dtype), y_cnv.astype(h.dtype)], axis=-1)
    return y @ w_out


def hierarchical_moe(h, w_rg, b_rg, w_re, b_re, w_gate, w_up, w_down):
    f32 = jnp.float32
    bsz, seq, d = h.shape
    m = bsz * seq
    hf = h.reshape(m, d)
    p_g = jax.nn.softmax((hf @ w_rg + b_rg).astype(f32), axis=-1)
    g_sel = jnp.argmax(p_g, axis=-1).astype(jnp.int32)
    p_gsel = jnp.take_along_axis(p_g, g_sel[:, None], axis=-1)
    e_logits = (hf @ w_re + b_re).astype(f32).reshape(m, N_EXPERT_GROUPS, EXPERTS_PER_GROUP)
    e_logits = jnp.take_along_axis(e_logits, g_sel[:, None, None], axis=1)[:, 0]
    top_p, top_i = lax.top_k(jax.nn.softmax(e_logits, axis=-1), TOP_K)
    weights = top_p / jnp.sum(top_p, axis=-1, keepdims=True) * p_gsel
    expert = g_sel[:, None] * EXPERTS_PER_GROUP + top_i.astype(jnp.int32)
    n_assign = m * TOP_K
    flat_e = expert.reshape(-1)
    flat_w = weights.reshape(-1)
    flat_tok = jnp.arange(n_assign, dtype=jnp.int32) // TOP_K
    order = jnp.argsort(flat_e)
    se, stok, sw = flat_e[order], flat_tok[order], flat_w[order]
    counts = jnp.zeros((N_EXPERTS,), jnp.int32).at[flat_e].add(1)
    padded = (counts + MOE_BLOCK - 1) // MOE_BLOCK * MOE_BLOCK
    start = jnp.cumsum(counts) - counts
    pad_end = jnp.cumsum(padded)
    pad_start = pad_end - padded
    dest = pad_start[se] + jnp.arange(n_assign, dtype=jnp.int32) - start[se]
    n_blocks = (n_assign + MOE_BLOCK - 1) // MOE_BLOCK + N_EXPERTS
    rows = n_blocks * MOE_BLOCK
    xs = jnp.zeros((rows, d), h.dtype).at[dest].set(hf[stok])
    block_e = jnp.minimum(
        jnp.searchsorted(pad_end, jnp.arange(n_blocks, dtype=jnp.int32) * MOE_BLOCK, side='right'),
        N_EXPERTS - 1)

    def expert_block(args):
        xb, e = args
        hid = jax.nn.silu(xb @ w_gate[e]) * (xb @ w_up[e])
        return hid @ w_down[e]

    ys = lax.map(expert_block, (xs.reshape(n_blocks, MOE_BLOCK, d), block_e)).reshape(rows, d)
    contrib = (ys[dest] * sw[:, None]).astype(h.dtype)
    out = jnp.zeros((m, d), h.dtype).at[stok].add(contrib)
    return out.reshape(bsz, seq, d)


def setup_inputs(seed: int = 0) -> dict:
    key = jax.random.key(seed)
    ks = jax.random.split(key, 32)
    nrm = jax.random.normal
    uni = jax.random.uniform
    f32 = jnp.float32
    dt0 = jnp.exp(uni(ks[8], (DEPTH, SSD_HEADS), f32, np.log(1e-3), np.log(1e-1)))
    return {
        'x': nrm(ks[0], (BATCH, SEQ, D_MODEL), f32),
        'c': nrm(ks[1], (BATCH, D_MODEL), f32),
        'ada_w': nrm(ks[2], (DEPTH, D_MODEL, 6 * D_MODEL), f32) * (0.5 * D_MODEL ** -0.5),
        'ada_b': 0.02 * nrm(ks[3], (DEPTH, 6 * D_MODEL), f32),
        'norm_mix_g': 1.0 + 0.02 * nrm(ks[4], (DEPTH, D_MODEL), f32),
        'w_in': nrm(ks[5], (DEPTH, D_MODEL, D_IN), f32) * D_MODEL ** -0.5,
        'ssd_conv_w': nrm(ks[6], (DEPTH, SSD_CONV, XBC_WIDTH), f32) * SSD_CONV ** -0.5,
        'ssd_conv_b': 0.02 * nrm(ks[7], (DEPTH, XBC_WIDTH), f32),
        'ssd_dt_bias': dt0 + jnp.log(-jnp.expm1(-dt0)),
        'ssd_a_log': jnp.log(uni(ks[9], (DEPTH, SSD_HEADS), f32, 1.0, 16.0)),
        'ssd_d': 1.0 + 0.02 * nrm(ks[10], (DEPTH, SSD_HEADS), f32),
        'ssd_norm_g': 1.0 + 0.02 * nrm(ks[11], (DEPTH, SSD_WIDTH), f32),
        'fox_f_bias': uni(ks[12], (DEPTH, ATTN_HEADS), f32, 1.0, 4.0),
        'fox_norm_g': 1.0 + 0.02 * nrm(ks[13], (DEPTH, ATTN_WIDTH), f32),
        'cm_conv_w': nrm(ks[14], (DEPTH, CONV_KERNEL, CONV_WIDTH), f32) * CONV_KERNEL ** -0.5,
        'cm_conv_b': 0.02 * nrm(ks[15], (DEPTH, CONV_WIDTH), f32),
        'cm_ln_g': 1.0 + 0.02 * nrm(ks[16], (DEPTH, CONV_WIDTH), f32),
        'cm_ln_b': 0.02 * nrm(ks[17], (DEPTH, CONV_WIDTH), f32),
        'w_out': nrm(ks[18], (DEPTH, D_MIX, D_MODEL), f32) * D_MIX ** -0.5,
        'norm_ffn_g': 1.0 + 0.02 * nrm(ks[19], (DEPTH, D_MODEL), f32),
        'w_router_group': nrm(ks[20], (DEPTH, D_MODEL, N_EXPERT_GROUPS), f32) * D_MODEL ** -0.5,
        'b_router_group': 0.01 * nrm(ks[21], (DEPTH, N_EXPERT_GROUPS), f32),
        'w_router_expert': nrm(ks[22], (DEPTH, D_MODEL, N_EXPERTS), f32) * D_MODEL ** -0.5,
        'b_router_expert': 0.01 * nrm(ks[23], (DEPTH, N_EXPERTS), f32),
        'w_gate': nrm(ks[24], (DEPTH, N_EXPERTS, D_MODEL, D_EXPERT), f32) * D_MODEL ** -0.5,
        'w_up': nrm(ks[25], (DEPTH, N_EXPERTS, D_MODEL, D_EXPERT), f32) * D_MODEL ** -0.5,
        'w_down': nrm(ks[26], (DEPTH, N_EXPERTS, D_EXPERT, D_MODEL), f32) * D_EXPERT ** -0.5,
        'final_norm_g': 1.0 + 0.02 * nrm(ks[27], (D_MODEL,), f32),
    }


def reference(x, c, ada_w, ada_b, norm_mix_g, w_in, ssd_conv_w, ssd_conv_b, ssd_dt_bias,
              ssd_a_log, ssd_d, ssd_norm_g, fox_f_bias, fox_norm_g, cm_conv_w, cm_conv_b,
              cm_ln_g, cm_ln_b, w_out, norm_ffn_g, w_router_group, b_router_group,
              w_router_expert, b_router_expert, w_gate, w_up, w_down, final_norm_g):
    cond = jax.nn.silu(c)
    for l in range(DEPTH):
        mod = (cond @ ada_w[l] + ada_b[l])[:, None, :]
        sh_m, sc_m, g_m, sh_f, sc_f, g_f = jnp.split(mod, 6, axis=-1)
        h = rms_norm(x, norm_mix_g[l]) * (1.0 + sc_m) + sh_m
        y = hybrid_mixer(h, w_in[l], ssd_conv_w[l], ssd_conv_b[l], ssd_dt_bias[l], ssd_a_log[l],
                         ssd_d[l], ssd_norm_g[l], fox_f_bias[l], fox_norm_g[l], cm_conv_w[l],
                         cm_conv_b[l], cm_ln_g[l], cm_ln_b[l], w_out[l])
        x = x + g_m * y
        h = rms_norm(x, norm_ffn_g[l]) * (1.0 + sc_f) + sh_f
        y = hierarchical_moe(h, w_router_group[l], b_router_group[l], w_router_expert[l],
                             b_router_expert[l], w_gate[l], w_up[l], w_down[l])
        x = x + g_f * y
    return rms_norm(x, final_norm_g)
```

```python
import functools

import jax
import jax.numpy as jnp
import numpy as np
from jax import lax
from jax.experimental import pallas as pl
from jax.experimental.pallas import tpu as pltpu

F32 = jnp.float32
BF16 = jnp.bfloat16
HIGHEST = lax.Precision.HIGHEST

D_MODEL = 1024
DEPTH = 4
SSD_WIDTH = 512
SSD_HEADS = 8
SSD_HEAD_DIM = 64
SSD_STATE = 128
SSD_CONV = 4
SSD_CHUNK = 128
XBC_WIDTH = 1024
ATTN_WIDTH = 256
ATTN_HEADS = 4
ATTN_HEAD_DIM = 64
CONV_WIDTH = 256
CONV_KERNEL = 31
N_GROUPS = 4
PER_GROUP = 8
N_EXPERTS = 32
D_EXPERT = 512
MOE_BLOCK = 256
EPS = 1e-6

LANES = 128
COL_XBC = 0
COL_Z = 1024
COL_QKV = 1536
COL_GA = 2304
COL_GB = 2560
COL_SM = 2816
NP = 2944
SM_DT = 0
SM_F = 8

VMEM_LIMIT = 56 * 1024 * 1024


def _cparams(sem):
    return pltpu.CompilerParams(dimension_semantics=sem, vmem_limit_bytes=VMEM_LIMIT)


def _sigmoid(x):
    return 1.0 / (1.0 + jnp.exp(-x))


def _silu(x):
    return x * _sigmoid(x)


def _softplus(x):
    return jnp.maximum(x, 0.0) + jnp.log1p(jnp.exp(-jnp.abs(x)))


def _iota(shape, dim):
    return lax.broadcasted_iota(jnp.int32, shape, dim)


def _mod_kernel(c_ref, w_ref, b_ref, o_ref):
    cond = _silu(c_ref[...])
    o_ref[0] = jnp.dot(cond, w_ref[0], precision=HIGHEST, preferred_element_type=F32) + b_ref[0]


def _modulation(c, ada_w, ada_b):
    bsz = c.shape[0]
    rows = 8
    cpad = jnp.zeros((rows, D_MODEL), F32).at[:bsz].set(c)
    tn = 1536
    n6 = 6 * D_MODEL
    out = pl.pallas_call(
        _mod_kernel,
        grid=(DEPTH, n6 // tn),
        in_specs=[
            pl.BlockSpec((rows, D_MODEL), lambda l, j: (0, 0)),
            pl.BlockSpec((1, D_MODEL, tn), lambda l, j: (l, 0, j)),
            pl.BlockSpec((1, 1, tn), lambda l, j: (l, 0, j)),
        ],
        out_specs=pl.BlockSpec((1, rows, tn), lambda l, j: (l, 0, j)),
        out_shape=jax.ShapeDtypeStruct((DEPTH, rows, n6), F32),
        compiler_params=_cparams(("arbitrary", "arbitrary")),
        name="adaln_mod",
    )(cpad, ada_w, ada_b.reshape(DEPTH, 1, n6))
    return out[:, :bsz]


def _inproj_kernel(x_ref, a_ref, s_ref, w_ref, o_ref):
    x = x_ref[...]
    ms = jnp.mean(x * x, axis=-1, keepdims=True)
    h = x * lax.rsqrt(ms + EPS) * a_ref[0] + s_ref[0]
    o_ref[...] = jnp.dot(h.astype(BF16), w_ref[...], preferred_element_type=F32)


def _inproj(x2, a, s, w_pack, seq, tm):
    m = x2.shape[0]
    per_b = seq // tm
    return pl.pallas_call(
        _inproj_kernel,
        grid=(m // tm,),
        in_specs=[
            pl.BlockSpec((tm, D_MODEL), lambda i: (i, 0)),
            pl.BlockSpec((1, 1, D_MODEL), lambda i: (i // per_b, 0, 0)),
            pl.BlockSpec((1, 1, D_MODEL), lambda i: (i // per_b, 0, 0)),
            pl.BlockSpec((D_MODEL, NP), lambda i: (0, 0)),
        ],
        out_specs=pl.BlockSpec((tm, NP), lambda i: (i, 0)),
        out_shape=jax.ShapeDtypeStruct((m, NP), F32),
        compiler_params=_cparams(("arbitrary",)),
        name="in_proj",
    )(x2, a, s, w_pack)


def _ssd_kernel(z_ref, xbc_ref, sm_ref, dtt_ref, cw_ref, cb_ref, dtb_ref, alog_ref, dtbt_ref,
                alogt_ref, e_ref, dx_ref, ng_ref, y_ref, xpad, xc, prev, *, tt):
    t = pl.program_id(1)

    @pl.when(t == 0)
    def _():
        xpad[0:8, :] = jnp.zeros((8, XBC_WIDTH), F32)
        prev[...] = jnp.zeros(prev.shape, F32)

    xpad[8:8 + tt, :] = xbc_ref[...]
    acc = jnp.broadcast_to(cb_ref[...], (tt, XBC_WIDTH))
    for k in range(SSD_CONV):
        off = 8 - (SSD_CONV - 1) + k
        acc = acc + cw_ref[k:k + 1, :] * xpad[off:off + tt, :]
    xc[...] = _silu(acc)
    xpad[0:8, :] = xpad[tt:tt + 8, :]

    cl = SSD_CHUNK
    row = _iota((cl, cl), 0)
    col = _iota((cl, cl), 1)
    causal = row >= col
    tril = causal.astype(F32)
    triu = (row <= col).astype(F32)
    lo = col < SSD_HEAD_DIM
    lane1 = _iota((1, LANES), 1)
    a_row = jnp.where(lane1 < SSD_HEADS, -jnp.exp(alog_ref[...]), 0.0)
    a_col = -jnp.exp(alogt_ref[...])
    expand = e_ref[...]

    def chunk(c, carry):
        r0 = pl.multiple_of(c * cl, cl)
        xs = xc[pl.ds(r0, cl), 0:SSD_WIDTH]
        bmat = xc[pl.ds(r0, cl), SSD_WIDTH:SSD_WIDTH + 2 * SSD_STATE]
        cmat = xc[pl.ds(r0, cl), SSD_WIDTH + 2 * SSD_STATE:XBC_WIDTH]
        dt = _softplus(sm_ref[pl.ds(r0, cl), :] + dtb_ref[...])
        da = dt * a_row
        acs = jnp.dot(tril, da, precision=HIGHEST, preferred_element_type=F32)
        dtt = _softplus(dtt_ref[0, :, pl.ds(r0, cl)] + dtbt_ref[...])
        acst = jnp.dot(dtt * a_col, triu, precision=HIGHEST, preferred_element_type=F32)
        dt_x = jnp.dot(dt, expand, precision=HIGHEST, preferred_element_type=F32)
        acs_x = jnp.dot(acs, expand, precision=HIGHEST, preferred_element_type=F32)
        last = acs_x[cl - 1:cl, :]
        eacs_x = jnp.exp(acs_x)
        dte_x = jnp.exp(last - acs_x)
        cd_x = jnp.exp(last)
        xdt = xs * dt_x
        xdte = (xdt * dte_x).astype(BF16)
        zc = z_ref[pl.ds(r0, cl), :]
        for g in range(2):
            bg = bmat[:, g * SSD_STATE:(g + 1) * SSD_STATE]
            cg = cmat[:, g * SSD_STATE:(g + 1) * SSD_STATE].astype(BF16)
            bgt = bg.T.astype(BF16)
            cbm = jnp.dot(cg, bgt, preferred_element_type=F32)
            pair_out = []
            for j in range(2):
                p = 2 * g + j
                sl = slice(p * LANES, (p + 1) * LANES)
                xp = xdt[:, sl]
                yd = jnp.zeros((cl, LANES), F32)
                for half in range(2):
                    h = 2 * p + half
                    seg = acs[:, h:h + 1] - acst[h:h + 1, :]
                    dec = jnp.exp(jnp.where(causal, seg, -jnp.inf))
                    gm = (cbm * dec).astype(BF16)
                    own = lo if half == 0 else jnp.logical_not(lo)
                    xm = jnp.where(own, xp, 0.0).astype(BF16)
                    yd = yd + jnp.dot(gm, xm, preferred_element_type=F32)
                prev_p = prev[:, sl]
                yo = jnp.dot(cg, prev_p.astype(BF16), preferred_element_type=F32) * eacs_x[:, sl]
                st = jnp.dot(bgt, xdte[:, sl], preferred_element_type=F32)
                prev[:, sl] = prev_p * cd_x[:, sl] + st
                pair_out.append(yd + yo + xs[:, sl] * dx_ref[:, sl])
            gs = slice(g * 2 * LANES, (g + 1) * 2 * LANES)
            yg = jnp.concatenate(pair_out, axis=-1) * _silu(zc[:, gs])
            ms = jnp.mean(yg * yg, axis=-1, keepdims=True)
            y_ref[pl.ds(r0, cl), gs] = yg * lax.rsqrt(ms + EPS) * ng_ref[:, gs]
        return carry

    lax.fori_loop(0, tt // cl, chunk, 0)


def _ssd(proj, dtt, cw, cb, dtb, alog, dx, ng, bsz, seq, tt):
    m = bsz * seq
    nt = seq // tt
    pad = LANES - SSD_HEADS
    dtb_row = jnp.pad(dtb, (0, pad)).reshape(1, LANES)
    alog_row = jnp.pad(alog, (0, pad)).reshape(1, LANES)
    dtb_col = jnp.broadcast_to(dtb[:, None], (SSD_HEADS, SSD_CHUNK))
    alog_col = jnp.broadcast_to(alog[:, None], (SSD_HEADS, SSD_CHUNK))
    expand = np.zeros((LANES, SSD_WIDTH), np.float32)
    for h in range(SSD_HEADS):
        expand[h, h * SSD_HEAD_DIM:(h + 1) * SSD_HEAD_DIM] = 1.0
    dx_row = jnp.repeat(dx, SSD_HEAD_DIM).reshape(1, SSD_WIDTH)
    full = lambda shape: pl.BlockSpec(shape, lambda b, t: (0,) * len(shape))
    return pl.pallas_call(
        functools.partial(_ssd_kernel, tt=tt),
        grid=(bsz, nt),
        in_specs=[
            pl.BlockSpec((tt, SSD_WIDTH), lambda b, t: (b * nt + t, COL_Z // SSD_WIDTH)),
            pl.BlockSpec((tt, XBC_WIDTH), lambda b, t: (b * nt + t, COL_XBC // XBC_WIDTH + 0)),
            pl.BlockSpec((tt, LANES), lambda b, t: (b * nt + t, COL_SM // LANES)),
            pl.BlockSpec((1, SSD_HEADS, tt), lambda b, t: (b, 0, t)),
            full((SSD_CONV, XBC_WIDTH)),
            full((1, XBC_WIDTH)),
            full((1, LANES)),
            full((1, LANES)),
            full((SSD_HEADS, SSD_CHUNK)),
            full((SSD_HEADS, SSD_CHUNK)),
            full((LANES, SSD_WIDTH)),
            full((1, SSD_WIDTH)),
            full((1, SSD_WIDTH)),
        ],
        out_specs=pl.BlockSpec((tt, SSD_WIDTH), lambda b, t: (b * nt + t, 0)),
        out_shape=jax.ShapeDtypeStruct((m, SSD_WIDTH), F32),
        scratch_shapes=[
            pltpu.VMEM((tt + 8, XBC_WIDTH), F32),
            pltpu.VMEM((tt, XBC_WIDTH), F32),
            pltpu.VMEM((SSD_STATE, SSD_WIDTH), F32),
        ],
        compiler_params=_cparams(("arbitrary", "arbitrary")),
        name="ssd_scan",
    )(proj, proj, proj, dtt, cw, cb.reshape(1, XBC_WIDTH), dtb_row, alog_row, dtb_col, alog_col,
      jnp.asarray(expand), dx_row, ng.reshape(1, SSD_WIDTH))


def _foxprep_kernel(qkv_ref, sm_ref, fb_ref, q_out, k_out, v_out, carry, *, tt):
    t = pl.program_id(1)

    @pl.when(t == 0)
    def _():
        carry[...] = jnp.zeros(carry.shape, F32)

    logit = sm_ref[...] + fb_ref[...]
    logf = -_softplus(-logit)
    row = _iota((tt, tt), 0)
    col = _iota((tt, tt), 1)
    tril = (row >= col).astype(F32)
    cum = jnp.dot(tril, logf, precision=HIGHEST, preferred_element_type=F32) + carry[...]
    carry[...] = cum[tt - 1:tt, :]

    lane = _iota((tt, LANES), 1)
    scale = ATTN_HEAD_DIM ** -0.5
    for h in range(ATTN_HEADS):
        pair, half = h // 2, h % 2
        own = (lane < ATTN_HEAD_DIM) if half == 0 else (lane >= ATTN_HEAD_DIM)
        a0 = ATTN_HEAD_DIM * (1 - half)
        cs = jnp.broadcast_to(cum[:, SM_F + h:SM_F + h + 1], (tt, LANES))
        hi = cs.astype(BF16).astype(F32)
        r1 = cs - hi
        mid = r1.astype(BF16).astype(F32)
        low = r1 - mid
        qp = qkv_ref[:, pair * LANES:(pair + 1) * LANES]
        kp = qkv_ref[:, ATTN_WIDTH + pair * LANES:ATTN_WIDTH + (pair + 1) * LANES]
        vp = qkv_ref[:, 2 * ATTN_WIDTH + pair * LANES:2 * ATTN_WIDTH + (pair + 1) * LANES]
        qa = jnp.where(lane == a0, hi, jnp.where(lane == a0 + 1, mid, jnp.where(
            lane == a0 + 2, low, jnp.where((lane >= a0 + 3) & (lane < a0 + 6), 1.0, 0.0))))
        ka = jnp.where(lane == a0 + 3, -hi, jnp.where(lane == a0 + 4, -mid, jnp.where(
            lane == a0 + 5, -low, jnp.where((lane >= a0) & (lane < a0 + 3), 1.0, 0.0))))
        q_out[0, h] = jnp.where(own, qp * scale, qa).astype(BF16)
        k_out[0, h] = jnp.where(own, kp, ka).astype(BF16)
        v_out[0, h] = jnp.where(own, vp, jnp.where(lane == a0, 1.0, 0.0)).astype(BF16)


def _foxprep(proj, fb, bsz, seq, tt):
    nt = seq // tt
    fb_row = jnp.zeros((1, LANES), F32).at[0, SM_F:SM_F + ATTN_HEADS].set(fb)
    shp = jax.ShapeDtypeStruct((bsz, ATTN_HEADS, seq, LANES), BF16)
    ospec = pl.BlockSpec((1, ATTN_HEADS, tt, LANES), lambda b, t: (b, 0, t, 0))
    return pl.pallas_call(
        functools.partial(_foxprep_kernel, tt=tt),
        grid=(bsz, nt),
        in_specs=[
            pl.BlockSpec((tt, 3 * ATTN_WIDTH), lambda b, t: (b * nt + t, COL_QKV // (3 * ATTN_WIDTH))),
            pl.BlockSpec((tt, LANES), lambda b, t: (b * nt + t, COL_SM // LANES)),
            pl.BlockSpec((1, LANES), lambda b, t: (0, 0)),
        ],
        out_specs=[ospec, ospec, ospec],
        out_shape=[shp, shp, shp],
        scratch_shapes=[pltpu.VMEM((1, LANES), F32)],
        compiler_params=_cparams(("arbitrary", "arbitrary")),
        name="fox_prep",
    )(proj, proj, fb_row)


def _fox_kernel(q_ref, k_ref, v_ref, o_ref, *, tq, ones_lane):
    i = pl.program_id(2)
    h = pl.program_id(1)
    q = q_ref[0, 0]
    nt = (((1,), (1,)), ((), ()))

    def step(kblk, vblk, m, acc, mask):
        s = lax.dot_general(q, kblk, nt, preferred_element_type=F32)
        if mask is not None:
            s = jnp.where(mask, s, -jnp.inf)
        m_new = jnp.maximum(m, jnp.max(s, axis=-1, keepdims=True))
        p = jnp.exp(s - m_new)
        alpha = jnp.exp(m - m_new)
        acc = acc * alpha + jnp.dot(p.astype(BF16), vblk, preferred_element_type=F32)
        return m_new, acc

    def body(j, carry):
        m, acc = carry
        r0 = pl.multiple_of(j * tq, tq)
        return step(k_ref[0, 0, pl.ds(r0, tq), :], v_ref[0, 0, pl.ds(r0, tq), :], m, acc, None)

    m0 = jnp.full((tq, 1), -jnp.inf, F32)
    acc0 = jnp.zeros((tq, LANES), F32)
    m, acc = lax.fori_loop(0, i, body, (m0, acc0))
    d0 = pl.multiple_of(i * tq, tq)
    diag = _iota((tq, tq), 0) >= _iota((tq, tq), 1)
    m, acc = step(k_ref[0, 0, pl.ds(d0, tq), :], v_ref[0, 0, pl.ds(d0, tq), :], m, acc, diag)
    lane = _iota((tq, LANES), 1)
    denom_lane = jnp.where((h % 2) == 1, ones_lane[1], ones_lane[0])
    denom = jnp.sum(jnp.where(lane == denom_lane, acc, 0.0), axis=-1, keepdims=True)
    o_ref[0, 0] = acc / denom


def _fox(qa, ka, va, tq):
    bsz, nh, seq, _ = qa.shape
    kv_spec = pl.BlockSpec((1, 1, seq, LANES), lambda b, h, i: (b, h, 0, 0))
    q_spec = pl.BlockSpec((1, 1, tq, LANES), lambda b, h, i: (b, h, i, 0))
    return pl.pallas_call(
        functools.partial(_fox_kernel, tq=tq, ones_lane=(ATTN_HEAD_DIM, 0)),
        grid=(bsz, nh, seq // tq),
        in_specs=[q_spec, kv_spec, kv_spec],
        out_specs=q_spec,
        out_shape=jax.ShapeDtypeStruct((bsz, nh, seq, LANES), F32),
        compiler_params=_cparams(("arbitrary", "arbitrary", "arbitrary")),
        name="fox_attn",
    )(qa, ka, va)


CONF_HALO = 32
CONF_ROWS = 64


def _conf_kernel(ga_ref, gb_ref, w_ref, b_ref, lg_ref, lb_ref, y_ref, upad, *, tt):
    t = pl.program_id(1)

    @pl.when(t == 0)
    def _():
        upad[0:CONF_HALO, :] = jnp.zeros((CONF_HALO, CONV_WIDTH), F32)

    upad[CONF_HALO:CONF_HALO + tt, :] = ga_ref[...] * _sigmoid(gb_ref[...])
    base = CONF_HALO - (CONV_KERNEL - 1)
    for r in range(tt // CONF_ROWS):
        r0 = r * CONF_ROWS
        acc = jnp.broadcast_to(b_ref[...], (CONF_ROWS, CONV_WIDTH))
        for k in range(CONV_KERNEL):
            acc = acc + w_ref[k:k + 1, :] * upad[r0 + base + k:r0 + base + k + CONF_ROWS, :]
        mu = jnp.mean(acc, axis=-1, keepdims=True)
        cen = acc - mu
        var = jnp.mean(cen * cen, axis=-1, keepdims=True)
        y = cen * lax.rsqrt(var + EPS) * lg_ref[...] + lb_ref[...]
        y_ref[r0:r0 + CONF_ROWS, :] = _silu(y)
    upad[0:CONF_HALO, :] = upad[tt:tt + CONF_HALO, :]


def _conformer(proj, w, b, lg, lb, bsz, seq, tt):
    m = bsz * seq
    nt = seq // tt
    full = lambda shape: pl.BlockSpec(shape, lambda bb, t: (0,) * len(shape))
    return pl.pallas_call(
        functools.partial(_conf_kernel, tt=tt),
        grid=(bsz, nt),
        in_specs=[
            pl.BlockSpec((tt, CONV_WIDTH), lambda bb, t: (bb * nt + t, COL_GA // CONV_WIDTH)),
            pl.BlockSpec((tt, CONV_WIDTH), lambda bb, t: (bb * nt + t, COL_GB // CONV_WIDTH)),
            full((CONV_KERNEL, CONV_WIDTH)),
            full((1, CONV_WIDTH)),
            full((1, CONV_WIDTH)),
            full((1, CONV_WIDTH)),
        ],
        out_specs=pl.BlockSpec((tt, CONV_WIDTH), lambda bb, t: (bb * nt + t, 0)),
        out_shape=jax.ShapeDtypeStruct((m, CONV_WIDTH), F32),
        scratch_shapes=[pltpu.VMEM((tt + CONF_HALO, CONV_WIDTH), F32)],
        compiler_params=_cparams(("arbitrary", "arbitrary")),
        name="conformer_conv",
    )(proj, proj, w, b.reshape(1, -1), lg.reshape(1, -1), lb.reshape(1, -1))


ROUTE_BIG = 1e9


def _split3(x):
    hi = x.astype(BF16)
    lo = (x - hi.astype(F32)).astype(BF16)
    return hi, lo


def _outproj_kernel(x_ref, ys_ref, oa_ref, yc_ref, wo_ref, fg_ref, gm_ref, a2_ref, s2_ref,
                    wrh_ref, wrl_ref, br_ref, xn_ref, h2_ref, rw_ref, re_ref):
    tm = x_ref.shape[0]
    lane = _iota((tm, LANES), 1)
    lo = lane < ATTN_HEAD_DIM
    att = jnp.concatenate([jnp.where(lo, oa_ref[0, 0], oa_ref[0, 1]),
                           jnp.where(lo, oa_ref[0, 2], oa_ref[0, 3])], axis=-1)
    ms = jnp.mean(att * att, axis=-1, keepdims=True)
    att = att * lax.rsqrt(ms + EPS) * fg_ref[...]
    y = jnp.dot(ys_ref[...].astype(BF16), wo_ref[0:SSD_WIDTH, :], preferred_element_type=F32)
    y = y + jnp.dot(att.astype(BF16), wo_ref[SSD_WIDTH:SSD_WIDTH + ATTN_WIDTH, :],
                    preferred_element_type=F32)
    y = y + jnp.dot(yc_ref[...].astype(BF16), wo_ref[SSD_WIDTH + ATTN_WIDTH:, :],
                    preferred_element_type=F32)
    xn = x_ref[...] + gm_ref[0] * y
    xn_ref[...] = xn
    ms2 = jnp.mean(xn * xn, axis=-1, keepdims=True)
    h2 = xn * lax.rsqrt(ms2 + EPS) * a2_ref[0] + s2_ref[0]
    h2_ref[...] = h2

    hh, hl = _split3(h2)
    logits = (jnp.dot(hh, wrh_ref[...], preferred_element_type=F32)
              + jnp.dot(hl, wrh_ref[...], preferred_element_type=F32)
              + jnp.dot(hh, wrl_ref[...], preferred_element_type=F32)) + br_ref[...]
    lanef = lane.astype(F32)
    lg = jnp.where(lane < N_GROUPS, logits, -jnp.inf)
    gmax = jnp.max(lg, axis=-1, keepdims=True)
    gsum = jnp.sum(jnp.exp(lg - gmax), axis=-1, keepdims=True)
    gidx = jnp.min(jnp.where(lg == gmax, lanef, ROUTE_BIG), axis=-1, keepdims=True)
    e_lo = N_GROUPS + PER_GROUP * gidx
    le = jnp.where((lanef >= e_lo) & (lanef < e_lo + PER_GROUP), logits, -jnp.inf)
    m1 = jnp.max(le, axis=-1, keepdims=True)
    i1 = jnp.min(jnp.where(le == m1, lanef, ROUTE_BIG), axis=-1, keepdims=True)
    le2 = jnp.where(lanef == i1, -jnp.inf, le)
    m2 = jnp.max(le2, axis=-1, keepdims=True)
    i2 = jnp.min(jnp.where(le2 == m2, lanef, ROUTE_BIG), axis=-1, keepdims=True)
    esum = jnp.sum(jnp.exp(le - m1), axis=-1, keepdims=True)
    p1 = 1.0 / esum
    p2 = jnp.exp(m2 - m1) / esum
    psel = 1.0 / gsum
    w1 = p1 / (p1 + p2) * psel
    w2 = p2 / (p1 + p2) * psel
    rw_ref[...] = jnp.where(lane == 0, w1, jnp.where(lane == 1, w2, 0.0))
    re_ref[...] = jnp.where(lane == 0, i1 - N_GROUPS, jnp.where(lane == 1, i2 - N_GROUPS, 0.0))


def _outproj(x2, ys, oa, yc, wo, fg, gm, a2, s2, wrh, wrl, br, seq, tm):
    m = x2.shape[0]
    per_b = seq // tm
    row = lambda w: pl.BlockSpec((tm, w), lambda i: (i, 0))
    bvec = pl.BlockSpec((1, 1, D_MODEL), lambda i: (i // per_b, 0, 0))
    full = lambda shape: pl.BlockSpec(shape, lambda i: (0,) * len(shape))
    return pl.pallas_call(
        _outproj_kernel,
        grid=(m // tm,),
        in_specs=[
            row(D_MODEL), row(SSD_WIDTH),
            pl.BlockSpec((1, ATTN_HEADS, tm, LANES), lambda i: (i // per_b, 0, i % per_b, 0)),
            row(CONV_WIDTH),
            full((D_MODEL, D_MODEL)), full((1, ATTN_WIDTH)), bvec, bvec, bvec,
            full((D_MODEL, LANES)), full((D_MODEL, LANES)), full((1, LANES)),
        ],
        out_specs=[row(D_MODEL), row(D_MODEL), row(LANES), row(LANES)],
        out_shape=[jax.ShapeDtypeStruct((m, D_MODEL), F32), jax.ShapeDtypeStruct((m, D_MODEL), F32),
                   jax.ShapeDtypeStruct((m, LANES), F32), jax.ShapeDtypeStruct((m, LANES), F32)],
        compiler_params=_cparams(("arbitrary",)),
        name="out_proj_router",
    )(x2, ys, oa, yc, wo, fg, gm, a2, s2, wrh, wrl, br)


def _rank_kernel(re_ref, pos_ref, cnt_ref, carry):
    i = pl.program_id(0)
    tr = re_ref.shape[0]

    @pl.when(i == 0)
    def _():
        carry[...] = jnp.zeros(carry.shape, F32)

    e = re_ref[...]
    lanef = _iota((tr, LANES), 1).astype(F32)
    oh0 = (jnp.broadcast_to(e[:, 0:1], (tr, LANES)) == lanef).astype(F32)
    oh1 = (jnp.broadcast_to(e[:, 1:2], (tr, LANES)) == lanef).astype(F32)
    oh = oh0 + oh1
    before = (_iota((tr, tr), 0) > _iota((tr, tr), 1)).astype(BF16)
    rank = jnp.dot(before, oh.astype(BF16), preferred_element_type=F32) + carry[...]
    pos0 = jnp.sum(oh0 * rank, axis=-1, keepdims=True)
    pos1 = jnp.sum(oh1 * rank, axis=-1, keepdims=True)
    lane = _iota((tr, LANES), 1)
    pos_ref[...] = jnp.where(lane == 0, pos0, jnp.where(lane == 1, pos1, 0.0))
    carry[...] = carry[...] + jnp.sum(oh, axis=0, keepdims=True)
    cnt_ref[...] = jnp.broadcast_to(carry[...], cnt_ref.shape)


def _rank(re, tr):
    m = re.shape[0]
    return pl.pallas_call(
        _rank_kernel,
        grid=(m // tr,),
        in_specs=[pl.BlockSpec((tr, LANES), lambda i: (i, 0))],
        out_specs=[pl.BlockSpec((tr, LANES), lambda i: (i, 0)),
                   pl.BlockSpec((8, LANES), lambda i: (0, 0))],
        out_shape=[jax.ShapeDtypeStruct((m, LANES), F32), jax.ShapeDtypeStruct((8, LANES), F32)],
        scratch_shapes=[pltpu.VMEM((1, LANES), F32)],
        compiler_params=_cparams(("arbitrary",)),
        name="expert_rank",
    )(re)


def _expert_kernel(be_ref, code_ref, nu_ref, h_hbm, wg_ref, wu_ref, wd_ref, y_hbm,
                   xbuf, ybuf, wgb, wub, wdb, gsem, ssem, *, n_tok):
    i = pl.program_id(0)
    slot = i % 2
    n_used = nu_ref[0]
    rows = MOE_BLOCK

    def gather_copy(tok, sl, r):
        return pltpu.make_async_copy(h_hbm.at[pl.ds(tok, 1), :], xbuf.at[sl, pl.ds(r, 1), :],
                                     gsem.at[sl])

    def scatter_copy(dst, sl, r):
        return pltpu.make_async_copy(ybuf.at[sl, pl.ds(r, 1), :], y_hbm.at[pl.ds(dst, 1), :],
                                     ssem.at[sl])

    def issue_gather(blk, sl):
        def body(r, c):
            tok = jnp.minimum(lax.shift_right_logical(code_ref[blk * rows + r], 1), n_tok - 1)
            gather_copy(tok, sl, r).start()
            return c
        lax.fori_loop(0, rows, body, 0)

    def wait_gather(sl):
        def body(r, c):
            gather_copy(0, sl, r).wait()
            return c
        lax.fori_loop(0, rows, body, 0)

    def issue_scatter(blk, sl):
        def body(r, c):
            dst = code_ref[blk * rows + r]

            @pl.when(dst < 2 * n_tok)
            def _():
                scatter_copy(dst, sl, r).start()
            return c
        lax.fori_loop(0, rows, body, 0)

    def wait_scatter(blk, sl):
        def body(r, c):
            @pl.when(code_ref[blk * rows + r] < 2 * n_tok)
            def _():
                scatter_copy(0, sl, r).wait()
            return c
        lax.fori_loop(0, rows, body, 0)

    @pl.when(i == 0)
    def _():
        issue_gather(0, 0)

    @pl.when(i + 1 < n_used)
    def _():
        issue_gather(i + 1, 1 - slot)

    @pl.when(i < n_used)
    def _():
        wait_gather(slot)
        prev_e = be_ref[jnp.maximum(i - 1, 0)]

        @pl.when((i == 0) | (be_ref[i] != prev_e))
        def _():
            wgb[...] = wg_ref[0].astype(BF16)
            wub[...] = wu_ref[0].astype(BF16)
            wdb[...] = wd_ref[0].astype(BF16)

        x = xbuf[slot].astype(BF16)
        hid = _silu(jnp.dot(x, wgb[...], preferred_element_type=F32)) * jnp.dot(
            x, wub[...], preferred_element_type=F32)
        y = jnp.dot(hid.astype(BF16), wdb[...], preferred_element_type=F32)

        @pl.when(i >= 2)
        def _():
            wait_scatter(i - 2, slot)

        ybuf[slot] = y
        issue_scatter(i, slot)

        @pl.when(i == n_used - 1)
        def _():
            @pl.when(i >= 1)
            def _():
                wait_scatter(i - 1, 1 - slot)
            wait_scatter(i, slot)


def _experts(block_e, code, n_used, h2, wg, wu, wd):
    m = h2.shape[0]
    n_blocks = block_e.shape[0]
    out_rows = 2 * m
    grid_spec = pltpu.PrefetchScalarGridSpec(
        num_scalar_prefetch=3,
        grid=(n_blocks,),
        in_specs=[
            pl.BlockSpec(memory_space=pl.ANY),
            pl.BlockSpec((1, D_MODEL, D_EXPERT), lambda i, be, cd, nu: (be[i], 0, 0)),
            pl.BlockSpec((1, D_MODEL, D_EXPERT), lambda i, be, cd, nu: (be[i], 0, 0)),
            pl.BlockSpec((1, D_EXPERT, D_MODEL), lambda i, be, cd, nu: (be[i], 0, 0)),
        ],
        out_specs=pl.BlockSpec(memory_space=pl.ANY),
        scratch_shapes=[
            pltpu.VMEM((2, MOE_BLOCK, D_MODEL), F32),
            pltpu.VMEM((2, MOE_BLOCK, D_MODEL), F32),
            pltpu.VMEM((D_MODEL, D_EXPERT), BF16),
            pltpu.VMEM((D_MODEL, D_EXPERT), BF16),
            pltpu.VMEM((D_EXPERT, D_MODEL), BF16),
            pltpu.SemaphoreType.DMA((2,)),
            pltpu.SemaphoreType.DMA((2,)),
        ],
    )
    return pl.pallas_call(
        functools.partial(_expert_kernel, n_tok=m),
        grid_spec=grid_spec,
        out_shape=jax.ShapeDtypeStruct((out_rows, D_MODEL), F32),
        compiler_params=_cparams(("arbitrary",)),
        name="moe_experts",
    )(block_e, code, n_used, h2, wg, wu, wd)


def _combine_kernel(x_ref, y_ref, rw_ref, gf_ref, fg_ref, o_ref, *, final):
    w = rw_ref[...]
    moe = w[:, 0:1] * y_ref[:, 0:D_MODEL] + w[:, 1:2] * y_ref[:, D_MODEL:2 * D_MODEL]
    x = x_ref[...] + gf_ref[0] * moe
    if final:
        ms = jnp.mean(x * x, axis=-1, keepdims=True)
        x = x * lax.rsqrt(ms + EPS) * fg_ref[...]
    o_ref[...] = x


def _combine(x2, y2, rw, gf, fg, seq, tm, final):
    m = x2.shape[0]
    per_b = seq // tm
    return pl.pallas_call(
        functools.partial(_combine_kernel, final=final),
        grid=(m // tm,),
        in_specs=[
            pl.BlockSpec((tm, D_MODEL), lambda i: (i, 0)),
            pl.BlockSpec((tm, 2 * D_MODEL), lambda i: (i, 0)),
            pl.BlockSpec((tm, LANES), lambda i: (i, 0)),
            pl.BlockSpec((1, 1, D_MODEL), lambda i: (i // per_b, 0, 0)),
            pl.BlockSpec((1, D_MODEL), lambda i: (0, 0)),
        ],
        out_specs=pl.BlockSpec((tm, D_MODEL), lambda i: (i, 0)),
        out_shape=jax.ShapeDtypeStruct((m, D_MODEL), F32),
        compiler_params=_cparams(("arbitrary",)),
        name="moe_combine",
    )(x2, y2, rw, gf, fg)


def _pack_w_in(w):
    d_dt = SSD_WIDTH + XBC_WIDTH
    d_q = d_dt + SSD_HEADS
    d_f = d_q + 3 * ATTN_WIDTH
    d_ga = d_f + ATTN_HEADS
    small = jnp.concatenate([w[:, d_dt:d_q], w[:, d_f:d_ga],
                             jnp.zeros((D_MODEL, LANES - SSD_HEADS - ATTN_HEADS), w.dtype)], axis=1)
    return jnp.concatenate([w[:, SSD_WIDTH:d_dt], w[:, :SSD_WIDTH], w[:, d_q:d_f], w[:, d_ga:], small],
                           axis=1).astype(BF16)


def kernel(x, c, ada_w, ada_b, norm_mix_g, w_in, ssd_conv_w, ssd_conv_b, ssd_dt_bias, ssd_a_log,
           ssd_d, ssd_norm_g, fox_f_bias, fox_norm_g, cm_conv_w, cm_conv_b, cm_ln_g, cm_ln_b, w_out,
           norm_ffn_g, w_router_group, b_router_group, w_router_expert, b_router_expert, w_gate,
           w_up, w_down, final_norm_g):
    bsz, seq, d = x.shape
    m = bsz * seq
    tm = min(512, seq)
    tt = min(256, seq)
    n_assign = 2 * m
    n_blocks = (n_assign + MOE_BLOCK - 1) // MOE_BLOCK + N_EXPERTS

    mod = _modulation(c, ada_w, ada_b)
    x2 = x.reshape(m, d)
    for l in range(DEPTH):
        sh_m, sc_m, g_m, sh_f, sc_f, g_f = [v.reshape(bsz, 1, d) for v in jnp.split(mod[l], 6, axis=-1)]
        a_m = norm_mix_g[l][None, None, :] * (1.0 + sc_m)
        proj = _inproj(x2, a_m, sh_m, _pack_w_in(w_in[l]), seq, tm)
        dtt = proj[:, COL_SM + SM_DT:COL_SM + SM_DT + SSD_HEADS].reshape(bsz, seq, SSD_HEADS)
        dtt = dtt.transpose(0, 2, 1)
        y_ssd = _ssd(proj, dtt, ssd_conv_w[l], ssd_conv_b[l], ssd_dt_bias[l], ssd_a_log[l],
                     ssd_d[l], ssd_norm_g[l], bsz, seq, tt)
        qa, ka, va = _foxprep(proj, fox_f_bias[l], bsz, seq, tt)
        o_att = _fox(qa, ka, va, tt)
        y_cnv = _conformer(proj, cm_conv_w[l], cm_conv_b[l], cm_ln_g[l], cm_ln_b[l], bsz, seq, tt)

        w_r = jnp.concatenate([w_router_group[l], w_router_expert[l],
                               jnp.zeros((d, LANES - N_GROUPS - N_EXPERTS), F32)], axis=1)
        w_rh = w_r.astype(BF16)
        w_rl = (w_r - w_rh.astype(F32)).astype(BF16)
        b_r = jnp.concatenate([b_router_group[l], b_router_expert[l],
                               jnp.zeros((LANES - N_GROUPS - N_EXPERTS,), F32)]).reshape(1, LANES)
        a_f = norm_ffn_g[l][None, None, :] * (1.0 + sc_f)
        x2, h2, rw, re = _outproj(x2, y_ssd, o_att, y_cnv, w_out[l].astype(BF16),
                                  fox_norm_g[l].reshape(1, -1), g_m, a_f, sh_f, w_rh, w_rl, b_r,
                                  seq, tm)

        pos, cnt = _rank(re, tm)
        counts = cnt[0, :N_EXPERTS].astype(jnp.int32)
        padded = (counts + MOE_BLOCK - 1) // MOE_BLOCK * MOE_BLOCK
        pad_end = jnp.cumsum(padded)
        pad_start = pad_end - padded
        e_idx = re[:, :2].astype(jnp.int32)
        dest = pad_start[e_idx] + pos[:, :2].astype(jnp.int32)
        slot_code = 2 * jnp.arange(m, dtype=jnp.int32)[:, None] + jnp.arange(2, dtype=jnp.int32)[None, :]
        code = jnp.full((n_blocks * MOE_BLOCK,), 2 * m, jnp.int32).at[dest.reshape(-1)].set(
            slot_code.reshape(-1))
        block_e = jnp.minimum(
            jnp.searchsorted(pad_end, jnp.arange(n_blocks, dtype=jnp.int32) * MOE_BLOCK, side='right'),
            N_EXPERTS - 1).astype(jnp.int32)
        n_used = (pad_end[-1] // MOE_BLOCK).astype(jnp.int32).reshape(1)
        yk = _experts(block_e, code, n_used, h2, w_gate[l], w_up[l], w_down[l])
        x2 = _combine(x2, yk.reshape(m, 2 * d), rw, g_f, final_norm_g.reshape(1, d), seq, tm,
                      final=(l == DEPTH - 1))
    return x2.reshape(bsz, seq, d)
```

```python
import functools

import jax
import jax.numpy as jnp
import numpy as np
from jax import lax
from jax.experimental import pallas as pl
from jax.experimental.pallas import tpu as pltpu

F32 = jnp.float32
BF16 = jnp.bfloat16
HIGHEST = lax.Precision.HIGHEST

D_MODEL = 1024
DEPTH = 4
SSD_WIDTH = 512
SSD_HEADS = 8
SSD_HEAD_DIM = 64
SSD_STATE = 128
SSD_CONV = 4
SSD_CHUNK = 128
XBC_WIDTH = 1024
ATTN_WIDTH = 256
ATTN_HEADS = 4
ATTN_HEAD_DIM = 64
CONV_WIDTH = 256
CONV_KERNEL = 31
N_GROUPS = 4
PER_GROUP = 8
N_EXPERTS = 32
D_EXPERT = 512
MOE_BLOCK = 256
EPS = 1e-6

LANES = 128
COL_XBC = 0
COL_Z = 1024
COL_QKV = 1536
COL_GA = 2304
COL_GB = 2560
COL_SM = 2816
NP = 2944
SM_DT = 0
SM_F = 8

VMEM_LIMIT = 56 * 1024 * 1024


def _cparams(sem):
    return pltpu.CompilerParams(dimension_semantics=sem, vmem_limit_bytes=VMEM_LIMIT)


def _sigmoid(x):
    return 1.0 / (1.0 + jnp.exp(-x))


def _silu(x):
    return x * _sigmoid(x)


def _softplus(x):
    return jnp.maximum(x, 0.0) + jnp.log1p(jnp.exp(-jnp.abs(x)))


def _iota(shape, dim):
    return lax.broadcasted_iota(jnp.int32, shape, dim)


def _mod_kernel(c_ref, w_ref, b_ref, o_ref):
    cond = _silu(c_ref[...])
    o_ref[0] = jnp.dot(cond, w_ref[0], precision=HIGHEST, preferred_element_type=F32) + b_ref[0]


def _modulation(c, ada_w, ada_b):
    bsz = c.shape[0]
    rows = 8
    cpad = jnp.zeros((rows, D_MODEL), F32).at[:bsz].set(c)
    tn = 1536
    n6 = 6 * D_MODEL
    out = pl.pallas_call(
        _mod_kernel,
        grid=(DEPTH, n6 // tn),
        in_specs=[
            pl.BlockSpec((rows, D_MODEL), lambda l, j: (0, 0)),
            pl.BlockSpec((1, D_MODEL, tn), lambda l, j: (l, 0, j)),
            pl.BlockSpec((1, 1, tn), lambda l, j: (l, 0, j)),
        ],
        out_specs=pl.BlockSpec((1, rows, tn), lambda l, j: (l, 0, j)),
        out_shape=jax.ShapeDtypeStruct((DEPTH, rows, n6), F32),
        compiler_params=_cparams(("arbitrary", "arbitrary")),
        name="adaln_mod",
    )(cpad, ada_w, ada_b.reshape(DEPTH, 1, n6))
    return out[:, :bsz]


def _inproj_kernel(x_ref, a_ref, s_ref, w_ref, o_ref):
    x = x_ref[...]
    ms = jnp.mean(x * x, axis=-1, keepdims=True)
    h = x * lax.rsqrt(ms + EPS) * a_ref[0] + s_ref[0]
    o_ref[...] = jnp.dot(h.astype(BF16), w_ref[...], preferred_element_type=F32)


def _inproj(x2, a, s, w_pack, seq, tm):
    m = x2.shape[0]
    per_b = seq // tm
    return pl.pallas_call(
        _inproj_kernel,
        grid=(m // tm,),
        in_specs=[
            pl.BlockSpec((tm, D_MODEL), lambda i: (i, 0)),
            pl.BlockSpec((1, 1, D_MODEL), lambda i: (i // per_b, 0, 0)),
            pl.BlockSpec((1, 1, D_MODEL), lambda i: (i // per_b, 0, 0)),
            pl.BlockSpec((D_MODEL, NP), lambda i: (0, 0)),
        ],
        out_specs=pl.BlockSpec((tm, NP), lambda i: (i, 0)),
        out_shape=jax.ShapeDtypeStruct((m, NP), F32),
        compiler_params=_cparams(("arbitrary",)),
        name="in_proj",
    )(x2, a, s, w_pack)


def _ssd_kernel(z_ref, xbc_ref, sm_ref, dtt_ref, cw_ref, cb_ref, dtb_ref, alog_ref, dtbt_ref,
                alogt_ref, e_ref, dx_ref, ng_ref, y_ref, xpad, xc, prev, *, tt):
    t = pl.program_id(1)

    @pl.when(t == 0)
    def _():
        xpad[0:8, :] = jnp.zeros((8, XBC_WIDTH), F32)
        prev[...] = jnp.zeros(prev.shape, F32)

    xpad[8:8 + tt, :] = xbc_ref[...]
    acc = jnp.broadcast_to(cb_ref[...], (tt, XBC_WIDTH))
    for k in range(SSD_CONV):
        off = 8 - (SSD_CONV - 1) + k
        acc = acc + cw_ref[k:k + 1, :] * xpad[off:off + tt, :]
    xc[...] = _silu(acc)
    xpad[0:8, :] = xpad[tt:tt + 8, :]

    cl = SSD_CHUNK
    row = _iota((cl, cl), 0)
    col = _iota((cl, cl), 1)
    causal = row >= col
    tril = causal.astype(F32)
    triu = (row <= col).astype(F32)
    lo = col < SSD_HEAD_DIM
    lane1 = _iota((1, LANES), 1)
    a_row = jnp.where(lane1 < SSD_HEADS, -jnp.exp(alog_ref[...]), 0.0)
    a_col = -jnp.exp(alogt_ref[...])
    expand = e_ref[...]

    def chunk(c, carry):
        r0 = pl.multiple_of(c * cl, cl)
        xs = xc[pl.ds(r0, cl), 0:SSD_WIDTH]
        bmat = xc[pl.ds(r0, cl), SSD_WIDTH:SSD_WIDTH + 2 * SSD_STATE]
        cmat = xc[pl.ds(r0, cl), SSD_WIDTH + 2 * SSD_STATE:XBC_WIDTH]
        dt = _softplus(sm_ref[pl.ds(r0, cl), :] + dtb_ref[...])
        da = dt * a_row
        acs = jnp.dot(tril, da, precision=HIGHEST, preferred_element_type=F32)
        dtt = _softplus(dtt_ref[0, :, pl.ds(r0, cl)] + dtbt_ref[...])
        acst = jnp.dot(dtt * a_col, triu, precision=HIGHEST, preferred_element_type=F32)
        dt_x = jnp.dot(dt, expand, precision=HIGHEST, preferred_element_type=F32)
        acs_x = jnp.dot(acs, expand, precision=HIGHEST, preferred_element_type=F32)
        last = acs_x[cl - 1:cl, :]
        eacs_x = jnp.exp(acs_x)
        dte_x = jnp.exp(last - acs_x)
        cd_x = jnp.exp(last)
        xdt = xs * dt_x
        xdte = (xdt * dte_x).astype(BF16)
        zc = z_ref[pl.ds(r0, cl), :]
        for g in range(2):
            bg = bmat[:, g * SSD_STATE:(g + 1) * SSD_STATE]
            cg = cmat[:, g * SSD_STATE:(g + 1) * SSD_STATE].astype(BF16)
            bgt = bg.T.astype(BF16)
            cbm = jnp.dot(cg, bgt, preferred_element_type=F32)
            pair_out = []
            for j in range(2):
                p = 2 * g + j
                sl = slice(p * LANES, (p + 1) * LANES)
                xp = xdt[:, sl]
                yd = jnp.zeros((cl, LANES), F32)
                for half in range(2):
                    h = 2 * p + half
                    seg = acs[:, h:h + 1] - acst[h:h + 1, :]
                    dec = jnp.exp(jnp.where(causal, seg, -jnp.inf))
                    gm = (cbm * dec).astype(BF16)
                    own = lo if half == 0 else jnp.logical_not(lo)
                    xm = jnp.where(own, xp, 0.0).astype(BF16)
                    yd = yd + jnp.dot(gm, xm, preferred_element_type=F32)
                prev_p = prev[:, sl]
                yo = jnp.dot(cg, prev_p.astype(BF16), preferred_element_type=F32) * eacs_x[:, sl]
                st = jnp.dot(bgt, xdte[:, sl], preferred_element_type=F32)
                prev[:, sl] = prev_p * cd_x[:, sl] + st
                pair_out.append(yd + yo + xs[:, sl] * dx_ref[:, sl])
            gs = slice(g * 2 * LANES, (g + 1) * 2 * LANES)
            yg = jnp.concatenate(pair_out, axis=-1) * _silu(zc[:, gs])
            ms = jnp.mean(yg * yg, axis=-1, keepdims=True)
            y_ref[pl.ds(r0, cl), gs] = yg * lax.rsqrt(ms + EPS) * ng_ref[:, gs]
        return carry

    lax.fori_loop(0, tt // cl, chunk, 0)


def _ssd(proj, dtt, cw, cb, dtb, alog, dx, ng, bsz, seq, tt):
    m = bsz * seq
    nt = seq // tt
    pad = LANES - SSD_HEADS
    dtb_row = jnp.pad(dtb, (0, pad)).reshape(1, LANES)
    alog_row = jnp.pad(alog, (0, pad)).reshape(1, LANES)
    dtb_col = jnp.broadcast_to(dtb[:, None], (SSD_HEADS, SSD_CHUNK))
    alog_col = jnp.broadcast_to(alog[:, None], (SSD_HEADS, SSD_CHUNK))
    expand = np.zeros((LANES, SSD_WIDTH), np.float32)
    for h in range(SSD_HEADS):
        expand[h, h * SSD_HEAD_DIM:(h + 1) * SSD_HEAD_DIM] = 1.0
    dx_row = jnp.repeat(dx, SSD_HEAD_DIM).reshape(1, SSD_WIDTH)
    full = lambda shape: pl.BlockSpec(shape, lambda b, t: (0,) * len(shape))
    return pl.pallas_call(
        functools.partial(_ssd_kernel, tt=tt),
        grid=(bsz, nt),
        in_specs=[
            pl.BlockSpec((tt, SSD_WIDTH), lambda b, t: (b * nt + t, COL_Z // SSD_WIDTH)),
            pl.BlockSpec((tt, XBC_WIDTH), lambda b, t: (b * nt + t, COL_XBC // XBC_WIDTH + 0)),
            pl.BlockSpec((tt, LANES), lambda b, t: (b * nt + t, COL_SM // LANES)),
            pl.BlockSpec((1, SSD_HEADS, tt), lambda b, t: (b, 0, t)),
            full((SSD_CONV, XBC_WIDTH)),
            full((1, XBC_WIDTH)),
            full((1, LANES)),
            full((1, LANES)),
            full((SSD_HEADS, SSD_CHUNK)),
            full((SSD_HEADS, SSD_CHUNK)),
            full((LANES, SSD_WIDTH)),
            full((1, SSD_WIDTH)),
            full((1, SSD_WIDTH)),
        ],
        out_specs=pl.BlockSpec((tt, SSD_WIDTH), lambda b, t: (b * nt + t, 0)),
        out_shape=jax.ShapeDtypeStruct((m, SSD_WIDTH), F32),
        scratch_shapes=[
            pltpu.VMEM((tt + 8, XBC_WIDTH), F32),
            pltpu.VMEM((tt, XBC_WIDTH), F32),
            pltpu.VMEM((SSD_STATE, SSD_WIDTH), F32),
        ],
        compiler_params=_cparams(("arbitrary", "arbitrary")),
        name="ssd_scan",
    )(proj, proj, proj, dtt, cw, cb.reshape(1, XBC_WIDTH), dtb_row, alog_row, dtb_col, alog_col,
      jnp.asarray(expand), dx_row, ng.reshape(1, SSD_WIDTH))


LOG2E = 1.4426950408889634
ATT_BLOCK = 256
PRUNE_LOG2 = 140.0
AUX_CUMEND = 0
AUX_KMAX = 1


def _foxprep_kernel(qkv_ref, sm_ref, fb_ref, q_out, k_out, v_out, aux_out, carry, *, tt):
    t = pl.program_id(1)

    @pl.when(t == 0)
    def _():
        carry[...] = jnp.zeros(carry.shape, F32)
        aux_out[...] = jnp.zeros(aux_out.shape, F32)

    logit = sm_ref[...] + fb_ref[...]
    logf = -_softplus(-logit)
    row = _iota((tt, tt), 0)
    col = _iota((tt, tt), 1)
    tril = (row >= col).astype(F32)
    cum = jnp.dot(tril, logf, precision=HIGHEST, preferred_element_type=F32) + carry[...]
    carry[...] = cum[tt - 1:tt, :]

    lane = _iota((tt, LANES), 1)
    lane1 = _iota((1, LANES), 1)
    scale = ATTN_HEAD_DIM ** -0.5 * LOG2E
    for h in range(ATTN_HEADS):
        pair, half = h // 2, h % 2
        own = (lane < ATTN_HEAD_DIM) if half == 0 else (lane >= ATTN_HEAD_DIM)
        a0 = ATTN_HEAD_DIM * (1 - half)
        cs = jnp.broadcast_to(cum[:, SM_F + h:SM_F + h + 1], (tt, LANES)) * LOG2E
        hi = cs.astype(BF16).astype(F32)
        r1 = cs - hi
        mid = r1.astype(BF16).astype(F32)
        low = r1 - mid
        qp = qkv_ref[:, pair * LANES:(pair + 1) * LANES]
        kp = qkv_ref[:, ATTN_WIDTH + pair * LANES:ATTN_WIDTH + (pair + 1) * LANES]
        vp = qkv_ref[:, 2 * ATTN_WIDTH + pair * LANES:2 * ATTN_WIDTH + (pair + 1) * LANES]
        qa = jnp.where(lane == a0, hi, jnp.where(lane == a0 + 1, mid, jnp.where(
            lane == a0 + 2, low, jnp.where((lane >= a0 + 3) & (lane < a0 + 6), 1.0, 0.0))))
        ka = jnp.where(lane == a0 + 3, -hi, jnp.where(lane == a0 + 4, -mid, jnp.where(
            lane == a0 + 5, -low, jnp.where((lane >= a0) & (lane < a0 + 3), 1.0, 0.0))))
        q_out[0, h] = jnp.where(own, qp * scale, qa).astype(BF16)
        kb = jnp.where(own, kp, ka).astype(BF16)
        k_out[0, h] = kb
        v_out[0, h] = jnp.where(own, vp, jnp.where(lane == a0, 1.0, 0.0)).astype(BF16)
        kf = jnp.where(own, kb.astype(F32), 0.0)
        kn2 = jnp.max(jnp.sum(kf * kf, axis=-1, keepdims=True), axis=0, keepdims=True)
        here = lane1 == t
        aux_out[0, h, AUX_CUMEND:AUX_CUMEND + 1, :] = jnp.where(
            here, cs[tt - 1:tt, :], aux_out[0, h, AUX_CUMEND:AUX_CUMEND + 1, :])
        aux_out[0, h, AUX_KMAX:AUX_KMAX + 1, :] = jnp.where(
            here, jnp.sqrt(kn2), aux_out[0, h, AUX_KMAX:AUX_KMAX + 1, :])


def _foxprep(proj, fb, bsz, seq):
    tt = ATT_BLOCK
    nt = seq // tt
    assert nt <= LANES
    fb_row = jnp.zeros((1, LANES), F32).at[0, SM_F:SM_F + ATTN_HEADS].set(fb)
    shp = jax.ShapeDtypeStruct((bsz, ATTN_HEADS, seq, LANES), BF16)
    ospec = pl.BlockSpec((1, ATTN_HEADS, tt, LANES), lambda b, t: (b, 0, t, 0))
    return pl.pallas_call(
        functools.partial(_foxprep_kernel, tt=tt),
        grid=(bsz, nt),
        in_specs=[
            pl.BlockSpec((tt, 3 * ATTN_WIDTH), lambda b, t: (b * nt + t, COL_QKV // (3 * ATTN_WIDTH))),
            pl.BlockSpec((tt, LANES), lambda b, t: (b * nt + t, COL_SM // LANES)),
            pl.BlockSpec((1, LANES), lambda b, t: (0, 0)),
        ],
        out_specs=[ospec, ospec, ospec,
                   pl.BlockSpec((1, ATTN_HEADS, 8, LANES), lambda b, t: (b, 0, 0, 0))],
        out_shape=[shp, shp, shp, jax.ShapeDtypeStruct((bsz, ATTN_HEADS, 8, LANES), F32)],
        scratch_shapes=[pltpu.VMEM((1, LANES), F32)],
        compiler_params=_cparams(("arbitrary", "arbitrary")),
        name="fox_prep",
    )(proj, proj, fb_row)


def _fox_kernel(q_ref, k_ref, v_ref, aux_ref, o_ref, *, tq):
    i = pl.program_id(2)
    nt = (((1,), (1,)), ((), ()))
    lane = _iota((tq, LANES), 1)
    lane1 = _iota((1, LANES), 1)
    qs = (q_ref[0, 0], q_ref[0, 1])

    def step(hh, r0, m, acc, mask):
        kblk = k_ref[0, hh, pl.ds(r0, tq), :]
        vblk = v_ref[0, hh, pl.ds(r0, tq), :]
        s = lax.dot_general(qs[hh], kblk, nt, preferred_element_type=F32)
        if mask is not None:
            s = jnp.where(mask, s, -jnp.inf)
        m_new = jnp.maximum(m, jnp.max(s, axis=-1, keepdims=True))
        p = jnp.exp2(s - m_new)
        alpha = jnp.exp2(m - m_new)
        acc = acc * alpha + jnp.dot(p.astype(BF16), vblk, preferred_element_type=F32)
        return m_new, acc

    d0 = pl.multiple_of(i * tq, tq)
    diag = _iota((tq, tq), 0) >= _iota((tq, tq), 1)
    m_init = jnp.full((tq, 1), -jnp.inf, F32)
    acc_init = jnp.zeros((tq, LANES), F32)
    i_f = i.astype(F32)
    carry = []
    first = []
    for hh in range(2):
        m, acc = step(hh, d0, m_init, acc_init, diag)
        carry += [m, acc]
        own = (lane < ATTN_HEAD_DIM) if hh == 0 else (lane >= ATTN_HEAD_DIM)
        a0 = ATTN_HEAD_DIM * (1 - hh)
        qf = qs[hh].astype(F32)
        qn = jnp.sqrt(jnp.sum(jnp.where(own, qf * qf, 0.0), axis=-1, keepdims=True))
        cum_t = jnp.sum(jnp.where((lane >= a0) & (lane < a0 + 3), qf, 0.0), axis=-1, keepdims=True)
        aux = aux_ref[0, hh]
        kmax = jnp.max(jnp.where(lane1 <= i, aux[AUX_KMAX:AUX_KMAX + 1, :], 0.0), axis=-1, keepdims=True)
        slack = jnp.max(qn * kmax + cum_t - m, axis=0, keepdims=True)
        live = (lane1 < i) & (slack - aux[AUX_CUMEND:AUX_CUMEND + 1, :] > -PRUNE_LOG2)
        first.append(jnp.min(jnp.where(live, lane1.astype(F32), i_f)))
    j_first = jnp.minimum(first[0], first[1]).astype(jnp.int32)

    def body(j, c):
        r0 = pl.multiple_of(j * tq, tq)
        m0, a0_, m1, a1_ = c
        m0, a0_ = step(0, r0, m0, a0_, None)
        m1, a1_ = step(1, r0, m1, a1_, None)
        return m0, a0_, m1, a1_

    _, acc0, _, acc1 = lax.fori_loop(j_first, i, body, tuple(carry))
    den0 = jnp.sum(jnp.where(lane == ATTN_HEAD_DIM, acc0, 0.0), axis=-1, keepdims=True)
    den1 = jnp.sum(jnp.where(lane == 0, acc1, 0.0), axis=-1, keepdims=True)
    o_ref[...] = jnp.where(lane < ATTN_HEAD_DIM, acc0 / den0, acc1 / den1)


def _fox(qa, ka, va, aux):
    bsz, nh, seq, _ = qa.shape
    tq = ATT_BLOCK
    nq = seq // tq
    kv_spec = pl.BlockSpec((1, 2, seq, LANES), lambda b, p, i: (b, p, 0, 0))
    return pl.pallas_call(
        functools.partial(_fox_kernel, tq=tq),
        grid=(bsz, nh // 2, nq),
        in_specs=[pl.BlockSpec((1, 2, tq, LANES), lambda b, p, i: (b, p, i, 0)), kv_spec, kv_spec,
                  pl.BlockSpec((1, 2, 8, LANES), lambda b, p, i: (b, p, 0, 0))],
        out_specs=pl.BlockSpec((tq, LANES), lambda b, p, i: (b * nq + i, p)),
        out_shape=jax.ShapeDtypeStruct((bsz * seq, ATTN_WIDTH), F32),
        compiler_params=_cparams(("arbitrary", "arbitrary", "arbitrary")),
        name="fox_attn",
    )(qa, ka, va, aux)


CONF_HALO = 32
CONF_ROWS = 64


def _conf_kernel(ga_ref, gb_ref, w_ref, b_ref, lg_ref, lb_ref, y_ref, upad, *, tt):
    t = pl.program_id(1)

    @pl.when(t == 0)
    def _():
        upad[0:CONF_HALO, :] = jnp.zeros((CONF_HALO, CONV_WIDTH), F32)

    upad[CONF_HALO:CONF_HALO + tt, :] = ga_ref[...] * _sigmoid(gb_ref[...])
    base = CONF_HALO - (CONV_KERNEL - 1)
    for r in range(tt // CONF_ROWS):
        r0 = r * CONF_ROWS
        acc = jnp.broadcast_to(b_ref[...], (CONF_ROWS, CONV_WIDTH))
        for k in range(CONV_KERNEL):
            acc = acc + w_ref[k:k + 1, :] * upad[r0 + base + k:r0 + base + k + CONF_ROWS, :]
        mu = jnp.mean(acc, axis=-1, keepdims=True)
        cen = acc - mu
        var = jnp.mean(cen * cen, axis=-1, keepdims=True)
        y = cen * lax.rsqrt(var + EPS) * lg_ref[...] + lb_ref[...]
        y_ref[r0:r0 + CONF_ROWS, :] = _silu(y)
    upad[0:CONF_HALO, :] = upad[tt:tt + CONF_HALO, :]


def _conformer(proj, w, b, lg, lb, bsz, seq, tt):
    m = bsz * seq
    nt = seq // tt
    full = lambda shape: pl.BlockSpec(shape, lambda bb, t: (0,) * len(shape))
    return pl.pallas_call(
        functools.partial(_conf_kernel, tt=tt),
        grid=(bsz, nt),
        in_specs=[
            pl.BlockSpec((tt, CONV_WIDTH), lambda bb, t: (bb * nt + t, COL_GA // CONV_WIDTH)),
            pl.BlockSpec((tt, CONV_WIDTH), lambda bb, t: (bb * nt + t, COL_GB // CONV_WIDTH)),
            full((CONV_KERNEL, CONV_WIDTH)),
            full((1, CONV_WIDTH)),
            full((1, CONV_WIDTH)),
            full((1, CONV_WIDTH)),
        ],
        out_specs=pl.BlockSpec((tt, CONV_WIDTH), lambda bb, t: (bb * nt + t, 0)),
        out_shape=jax.ShapeDtypeStruct((m, CONV_WIDTH), F32),
        scratch_shapes=[pltpu.VMEM((tt + CONF_HALO, CONV_WIDTH), F32)],
        compiler_params=_cparams(("arbitrary", "arbitrary")),
        name="conformer_conv",
    )(proj, proj, w, b.reshape(1, -1), lg.reshape(1, -1), lb.reshape(1, -1))


ROUTE_BIG = 1e9


def _split3(x):
    hi = x.astype(BF16)
    lo = (x - hi.astype(F32)).astype(BF16)
    return hi, lo


def _outproj_kernel(x_ref, ys_ref, oa_ref, yc_ref, wo_ref, fg_ref, gm_ref, a2_ref, s2_ref,
                    wrh_ref, wrl_ref, br_ref, xn_ref, h2_ref, rw_ref, re_ref):
    tm = x_ref.shape[0]
    lane = _iota((tm, LANES), 1)
    att = oa_ref[...]
    ms = jnp.mean(att * att, axis=-1, keepdims=True)
    att = att * lax.rsqrt(ms + EPS) * fg_ref[...]
    y = jnp.dot(ys_ref[...].astype(BF16), wo_ref[0:SSD_WIDTH, :], preferred_element_type=F32)
    y = y + jnp.dot(att.astype(BF16), wo_ref[SSD_WIDTH:SSD_WIDTH + ATTN_WIDTH, :],
                    preferred_element_type=F32)
    y = y + jnp.dot(yc_ref[...].astype(BF16), wo_ref[SSD_WIDTH + ATTN_WIDTH:, :],
                    preferred_element_type=F32)
    xn = x_ref[...] + gm_ref[0] * y
    xn_ref[...] = xn
    ms2 = jnp.mean(xn * xn, axis=-1, keepdims=True)
    h2 = xn * lax.rsqrt(ms2 + EPS) * a2_ref[0] + s2_ref[0]
    h2_ref[...] = h2

    hh, hl = _split3(h2)
    logits = (jnp.dot(hh, wrh_ref[...], preferred_element_type=F32)
              + jnp.dot(hl, wrh_ref[...], preferred_element_type=F32)
              + jnp.dot(hh, wrl_ref[...], preferred_element_type=F32)) + br_ref[...]
    lanef = lane.astype(F32)
    lg = jnp.where(lane < N_GROUPS, logits, -jnp.inf)
    gmax = jnp.max(lg, axis=-1, keepdims=True)
    gsum = jnp.sum(jnp.exp(lg - gmax), axis=-1, keepdims=True)
    gidx = jnp.min(jnp.where(lg == gmax, lanef, ROUTE_BIG), axis=-1, keepdims=True)
    e_lo = N_GROUPS + PER_GROUP * gidx
    le = jnp.where((lanef >= e_lo) & (lanef < e_lo + PER_GROUP), logits, -jnp.inf)
    m1 = jnp.max(le, axis=-1, keepdims=True)
    i1 = jnp.min(jnp.where(le == m1, lanef, ROUTE_BIG), axis=-1, keepdims=True)
    le2 = jnp.where(lanef == i1, -jnp.inf, le)
    m2 = jnp.max(le2, axis=-1, keepdims=True)
    i2 = jnp.min(jnp.where(le2 == m2, lanef, ROUTE_BIG), axis=-1, keepdims=True)
    esum = jnp.sum(jnp.exp(le - m1), axis=-1, keepdims=True)
    p1 = 1.0 / esum
    p2 = jnp.exp(m2 - m1) / esum
    psel = 1.0 / gsum
    w1 = p1 / (p1 + p2) * psel
    w2 = p2 / (p1 + p2) * psel
    rw_ref[...] = jnp.where(lane == 0, w1, jnp.where(lane == 1, w2, 0.0))
    re_ref[...] = jnp.where(lane == 0, i1 - N_GROUPS, jnp.where(lane == 1, i2 - N_GROUPS, 0.0))


def _outproj(x2, ys, oa, yc, wo, fg, gm, a2, s2, wrh, wrl, br, seq, tm):
    m = x2.shape[0]
    per_b = seq // tm
    row = lambda w: pl.BlockSpec((tm, w), lambda i: (i, 0))
    bvec = pl.BlockSpec((1, 1, D_MODEL), lambda i: (i // per_b, 0, 0))
    full = lambda shape: pl.BlockSpec(shape, lambda i: (0,) * len(shape))
    return pl.pallas_call(
        _outproj_kernel,
        grid=(m // tm,),
        in_specs=[
            row(D_MODEL), row(SSD_WIDTH), row(ATTN_WIDTH), row(CONV_WIDTH),
            full((D_MODEL, D_MODEL)), full((1, ATTN_WIDTH)), bvec, bvec, bvec,
            full((D_MODEL, LANES)), full((D_MODEL, LANES)), full((1, LANES)),
        ],
        out_specs=[row(D_MODEL), row(D_MODEL), row(LANES), row(LANES)],
        out_shape=[jax.ShapeDtypeStruct((m, D_MODEL), F32), jax.ShapeDtypeStruct((m, D_MODEL), F32),
                   jax.ShapeDtypeStruct((m, LANES), F32), jax.ShapeDtypeStruct((m, LANES), F32)],
        compiler_params=_cparams(("arbitrary",)),
        name="out_proj_router",
    )(x2, ys, oa, yc, wo, fg, gm, a2, s2, wrh, wrl, br)


def _rank_kernel(re_ref, pos_ref, cnt_ref, carry):
    i = pl.program_id(0)
    tr = re_ref.shape[0]

    @pl.when(i == 0)
    def _():
        carry[...] = jnp.zeros(carry.shape, F32)

    e = re_ref[...]
    lanef = _iota((tr, LANES), 1).astype(F32)
    oh0 = (jnp.broadcast_to(e[:, 0:1], (tr, LANES)) == lanef).astype(F32)
    oh1 = (jnp.broadcast_to(e[:, 1:2], (tr, LANES)) == lanef).astype(F32)
    oh = oh0 + oh1
    before = (_iota((tr, tr), 0) > _iota((tr, tr), 1)).astype(BF16)
    rank = jnp.dot(before, oh.astype(BF16), preferred_element_type=F32) + carry[...]
    pos0 = jnp.sum(oh0 * rank, axis=-1, keepdims=True)
    pos1 = jnp.sum(oh1 * rank, axis=-1, keepdims=True)
    lane = _iota((tr, LANES), 1)
    pos_ref[...] = jnp.where(lane == 0, pos0, jnp.where(lane == 1, pos1, 0.0))
    carry[...] = carry[...] + jnp.sum(oh, axis=0, keepdims=True)
    cnt_ref[...] = jnp.broadcast_to(carry[...], cnt_ref.shape)


def _rank(re, tr):
    m = re.shape[0]
    return pl.pallas_call(
        _rank_kernel,
        grid=(m // tr,),
        in_specs=[pl.BlockSpec((tr, LANES), lambda i: (i, 0))],
        out_specs=[pl.BlockSpec((tr, LANES), lambda i: (i, 0)),
                   pl.BlockSpec((8, LANES), lambda i: (0, 0))],
        out_shape=[jax.ShapeDtypeStruct((m, LANES), F32), jax.ShapeDtypeStruct((8, LANES), F32)],
        scratch_shapes=[pltpu.VMEM((1, LANES), F32)],
        compiler_params=_cparams(("arbitrary",)),
        name="expert_rank",
    )(re)


def _expert_kernel(be_ref, code_ref, nv_ref, nu_ref, h_hbm, wg_ref, wu_ref, wd_ref, y_hbm,
                   xbuf, ybuf, wgb, wub, wdb, gsem, ssem, *, n_tok):
    i = pl.program_id(0)
    slot = i % 2
    n_used = nu_ref[0]
    rows = MOE_BLOCK

    def issue_gather(blk, sl):
        def body(r, c):
            cd = code_ref[blk * rows + r]
            tok = jnp.minimum(jnp.where(cd >= n_tok, cd - n_tok, cd), n_tok - 1)
            pltpu.make_async_copy(h_hbm.at[pl.ds(tok, 1), :], xbuf.at[sl, pl.ds(r, 1), :],
                                  gsem.at[sl]).start()
            return c
        lax.fori_loop(0, rows, body, 0, unroll=8)

    def wait_gather(sl):
        pltpu.make_async_copy(h_hbm.at[pl.ds(0, rows), :], xbuf.at[sl], gsem.at[sl]).wait()

    def issue_scatter(blk, sl):
        def body(r, c):
            pltpu.make_async_copy(ybuf.at[sl, pl.ds(r, 1), :],
                                  y_hbm.at[pl.ds(code_ref[blk * rows + r], 1), :], ssem.at[sl]).start()
            return c
        lax.fori_loop(0, nv_ref[blk], body, 0)

    def wait_scatter(blk, sl):
        n = nv_ref[blk]
        for bit in range(rows.bit_length()):
            size = 1 << bit

            @pl.when((lax.shift_right_logical(n, bit) & 1) == 1)
            def _():
                pltpu.make_async_copy(ybuf.at[sl, pl.ds(0, size), :], y_hbm.at[pl.ds(0, size), :],
                                      ssem.at[sl]).wait()

    @pl.when(i == 0)
    def _():
        issue_gather(0, 0)

    @pl.when(i + 1 < n_used)
    def _():
        issue_gather(i + 1, 1 - slot)

    @pl.when(i < n_used)
    def _():
        wait_gather(slot)
        prev_e = be_ref[jnp.maximum(i - 1, 0)]

        @pl.when((i == 0) | (be_ref[i] != prev_e))
        def _():
            wgb[...] = wg_ref[0, 0].astype(BF16)
            wub[...] = wu_ref[0, 0].astype(BF16)
            wdb[...] = wd_ref[0, 0].astype(BF16)

        x = xbuf[slot].astype(BF16)
        hid = _silu(jnp.dot(x, wgb[...], preferred_element_type=F32)) * jnp.dot(
            x, wub[...], preferred_element_type=F32)
        y = jnp.dot(hid.astype(BF16), wdb[...], preferred_element_type=F32)

        @pl.when(i >= 2)
        def _():
            wait_scatter(i - 2, slot)

        ybuf[slot] = y
        issue_scatter(i, slot)

        @pl.when(i == n_used - 1)
        def _():
            @pl.when(i >= 1)
            def _():
                wait_scatter(i - 1, 1 - slot)
            wait_scatter(i, slot)


def _experts(block_e, code, n_valid, n_used, h2, wg, wu, wd, layer):
    m = h2.shape[0]
    n_blocks = block_e.shape[0]
    out_rows = 2 * m
    wspec = lambda shape: pl.BlockSpec((1, 1) + shape, lambda i, be, cd, nv, nu: (layer, be[i], 0, 0))
    grid_spec = pltpu.PrefetchScalarGridSpec(
        num_scalar_prefetch=4,
        grid=(n_blocks,),
        in_specs=[
            pl.BlockSpec(memory_space=pl.ANY),
            wspec((D_MODEL, D_EXPERT)), wspec((D_MODEL, D_EXPERT)), wspec((D_EXPERT, D_MODEL)),
        ],
        out_specs=pl.BlockSpec(memory_space=pl.ANY),
        scratch_shapes=[
            pltpu.VMEM((2, MOE_BLOCK, D_MODEL), F32),
            pltpu.VMEM((2, MOE_BLOCK, D_MODEL), F32),
            pltpu.VMEM((D_MODEL, D_EXPERT), BF16),
            pltpu.VMEM((D_MODEL, D_EXPERT), BF16),
            pltpu.VMEM((D_EXPERT, D_MODEL), BF16),
            pltpu.SemaphoreType.DMA((2,)),
            pltpu.SemaphoreType.DMA((2,)),
        ],
    )
    return pl.pallas_call(
        functools.partial(_expert_kernel, n_tok=m),
        grid_spec=grid_spec,
        out_shape=jax.ShapeDtypeStruct((out_rows, D_MODEL), F32),
        compiler_params=_cparams(("arbitrary",)),
        name="moe_experts",
    )(block_e, code, n_valid, n_used, h2, wg, wu, wd)


def _combine_kernel(x_ref, y0_ref, y1_ref, rw_ref, gf_ref, fg_ref, o_ref, *, final):
    w = rw_ref[...]
    moe = w[:, 0:1] * y0_ref[...] + w[:, 1:2] * y1_ref[...]
    x = x_ref[...] + gf_ref[0] * moe
    if final:
        ms = jnp.mean(x * x, axis=-1, keepdims=True)
        x = x * lax.rsqrt(ms + EPS) * fg_ref[...]
    o_ref[...] = x


def _combine(x2, yk, rw, gf, fg, seq, tm, final):
    m = x2.shape[0]
    per_b = seq // tm
    nt = m // tm
    return pl.pallas_call(
        functools.partial(_combine_kernel, final=final),
        grid=(nt,),
        in_specs=[
            pl.BlockSpec((tm, D_MODEL), lambda i: (i, 0)),
            pl.BlockSpec((tm, D_MODEL), lambda i: (i, 0)),
            pl.BlockSpec((tm, D_MODEL), lambda i: (nt + i, 0)),
            pl.BlockSpec((tm, LANES), lambda i: (i, 0)),
            pl.BlockSpec((1, 1, D_MODEL), lambda i: (i // per_b, 0, 0)),
            pl.BlockSpec((1, D_MODEL), lambda i: (0, 0)),
        ],
        out_specs=pl.BlockSpec((tm, D_MODEL), lambda i: (i, 0)),
        out_shape=jax.ShapeDtypeStruct((m, D_MODEL), F32),
        compiler_params=_cparams(("arbitrary",)),
        name="moe_combine",
    )(x2, yk, yk, rw, gf, fg)


def _pack_w_in(w):
    d_dt = SSD_WIDTH + XBC_WIDTH
    d_q = d_dt + SSD_HEADS
    d_f = d_q + 3 * ATTN_WIDTH
    d_ga = d_f + ATTN_HEADS
    small = jnp.concatenate([w[:, d_dt:d_q], w[:, d_f:d_ga],
                             jnp.zeros((D_MODEL, LANES - SSD_HEADS - ATTN_HEADS), w.dtype)], axis=1)
    return jnp.concatenate([w[:, SSD_WIDTH:d_dt], w[:, :SSD_WIDTH], w[:, d_q:d_f], w[:, d_ga:], small],
                           axis=1).astype(BF16)


def kernel(x, c, ada_w, ada_b, norm_mix_g, w_in, ssd_conv_w, ssd_conv_b, ssd_dt_bias, ssd_a_log,
           ssd_d, ssd_norm_g, fox_f_bias, fox_norm_g, cm_conv_w, cm_conv_b, cm_ln_g, cm_ln_b, w_out,
           norm_ffn_g, w_router_group, b_router_group, w_router_expert, b_router_expert, w_gate,
           w_up, w_down, final_norm_g):
    bsz, seq, d = x.shape
    m = bsz * seq
    tm = min(512, seq)
    tt = min(256, seq)
    n_assign = 2 * m
    n_blocks = (n_assign + MOE_BLOCK - 1) // MOE_BLOCK + N_EXPERTS

    mod = _modulation(c, ada_w, ada_b)
    x2 = x.reshape(m, d)
    for l in range(DEPTH):
        sh_m, sc_m, g_m, sh_f, sc_f, g_f = [v.reshape(bsz, 1, d) for v in jnp.split(mod[l], 6, axis=-1)]
        a_m = norm_mix_g[l][None, None, :] * (1.0 + sc_m)
        proj = _inproj(x2, a_m, sh_m, _pack_w_in(w_in[l]), seq, tm)
        dtt = proj[:, COL_SM + SM_DT:COL_SM + SM_DT + SSD_HEADS].reshape(bsz, seq, SSD_HEADS)
        dtt = dtt.transpose(0, 2, 1)
        y_ssd = _ssd(proj, dtt, ssd_conv_w[l], ssd_conv_b[l], ssd_dt_bias[l], ssd_a_log[l],
                     ssd_d[l], ssd_norm_g[l], bsz, seq, tt)
        qa, ka, va, aux = _foxprep(proj, fox_f_bias[l], bsz, seq)
        o_att = _fox(qa, ka, va, aux)
        y_cnv = _conformer(proj, cm_conv_w[l], cm_conv_b[l], cm_ln_g[l], cm_ln_b[l], bsz, seq, tt)

        w_r = jnp.concatenate([w_router_group[l], w_router_expert[l],
                               jnp.zeros((d, LANES - N_GROUPS - N_EXPERTS), F32)], axis=1)
        w_rh = w_r.astype(BF16)
        w_rl = (w_r - w_rh.astype(F32)).astype(BF16)
        b_r = jnp.concatenate([b_router_group[l], b_router_expert[l],
                               jnp.zeros((LANES - N_GROUPS - N_EXPERTS,), F32)]).reshape(1, LANES)
        a_f = norm_ffn_g[l][None, None, :] * (1.0 + sc_f)
        x2, h2, rw, re = _outproj(x2, y_ssd, o_att, y_cnv, w_out[l].astype(BF16),
                                  fox_norm_g[l].reshape(1, -1), g_m, a_f, sh_f, w_rh, w_rl, b_r,
                                  seq, tm)

        pos, cnt = _rank(re, tm)
        counts = cnt[0, :N_EXPERTS].astype(jnp.int32)
        padded = (counts + MOE_BLOCK - 1) // MOE_BLOCK * MOE_BLOCK
        pad_end = jnp.cumsum(padded)
        pad_start = pad_end - padded
        e_idx = re[:, :2].astype(jnp.int32)
        dest = pad_start[e_idx] + pos[:, :2].astype(jnp.int32)
        slot_code = jnp.arange(m, dtype=jnp.int32)[:, None] + m * jnp.arange(2, dtype=jnp.int32)[None, :]
        code = jnp.full((n_blocks * MOE_BLOCK,), 2 * m, jnp.int32).at[dest.reshape(-1)].set(
            slot_code.reshape(-1))
        blk_start = jnp.arange(n_blocks, dtype=jnp.int32) * MOE_BLOCK
        block_e = jnp.minimum(jnp.sum(pad_end[None, :] <= blk_start[:, None], axis=1),
                              N_EXPERTS - 1).astype(jnp.int32)
        n_valid = jnp.clip((pad_start + counts)[block_e] - blk_start, 0, MOE_BLOCK).astype(jnp.int32)
        n_used = (pad_end[-1] // MOE_BLOCK).astype(jnp.int32).reshape(1)
        yk = _experts(block_e, code, n_valid, n_used, h2, w_gate, w_up, w_down, l)
        x2 = _combine(x2, yk, rw, g_f, final_norm_g.reshape(1, d), seq, tm, final=(l == DEPTH - 1))
    return x2.reshape(bsz, seq, d)
```

```python
import functools

import jax
import jax.numpy as jnp
import numpy as np
from jax import lax
from jax.experimental import pallas as pl
from jax.experimental.pallas import tpu as pltpu

F32 = jnp.float32
BF16 = jnp.bfloat16
HIGHEST = lax.Precision.HIGHEST

D_MODEL = 1024
DEPTH = 4
SSD_WIDTH = 512
SSD_HEADS = 8
SSD_HEAD_DIM = 64
SSD_STATE = 128
SSD_CONV = 4
SSD_CHUNK = 128
XBC_WIDTH = 1024
ATTN_WIDTH = 256
ATTN_HEADS = 4
ATTN_HEAD_DIM = 64
CONV_WIDTH = 256
CONV_KERNEL = 31
N_GROUPS = 4
PER_GROUP = 8
N_EXPERTS = 32
D_EXPERT = 512
MOE_BLOCK = 256
EPS = 1e-6

LANES = 128
COL_XBC = 0
COL_Z = 1024
COL_QKV = 1536
COL_GA = 2304
COL_GB = 2560
COL_SM = 2816
NP = 2944
SM_DT = 0
SM_F = 8

VMEM_LIMIT = 56 * 1024 * 1024


def _cparams(sem):
    return pltpu.CompilerParams(dimension_semantics=sem, vmem_limit_bytes=VMEM_LIMIT)


def _sigmoid(x):
    return 1.0 / (1.0 + jnp.exp(-x))


def _silu(x):
    return x * _sigmoid(x)


def _softplus(x):
    return jnp.maximum(x, 0.0) + jnp.log1p(jnp.exp(-jnp.abs(x)))


def _iota(shape, dim):
    return lax.broadcasted_iota(jnp.int32, shape, dim)


def _mod_kernel(c_ref, w_ref, b_ref, o_ref):
    cond = _silu(c_ref[...])
    o_ref[0] = jnp.dot(cond, w_ref[0], precision=HIGHEST, preferred_element_type=F32) + b_ref[0]


def _modulation(c, ada_w, ada_b):
    bsz = c.shape[0]
    rows = 8
    cpad = jnp.zeros((rows, D_MODEL), F32).at[:bsz].set(c)
    tn = 1536
    n6 = 6 * D_MODEL
    out = pl.pallas_call(
        _mod_kernel,
        grid=(DEPTH, n6 // tn),
        in_specs=[
            pl.BlockSpec((rows, D_MODEL), lambda l, j: (0, 0)),
            pl.BlockSpec((1, D_MODEL, tn), lambda l, j: (l, 0, j)),
            pl.BlockSpec((1, 1, tn), lambda l, j: (l, 0, j)),
        ],
        out_specs=pl.BlockSpec((1, rows, tn), lambda l, j: (l, 0, j)),
        out_shape=jax.ShapeDtypeStruct((DEPTH, rows, n6), F32),
        compiler_params=_cparams(("arbitrary", "arbitrary")),
        name="adaln_mod",
    )(cpad, ada_w, ada_b.reshape(DEPTH, 1, n6))
    return out[:, :bsz]


def _inproj_kernel(x_ref, a_ref, s_ref, w_ref, o_ref):
    x = x_ref[...]
    ms = jnp.mean(x * x, axis=-1, keepdims=True)
    h = x * lax.rsqrt(ms + EPS) * a_ref[0] + s_ref[0]
    o_ref[...] = jnp.dot(h.astype(BF16), w_ref[...], preferred_element_type=F32)


def _inproj(x2, a, s, w_pack, seq, tm):
    m = x2.shape[0]
    per_b = seq // tm
    return pl.pallas_call(
        _inproj_kernel,
        grid=(m // tm,),
        in_specs=[
            pl.BlockSpec((tm, D_MODEL), lambda i: (i, 0)),
            pl.BlockSpec((1, 1, D_MODEL), lambda i: (i // per_b, 0, 0)),
            pl.BlockSpec((1, 1, D_MODEL), lambda i: (i // per_b, 0, 0)),
            pl.BlockSpec((D_MODEL, NP), lambda i: (0, 0)),
        ],
        out_specs=pl.BlockSpec((tm, NP), lambda i: (i, 0)),
        out_shape=jax.ShapeDtypeStruct((m, NP), F32),
        compiler_params=_cparams(("arbitrary",)),
        name="in_proj",
    )(x2, a, s, w_pack)


def _ssd_kernel(z_ref, xbc_ref, sm_ref, dtt_ref, cw_ref, cb_ref, dtb_ref, alog_ref, dtbt_ref,
                alogt_ref, e_ref, dx_ref, ng_ref, y_ref, xpad, xc, prev, *, tt):
    t = pl.program_id(1)

    @pl.when(t == 0)
    def _():
        xpad[0:8, :] = jnp.zeros((8, XBC_WIDTH), F32)
        prev[...] = jnp.zeros(prev.shape, F32)

    xpad[8:8 + tt, :] = xbc_ref[...]
    acc = jnp.broadcast_to(cb_ref[...], (tt, XBC_WIDTH))
    for k in range(SSD_CONV):
        off = 8 - (SSD_CONV - 1) + k
        acc = acc + cw_ref[k:k + 1, :] * xpad[off:off + tt, :]
    xc[...] = _silu(acc)
    xpad[0:8, :] = xpad[tt:tt + 8, :]

    cl = SSD_CHUNK
    row = _iota((cl, cl), 0)
    col = _iota((cl, cl), 1)
    causal = row >= col
    tril = causal.astype(F32)
    triu = (row <= col).astype(F32)
    lo = col < SSD_HEAD_DIM
    lane1 = _iota((1, LANES), 1)
    a_row = jnp.where(lane1 < SSD_HEADS, -jnp.exp(alog_ref[...]), 0.0)
    a_col = -jnp.exp(alogt_ref[...])
    expand = e_ref[...]

    def chunk(c, carry):
        r0 = pl.multiple_of(c * cl, cl)
        xs = xc[pl.ds(r0, cl), 0:SSD_WIDTH]
        bmat = xc[pl.ds(r0, cl), SSD_WIDTH:SSD_WIDTH + 2 * SSD_STATE]
        cmat = xc[pl.ds(r0, cl), SSD_WIDTH + 2 * SSD_STATE:XBC_WIDTH]
        dt = _softplus(sm_ref[pl.ds(r0, cl), :] + dtb_ref[...])
        da = dt * a_row
        acs = jnp.dot(tril, da, precision=HIGHEST, preferred_element_type=F32)
        dtt = _softplus(dtt_ref[0, :, pl.ds(r0, cl)] + dtbt_ref[...])
        acst = jnp.dot(dtt * a_col, triu, precision=HIGHEST, preferred_element_type=F32)
        dt_x = jnp.dot(dt, expand, precision=HIGHEST, preferred_element_type=F32)
        acs_x = jnp.dot(acs, expand, precision=HIGHEST, preferred_element_type=F32)
        last = acs_x[cl - 1:cl, :]
        eacs_x = jnp.exp(acs_x)
        dte_x = jnp.exp(last - acs_x)
        cd_x = jnp.exp(last)
        xdt = xs * dt_x
        xdte = (xdt * dte_x).astype(BF16)
        zc = z_ref[pl.ds(r0, cl), :]
        for g in range(2):
            bg = bmat[:, g * SSD_STATE:(g + 1) * SSD_STATE]
            cg = cmat[:, g * SSD_STATE:(g + 1) * SSD_STATE].astype(BF16)
            bgt = bg.T.astype(BF16)
            cbm = jnp.dot(cg, bgt, preferred_element_type=F32)
            pair_out = []
            for j in range(2):
                p = 2 * g + j
                sl = slice(p * LANES, (p + 1) * LANES)
                xp = xdt[:, sl]
                yd = jnp.zeros((cl, LANES), F32)
                for half in range(2):
                    h = 2 * p + half
                    seg = acs[:, h:h + 1] - acst[h:h + 1, :]
                    dec = jnp.exp(jnp.where(causal, seg, -jnp.inf))
                    gm = (cbm * dec).astype(BF16)
                    own = lo if half == 0 else jnp.logical_not(lo)
                    xm = jnp.where(own, xp, 0.0).astype(BF16)
                    yd = yd + jnp.dot(gm, xm, preferred_element_type=F32)
                prev_p = prev[:, sl]
                yo = jnp.dot(cg, prev_p.astype(BF16), preferred_element_type=F32) * eacs_x[:, sl]
                st = jnp.dot(bgt, xdte[:, sl], preferred_element_type=F32)
                prev[:, sl] = prev_p * cd_x[:, sl] + st
                pair_out.append(yd + yo + xs[:, sl] * dx_ref[:, sl])
            gs = slice(g * 2 * LANES, (g + 1) * 2 * LANES)
            yg = jnp.concatenate(pair_out, axis=-1) * _silu(zc[:, gs])
            ms = jnp.mean(yg * yg, axis=-1, keepdims=True)
            y_ref[pl.ds(r0, cl), gs] = yg * lax.rsqrt(ms + EPS) * ng_ref[:, gs]
        return carry

    lax.fori_loop(0, tt // cl, chunk, 0)


def _ssd(proj, dtt, cw, cb, dtb, alog, dx, ng, bsz, seq, tt):
    m = bsz * seq
    nt = seq // tt
    pad = LANES - SSD_HEADS
    dtb_row = jnp.pad(dtb, (0, pad)).reshape(1, LANES)
    alog_row = jnp.pad(alog, (0, pad)).reshape(1, LANES)
    dtb_col = jnp.broadcast_to(dtb[:, None], (SSD_HEADS, SSD_CHUNK))
    alog_col = jnp.broadcast_to(alog[:, None], (SSD_HEADS, SSD_CHUNK))
    expand = np.zeros((LANES, SSD_WIDTH), np.float32)
    for h in range(SSD_HEADS):
        expand[h, h * SSD_HEAD_DIM:(h + 1) * SSD_HEAD_DIM] = 1.0
    dx_row = jnp.repeat(dx, SSD_HEAD_DIM).reshape(1, SSD_WIDTH)
    full = lambda shape: pl.BlockSpec(shape, lambda b, t: (0,) * len(shape))
    return pl.pallas_call(
        functools.partial(_ssd_kernel, tt=tt),
        grid=(bsz, nt),
        in_specs=[
            pl.BlockSpec((tt, SSD_WIDTH), lambda b, t: (b * nt + t, COL_Z // SSD_WIDTH)),
            pl.BlockSpec((tt, XBC_WIDTH), lambda b, t: (b * nt + t, COL_XBC // XBC_WIDTH + 0)),
            pl.BlockSpec((tt, LANES), lambda b, t: (b * nt + t, COL_SM // LANES)),
            pl.BlockSpec((1, SSD_HEADS, tt), lambda b, t: (b, 0, t)),
            full((SSD_CONV, XBC_WIDTH)),
            full((1, XBC_WIDTH)),
            full((1, LANES)),
            full((1, LANES)),
            full((SSD_HEADS, SSD_CHUNK)),
            full((SSD_HEADS, SSD_CHUNK)),
            full((LANES, SSD_WIDTH)),
            full((1, SSD_WIDTH)),
            full((1, SSD_WIDTH)),
        ],
        out_specs=pl.BlockSpec((tt, SSD_WIDTH), lambda b, t: (b * nt + t, 0)),
        out_shape=jax.ShapeDtypeStruct((m, SSD_WIDTH), F32),
        scratch_shapes=[
            pltpu.VMEM((tt + 8, XBC_WIDTH), F32),
            pltpu.VMEM((tt, XBC_WIDTH), F32),
            pltpu.VMEM((SSD_STATE, SSD_WIDTH), F32),
        ],
        compiler_params=_cparams(("arbitrary", "arbitrary")),
        name="ssd_scan",
    )(proj, proj, proj, dtt, cw, cb.reshape(1, XBC_WIDTH), dtb_row, alog_row, dtb_col, alog_col,
      jnp.asarray(expand), dx_row, ng.reshape(1, SSD_WIDTH))


LOG2E = 1.4426950408889634
ATT_BLOCK = 256
PRUNE_LOG2 = 140.0
AUX_CUMEND = 0
AUX_KMAX = 1


def _foxprep_kernel(qkv_ref, sm_ref, fb_ref, q_out, k_out, v_out, aux_out, carry, *, tt):
    t = pl.program_id(1)

    @pl.when(t == 0)
    def _():
        carry[...] = jnp.zeros(carry.shape, F32)
        aux_out[...] = jnp.zeros(aux_out.shape, F32)

    logit = sm_ref[...] + fb_ref[...]
    logf = -_softplus(-logit)
    row = _iota((tt, tt), 0)
    col = _iota((tt, tt), 1)
    tril = (row >= col).astype(F32)
    cum = jnp.dot(tril, logf, precision=HIGHEST, preferred_element_type=F32) + carry[...]
    carry[...] = cum[tt - 1:tt, :]

    lane = _iota((tt, LANES), 1)
    lane1 = _iota((1, LANES), 1)
    scale = ATTN_HEAD_DIM ** -0.5 * LOG2E
    for h in range(ATTN_HEADS):
        pair, half = h // 2, h % 2
        own = (lane < ATTN_HEAD_DIM) if half == 0 else (lane >= ATTN_HEAD_DIM)
        a0 = ATTN_HEAD_DIM * (1 - half)
        cs = jnp.broadcast_to(cum[:, SM_F + h:SM_F + h + 1], (tt, LANES)) * LOG2E
        hi = cs.astype(BF16).astype(F32)
        r1 = cs - hi
        mid = r1.astype(BF16).astype(F32)
        low = r1 - mid
        qp = qkv_ref[:, pair * LANES:(pair + 1) * LANES]
        kp = qkv_ref[:, ATTN_WIDTH + pair * LANES:ATTN_WIDTH + (pair + 1) * LANES]
        vp = qkv_ref[:, 2 * ATTN_WIDTH + pair * LANES:2 * ATTN_WIDTH + (pair + 1) * LANES]
        qa = jnp.where(lane == a0, hi, jnp.where(lane == a0 + 1, mid, jnp.where(
            lane == a0 + 2, low, jnp.where((lane >= a0 + 3) & (lane < a0 + 6), 1.0, 0.0))))
        ka = jnp.where(lane == a0 + 3, -hi, jnp.where(lane == a0 + 4, -mid, jnp.where(
            lane == a0 + 5, -low, jnp.where((lane >= a0) & (lane < a0 + 3), 1.0, 0.0))))
        q_out[0, h] = jnp.where(own, qp * scale, qa).astype(BF16)
        kb = jnp.where(own, kp, ka).astype(BF16)
        k_out[0, h] = kb
        v_out[0, h] = jnp.where(own, vp, jnp.where(lane == a0, 1.0, 0.0)).astype(BF16)
        kf = jnp.where(own, kb.astype(F32), 0.0)
        kn2 = jnp.max(jnp.sum(kf * kf, axis=-1, keepdims=True), axis=0, keepdims=True)
        here = lane1 == t
        aux_out[0, h, AUX_CUMEND:AUX_CUMEND + 1, :] = jnp.where(
            here, cs[tt - 1:tt, :], aux_out[0, h, AUX_CUMEND:AUX_CUMEND + 1, :])
        aux_out[0, h, AUX_KMAX:AUX_KMAX + 1, :] = jnp.where(
            here, jnp.sqrt(kn2), aux_out[0, h, AUX_KMAX:AUX_KMAX + 1, :])


def _foxprep(proj, fb, bsz, seq):
    tt = ATT_BLOCK
    nt = seq // tt
    assert nt <= LANES
    fb_row = jnp.zeros((1, LANES), F32).at[0, SM_F:SM_F + ATTN_HEADS].set(fb)
    shp = jax.ShapeDtypeStruct((bsz, ATTN_HEADS, seq, LANES), BF16)
    ospec = pl.BlockSpec((1, ATTN_HEADS, tt, LANES), lambda b, t: (b, 0, t, 0))
    return pl.pallas_call(
        functools.partial(_foxprep_kernel, tt=tt),
        grid=(bsz, nt),
        in_specs=[
            pl.BlockSpec((tt, 3 * ATTN_WIDTH), lambda b, t: (b * nt + t, COL_QKV // (3 * ATTN_WIDTH))),
            pl.BlockSpec((tt, LANES), lambda b, t: (b * nt + t, COL_SM // LANES)),
            pl.BlockSpec((1, LANES), lambda b, t: (0, 0)),
        ],
        out_specs=[ospec, ospec, ospec,
                   pl.BlockSpec((1, ATTN_HEADS, 8, LANES), lambda b, t: (b, 0, 0, 0))],
        out_shape=[shp, shp, shp, jax.ShapeDtypeStruct((bsz, ATTN_HEADS, 8, LANES), F32)],
        scratch_shapes=[pltpu.VMEM((1, LANES), F32)],
        compiler_params=_cparams(("arbitrary", "arbitrary")),
        name="fox_prep",
    )(proj, proj, fb_row)


def _fox_kernel(q_ref, k_ref, v_ref, aux_ref, o_ref, *, tq):
    i = pl.program_id(2)
    nt = (((1,), (1,)), ((), ()))
    lane = _iota((tq, LANES), 1)
    lane1 = _iota((1, LANES), 1)
    qs = (q_ref[0, 0], q_ref[0, 1])

    def step(hh, r0, m, acc, mask):
        kblk = k_ref[0, hh, pl.ds(r0, tq), :]
        vblk = v_ref[0, hh, pl.ds(r0, tq), :]
        s = lax.dot_general(qs[hh], kblk, nt, preferred_element_type=F32)
        if mask is not None:
            s = jnp.where(mask, s, -jnp.inf)
        m_new = jnp.maximum(m, jnp.max(s, axis=-1, keepdims=True))
        p = jnp.exp2(s - m_new)
        alpha = jnp.exp2(m - m_new)
        acc = acc * alpha + jnp.dot(p.astype(BF16), vblk, preferred_element_type=F32)
        return m_new, acc

    d0 = pl.multiple_of(i * tq, tq)
    diag = _iota((tq, tq), 0) >= _iota((tq, tq), 1)
    m_init = jnp.full((tq, 1), -jnp.inf, F32)
    acc_init = jnp.zeros((tq, LANES), F32)
    i_f = i.astype(F32)
    carry = []
    first = []
    for hh in range(2):
        m, acc = step(hh, d0, m_init, acc_init, diag)
        carry += [m, acc]
        own = (lane < ATTN_HEAD_DIM) if hh == 0 else (lane >= ATTN_HEAD_DIM)
        a0 = ATTN_HEAD_DIM * (1 - hh)
        qf = qs[hh].astype(F32)
        qn = jnp.sqrt(jnp.sum(jnp.where(own, qf * qf, 0.0), axis=-1, keepdims=True))
        cum_t = jnp.sum(jnp.where((lane >= a0) & (lane < a0 + 3), qf, 0.0), axis=-1, keepdims=True)
        aux = aux_ref[0, hh]
        kmax = jnp.max(jnp.where(lane1 <= i, aux[AUX_KMAX:AUX_KMAX + 1, :], 0.0), axis=-1, keepdims=True)
        slack = jnp.max(qn * kmax + cum_t - m, axis=0, keepdims=True)
        live = (lane1 < i) & (slack - aux[AUX_CUMEND:AUX_CUMEND + 1, :] > -PRUNE_LOG2)
        first.append(jnp.min(jnp.where(live, lane1.astype(F32), i_f)))
    j_first = jnp.minimum(first[0], first[1]).astype(jnp.int32)

    def body(j, c):
        r0 = pl.multiple_of(j * tq, tq)
        m0, a0_, m1, a1_ = c
        m0, a0_ = step(0, r0, m0, a0_, None)
        m1, a1_ = step(1, r0, m1, a1_, None)
        return m0, a0_, m1, a1_

    _, acc0, _, acc1 = lax.fori_loop(j_first, i, body, tuple(carry))
    den0 = jnp.sum(jnp.where(lane == ATTN_HEAD_DIM, acc0, 0.0), axis=-1, keepdims=True)
    den1 = jnp.sum(jnp.where(lane == 0, acc1, 0.0), axis=-1, keepdims=True)
    o_ref[...] = jnp.where(lane < ATTN_HEAD_DIM, acc0 / den0, acc1 / den1)


def _fox(qa, ka, va, aux):
    bsz, nh, seq, _ = qa.shape
    tq = ATT_BLOCK
    nq = seq // tq
    kv_spec = pl.BlockSpec((1, 2, seq, LANES), lambda b, p, i: (b, p, 0, 0))
    return pl.pallas_call(
        functools.partial(_fox_kernel, tq=tq),
        grid=(bsz, nh // 2, nq),
        in_specs=[pl.BlockSpec((1, 2, tq, LANES), lambda b, p, i: (b, p, i, 0)), kv_spec, kv_spec,
                  pl.BlockSpec((1, 2, 8, LANES), lambda b, p, i: (b, p, 0, 0))],
        out_specs=pl.BlockSpec((tq, LANES), lambda b, p, i: (b * nq + i, p)),
        out_shape=jax.ShapeDtypeStruct((bsz * seq, ATTN_WIDTH), F32),
        compiler_params=_cparams(("arbitrary", "arbitrary", "arbitrary")),
        name="fox_attn",
    )(qa, ka, va, aux)


CONF_HALO = 32
CONF_ROWS = 64


def _conf_kernel(ga_ref, gb_ref, w_ref, b_ref, lg_ref, lb_ref, y_ref, upad, *, tt):
    t = pl.program_id(1)

    @pl.when(t == 0)
    def _():
        upad[0:CONF_HALO, :] = jnp.zeros((CONF_HALO, CONV_WIDTH), F32)

    upad[CONF_HALO:CONF_HALO + tt, :] = ga_ref[...] * _sigmoid(gb_ref[...])
    base = CONF_HALO - (CONV_KERNEL - 1)
    for r in range(tt // CONF_ROWS):
        r0 = r * CONF_ROWS
        acc = jnp.broadcast_to(b_ref[...], (CONF_ROWS, CONV_WIDTH))
        for k in range(CONV_KERNEL):
            acc = acc + w_ref[k:k + 1, :] * upad[r0 + base + k:r0 + base + k + CONF_ROWS, :]
        mu = jnp.mean(acc, axis=-1, keepdims=True)
        cen = acc - mu
        var = jnp.mean(cen * cen, axis=-1, keepdims=True)
        y = cen * lax.rsqrt(var + EPS) * lg_ref[...] + lb_ref[...]
        y_ref[r0:r0 + CONF_ROWS, :] = _silu(y)
    upad[0:CONF_HALO, :] = upad[tt:tt + CONF_HALO, :]


def _conformer(proj, w, b, lg, lb, bsz, seq, tt):
    m = bsz * seq
    nt = seq // tt
    full = lambda shape: pl.BlockSpec(shape, lambda bb, t: (0,) * len(shape))
    return pl.pallas_call(
        functools.partial(_conf_kernel, tt=tt),
        grid=(bsz, nt),
        in_specs=[
            pl.BlockSpec((tt, CONV_WIDTH), lambda bb, t: (bb * nt + t, COL_GA // CONV_WIDTH)),
            pl.BlockSpec((tt, CONV_WIDTH), lambda bb, t: (bb * nt + t, COL_GB // CONV_WIDTH)),
            full((CONV_KERNEL, CONV_WIDTH)),
            full((1, CONV_WIDTH)),
            full((1, CONV_WIDTH)),
            full((1, CONV_WIDTH)),
        ],
        out_specs=pl.BlockSpec((tt, CONV_WIDTH), lambda bb, t: (bb * nt + t, 0)),
        out_shape=jax.ShapeDtypeStruct((m, CONV_WIDTH), F32),
        scratch_shapes=[pltpu.VMEM((tt + CONF_HALO, CONV_WIDTH), F32)],
        compiler_params=_cparams(("arbitrary", "arbitrary")),
        name="conformer_conv",
    )(proj, proj, w, b.reshape(1, -1), lg.reshape(1, -1), lb.reshape(1, -1))


ROUTE_BIG = 1e9
CHUNK = 8
RT_W1, RT_W2, RT_ROW1, RT_ROW2 = 0, 1, 2, 3


def _split3(x):
    hi = x.astype(BF16)
    lo = (x - hi.astype(F32)).astype(BF16)
    return hi, lo


def _outproj_kernel(x_ref, ys_ref, oa_ref, yc_ref, wo_ref, fg_ref, gm_ref, a2_ref, s2_ref,
                    wrh_ref, wrl_ref, br_ref, xn_ref, h2_ref, rt_ref, rtt_ref, cnt_ref):
    tm = x_ref.shape[0]
    lane = _iota((tm, LANES), 1)
    att = oa_ref[...]
    ms = jnp.mean(att * att, axis=-1, keepdims=True)
    att = att * lax.rsqrt(ms + EPS) * fg_ref[...]
    y = jnp.dot(ys_ref[...].astype(BF16), wo_ref[0:SSD_WIDTH, :], preferred_element_type=F32)
    y = y + jnp.dot(att.astype(BF16), wo_ref[SSD_WIDTH:SSD_WIDTH + ATTN_WIDTH, :],
                    preferred_element_type=F32)
    y = y + jnp.dot(yc_ref[...].astype(BF16), wo_ref[SSD_WIDTH + ATTN_WIDTH:, :],
                    preferred_element_type=F32)
    xn = x_ref[...] + gm_ref[0] * y
    xn_ref[...] = xn
    ms2 = jnp.mean(xn * xn, axis=-1, keepdims=True)
    h2 = xn * lax.rsqrt(ms2 + EPS) * a2_ref[0] + s2_ref[0]
    h2_ref[...] = h2.astype(BF16)

    hh, hl = _split3(h2)
    logits = (jnp.dot(hh, wrh_ref[...], preferred_element_type=F32)
              + jnp.dot(hl, wrh_ref[...], preferred_element_type=F32)
              + jnp.dot(hh, wrl_ref[...], preferred_element_type=F32)) + br_ref[...]
    lanef = lane.astype(F32)
    lg = jnp.where(lane < N_GROUPS, logits, -jnp.inf)
    gmax = jnp.max(lg, axis=-1, keepdims=True)
    gsum = jnp.sum(jnp.exp(lg - gmax), axis=-1, keepdims=True)
    gidx = jnp.min(jnp.where(lg == gmax, lanef, ROUTE_BIG), axis=-1, keepdims=True)
    e_lo = N_GROUPS + PER_GROUP * gidx
    le = jnp.where((lanef >= e_lo) & (lanef < e_lo + PER_GROUP), logits, -jnp.inf)
    m1 = jnp.max(le, axis=-1, keepdims=True)
    i1 = jnp.min(jnp.where(le == m1, lanef, ROUTE_BIG), axis=-1, keepdims=True)
    le2 = jnp.where(lanef == i1, -jnp.inf, le)
    m2 = jnp.max(le2, axis=-1, keepdims=True)
    i2 = jnp.min(jnp.where(le2 == m2, lanef, ROUTE_BIG), axis=-1, keepdims=True)
    esum = jnp.sum(jnp.exp(le - m1), axis=-1, keepdims=True)
    p1 = 1.0 / esum
    p2 = jnp.exp(m2 - m1) / esum
    psel = 1.0 / gsum
    w1 = p1 / (p1 + p2) * psel
    w2 = p2 / (p1 + p2) * psel

    oh1 = (lanef == (i1 - N_GROUPS)).astype(F32)
    oh2 = (lanef == (i2 - N_GROUPS)).astype(F32)
    oh = oh1 + oh2
    cnt = jnp.sum(oh, axis=0, keepdims=True)
    before = (_iota((tm, tm), 0) > _iota((tm, tm), 1)).astype(BF16)
    rank = jnp.dot(before, oh.astype(BF16), preferred_element_type=F32)
    chunks = jnp.floor((cnt + (CHUNK - 1)) * (1.0 / CHUNK))
    below = (_iota((LANES, LANES), 0) < _iota((LANES, LANES), 1)).astype(BF16)
    seg_lo = CHUNK * jnp.dot(jnp.broadcast_to(chunks, (8, LANES)).astype(BF16), below,
                             preferred_element_type=F32)[0:1, :]
    lr1 = jnp.sum(oh1 * (rank + seg_lo), axis=-1, keepdims=True)
    lr2 = jnp.sum(oh2 * (rank + seg_lo), axis=-1, keepdims=True)
    route = jnp.where(lane == RT_W1, w1, jnp.where(lane == RT_W2, w2, jnp.where(
        lane == RT_ROW1, lr1, jnp.where(lane == RT_ROW2, lr2, 0.0))))
    rt_ref[...] = route
    rtt_ref[0] = route.T[0:8, :]
    cnt_ref[0] = jnp.broadcast_to(cnt, (8, LANES))


def _outproj(x2, ys, oa, yc, wo, fg, gm, a2, s2, wrh, wrl, br, seq, tm):
    m = x2.shape[0]
    per_b = seq // tm
    nt = m // tm
    row = lambda w: pl.BlockSpec((tm, w), lambda i: (i, 0))
    bvec = pl.BlockSpec((1, 1, D_MODEL), lambda i: (i // per_b, 0, 0))
    full = lambda shape: pl.BlockSpec(shape, lambda i: (0,) * len(shape))
    return pl.pallas_call(
        _outproj_kernel,
        grid=(nt,),
        in_specs=[
            row(D_MODEL), row(SSD_WIDTH), row(ATTN_WIDTH), row(CONV_WIDTH),
            full((D_MODEL, D_MODEL)), full((1, ATTN_WIDTH)), bvec, bvec, bvec,
            full((D_MODEL, LANES)), full((D_MODEL, LANES)), full((1, LANES)),
        ],
        out_specs=[row(D_MODEL), row(D_MODEL), row(LANES),
                   pl.BlockSpec((1, 8, tm), lambda i: (i, 0, 0)),
                   pl.BlockSpec((1, 8, LANES), lambda i: (i, 0, 0))],
        out_shape=[jax.ShapeDtypeStruct((m, D_MODEL), F32), jax.ShapeDtypeStruct((m, D_MODEL), BF16),
                   jax.ShapeDtypeStruct((m, LANES), F32), jax.ShapeDtypeStruct((nt, 8, tm), F32),
                   jax.ShapeDtypeStruct((nt, 8, LANES), F32)],
        compiler_params=_cparams(("arbitrary",)),
        name="out_proj_router",
    )(x2, ys, oa, yc, wo, fg, gm, a2, s2, wrh, wrl, br)


def _local_rows(tm):
    return -(-(2 * tm + (CHUNK - 1) * N_EXPERTS) // LANES) * LANES


def _wait_chunks(n, n_max, copy_of_size):
    for bit in range(n_max.bit_length()):
        @pl.when((lax.shift_right_logical(n, bit) & 1) == 1)
        def _():
            copy_of_size(CHUNK << bit).wait()


def _dispatch_kernel(lo_ref, gb_ref, n8_ref, tot_ref, ts_ref, tn_ref, h_ref, rtt_ref, xs_hbm,
                     xloc, zeros, sem, zsem, *, lr):
    t = pl.program_id(0)
    nt = pl.num_programs(0)
    slot = t % 2
    tm = h_ref.shape[0]

    def seg_copy(sl, src, dst):
        return pltpu.make_async_copy(xloc.at[sl, pl.ds(src, CHUNK), :], xs_hbm.at[pl.ds(dst, CHUNK), :],
                                     sem.at[sl])

    def wait_tile(tile, sl):
        _wait_chunks(tot_ref[tile], lr // CHUNK, lambda size: pltpu.make_async_copy(
            xloc.at[sl, pl.ds(0, size), :], xs_hbm.at[pl.ds(0, size), :], sem.at[sl]))

    @pl.when(t >= 2)
    def _():
        wait_tile(t - 2, slot)

    rows = _iota((lr, tm), 0).astype(F32)
    perm = jnp.where(rows == rtt_ref[0, RT_ROW1:RT_ROW1 + 1, :], 1.0,
                     jnp.where(rows == rtt_ref[0, RT_ROW2:RT_ROW2 + 1, :], 1.0, 0.0))
    xloc[slot] = jnp.dot(perm.astype(BF16), h_ref[...], preferred_element_type=F32)

    def per_expert(e, c):
        idx = t * N_EXPERTS + e
        src0 = lo_ref[idx] * CHUNK
        dst0 = gb_ref[idx] * CHUNK

        def per_chunk(k, c2):
            seg_copy(slot, pl.multiple_of(src0 + k * CHUNK, CHUNK),
                     pl.multiple_of(dst0 + k * CHUNK, CHUNK)).start()
            return c2
        lax.fori_loop(0, n8_ref[idx], per_chunk, 0)
        return c
    lax.fori_loop(0, N_EXPERTS, per_expert, 0)

    @pl.when(t == nt - 1)
    def _():
        zeros[...] = jnp.zeros(zeros.shape, F32)

        def zero_copy(dst):
            return pltpu.make_async_copy(zeros, xs_hbm.at[pl.ds(dst, CHUNK), :], zsem.at[0])

        def fill(e, c):
            dst0 = ts_ref[e] * CHUNK

            def one(k, c2):
                zero_copy(pl.multiple_of(dst0 + k * CHUNK, CHUNK)).start()
                return c2
            lax.fori_loop(0, tn_ref[e], one, 0)
            return c
        lax.fori_loop(0, N_EXPERTS + 1, fill, 0)

        def drain(e, c):
            def one(k, c2):
                zero_copy(0).wait()
                return c2
            lax.fori_loop(0, tn_ref[e], one, 0)
            return c
        lax.fori_loop(0, N_EXPERTS + 1, drain, 0)

        @pl.when(t >= 1)
        def _():
            wait_tile(t - 1, 1 - slot)
        wait_tile(t, slot)


def _dispatch(plan, h2, rtt, n_rows, tm):
    m = h2.shape[0]
    lr = _local_rows(tm)
    grid_spec = pltpu.PrefetchScalarGridSpec(
        num_scalar_prefetch=6,
        grid=(m // tm,),
        in_specs=[pl.BlockSpec((tm, D_MODEL), lambda t, *_: (t, 0)),
                  pl.BlockSpec((1, 8, tm), lambda t, *_: (t, 0, 0))],
        out_specs=pl.BlockSpec(memory_space=pl.ANY),
        scratch_shapes=[pltpu.VMEM((2, lr, D_MODEL), F32), pltpu.VMEM((CHUNK, D_MODEL), F32),
                        pltpu.SemaphoreType.DMA((2,)), pltpu.SemaphoreType.DMA((1,))],
    )
    return pl.pallas_call(
        functools.partial(_dispatch_kernel, lr=lr),
        grid_spec=grid_spec,
        out_shape=jax.ShapeDtypeStruct((n_rows, D_MODEL), F32),
        compiler_params=_cparams(("arbitrary",)),
        name="moe_dispatch",
    )(plan["lo8"], plan["gb8"], plan["n8"], plan["tot8"], plan["ts8"], plan["tn8"], h2, rtt)


def _expert_kernel(be_ref, nu_ref, x_ref, wg_ref, wu_ref, wd_ref, y_ref, wgb, wub, wdb):
    i = pl.program_id(0)

    @pl.when(i < nu_ref[0])
    def _():
        prev_e = be_ref[jnp.maximum(i - 1, 0)]

        @pl.when((i == 0) | (be_ref[i] != prev_e))
        def _():
            wgb[...] = wg_ref[0, 0].astype(BF16)
            wub[...] = wu_ref[0, 0].astype(BF16)
            wdb[...] = wd_ref[0, 0].astype(BF16)

        x = x_ref[...].astype(BF16)
        hid = _silu(jnp.dot(x, wgb[...], preferred_element_type=F32)) * jnp.dot(
            x, wub[...], preferred_element_type=F32)
        y_ref[...] = jnp.dot(hid.astype(BF16), wdb[...], preferred_element_type=F32)

    @pl.when(i >= nu_ref[0])
    def _():
        y_ref[...] = jnp.zeros(y_ref.shape, F32)


def _experts(block_e, n_used, xs, wg, wu, wd, layer):
    n_rows = xs.shape[0]
    wspec = lambda shape: pl.BlockSpec((1, 1) + shape, lambda i, be, nu: (layer, be[i], 0, 0))
    grid_spec = pltpu.PrefetchScalarGridSpec(
        num_scalar_prefetch=2,
        grid=(n_rows // MOE_BLOCK,),
        in_specs=[
            pl.BlockSpec((MOE_BLOCK, D_MODEL), lambda i, be, nu: (i, 0)),
            wspec((D_MODEL, D_EXPERT)), wspec((D_MODEL, D_EXPERT)), wspec((D_EXPERT, D_MODEL)),
        ],
        out_specs=pl.BlockSpec((MOE_BLOCK, D_MODEL), lambda i, be, nu: (i, 0)),
        scratch_shapes=[
            pltpu.VMEM((D_MODEL, D_EXPERT), BF16),
            pltpu.VMEM((D_MODEL, D_EXPERT), BF16),
            pltpu.VMEM((D_EXPERT, D_MODEL), BF16),
        ],
    )
    return pl.pallas_call(
        _expert_kernel,
        grid_spec=grid_spec,
        out_shape=jax.ShapeDtypeStruct((n_rows, D_MODEL), F32),
        compiler_params=_cparams(("arbitrary",)),
        name="moe_experts",
    )(block_e, n_used, xs, wg, wu, wd)


def _combine_kernel(lo_ref, gb_ref, n8_ref, tot_ref, x_ref, rt_ref, gf_ref, fg_ref, ys_hbm, o_ref,
                    yloc, sem, *, lr, final):
    t = pl.program_id(0)
    nt = pl.num_programs(0)
    slot = t % 2
    tm = x_ref.shape[0]

    def fetch(tile, sl):
        def per_expert(e, c):
            idx = tile * N_EXPERTS + e
            dst0 = lo_ref[idx] * CHUNK
            src0 = gb_ref[idx] * CHUNK

            def per_chunk(k, c2):
                pltpu.make_async_copy(ys_hbm.at[pl.ds(pl.multiple_of(src0 + k * CHUNK, CHUNK), CHUNK), :],
                                      yloc.at[sl, pl.ds(pl.multiple_of(dst0 + k * CHUNK, CHUNK), CHUNK), :],
                                      sem.at[sl]).start()
                return c2
            lax.fori_loop(0, n8_ref[idx], per_chunk, 0)
            return c
        lax.fori_loop(0, N_EXPERTS, per_expert, 0)

    @pl.when(t == 0)
    def _():
        yloc[...] = jnp.zeros(yloc.shape, F32)
        fetch(0, 0)

    @pl.when(t + 1 < nt)
    def _():
        fetch(t + 1, 1 - slot)

    _wait_chunks(tot_ref[t], lr // CHUNK, lambda size: pltpu.make_async_copy(
        ys_hbm.at[pl.ds(0, size), :], yloc.at[slot, pl.ds(0, size), :], sem.at[slot]))

    rt = rt_ref[...]
    ysb = yloc[slot].astype(BF16)
    cols = _iota((tm, lr), 1).astype(F32)
    pick1 = jnp.where(cols == rt[:, RT_ROW1:RT_ROW1 + 1], 1.0, 0.0).astype(BF16)
    pick2 = jnp.where(cols == rt[:, RT_ROW2:RT_ROW2 + 1], 1.0, 0.0).astype(BF16)
    y1 = jnp.dot(pick1, ysb, preferred_element_type=F32)
    y2 = jnp.dot(pick2, ysb, preferred_element_type=F32)
    moe = rt[:, RT_W1:RT_W1 + 1] * y1 + rt[:, RT_W2:RT_W2 + 1] * y2
    x = x_ref[...] + gf_ref[0] * moe
    if final:
        ms = jnp.mean(x * x, axis=-1, keepdims=True)
        x = x * lax.rsqrt(ms + EPS) * fg_ref[...]
    o_ref[...] = x


def _combine(plan, x2, ys, rt, gf, fg, seq, tm, final):
    m = x2.shape[0]
    per_b = seq // tm
    lr = _local_rows(tm)
    grid_spec = pltpu.PrefetchScalarGridSpec(
        num_scalar_prefetch=4,
        grid=(m // tm,),
        in_specs=[
            pl.BlockSpec((tm, D_MODEL), lambda i, *_: (i, 0)),
            pl.BlockSpec((tm, LANES), lambda i, *_: (i, 0)),
            pl.BlockSpec((1, 1, D_MODEL), lambda i, *_: (i // per_b, 0, 0)),
            pl.BlockSpec((1, D_MODEL), lambda i, *_: (0, 0)),
            pl.BlockSpec(memory_space=pl.ANY),
        ],
        out_specs=pl.BlockSpec((tm, D_MODEL), lambda i, *_: (i, 0)),
        scratch_shapes=[pltpu.VMEM((2, lr, D_MODEL), F32), pltpu.SemaphoreType.DMA((2,))],
    )
    return pl.pallas_call(
        functools.partial(_combine_kernel, lr=lr, final=final),
        grid_spec=grid_spec,
        out_shape=jax.ShapeDtypeStruct((m, D_MODEL), F32),
        compiler_params=_cparams(("arbitrary",)),
        name="moe_combine",
    )(plan["lo8"], plan["gb8"], plan["n8"], plan["tot8"], x2, rt, gf, fg, ys)


def _moe_plan(c, n_blocks):
    i32 = jnp.int32
    blk8 = MOE_BLOCK // CHUNK
    c8 = (c + CHUNK - 1) // CHUNK
    lo8 = jnp.cumsum(c8, axis=1) - c8
    per_e = jnp.sum(c8, axis=0)
    pad8 = (per_e + blk8 - 1) // blk8 * blk8
    end8 = jnp.cumsum(pad8)
    start8 = end8 - pad8
    gb8 = start8[None, :] + jnp.cumsum(c8, axis=0) - c8
    blk_start8 = jnp.arange(n_blocks, dtype=i32) * blk8
    block_e = jnp.minimum(jnp.sum(end8[None, :] <= blk_start8[:, None], axis=1), N_EXPERTS - 1)
    return {
        "lo8": lo8.reshape(-1).astype(i32), "gb8": gb8.reshape(-1).astype(i32),
        "n8": c8.reshape(-1).astype(i32), "tot8": jnp.sum(c8, axis=1).astype(i32),
        "ts8": jnp.concatenate([start8 + per_e, end8[-1:]]).astype(i32),
        "tn8": jnp.concatenate([pad8 - per_e, n_blocks * blk8 - end8[-1:]]).astype(i32),
        "block_e": block_e.astype(i32), "n_used": (end8[-1:] // blk8).astype(i32),
    }


def _pack_w_in(w):
    d_dt = SSD_WIDTH + XBC_WIDTH
    d_q = d_dt + SSD_HEADS
    d_f = d_q + 3 * ATTN_WIDTH
    d_ga = d_f + ATTN_HEADS
    small = jnp.concatenate([w[:, d_dt:d_q], w[:, d_f:d_ga],
                             jnp.zeros((D_MODEL, LANES - SSD_HEADS - ATTN_HEADS), w.dtype)], axis=1)
    return jnp.concatenate([w[:, SSD_WIDTH:d_dt], w[:, :SSD_WIDTH], w[:, d_q:d_f], w[:, d_ga:], small],
                           axis=1).astype(BF16)


def kernel(x, c, ada_w, ada_b, norm_mix_g, w_in, ssd_conv_w, ssd_conv_b, ssd_dt_bias, ssd_a_log,
           ssd_d, ssd_norm_g, fox_f_bias, fox_norm_g, cm_conv_w, cm_conv_b, cm_ln_g, cm_ln_b, w_out,
           norm_ffn_g, w_router_group, b_router_group, w_router_expert, b_router_expert, w_gate,
           w_up, w_down, final_norm_g):
    bsz, seq, d = x.shape
    m = bsz * seq
    tm = min(512, seq)
    tt = min(256, seq)
    n_blocks = -(-(2 * m + (CHUNK - 1) * (m // tm) * N_EXPERTS) // MOE_BLOCK) + N_EXPERTS

    mod = _modulation(c, ada_w, ada_b)
    x2 = x.reshape(m, d)
    for l in range(DEPTH):
        sh_m, sc_m, g_m, sh_f, sc_f, g_f = [v.reshape(bsz, 1, d) for v in jnp.split(mod[l], 6, axis=-1)]
        a_m = norm_mix_g[l][None, None, :] * (1.0 + sc_m)
        proj = _inproj(x2, a_m, sh_m, _pack_w_in(w_in[l]), seq, tm)
        dtt = proj[:, COL_SM + SM_DT:COL_SM + SM_DT + SSD_HEADS].reshape(bsz, seq, SSD_HEADS)
        dtt = dtt.transpose(0, 2, 1)
        y_ssd = _ssd(proj, dtt, ssd_conv_w[l], ssd_conv_b[l], ssd_dt_bias[l], ssd_a_log[l],
                     ssd_d[l], ssd_norm_g[l], bsz, seq, tt)
        qa, ka, va, aux = _foxprep(proj, fox_f_bias[l], bsz, seq)
        o_att = _fox(qa, ka, va, aux)
        y_cnv = _conformer(proj, cm_conv_w[l], cm_conv_b[l], cm_ln_g[l], cm_ln_b[l], bsz, seq, tt)

        w_r = jnp.concatenate([w_router_group[l], w_router_expert[l],
                               jnp.zeros((d, LANES - N_GROUPS - N_EXPERTS), F32)], axis=1)
        w_rh = w_r.astype(BF16)
        w_rl = (w_r - w_rh.astype(F32)).astype(BF16)
        b_r = jnp.concatenate([b_router_group[l], b_router_expert[l],
                               jnp.zeros((LANES - N_GROUPS - N_EXPERTS,), F32)]).reshape(1, LANES)
        a_f = norm_ffn_g[l][None, None, :] * (1.0 + sc_f)
        x2, h2, rt, rtt, cnt = _outproj(x2, y_ssd, o_att, y_cnv, w_out[l].astype(BF16),
                                        fox_norm_g[l].reshape(1, -1), g_m, a_f, sh_f, w_rh, w_rl, b_r,
                                        seq, tm)
        plan = _moe_plan(cnt[:, 0, :N_EXPERTS].astype(jnp.int32), n_blocks)
        xs = _dispatch(plan, h2, rtt, n_blocks * MOE_BLOCK, tm)
        ys = _experts(plan["block_e"], plan["n_used"], xs, w_gate, w_up, w_down, l)
        x2 = _combine(plan, x2, ys, rt, g_f, final_norm_g.reshape(1, d), seq, tm,
                      final=(l == DEPTH - 1))
    return x2.reshape(bsz, seq, d)
```

```python
import functools

import jax
import jax.numpy as jnp
import numpy as np
from jax import lax
from jax.experimental import pallas as pl
from jax.experimental.pallas import tpu as pltpu

F32 = jnp.float32
BF16 = jnp.bfloat16
HIGHEST = lax.Precision.HIGHEST

D_MODEL = 1024
DEPTH = 4
SSD_WIDTH = 512
SSD_HEADS = 8
SSD_HEAD_DIM = 64
SSD_STATE = 128
SSD_CONV = 4
SSD_CHUNK = 128
XBC_WIDTH = 1024
ATTN_WIDTH = 256
ATTN_HEADS = 4
ATTN_HEAD_DIM = 64
CONV_WIDTH = 256
CONV_KERNEL = 31
N_GROUPS = 4
PER_GROUP = 8
N_EXPERTS = 32
D_EXPERT = 512
MOE_BLOCK = 256
EPS = 1e-6

LANES = 128
COL_XBC = 0
COL_Z = 1024
COL_QKV = 1536
COL_GA = 2304
COL_GB = 2560
COL_SM = 2816
NP = 2944
SM_DT = 0
SM_F = 8

VMEM_LIMIT = 56 * 1024 * 1024


def _cparams(sem):
    return pltpu.CompilerParams(dimension_semantics=sem, vmem_limit_bytes=VMEM_LIMIT)


def _sigmoid(x):
    return 1.0 / (1.0 + jnp.exp(-x))


def _silu(x):
    return x * _sigmoid(x)


def _softplus(x):
    return jnp.maximum(x, 0.0) + jnp.log1p(jnp.exp(-jnp.abs(x)))


def _iota(shape, dim):
    return lax.broadcasted_iota(jnp.int32, shape, dim)


def _mod_kernel(c_ref, w_ref, b_ref, o_ref):
    cond = _silu(c_ref[...])
    o_ref[0] = jnp.dot(cond, w_ref[0], precision=HIGHEST, preferred_element_type=F32) + b_ref[0]


def _modulation(c, ada_w, ada_b):
    bsz = c.shape[0]
    rows = 8
    cpad = jnp.zeros((rows, D_MODEL), F32).at[:bsz].set(c)
    tn = 1536
    n6 = 6 * D_MODEL
    out = pl.pallas_call(
        _mod_kernel,
        grid=(DEPTH, n6 // tn),
        in_specs=[
            pl.BlockSpec((rows, D_MODEL), lambda l, j: (0, 0)),
            pl.BlockSpec((1, D_MODEL, tn), lambda l, j: (l, 0, j)),
            pl.BlockSpec((1, 1, tn), lambda l, j: (l, 0, j)),
        ],
        out_specs=pl.BlockSpec((1, rows, tn), lambda l, j: (l, 0, j)),
        out_shape=jax.ShapeDtypeStruct((DEPTH, rows, n6), F32),
        compiler_params=_cparams(("arbitrary", "arbitrary")),
        name="adaln_mod",
    )(cpad, ada_w, ada_b.reshape(DEPTH, 1, n6))
    return out[:, :bsz]


def _inproj_kernel(x_ref, a_ref, s_ref, w_ref, o_ref):
    x = x_ref[...]
    ms = jnp.mean(x * x, axis=-1, keepdims=True)
    h = x * lax.rsqrt(ms + EPS) * a_ref[0] + s_ref[0]
    o_ref[...] = jnp.dot(h.astype(BF16), w_ref[...], preferred_element_type=F32)


def _inproj(x2, a, s, w_pack, seq, tm):
    m = x2.shape[0]
    per_b = seq // tm
    return pl.pallas_call(
        _inproj_kernel,
        grid=(m // tm,),
        in_specs=[
            pl.BlockSpec((tm, D_MODEL), lambda i: (i, 0)),
            pl.BlockSpec((1, 1, D_MODEL), lambda i: (i // per_b, 0, 0)),
            pl.BlockSpec((1, 1, D_MODEL), lambda i: (i // per_b, 0, 0)),
            pl.BlockSpec((D_MODEL, NP), lambda i: (0, 0)),
        ],
        out_specs=pl.BlockSpec((tm, NP), lambda i: (i, 0)),
        out_shape=jax.ShapeDtypeStruct((m, NP), F32),
        compiler_params=_cparams(("arbitrary",)),
        name="in_proj",
    )(x2, a, s, w_pack)


def _ssd_kernel(z_ref, xbc_ref, sm_ref, cw_ref, cb_ref, dtb_ref, alog_ref, dtbt_ref,
                alogt_ref, e_ref, dx_ref, ng_ref, y_ref, xpad, xc, prev, *, tt):
    t = pl.program_id(1)

    @pl.when(t == 0)
    def _():
        xpad[0:8, :] = jnp.zeros((8, XBC_WIDTH), F32)
        prev[...] = jnp.zeros(prev.shape, F32)

    xpad[8:8 + tt, :] = xbc_ref[...]
    acc = jnp.broadcast_to(cb_ref[...], (tt, XBC_WIDTH))
    for k in range(SSD_CONV):
        off = 8 - (SSD_CONV - 1) + k
        acc = acc + cw_ref[k:k + 1, :] * xpad[off:off + tt, :]
    xc[...] = _silu(acc)
    xpad[0:8, :] = xpad[tt:tt + 8, :]

    cl = SSD_CHUNK
    row = _iota((cl, cl), 0)
    col = _iota((cl, cl), 1)
    causal = row >= col
    tril = causal.astype(F32)
    triu = (row <= col).astype(F32)
    lo = col < SSD_HEAD_DIM
    lane1 = _iota((1, LANES), 1)
    a_row = jnp.where(lane1 < SSD_HEADS, -jnp.exp(alog_ref[...]), 0.0)
    a_col = -jnp.exp(alogt_ref[...])
    expand = e_ref[...]

    def chunk(c, carry):
        r0 = pl.multiple_of(c * cl, cl)
        xs = xc[pl.ds(r0, cl), 0:SSD_WIDTH]
        bmat = xc[pl.ds(r0, cl), SSD_WIDTH:SSD_WIDTH + 2 * SSD_STATE]
        cmat = xc[pl.ds(r0, cl), SSD_WIDTH + 2 * SSD_STATE:XBC_WIDTH]
        sm = sm_ref[pl.ds(r0, cl), :]
        dt = _softplus(sm + dtb_ref[...])
        da = dt * a_row
        acs = jnp.dot(tril, da, precision=HIGHEST, preferred_element_type=F32)
        dtt = _softplus(sm.T[0:SSD_HEADS, :] + dtbt_ref[...])
        acst = jnp.dot(dtt * a_col, triu, precision=HIGHEST, preferred_element_type=F32)
        dt_x = jnp.dot(dt, expand, precision=HIGHEST, preferred_element_type=F32)
        acs_x = jnp.dot(acs, expand, precision=HIGHEST, preferred_element_type=F32)
        last = acs_x[cl - 1:cl, :]
        eacs_x = jnp.exp(acs_x)
        dte_x = jnp.exp(last - acs_x)
        cd_x = jnp.exp(last)
        xdt = xs * dt_x
        xdte = (xdt * dte_x).astype(BF16)
        zc = z_ref[pl.ds(r0, cl), :]
        for g in range(2):
            bg = bmat[:, g * SSD_STATE:(g + 1) * SSD_STATE]
            cg = cmat[:, g * SSD_STATE:(g + 1) * SSD_STATE].astype(BF16)
            bgt = bg.T.astype(BF16)
            cbm = jnp.dot(cg, bgt, preferred_element_type=F32)
            pair_out = []
            for j in range(2):
                p = 2 * g + j
                sl = slice(p * LANES, (p + 1) * LANES)
                xp = xdt[:, sl]
                yd = jnp.zeros((cl, LANES), F32)
                for half in range(2):
                    h = 2 * p + half
                    seg = acs[:, h:h + 1] - acst[h:h + 1, :]
                    dec = jnp.exp(jnp.where(causal, seg, -jnp.inf))
                    gm = (cbm * dec).astype(BF16)
                    own = lo if half == 0 else jnp.logical_not(lo)
                    xm = jnp.where(own, xp, 0.0).astype(BF16)
                    yd = yd + jnp.dot(gm, xm, preferred_element_type=F32)
                prev_p = prev[:, sl]
                yo = jnp.dot(cg, prev_p.astype(BF16), preferred_element_type=F32) * eacs_x[:, sl]
                st = jnp.dot(bgt, xdte[:, sl], preferred_element_type=F32)
                prev[:, sl] = prev_p * cd_x[:, sl] + st
                pair_out.append(yd + yo + xs[:, sl] * dx_ref[:, sl])
            gs = slice(g * 2 * LANES, (g + 1) * 2 * LANES)
            yg = jnp.concatenate(pair_out, axis=-1) * _silu(zc[:, gs])
            ms = jnp.mean(yg * yg, axis=-1, keepdims=True)
            y_ref[pl.ds(r0, cl), gs] = yg * lax.rsqrt(ms + EPS) * ng_ref[:, gs]
        return carry

    lax.fori_loop(0, tt // cl, chunk, 0)


def _ssd(proj, cw, cb, dtb, alog, dx, ng, bsz, seq, tt):
    m = bsz * seq
    nt = seq // tt
    pad = LANES - SSD_HEADS
    dtb_row = jnp.pad(dtb, (0, pad)).reshape(1, LANES)
    alog_row = jnp.pad(alog, (0, pad)).reshape(1, LANES)
    dtb_col = jnp.broadcast_to(dtb[:, None], (SSD_HEADS, SSD_CHUNK))
    alog_col = jnp.broadcast_to(alog[:, None], (SSD_HEADS, SSD_CHUNK))
    expand = np.zeros((LANES, SSD_WIDTH), np.float32)
    for h in range(SSD_HEADS):
        expand[h, h * SSD_HEAD_DIM:(h + 1) * SSD_HEAD_DIM] = 1.0
    dx_row = jnp.repeat(dx, SSD_HEAD_DIM).reshape(1, SSD_WIDTH)
    full = lambda shape: pl.BlockSpec(shape, lambda b, t: (0,) * len(shape))
    return pl.pallas_call(
        functools.partial(_ssd_kernel, tt=tt),
        grid=(bsz, nt),
        in_specs=[
            pl.BlockSpec((tt, SSD_WIDTH), lambda b, t: (b * nt + t, COL_Z // SSD_WIDTH)),
            pl.BlockSpec((tt, XBC_WIDTH), lambda b, t: (b * nt + t, COL_XBC // XBC_WIDTH + 0)),
            pl.BlockSpec((tt, LANES), lambda b, t: (b * nt + t, COL_SM // LANES)),
            full((SSD_CONV, XBC_WIDTH)),
            full((1, XBC_WIDTH)),
            full((1, LANES)),
            full((1, LANES)),
            full((SSD_HEADS, SSD_CHUNK)),
            full((SSD_HEADS, SSD_CHUNK)),
            full((LANES, SSD_WIDTH)),
            full((1, SSD_WIDTH)),
            full((1, SSD_WIDTH)),
        ],
        out_specs=pl.BlockSpec((tt, SSD_WIDTH), lambda b, t: (b * nt + t, 0)),
        out_shape=jax.ShapeDtypeStruct((m, SSD_WIDTH), F32),
        scratch_shapes=[
            pltpu.VMEM((tt + 8, XBC_WIDTH), F32),
            pltpu.VMEM((tt, XBC_WIDTH), F32),
            pltpu.VMEM((SSD_STATE, SSD_WIDTH), F32),
        ],
        compiler_params=_cparams(("arbitrary", "arbitrary")),
        name="ssd_scan",
    )(proj, proj, proj, cw, cb.reshape(1, XBC_WIDTH), dtb_row, alog_row, dtb_col, alog_col,
      jnp.asarray(expand), dx_row, ng.reshape(1, SSD_WIDTH))


LOG2E = 1.4426950408889634
ATT_BLOCK = 256
PRUNE_LOG2 = 140.0
FAST_GAP_LOG2 = 60.0
AUX_CUMEND = 0
AUX_KMAX = 1


def _foxprep_kernel(qkv_ref, sm_ref, fb_ref, q_out, k_out, v_out, aux_out, carry, *, tt):
    t = pl.program_id(1)

    @pl.when(t == 0)
    def _():
        carry[...] = jnp.zeros(carry.shape, F32)
        aux_out[...] = jnp.zeros(aux_out.shape, F32)

    logit = sm_ref[...] + fb_ref[...]
    logf = -_softplus(-logit)
    row = _iota((tt, tt), 0)
    col = _iota((tt, tt), 1)
    tril = (row >= col).astype(F32)
    cum = jnp.dot(tril, logf, precision=HIGHEST, preferred_element_type=F32) + carry[...]
    carry[...] = cum[tt - 1:tt, :]

    lane = _iota((tt, LANES), 1)
    lane1 = _iota((1, LANES), 1)
    scale = ATTN_HEAD_DIM ** -0.5 * LOG2E
    for h in range(ATTN_HEADS):
        pair, half = h // 2, h % 2
        own = (lane < ATTN_HEAD_DIM) if half == 0 else (lane >= ATTN_HEAD_DIM)
        a0 = ATTN_HEAD_DIM * (1 - half)
        cs = jnp.broadcast_to(cum[:, SM_F + h:SM_F + h + 1], (tt, LANES)) * LOG2E
        hi = cs.astype(BF16).astype(F32)
        r1 = cs - hi
        mid = r1.astype(BF16).astype(F32)
        low = r1 - mid
        qp = qkv_ref[:, pair * LANES:(pair + 1) * LANES]
        kp = qkv_ref[:, ATTN_WIDTH + pair * LANES:ATTN_WIDTH + (pair + 1) * LANES]
        vp = qkv_ref[:, 2 * ATTN_WIDTH + pair * LANES:2 * ATTN_WIDTH + (pair + 1) * LANES]
        qa = jnp.where(lane == a0, hi, jnp.where(lane == a0 + 1, mid, jnp.where(
            lane == a0 + 2, low, jnp.where((lane >= a0 + 3) & (lane < a0 + 6), 1.0, 0.0))))
        ka = jnp.where(lane == a0 + 3, -hi, jnp.where(lane == a0 + 4, -mid, jnp.where(
            lane == a0 + 5, -low, jnp.where((lane >= a0) & (lane < a0 + 9), 1.0, 0.0))))
        q_out[0, h] = jnp.where(own, qp * scale, qa).astype(BF16)
        kb = jnp.where(own, kp, ka).astype(BF16)
        k_out[0, h] = kb
        v_out[0, h] = jnp.where(own, vp, jnp.where(lane == a0, 1.0, 0.0)).astype(BF16)
        kf = jnp.where(own, kb.astype(F32), 0.0)
        kn2 = jnp.max(jnp.sum(kf * kf, axis=-1, keepdims=True), axis=0, keepdims=True)
        here = lane1 == t
        aux_out[0, h, AUX_CUMEND:AUX_CUMEND + 1, :] = jnp.where(
            here, cs[tt - 1:tt, :], aux_out[0, h, AUX_CUMEND:AUX_CUMEND + 1, :])
        aux_out[0, h, AUX_KMAX:AUX_KMAX + 1, :] = jnp.where(
            here, jnp.sqrt(kn2), aux_out[0, h, AUX_KMAX:AUX_KMAX + 1, :])


def _foxprep(proj, fb, bsz, seq):
    tt = ATT_BLOCK
    nt = seq // tt
    assert nt <= LANES
    fb_row = jnp.zeros((1, LANES), F32).at[0, SM_F:SM_F + ATTN_HEADS].set(fb)
    shp = jax.ShapeDtypeStruct((bsz, ATTN_HEADS, seq, LANES), BF16)
    ospec = pl.BlockSpec((1, ATTN_HEADS, tt, LANES), lambda b, t: (b, 0, t, 0))
    return pl.pallas_call(
        functools.partial(_foxprep_kernel, tt=tt),
        grid=(bsz, nt),
        in_specs=[
            pl.BlockSpec((tt, 3 * ATTN_WIDTH), lambda b, t: (b * nt + t, COL_QKV // (3 * ATTN_WIDTH))),
            pl.BlockSpec((tt, LANES), lambda b, t: (b * nt + t, COL_SM // LANES)),
            pl.BlockSpec((1, LANES), lambda b, t: (0, 0)),
        ],
        out_specs=[ospec, ospec, ospec,
                   pl.BlockSpec((1, ATTN_HEADS, 8, LANES), lambda b, t: (b, 0, 0, 0))],
        out_shape=[shp, shp, shp, jax.ShapeDtypeStruct((bsz, ATTN_HEADS, 8, LANES), F32)],
        scratch_shapes=[pltpu.VMEM((1, LANES), F32)],
        compiler_params=_cparams(("arbitrary", "arbitrary")),
        name="fox_prep",
    )(proj, proj, fb_row)


def _fox_kernel(q_ref, k_ref, v_ref, aux_ref, o_ref, *, tq):
    i = pl.program_id(2)
    nt = (((1,), (1,)), ((), ()))
    lane = _iota((tq, LANES), 1)
    lane1 = _iota((1, LANES), 1)
    qs = (q_ref[0, 0], q_ref[0, 1])

    def step(hh, r0, m, acc, mask):
        kblk = k_ref[0, hh, pl.ds(r0, tq), :]
        vblk = v_ref[0, hh, pl.ds(r0, tq), :]
        s = lax.dot_general(qs[hh], kblk, nt, preferred_element_type=F32)
        if mask is not None:
            s = jnp.where(mask, s, -jnp.inf)
        m_new = jnp.maximum(m, jnp.max(s, axis=-1, keepdims=True))
        p = jnp.exp2(s - m_new)
        alpha = jnp.exp2(m - m_new)
        acc = acc * alpha + jnp.dot(p.astype(BF16), vblk, preferred_element_type=F32)
        return m_new, acc

    d0 = pl.multiple_of(i * tq, tq)
    diag = _iota((tq, tq), 0) >= _iota((tq, tq), 1)
    m_init = jnp.full((tq, 1), -jnp.inf, F32)
    acc_init = jnp.zeros((tq, LANES), F32)
    i_f = i.astype(F32)
    carry = []
    first = []
    gaps = []
    shifted = []
    for hh in range(2):
        m, acc = step(hh, d0, m_init, acc_init, diag)
        carry += [m, acc]
        own = (lane < ATTN_HEAD_DIM) if hh == 0 else (lane >= ATTN_HEAD_DIM)
        a0 = ATTN_HEAD_DIM * (1 - hh)
        qf = qs[hh].astype(F32)
        qn = jnp.sqrt(jnp.sum(jnp.where(own, qf * qf, 0.0), axis=-1, keepdims=True))
        cum_t = jnp.sum(jnp.where((lane >= a0) & (lane < a0 + 3), qf, 0.0), axis=-1, keepdims=True)
        aux = aux_ref[0, hh]
        cend = aux[AUX_CUMEND:AUX_CUMEND + 1, :]
        kmax = jnp.max(jnp.where(lane1 <= i, aux[AUX_KMAX:AUX_KMAX + 1, :], 0.0), axis=-1, keepdims=True)
        slack = jnp.max(qn * kmax + cum_t - m, axis=0, keepdims=True)
        live = (lane1 < i) & (slack - cend > -PRUNE_LOG2)
        first.append(jnp.min(jnp.where(live, lane1.astype(F32), i_f)))
        cend_prev = jnp.sum(jnp.where(lane1 == i - 1, cend, 0.0), axis=-1, keepdims=True)
        shift = jnp.maximum(m, qn * kmax + cum_t - cend_prev)
        s_hi = shift.astype(BF16).astype(F32)
        s_r = shift - s_hi
        s_mid = s_r.astype(BF16).astype(F32)
        s_lo = (s_r - s_mid).astype(BF16).astype(F32)
        shift = s_hi + s_mid + s_lo
        gaps.append(jnp.max(shift - m))
        q_sh = jnp.where(lane == a0 + 6, -s_hi, jnp.where(lane == a0 + 7, -s_mid, jnp.where(
            lane == a0 + 8, -s_lo, qf))).astype(BF16)
        shifted.append((q_sh, acc * jnp.exp2(m - shift)))
    j_first = jnp.minimum(first[0], first[1]).astype(jnp.int32)

    def fast(_):
        def blocks(c, r0, width):
            out = []
            for hh in range(2):
                s = lax.dot_general(shifted[hh][0], k_ref[0, hh, pl.ds(r0, width), :], nt,
                                    preferred_element_type=F32)
                out.append(c[hh] + jnp.dot(jnp.exp2(s).astype(BF16), v_ref[0, hh, pl.ds(r0, width), :],
                                           preferred_element_type=F32))
            return tuple(out)

        n = i - j_first
        odd = n % 2
        c = lax.cond(odd == 1, lambda c: blocks(c, pl.multiple_of(j_first * tq, tq), tq), lambda c: c,
                     (shifted[0][1], shifted[1][1]))
        j0 = j_first + odd

        def body(jj, c):
            return blocks(c, pl.multiple_of((j0 + 2 * jj) * tq, tq), 2 * tq)
        return lax.fori_loop(0, (n - odd) // 2, body, c)

    def online(_):
        def body(j, c):
            r0 = pl.multiple_of(j * tq, tq)
            m0, a0_, m1, a1_ = c
            m0, a0_ = step(0, r0, m0, a0_, None)
            m1, a1_ = step(1, r0, m1, a1_, None)
            return m0, a0_, m1, a1_
        _, a0_, _, a1_ = lax.fori_loop(j_first, i, body, tuple(carry))
        return a0_, a1_

    acc0, acc1 = lax.cond(jnp.maximum(gaps[0], gaps[1]) <= FAST_GAP_LOG2, fast, online, None)
    den0 = jnp.sum(jnp.where(lane == ATTN_HEAD_DIM, acc0, 0.0), axis=-1, keepdims=True)
    den1 = jnp.sum(jnp.where(lane == 0, acc1, 0.0), axis=-1, keepdims=True)
    o_ref[...] = jnp.where(lane < ATTN_HEAD_DIM, acc0 / den0, acc1 / den1)


def _fox(qa, ka, va, aux):
    bsz, nh, seq, _ = qa.shape
    tq = ATT_BLOCK
    nq = seq // tq
    kv_spec = pl.BlockSpec((1, 2, seq, LANES), lambda b, p, i: (b, p, 0, 0))
    return pl.pallas_call(
        functools.partial(_fox_kernel, tq=tq),
        grid=(bsz, nh // 2, nq),
        in_specs=[pl.BlockSpec((1, 2, tq, LANES), lambda b, p, i: (b, p, i, 0)), kv_spec, kv_spec,
                  pl.BlockSpec((1, 2, 8, LANES), lambda b, p, i: (b, p, 0, 0))],
        out_specs=pl.BlockSpec((tq, LANES), lambda b, p, i: (b * nq + i, p)),
        out_shape=jax.ShapeDtypeStruct((bsz * seq, ATTN_WIDTH), F32),
        compiler_params=_cparams(("arbitrary", "arbitrary", "arbitrary")),
        name="fox_attn",
    )(qa, ka, va, aux)


CONF_HALO = 32
CONF_ROWS = 64


def _conf_kernel(ga_ref, gb_ref, w_ref, b_ref, lg_ref, lb_ref, y_ref, upad, *, tt):
    t = pl.program_id(1)

    @pl.when(t == 0)
    def _():
        upad[0:CONF_HALO, :] = jnp.zeros((CONF_HALO, CONV_WIDTH), F32)

    upad[CONF_HALO:CONF_HALO + tt, :] = ga_ref[...] * _sigmoid(gb_ref[...])
    base = CONF_HALO - (CONV_KERNEL - 1)
    for r in range(tt // CONF_ROWS):
        r0 = r * CONF_ROWS
        acc = jnp.broadcast_to(b_ref[...], (CONF_ROWS, CONV_WIDTH))
        for k in range(CONV_KERNEL):
            acc = acc + w_ref[k:k + 1, :] * upad[r0 + base + k:r0 + base + k + CONF_ROWS, :]
        mu = jnp.mean(acc, axis=-1, keepdims=True)
        cen = acc - mu
        var = jnp.mean(cen * cen, axis=-1, keepdims=True)
        y = cen * lax.rsqrt(var + EPS) * lg_ref[...] + lb_ref[...]
        y_ref[r0:r0 + CONF_ROWS, :] = _silu(y)
    upad[0:CONF_HALO, :] = upad[tt:tt + CONF_HALO, :]


def _conformer(proj, w, b, lg, lb, bsz, seq, tt):
    m = bsz * seq
    nt = seq // tt
    full = lambda shape: pl.BlockSpec(shape, lambda bb, t: (0,) * len(shape))
    return pl.pallas_call(
        functools.partial(_conf_kernel, tt=tt),
        grid=(bsz, nt),
        in_specs=[
            pl.BlockSpec((tt, CONV_WIDTH), lambda bb, t: (bb * nt + t, COL_GA // CONV_WIDTH)),
            pl.BlockSpec((tt, CONV_WIDTH), lambda bb, t: (bb * nt + t, COL_GB // CONV_WIDTH)),
            full((CONV_KERNEL, CONV_WIDTH)),
            full((1, CONV_WIDTH)),
            full((1, CONV_WIDTH)),
            full((1, CONV_WIDTH)),
        ],
        out_specs=pl.BlockSpec((tt, CONV_WIDTH), lambda bb, t: (bb * nt + t, 0)),
        out_shape=jax.ShapeDtypeStruct((m, CONV_WIDTH), F32),
        scratch_shapes=[pltpu.VMEM((tt + CONF_HALO, CONV_WIDTH), F32)],
        compiler_params=_cparams(("arbitrary", "arbitrary")),
        name="conformer_conv",
    )(proj, proj, w, b.reshape(1, -1), lg.reshape(1, -1), lb.reshape(1, -1))


ROUTE_BIG = 1e9
CHUNK = 8
RT_W1, RT_W2, RT_ROW1, RT_ROW2 = 0, 1, 2, 3


def _split3(x):
    hi = x.astype(BF16)
    lo = (x - hi.astype(F32)).astype(BF16)
    return hi, lo


def _outproj_kernel(x_ref, ys_ref, oa_ref, yc_ref, wo_ref, fg_ref, gm_ref, a2_ref, s2_ref,
                    wrh_ref, wrl_ref, br_ref, xn_ref, h2_ref, rt_ref, rtt_ref, cnt_ref):
    tm = x_ref.shape[0]
    lane = _iota((tm, LANES), 1)
    att = oa_ref[...]
    ms = jnp.mean(att * att, axis=-1, keepdims=True)
    att = att * lax.rsqrt(ms + EPS) * fg_ref[...]
    y = jnp.dot(ys_ref[...].astype(BF16), wo_ref[0:SSD_WIDTH, :], preferred_element_type=F32)
    y = y + jnp.dot(att.astype(BF16), wo_ref[SSD_WIDTH:SSD_WIDTH + ATTN_WIDTH, :],
                    preferred_element_type=F32)
    y = y + jnp.dot(yc_ref[...].astype(BF16), wo_ref[SSD_WIDTH + ATTN_WIDTH:, :],
                    preferred_element_type=F32)
    xn = x_ref[...] + gm_ref[0] * y
    xn_ref[...] = xn
    ms2 = jnp.mean(xn * xn, axis=-1, keepdims=True)
    h2 = xn * lax.rsqrt(ms2 + EPS) * a2_ref[0] + s2_ref[0]
    h2_ref[...] = h2.astype(BF16)

    hh, hl = _split3(h2)
    logits = (jnp.dot(hh, wrh_ref[...], preferred_element_type=F32)
              + jnp.dot(hl, wrh_ref[...], preferred_element_type=F32)
              + jnp.dot(hh, wrl_ref[...], preferred_element_type=F32)) + br_ref[...]
    lanef = lane.astype(F32)
    lg = jnp.where(lane < N_GROUPS, logits, -jnp.inf)
    gmax = jnp.max(lg, axis=-1, keepdims=True)
    gsum = jnp.sum(jnp.exp(lg - gmax), axis=-1, keepdims=True)
    gidx = jnp.min(jnp.where(lg == gmax, lanef, ROUTE_BIG), axis=-1, keepdims=True)
    e_lo = N_GROUPS + PER_GROUP * gidx
    le = jnp.where((lanef >= e_lo) & (lanef < e_lo + PER_GROUP), logits, -jnp.inf)
    m1 = jnp.max(le, axis=-1, keepdims=True)
    i1 = jnp.min(jnp.where(le == m1, lanef, ROUTE_BIG), axis=-1, keepdims=True)
    le2 = jnp.where(lanef == i1, -jnp.inf, le)
    m2 = jnp.max(le2, axis=-1, keepdims=True)
    i2 = jnp.min(jnp.where(le2 == m2, lanef, ROUTE_BIG), axis=-1, keepdims=True)
    esum = jnp.sum(jnp.exp(le - m1), axis=-1, keepdims=True)
    p1 = 1.0 / esum
    p2 = jnp.exp(m2 - m1) / esum
    psel = 1.0 / gsum
    w1 = p1 / (p1 + p2) * psel
    w2 = p2 / (p1 + p2) * psel

    oh1 = (lanef == (i1 - N_GROUPS)).astype(F32)
    oh2 = (lanef == (i2 - N_GROUPS)).astype(F32)
    oh = oh1 + oh2
    cnt = jnp.sum(oh, axis=0, keepdims=True)
    before = (_iota((tm, tm), 0) > _iota((tm, tm), 1)).astype(BF16)
    rank = jnp.dot(before, oh.astype(BF16), preferred_element_type=F32)
    chunks = jnp.floor((cnt + (CHUNK - 1)) * (1.0 / CHUNK))
    below = (_iota((LANES, LANES), 0) < _iota((LANES, LANES), 1)).astype(BF16)
    seg_lo = CHUNK * jnp.dot(jnp.broadcast_to(chunks, (8, LANES)).astype(BF16), below,
                             preferred_element_type=F32)[0:1, :]
    lr1 = jnp.sum(oh1 * (rank + seg_lo), axis=-1, keepdims=True)
    lr2 = jnp.sum(oh2 * (rank + seg_lo), axis=-1, keepdims=True)
    route = jnp.where(lane == RT_W1, w1, jnp.where(lane == RT_W2, w2, jnp.where(
        lane == RT_ROW1, lr1, jnp.where(lane == RT_ROW2, lr2, 0.0))))
    rt_ref[...] = route
    rtt_ref[0] = route.T[0:8, :]
    cnt_ref[0] = jnp.broadcast_to(cnt, (8, LANES))


def _outproj(x2, ys, oa, yc, wo, fg, gm, a2, s2, wrh, wrl, br, seq, tm):
    m = x2.shape[0]
    per_b = seq // tm
    nt = m // tm
    row = lambda w: pl.BlockSpec((tm, w), lambda i: (i, 0))
    bvec = pl.BlockSpec((1, 1, D_MODEL), lambda i: (i // per_b, 0, 0))
    full = lambda shape: pl.BlockSpec(shape, lambda i: (0,) * len(shape))
    return pl.pallas_call(
        _outproj_kernel,
        grid=(nt,),
        in_specs=[
            row(D_MODEL), row(SSD_WIDTH), row(ATTN_WIDTH), row(CONV_WIDTH),
            full((D_MODEL, D_MODEL)), full((1, ATTN_WIDTH)), bvec, bvec, bvec,
            full((D_MODEL, LANES)), full((D_MODEL, LANES)), full((1, LANES)),
        ],
        out_specs=[row(D_MODEL), row(D_MODEL), row(LANES),
                   pl.BlockSpec((1, 8, tm), lambda i: (i, 0, 0)),
                   pl.BlockSpec((1, 8, LANES), lambda i: (i, 0, 0))],
        out_shape=[jax.ShapeDtypeStruct((m, D_MODEL), F32), jax.ShapeDtypeStruct((m, D_MODEL), BF16),
                   jax.ShapeDtypeStruct((m, LANES), F32), jax.ShapeDtypeStruct((nt, 8, tm), F32),
                   jax.ShapeDtypeStruct((nt, 8, LANES), F32)],
        compiler_params=_cparams(("arbitrary",)),
        name="out_proj_router",
    )(x2, ys, oa, yc, wo, fg, gm, a2, s2, wrh, wrl, br)


def _local_rows(tm):
    return -(-(2 * tm + (CHUNK - 1) * N_EXPERTS) // LANES) * LANES


def _wait_chunks(n, n_max, copy_of_size):
    for bit in range(n_max.bit_length()):
        @pl.when((lax.shift_right_logical(n, bit) & 1) == 1)
        def _():
            copy_of_size(CHUNK << bit).wait()


def _dispatch_kernel(lo_ref, gb_ref, n8_ref, tot_ref, ts_ref, tn_ref, h_ref, rtt_ref, xs_hbm,
                     xloc, zeros, sem, zsem, *, lr):
    t = pl.program_id(0)
    nt = pl.num_programs(0)
    slot = t % 2
    tm = h_ref.shape[0]

    def seg_copy(sl, src, dst):
        return pltpu.make_async_copy(xloc.at[sl, pl.ds(src, CHUNK), :], xs_hbm.at[pl.ds(dst, CHUNK), :],
                                     sem.at[sl])

    def wait_tile(tile, sl):
        _wait_chunks(tot_ref[tile], lr // CHUNK, lambda size: pltpu.make_async_copy(
            xloc.at[sl, pl.ds(0, size), :], xs_hbm.at[pl.ds(0, size), :], sem.at[sl]))

    @pl.when(t >= 2)
    def _():
        wait_tile(t - 2, slot)

    rows = _iota((lr, tm), 0).astype(F32)
    perm = jnp.where(rows == rtt_ref[0, RT_ROW1:RT_ROW1 + 1, :], 1.0,
                     jnp.where(rows == rtt_ref[0, RT_ROW2:RT_ROW2 + 1, :], 1.0, 0.0))
    xloc[slot] = jnp.dot(perm.astype(BF16), h_ref[...], preferred_element_type=F32)

    def per_expert(e, c):
        idx = t * N_EXPERTS + e
        src0 = lo_ref[idx] * CHUNK
        dst0 = gb_ref[idx] * CHUNK

        def per_chunk(k, c2):
            seg_copy(slot, pl.multiple_of(src0 + k * CHUNK, CHUNK),
                     pl.multiple_of(dst0 + k * CHUNK, CHUNK)).start()
            return c2
        lax.fori_loop(0, n8_ref[idx], per_chunk, 0)
        return c
    lax.fori_loop(0, N_EXPERTS, per_expert, 0)

    @pl.when(t == nt - 1)
    def _():
        zeros[...] = jnp.zeros(zeros.shape, F32)

        def zero_copy(dst):
            return pltpu.make_async_copy(zeros, xs_hbm.at[pl.ds(dst, CHUNK), :], zsem.at[0])

        def fill(e, c):
            dst0 = ts_ref[e] * CHUNK

            def one(k, c2):
                zero_copy(pl.multiple_of(dst0 + k * CHUNK, CHUNK)).start()
                return c2
            lax.fori_loop(0, tn_ref[e], one, 0)
            return c
        lax.fori_loop(0, N_EXPERTS + 1, fill, 0)

        def drain(e, c):
            def one(k, c2):
                zero_copy(0).wait()
                return c2
            lax.fori_loop(0, tn_ref[e], one, 0)
            return c
        lax.fori_loop(0, N_EXPERTS + 1, drain, 0)

        @pl.when(t >= 1)
        def _():
            wait_tile(t - 1, 1 - slot)
        wait_tile(t, slot)


def _dispatch(plan, h2, rtt, n_rows, tm):
    m = h2.shape[0]
    lr = _local_rows(tm)
    grid_spec = pltpu.PrefetchScalarGridSpec(
        num_scalar_prefetch=6,
        grid=(m // tm,),
        in_specs=[pl.BlockSpec((tm, D_MODEL), lambda t, *_: (t, 0)),
                  pl.BlockSpec((1, 8, tm), lambda t, *_: (t, 0, 0))],
        out_specs=pl.BlockSpec(memory_space=pl.ANY),
        scratch_shapes=[pltpu.VMEM((2, lr, D_MODEL), F32), pltpu.VMEM((CHUNK, D_MODEL), F32),
                        pltpu.SemaphoreType.DMA((2,)), pltpu.SemaphoreType.DMA((1,))],
    )
    return pl.pallas_call(
        functools.partial(_dispatch_kernel, lr=lr),
        grid_spec=grid_spec,
        out_shape=jax.ShapeDtypeStruct((n_rows, D_MODEL), F32),
        compiler_params=_cparams(("arbitrary",)),
        name="moe_dispatch",
    )(plan["lo8"], plan["gb8"], plan["n8"], plan["tot8"], plan["ts8"], plan["tn8"], h2, rtt)


def _expert_kernel(be_ref, nu_ref, x_ref, wg_ref, wu_ref, wd_ref, y_ref, wgb, wub, wdb):
    i = pl.program_id(0)

    @pl.when(i < nu_ref[0])
    def _():
        prev_e = be_ref[jnp.maximum(i - 1, 0)]

        @pl.when((i == 0) | (be_ref[i] != prev_e))
        def _():
            wgb[...] = wg_ref[0, 0].astype(BF16)
            wub[...] = wu_ref[0, 0].astype(BF16)
            wdb[...] = wd_ref[0, 0].astype(BF16)

        x = x_ref[...].astype(BF16)
        hid = _silu(jnp.dot(x, wgb[...], preferred_element_type=F32)) * jnp.dot(
            x, wub[...], preferred_element_type=F32)
        y_ref[...] = jnp.dot(hid.astype(BF16), wdb[...], preferred_element_type=F32)

    @pl.when(i >= nu_ref[0])
    def _():
        y_ref[...] = jnp.zeros(y_ref.shape, F32)


def _experts(block_e, n_used, xs, wg, wu, wd, layer):
    n_rows = xs.shape[0]
    wspec = lambda shape: pl.BlockSpec((1, 1) + shape, lambda i, be, nu: (layer, be[i], 0, 0))
    grid_spec = pltpu.PrefetchScalarGridSpec(
        num_scalar_prefetch=2,
        grid=(n_rows // MOE_BLOCK,),
        in_specs=[
            pl.BlockSpec((MOE_BLOCK, D_MODEL), lambda i, be, nu: (i, 0)),
            wspec((D_MODEL, D_EXPERT)), wspec((D_MODEL, D_EXPERT)), wspec((D_EXPERT, D_MODEL)),
        ],
        out_specs=pl.BlockSpec((MOE_BLOCK, D_MODEL), lambda i, be, nu: (i, 0)),
        scratch_shapes=[
            pltpu.VMEM((D_MODEL, D_EXPERT), BF16),
            pltpu.VMEM((D_MODEL, D_EXPERT), BF16),
            pltpu.VMEM((D_EXPERT, D_MODEL), BF16),
        ],
    )
    return pl.pallas_call(
        _expert_kernel,
        grid_spec=grid_spec,
        out_shape=jax.ShapeDtypeStruct((n_rows, D_MODEL), F32),
        compiler_params=_cparams(("arbitrary",)),
        name="moe_experts",
    )(block_e, n_used, xs, wg, wu, wd)


def _combine_kernel(lo_ref, gb_ref, n8_ref, tot_ref, x_ref, rt_ref, gf_ref, fg_ref, ys_hbm, o_ref,
                    yloc, sem, *, lr, final):
    t = pl.program_id(0)
    nt = pl.num_programs(0)
    slot = t % 2
    tm = x_ref.shape[0]

    def fetch(tile, sl):
        def per_expert(e, c):
            idx = tile * N_EXPERTS + e
            dst0 = lo_ref[idx] * CHUNK
            src0 = gb_ref[idx] * CHUNK

            def per_chunk(k, c2):
                pltpu.make_async_copy(ys_hbm.at[pl.ds(pl.multiple_of(src0 + k * CHUNK, CHUNK), CHUNK), :],
                                      yloc.at[sl, pl.ds(pl.multiple_of(dst0 + k * CHUNK, CHUNK), CHUNK), :],
                                      sem.at[sl]).start()
                return c2
            lax.fori_loop(0, n8_ref[idx], per_chunk, 0)
            return c
        lax.fori_loop(0, N_EXPERTS, per_expert, 0)

    @pl.when(t == 0)
    def _():
        yloc[...] = jnp.zeros(yloc.shape, F32)
        fetch(0, 0)

    @pl.when(t + 1 < nt)
    def _():
        fetch(t + 1, 1 - slot)

    _wait_chunks(tot_ref[t], lr // CHUNK, lambda size: pltpu.make_async_copy(
        ys_hbm.at[pl.ds(0, size), :], yloc.at[slot, pl.ds(0, size), :], sem.at[slot]))

    rt = rt_ref[...]
    ysb = yloc[slot].astype(BF16)
    cols = _iota((tm, lr), 1).astype(F32)
    pick1 = jnp.where(cols == rt[:, RT_ROW1:RT_ROW1 + 1], 1.0, 0.0).astype(BF16)
    pick2 = jnp.where(cols == rt[:, RT_ROW2:RT_ROW2 + 1], 1.0, 0.0).astype(BF16)
    y1 = jnp.dot(pick1, ysb, preferred_element_type=F32)
    y2 = jnp.dot(pick2, ysb, preferred_element_type=F32)
    moe = rt[:, RT_W1:RT_W1 + 1] * y1 + rt[:, RT_W2:RT_W2 + 1] * y2
    x = x_ref[...] + gf_ref[0] * moe
    if final:
        ms = jnp.mean(x * x, axis=-1, keepdims=True)
        x = x * lax.rsqrt(ms + EPS) * fg_ref[...]
    o_ref[...] = x


def _combine(plan, x2, ys, rt, gf, fg, seq, tm, final):
    m = x2.shape[0]
    per_b = seq // tm
    lr = _local_rows(tm)
    grid_spec = pltpu.PrefetchScalarGridSpec(
        num_scalar_prefetch=4,
        grid=(m // tm,),
        in_specs=[
            pl.BlockSpec((tm, D_MODEL), lambda i, *_: (i, 0)),
            pl.BlockSpec((tm, LANES), lambda i, *_: (i, 0)),
            pl.BlockSpec((1, 1, D_MODEL), lambda i, *_: (i // per_b, 0, 0)),
            pl.BlockSpec((1, D_MODEL), lambda i, *_: (0, 0)),
            pl.BlockSpec(memory_space=pl.ANY),
        ],
        out_specs=pl.BlockSpec((tm, D_MODEL), lambda i, *_: (i, 0)),
        scratch_shapes=[pltpu.VMEM((2, lr, D_MODEL), F32), pltpu.SemaphoreType.DMA((2,))],
    )
    return pl.pallas_call(
        functools.partial(_combine_kernel, lr=lr, final=final),
        grid_spec=grid_spec,
        out_shape=jax.ShapeDtypeStruct((m, D_MODEL), F32),
        compiler_params=_cparams(("arbitrary",)),
        name="moe_combine",
    )(plan["lo8"], plan["gb8"], plan["n8"], plan["tot8"], x2, rt, gf, fg, ys)


def _moe_plan(c, n_blocks):
    i32 = jnp.int32
    blk8 = MOE_BLOCK // CHUNK
    c8 = (c + CHUNK - 1) // CHUNK
    lo8 = jnp.cumsum(c8, axis=1) - c8
    per_e = jnp.sum(c8, axis=0)
    pad8 = (per_e + blk8 - 1) // blk8 * blk8
    end8 = jnp.cumsum(pad8)
    start8 = end8 - pad8
    gb8 = start8[None, :] + jnp.cumsum(c8, axis=0) - c8
    blk_start8 = jnp.arange(n_blocks, dtype=i32) * blk8
    block_e = jnp.minimum(jnp.sum(end8[None, :] <= blk_start8[:, None], axis=1), N_EXPERTS - 1)
    return {
        "lo8": lo8.reshape(-1).astype(i32), "gb8": gb8.reshape(-1).astype(i32),
        "n8": c8.reshape(-1).astype(i32), "tot8": jnp.sum(c8, axis=1).astype(i32),
        "ts8": jnp.concatenate([start8 + per_e, end8[-1:]]).astype(i32),
        "tn8": jnp.concatenate([pad8 - per_e, n_blocks * blk8 - end8[-1:]]).astype(i32),
        "block_e": block_e.astype(i32), "n_used": (end8[-1:] // blk8).astype(i32),
    }


def _pack_w_in(w):
    d_dt = SSD_WIDTH + XBC_WIDTH
    d_q = d_dt + SSD_HEADS
    d_f = d_q + 3 * ATTN_WIDTH
    d_ga = d_f + ATTN_HEADS
    small = jnp.concatenate([w[:, d_dt:d_q], w[:, d_f:d_ga],
                             jnp.zeros((D_MODEL, LANES - SSD_HEADS - ATTN_HEADS), w.dtype)], axis=1)
    return jnp.concatenate([w[:, SSD_WIDTH:d_dt], w[:, :SSD_WIDTH], w[:, d_q:d_f], w[:, d_ga:], small],
                           axis=1).astype(BF16)


def kernel(x, c, ada_w, ada_b, norm_mix_g, w_in, ssd_conv_w, ssd_conv_b, ssd_dt_bias, ssd_a_log,
           ssd_d, ssd_norm_g, fox_f_bias, fox_norm_g, cm_conv_w, cm_conv_b, cm_ln_g, cm_ln_b, w_out,
           norm_ffn_g, w_router_group, b_router_group, w_router_expert, b_router_expert, w_gate,
           w_up, w_down, final_norm_g):
    bsz, seq, d = x.shape
    m = bsz * seq
    tm = min(512, seq)
    tt = min(256, seq)
    n_blocks = -(-(2 * m + (CHUNK - 1) * (m // tm) * N_EXPERTS) // MOE_BLOCK) + N_EXPERTS

    mod = _modulation(c, ada_w, ada_b)
    x2 = x.reshape(m, d)
    for l in range(DEPTH):
        sh_m, sc_m, g_m, sh_f, sc_f, g_f = [v.reshape(bsz, 1, d) for v in jnp.split(mod[l], 6, axis=-1)]
        a_m = norm_mix_g[l][None, None, :] * (1.0 + sc_m)
        proj = _inproj(x2, a_m, sh_m, _pack_w_in(w_in[l]), seq, tm)
        y_ssd = _ssd(proj, ssd_conv_w[l], ssd_conv_b[l], ssd_dt_bias[l], ssd_a_log[l],
                     ssd_d[l], ssd_norm_g[l], bsz, seq, tt)
        qa, ka, va, aux = _foxprep(proj, fox_f_bias[l], bsz, seq)
        o_att = _fox(qa, ka, va, aux)
        y_cnv = _conformer(proj, cm_conv_w[l], cm_conv_b[l], cm_ln_g[l], cm_ln_b[l], bsz, seq, tt)

        w_r = jnp.concatenate([w_router_group[l], w_router_expert[l],
                               jnp.zeros((d, LANES - N_GROUPS - N_EXPERTS), F32)], axis=1)
        w_rh = w_r.astype(BF16)
        w_rl = (w_r - w_rh.astype(F32)).astype(BF16)
        b_r = jnp.concatenate([b_router_group[l], b_router_expert[l],
                               jnp.zeros((LANES - N_GROUPS - N_EXPERTS,), F32)]).reshape(1, LANES)
        a_f = norm_ffn_g[l][None, None, :] * (1.0 + sc_f)
        x2, h2, rt, rtt, cnt = _outproj(x2, y_ssd, o_att, y_cnv, w_out[l].astype(BF16),
                                        fox_norm_g[l].reshape(1, -1), g_m, a_f, sh_f, w_rh, w_rl, b_r,
                                        seq, tm)
        plan = _moe_plan(cnt[:, 0, :N_EXPERTS].astype(jnp.int32), n_blocks)
        xs = _dispatch(plan, h2, rtt, n_blocks * MOE_BLOCK, tm)
        ys = _experts(plan["block_e"], plan["n_used"], xs, w_gate, w_up, w_down, l)
        x2 = _combine(plan, x2, ys, rt, g_f, final_norm_g.reshape(1, d), seq, tm,
                      final=(l == DEPTH - 1))
    return x2.reshape(bsz, seq, d)
```

```python
import functools

import jax
import jax.numpy as jnp
import numpy as np
from jax import lax
from jax.experimental import pallas as pl
from jax.experimental.pallas import tpu as pltpu

F32 = jnp.float32
BF16 = jnp.bfloat16
HIGHEST = lax.Precision.HIGHEST

D_MODEL = 1024
DEPTH = 4
SSD_WIDTH = 512
SSD_HEADS = 8
SSD_HEAD_DIM = 64
SSD_STATE = 128
SSD_CONV = 4
SSD_CHUNK = 128
XBC_WIDTH = 1024
ATTN_WIDTH = 256
ATTN_HEADS = 4
ATTN_HEAD_DIM = 64
CONV_WIDTH = 256
CONV_KERNEL = 31
N_GROUPS = 4
PER_GROUP = 8
N_EXPERTS = 32
D_EXPERT = 512
MOE_BLOCK = 512
EPS = 1e-6

LANES = 128
COL_XBC = 0
COL_Z = 1024
COL_QKV = 1536
COL_GA = 2304
COL_GB = 2560
COL_SM = 2816
NP = 2944
SM_DT = 0
SM_F = 8

VMEM_LIMIT = 56 * 1024 * 1024


def _cparams(sem):
    return pltpu.CompilerParams(dimension_semantics=sem, vmem_limit_bytes=VMEM_LIMIT)


def _sigmoid(x):
    return 1.0 / (1.0 + jnp.exp(-x))


def _silu(x):
    return x * _sigmoid(x)


def _softplus(x):
    return jnp.maximum(x, 0.0) + jnp.log1p(jnp.exp(-jnp.abs(x)))


def _iota(shape, dim):
    return lax.broadcasted_iota(jnp.int32, shape, dim)


def _split_bf16x3(x):
    hi = x.astype(BF16)
    r1 = x - hi.astype(F32)
    mid = r1.astype(BF16)
    lo = (r1 - mid.astype(F32)).astype(BF16)
    return hi, mid, lo


def _dot_onehot_rhs(x, sel):
    return sum(jnp.dot(part, sel, preferred_element_type=F32) for part in _split_bf16x3(x))


def _dot_onehot_lhs(sel, x):
    return sum(jnp.dot(sel, part, preferred_element_type=F32) for part in _split_bf16x3(x))


def _mod_kernel(c_ref, w_ref, b_ref, o_ref):
    cond = _silu(c_ref[...])
    o_ref[0] = jnp.dot(cond, w_ref[0], precision=HIGHEST, preferred_element_type=F32) + b_ref[0]


def _modulation(c, ada_w, ada_b):
    bsz = c.shape[0]
    rows = 8
    cpad = jnp.zeros((rows, D_MODEL), F32).at[:bsz].set(c)
    tn = 1536
    n6 = 6 * D_MODEL
    out = pl.pallas_call(
        _mod_kernel,
        grid=(DEPTH, n6 // tn),
        in_specs=[
            pl.BlockSpec((rows, D_MODEL), lambda l, j: (0, 0)),
            pl.BlockSpec((1, D_MODEL, tn), lambda l, j: (l, 0, j)),
            pl.BlockSpec((1, 1, tn), lambda l, j: (l, 0, j)),
        ],
        out_specs=pl.BlockSpec((1, rows, tn), lambda l, j: (l, 0, j)),
        out_shape=jax.ShapeDtypeStruct((DEPTH, rows, n6), F32),
        compiler_params=_cparams(("arbitrary", "arbitrary")),
        name="adaln_mod",
    )(cpad, ada_w, ada_b.reshape(DEPTH, 1, n6))
    return out[:, :bsz]


def _inproj_kernel(x_ref, a_ref, s_ref, *refs):
    w_refs, o_ref = refs[:-1], refs[-1]
    x = x_ref[...]
    ms = jnp.mean(x * x, axis=-1, keepdims=True)
    h = (x * lax.rsqrt(ms + EPS) * a_ref[0] + s_ref[0]).astype(BF16)
    col = 0
    for w_ref in w_refs:
        width = w_ref.shape[-1]
        o_ref[:, col:col + width] = jnp.dot(h, w_ref[0], preferred_element_type=F32)
        col += width


def _inproj(x2, a, s, w_sections, layer, seq, tm):
    m = x2.shape[0]
    per_b = seq // tm
    assert sum(w.shape[-1] for w in w_sections) == NP
    return pl.pallas_call(
        _inproj_kernel,
        grid=(m // tm,),
        in_specs=[
            pl.BlockSpec((tm, D_MODEL), lambda i: (i, 0)),
            pl.BlockSpec((1, 1, D_MODEL), lambda i: (i // per_b, 0, 0)),
            pl.BlockSpec((1, 1, D_MODEL), lambda i: (i // per_b, 0, 0)),
        ] + [pl.BlockSpec((1, D_MODEL, w.shape[-1]), lambda i: (layer, 0, 0)) for w in w_sections],
        out_specs=pl.BlockSpec((tm, NP), lambda i: (i, 0)),
        out_shape=jax.ShapeDtypeStruct((m, NP), F32),
        compiler_params=_cparams(("arbitrary",)),
        name="in_proj",
    )(x2, a, s, *w_sections)


def _ssd_kernel(z_ref, xbc_ref, sm_ref, cw_ref, cb_ref, dtb_ref, alog_ref, dtbt_ref,
                alogt_ref, e_ref, dx_ref, ng_ref, y_ref, xpad, xc, prev, *, tt):
    t = pl.program_id(1)

    @pl.when(t == 0)
    def _():
        xpad[0:8, :] = jnp.zeros((8, XBC_WIDTH), F32)
        prev[...] = jnp.zeros(prev.shape, F32)

    xpad[8:8 + tt, :] = xbc_ref[...]
    acc = jnp.broadcast_to(cb_ref[...], (tt, XBC_WIDTH))
    for k in range(SSD_CONV):
        off = 8 - (SSD_CONV - 1) + k
        acc = acc + cw_ref[k:k + 1, :] * xpad[off:off + tt, :]
    xc[...] = _silu(acc)
    xpad[0:8, :] = xpad[tt:tt + 8, :]

    cl = SSD_CHUNK
    row = _iota((cl, cl), 0)
    col = _iota((cl, cl), 1)
    causal = row >= col
    tril = jnp.where(causal, 1.0, 0.0).astype(BF16)
    triu = jnp.where(row <= col, 1.0, 0.0).astype(BF16)
    lo = col < SSD_HEAD_DIM
    lane1 = _iota((1, LANES), 1)
    a_row = jnp.where(lane1 < SSD_HEADS, -jnp.exp(alog_ref[...]), 0.0)
    a_col = -jnp.exp(alogt_ref[...])
    expand = e_ref[...]

    def chunk(c, carry):
        r0 = pl.multiple_of(c * cl, cl)
        xs = xc[pl.ds(r0, cl), 0:SSD_WIDTH]
        bmat = xc[pl.ds(r0, cl), SSD_WIDTH:SSD_WIDTH + 2 * SSD_STATE]
        cmat = xc[pl.ds(r0, cl), SSD_WIDTH + 2 * SSD_STATE:XBC_WIDTH]
        sm = sm_ref[pl.ds(r0, cl), :]
        dt = _softplus(sm + dtb_ref[...])
        da = dt * a_row
        acs = _dot_onehot_lhs(tril, da)
        dtt = _softplus(sm.T[0:SSD_HEADS, :] + dtbt_ref[...])
        acst = _dot_onehot_rhs(dtt * a_col, triu)
        dt_x = _dot_onehot_rhs(dt, expand)
        acs_x = _dot_onehot_rhs(acs, expand)
        last = acs_x[cl - 1:cl, :]
        eacs_x = jnp.exp(acs_x)
        dte_x = jnp.exp(last - acs_x)
        cd_x = jnp.exp(last)
        xdt = xs * dt_x
        xdte = (xdt * dte_x).astype(BF16)
        zc = z_ref[pl.ds(r0, cl), :]
        for g in range(2):
            bg = bmat[:, g * SSD_STATE:(g + 1) * SSD_STATE]
            cg = cmat[:, g * SSD_STATE:(g + 1) * SSD_STATE].astype(BF16)
            bgt = bg.T.astype(BF16)
            cbm = jnp.dot(cg, bgt, preferred_element_type=F32)
            pair_out = []
            for j in range(2):
                p = 2 * g + j
                sl = slice(p * LANES, (p + 1) * LANES)
                xp = xdt[:, sl]
                yd = jnp.zeros((cl, LANES), F32)
                for half in range(2):
                    h = 2 * p + half
                    seg = acs[:, h:h + 1] - acst[h:h + 1, :]
                    dec = jnp.exp(jnp.where(causal, seg, -jnp.inf))
                    gm = (cbm * dec).astype(BF16)
                    own = lo if half == 0 else jnp.logical_not(lo)
                    xm = jnp.where(own, xp, 0.0).astype(BF16)
                    yd = yd + jnp.dot(gm, xm, preferred_element_type=F32)
                prev_p = prev[:, sl]
                yo = jnp.dot(cg, prev_p.astype(BF16), preferred_element_type=F32) * eacs_x[:, sl]
                st = jnp.dot(bgt, xdte[:, sl], preferred_element_type=F32)
                prev[:, sl] = prev_p * cd_x[:, sl] + st
                pair_out.append(yd + yo + xs[:, sl] * dx_ref[:, sl])
            gs = slice(g * 2 * LANES, (g + 1) * 2 * LANES)
            yg = jnp.concatenate(pair_out, axis=-1) * _silu(zc[:, gs])
            ms = jnp.mean(yg * yg, axis=-1, keepdims=True)
            y_ref[pl.ds(r0, cl), gs] = yg * lax.rsqrt(ms + EPS) * ng_ref[:, gs]
        return carry

    lax.fori_loop(0, tt // cl, chunk, 0)


def _ssd(proj, cw, cb, dtb, alog, dx, ng, bsz, seq, tt):
    m = bsz * seq
    nt = seq // tt
    pad = LANES - SSD_HEADS
    dtb_row = jnp.pad(dtb, (0, pad)).reshape(1, LANES)
    alog_row = jnp.pad(alog, (0, pad)).reshape(1, LANES)
    dtb_col = jnp.broadcast_to(dtb[:, None], (SSD_HEADS, SSD_CHUNK))
    alog_col = jnp.broadcast_to(alog[:, None], (SSD_HEADS, SSD_CHUNK))
    expand = np.zeros((LANES, SSD_WIDTH), np.float32)
    for h in range(SSD_HEADS):
        expand[h, h * SSD_HEAD_DIM:(h + 1) * SSD_HEAD_DIM] = 1.0
    dx_row = jnp.repeat(dx, SSD_HEAD_DIM).reshape(1, SSD_WIDTH)
    full = lambda shape: pl.BlockSpec(shape, lambda b, t: (0,) * len(shape))
    return pl.pallas_call(
        functools.partial(_ssd_kernel, tt=tt),
        grid=(bsz, nt),
        in_specs=[
            pl.BlockSpec((tt, SSD_WIDTH), lambda b, t: (b * nt + t, COL_Z // SSD_WIDTH)),
            pl.BlockSpec((tt, XBC_WIDTH), lambda b, t: (b * nt + t, COL_XBC // XBC_WIDTH + 0)),
            pl.BlockSpec((tt, LANES), lambda b, t: (b * nt + t, COL_SM // LANES)),
            full((SSD_CONV, XBC_WIDTH)),
            full((1, XBC_WIDTH)),
            full((1, LANES)),
            full((1, LANES)),
            full((SSD_HEADS, SSD_CHUNK)),
            full((SSD_HEADS, SSD_CHUNK)),
            full((LANES, SSD_WIDTH)),
            full((1, SSD_WIDTH)),
            full((1, SSD_WIDTH)),
        ],
        out_specs=pl.BlockSpec((tt, SSD_WIDTH), lambda b, t: (b * nt + t, 0)),
        out_shape=jax.ShapeDtypeStruct((m, SSD_WIDTH), F32),
        scratch_shapes=[
            pltpu.VMEM((tt + 8, XBC_WIDTH), F32),
            pltpu.VMEM((tt, XBC_WIDTH), F32),
            pltpu.VMEM((SSD_STATE, SSD_WIDTH), F32),
        ],
        compiler_params=_cparams(("arbitrary", "arbitrary")),
        name="ssd_scan",
    )(proj, proj, proj, cw, cb.reshape(1, XBC_WIDTH), dtb_row, alog_row, dtb_col, alog_col,
      jnp.asarray(expand, dtype=BF16), dx_row, ng.reshape(1, SSD_WIDTH))


LOG2E = 1.4426950408889634
ATT_BLOCK = 256
PRUNE_LOG2 = 140.0
FAST_GAP_LOG2 = 60.0
AUX_CUMEND = 0
AUX_KMAX = 1


def _foxprep_kernel(qkv_ref, sm_ref, fb_ref, q_out, k_out, v_out, aux_out, carry, *, tt):
    t = pl.program_id(1)

    @pl.when(t == 0)
    def _():
        carry[...] = jnp.zeros(carry.shape, F32)
        aux_out[...] = jnp.zeros(aux_out.shape, F32)

    logit = sm_ref[...] + fb_ref[...]
    logf = -_softplus(-logit)
    row = _iota((tt, tt), 0)
    col = _iota((tt, tt), 1)
    tril = jnp.where(row >= col, 1.0, 0.0).astype(BF16)
    cum = _dot_onehot_lhs(tril, logf) + carry[...]
    carry[...] = cum[tt - 1:tt, :]

    lane = _iota((tt, LANES), 1)
    lane1 = _iota((1, LANES), 1)
    scale = ATTN_HEAD_DIM ** -0.5 * LOG2E
    for h in range(ATTN_HEADS):
        pair, half = h // 2, h % 2
        own = (lane < ATTN_HEAD_DIM) if half == 0 else (lane >= ATTN_HEAD_DIM)
        a0 = ATTN_HEAD_DIM * (1 - half)
        cs = jnp.broadcast_to(cum[:, SM_F + h:SM_F + h + 1], (tt, LANES)) * LOG2E
        hi = cs.astype(BF16).astype(F32)
        r1 = cs - hi
        mid = r1.astype(BF16).astype(F32)
        low = r1 - mid
        qp = qkv_ref[:, pair * LANES:(pair + 1) * LANES]
        kp = qkv_ref[:, ATTN_WIDTH + pair * LANES:ATTN_WIDTH + (pair + 1) * LANES]
        vp = qkv_ref[:, 2 * ATTN_WIDTH + pair * LANES:2 * ATTN_WIDTH + (pair + 1) * LANES]
        qa = jnp.where(lane == a0, hi, jnp.where(lane == a0 + 1, mid, jnp.where(
            lane == a0 + 2, low, jnp.where((lane >= a0 + 3) & (lane < a0 + 6), 1.0, 0.0))))
        ka = jnp.where(lane == a0 + 3, -hi, jnp.where(lane == a0 + 4, -mid, jnp.where(
            lane == a0 + 5, -low, jnp.where((lane >= a0) & (lane < a0 + 9), 1.0, 0.0))))
        q_out[0, h] = jnp.where(own, qp * scale, qa).astype(BF16)
        kb = jnp.where(own, kp, ka).astype(BF16)
        k_out[0, h] = kb
        v_out[0, h] = jnp.where(own, vp, jnp.where(lane == a0, 1.0, 0.0)).astype(BF16)
        kf = jnp.where(own, kb.astype(F32), 0.0)
        kn2 = jnp.max(jnp.sum(kf * kf, axis=-1, keepdims=True), axis=0, keepdims=True)
        here = lane1 == t
        aux_out[0, h, AUX_CUMEND:AUX_CUMEND + 1, :] = jnp.where(
            here, cs[tt - 1:tt, :], aux_out[0, h, AUX_CUMEND:AUX_CUMEND + 1, :])
        aux_out[0, h, AUX_KMAX:AUX_KMAX + 1, :] = jnp.where(
            here, jnp.sqrt(kn2), aux_out[0, h, AUX_KMAX:AUX_KMAX + 1, :])


def _foxprep(proj, fb, bsz, seq):
    tt = ATT_BLOCK
    nt = seq // tt
    assert nt <= LANES
    fb_row = jnp.zeros((1, LANES), F32).at[0, SM_F:SM_F + ATTN_HEADS].set(fb)
    shp = jax.ShapeDtypeStruct((bsz, ATTN_HEADS, seq, LANES), BF16)
    ospec = pl.BlockSpec((1, ATTN_HEADS, tt, LANES), lambda b, t: (b, 0, t, 0))
    return pl.pallas_call(
        functools.partial(_foxprep_kernel, tt=tt),
        grid=(bsz, nt),
        in_specs=[
            pl.BlockSpec((tt, 3 * ATTN_WIDTH), lambda b, t: (b * nt + t, COL_QKV // (3 * ATTN_WIDTH))),
            pl.BlockSpec((tt, LANES), lambda b, t: (b * nt + t, COL_SM // LANES)),
            pl.BlockSpec((1, LANES), lambda b, t: (0, 0)),
        ],
        out_specs=[ospec, ospec, ospec,
                   pl.BlockSpec((1, ATTN_HEADS, 8, LANES), lambda b, t: (b, 0, 0, 0))],
        out_shape=[shp, shp, shp, jax.ShapeDtypeStruct((bsz, ATTN_HEADS, 8, LANES), F32)],
        scratch_shapes=[pltpu.VMEM((1, LANES), F32)],
        compiler_params=_cparams(("arbitrary", "arbitrary")),
        name="fox_prep",
    )(proj, proj, fb_row)


def _fox_kernel(q_ref, k_ref, v_ref, aux_ref, o_ref, *, tq):
    i = pl.program_id(2)
    nt = (((1,), (1,)), ((), ()))
    lane = _iota((tq, LANES), 1)
    lane1 = _iota((1, LANES), 1)
    qs = (q_ref[0, 0], q_ref[0, 1])

    def step(hh, r0, m, acc, mask):
        kblk = k_ref[0, hh, pl.ds(r0, tq), :]
        vblk = v_ref[0, hh, pl.ds(r0, tq), :]
        s = lax.dot_general(qs[hh], kblk, nt, preferred_element_type=F32)
        if mask is not None:
            s = jnp.where(mask, s, -jnp.inf)
        m_new = jnp.maximum(m, jnp.max(s, axis=-1, keepdims=True))
        p = jnp.exp2(s - m_new)
        alpha = jnp.exp2(m - m_new)
        acc = acc * alpha + jnp.dot(p.astype(BF16), vblk, preferred_element_type=F32)
        return m_new, acc

    d0 = pl.multiple_of(i * tq, tq)
    diag = _iota((tq, tq), 0) >= _iota((tq, tq), 1)
    m_init = jnp.full((tq, 1), -jnp.inf, F32)
    acc_init = jnp.zeros((tq, LANES), F32)
    i_f = i.astype(F32)
    carry = []
    first = []
    gaps = []
    shifted = []
    for hh in range(2):
        m, acc = step(hh, d0, m_init, acc_init, diag)
        carry += [m, acc]
        own = (lane < ATTN_HEAD_DIM) if hh == 0 else (lane >= ATTN_HEAD_DIM)
        a0 = ATTN_HEAD_DIM * (1 - hh)
        qf = qs[hh].astype(F32)
        qn = jnp.sqrt(jnp.sum(jnp.where(own, qf * qf, 0.0), axis=-1, keepdims=True))
        cum_t = jnp.sum(jnp.where((lane >= a0) & (lane < a0 + 3), qf, 0.0), axis=-1, keepdims=True)
        aux = aux_ref[0, hh]
        cend = aux[AUX_CUMEND:AUX_CUMEND + 1, :]
        kmax = jnp.max(jnp.where(lane1 <= i, aux[AUX_KMAX:AUX_KMAX + 1, :], 0.0), axis=-1, keepdims=True)
        slack = jnp.max(qn * kmax + cum_t - m, axis=0, keepdims=True)
        live = (lane1 < i) & (slack - cend > -PRUNE_LOG2)
        first.append(jnp.min(jnp.where(live, lane1.astype(F32), i_f)))
        cend_prev = jnp.sum(jnp.where(lane1 == i - 1, cend, 0.0), axis=-1, keepdims=True)
        shift = jnp.maximum(m, qn * kmax + cum_t - cend_prev)
        s_hi = shift.astype(BF16).astype(F32)
        s_r = shift - s_hi
        s_mid = s_r.astype(BF16).astype(F32)
        s_lo = (s_r - s_mid).astype(BF16).astype(F32)
        shift = s_hi + s_mid + s_lo
        gaps.append(jnp.max(shift - m))
        q_sh = jnp.where(lane == a0 + 6, -s_hi, jnp.where(lane == a0 + 7, -s_mid, jnp.where(
            lane == a0 + 8, -s_lo, qf))).astype(BF16)
        shifted.append((q_sh, acc * jnp.exp2(m - shift)))
    j_first = jnp.minimum(first[0], first[1]).astype(jnp.int32)

    def fast(_):
        def blocks(c, r0, width):
            out = []
            for hh in range(2):
                s = lax.dot_general(shifted[hh][0], k_ref[0, hh, pl.ds(r0, width), :], nt,
                                    preferred_element_type=F32)
                out.append(c[hh] + jnp.dot(jnp.exp2(s).astype(BF16), v_ref[0, hh, pl.ds(r0, width), :],
                                           preferred_element_type=F32))
            return tuple(out)

        n = i - j_first
        odd = n % 2
        c = lax.cond(odd == 1, lambda c: blocks(c, pl.multiple_of(j_first * tq, tq), tq), lambda c: c,
                     (shifted[0][1], shifted[1][1]))
        j0 = j_first + odd

        def body(jj, c):
            return blocks(c, pl.multiple_of((j0 + 2 * jj) * tq, tq), 2 * tq)
        return lax.fori_loop(0, (n - odd) // 2, body, c)

    def online(_):
        def body(j, c):
            r0 = pl.multiple_of(j * tq, tq)
            m0, a0_, m1, a1_ = c
            m0, a0_ = step(0, r0, m0, a0_, None)
            m1, a1_ = step(1, r0, m1, a1_, None)
            return m0, a0_, m1, a1_
        _, a0_, _, a1_ = lax.fori_loop(j_first, i, body, tuple(carry))
        return a0_, a1_

    acc0, acc1 = lax.cond(jnp.maximum(gaps[0], gaps[1]) <= FAST_GAP_LOG2, fast, online, None)
    den0 = jnp.sum(jnp.where(lane == ATTN_HEAD_DIM, acc0, 0.0), axis=-1, keepdims=True)
    den1 = jnp.sum(jnp.where(lane == 0, acc1, 0.0), axis=-1, keepdims=True)
    o_ref[...] = jnp.where(lane < ATTN_HEAD_DIM, acc0 / den0, acc1 / den1)


def _fox(qa, ka, va, aux):
    bsz, nh, seq, _ = qa.shape
    tq = ATT_BLOCK
    nq = seq // tq
    kv_spec = pl.BlockSpec((1, 2, seq, LANES), lambda b, p, i: (b, p, 0, 0))
    return pl.pallas_call(
        functools.partial(_fox_kernel, tq=tq),
        grid=(bsz, nh // 2, nq),
        in_specs=[pl.BlockSpec((1, 2, tq, LANES), lambda b, p, i: (b, p, i, 0)), kv_spec, kv_spec,
                  pl.BlockSpec((1, 2, 8, LANES), lambda b, p, i: (b, p, 0, 0))],
        out_specs=pl.BlockSpec((tq, LANES), lambda b, p, i: (b * nq + i, p)),
        out_shape=jax.ShapeDtypeStruct((bsz * seq, ATTN_WIDTH), F32),
        compiler_params=_cparams(("arbitrary", "arbitrary", "arbitrary")),
        name="fox_attn",
    )(qa, ka, va, aux)


CONF_HALO = 32
CONF_ROWS = 64


def _conf_kernel(ga_ref, gb_ref, w_ref, b_ref, lg_ref, lb_ref, y_ref, upad, *, tt):
    t = pl.program_id(1)

    @pl.when(t == 0)
    def _():
        upad[0:CONF_HALO, :] = jnp.zeros((CONF_HALO, CONV_WIDTH), F32)

    upad[CONF_HALO:CONF_HALO + tt, :] = ga_ref[...] * _sigmoid(gb_ref[...])
    base = CONF_HALO - (CONV_KERNEL - 1)
    for r in range(tt // CONF_ROWS):
        r0 = r * CONF_ROWS
        acc = jnp.broadcast_to(b_ref[...], (CONF_ROWS, CONV_WIDTH))
        for k in range(CONV_KERNEL):
            acc = acc + w_ref[k:k + 1, :] * upad[r0 + base + k:r0 + base + k + CONF_ROWS, :]
        mu = jnp.mean(acc, axis=-1, keepdims=True)
        cen = acc - mu
        var = jnp.mean(cen * cen, axis=-1, keepdims=True)
        y = cen * lax.rsqrt(var + EPS) * lg_ref[...] + lb_ref[...]
        y_ref[r0:r0 + CONF_ROWS, :] = _silu(y)
    upad[0:CONF_HALO, :] = upad[tt:tt + CONF_HALO, :]


def _conformer(proj, w, b, lg, lb, bsz, seq, tt):
    m = bsz * seq
    nt = seq // tt
    full = lambda shape: pl.BlockSpec(shape, lambda bb, t: (0,) * len(shape))
    return pl.pallas_call(
        functools.partial(_conf_kernel, tt=tt),
        grid=(bsz, nt),
        in_specs=[
            pl.BlockSpec((tt, CONV_WIDTH), lambda bb, t: (bb * nt + t, COL_GA // CONV_WIDTH)),
            pl.BlockSpec((tt, CONV_WIDTH), lambda bb, t: (bb * nt + t, COL_GB // CONV_WIDTH)),
            full((CONV_KERNEL, CONV_WIDTH)),
            full((1, CONV_WIDTH)),
            full((1, CONV_WIDTH)),
            full((1, CONV_WIDTH)),
        ],
        out_specs=pl.BlockSpec((tt, CONV_WIDTH), lambda bb, t: (bb * nt + t, 0)),
        out_shape=jax.ShapeDtypeStruct((m, CONV_WIDTH), F32),
        scratch_shapes=[pltpu.VMEM((tt + CONF_HALO, CONV_WIDTH), F32)],
        compiler_params=_cparams(("arbitrary", "arbitrary")),
        name="conformer_conv",
    )(proj, proj, w, b.reshape(1, -1), lg.reshape(1, -1), lb.reshape(1, -1))


ROUTE_BIG = 1e9
CHUNK = 8
RT_W1, RT_W2, RT_ROW1, RT_ROW2 = 0, 1, 2, 3


def _split3(x):
    hi = x.astype(BF16)
    lo = (x - hi.astype(F32)).astype(BF16)
    return hi, lo


def _outproj_kernel(x_ref, ys_ref, oa_ref, yc_ref, wo_ref, fg_ref, gm_ref, a2_ref, s2_ref,
                    wrh_ref, wrl_ref, br_ref, xn_ref, h2_ref, rt_ref, rtt_ref, cnt_ref):
    tm = x_ref.shape[0]
    lane = _iota((tm, LANES), 1)
    att = oa_ref[...]
    ms = jnp.mean(att * att, axis=-1, keepdims=True)
    att = att * lax.rsqrt(ms + EPS) * fg_ref[...]
    y = jnp.dot(ys_ref[...].astype(BF16), wo_ref[0:SSD_WIDTH, :], preferred_element_type=F32)
    y = y + jnp.dot(att.astype(BF16), wo_ref[SSD_WIDTH:SSD_WIDTH + ATTN_WIDTH, :],
                    preferred_element_type=F32)
    y = y + jnp.dot(yc_ref[...].astype(BF16), wo_ref[SSD_WIDTH + ATTN_WIDTH:, :],
                    preferred_element_type=F32)
    xn = x_ref[...] + gm_ref[0] * y
    xn_ref[...] = xn
    ms2 = jnp.mean(xn * xn, axis=-1, keepdims=True)
    h2 = xn * lax.rsqrt(ms2 + EPS) * a2_ref[0] + s2_ref[0]
    h2_ref[...] = h2.astype(BF16)

    hh, hl = _split3(h2)
    logits = (jnp.dot(hh, wrh_ref[...], preferred_element_type=F32)
              + jnp.dot(hl, wrh_ref[...], preferred_element_type=F32)
              + jnp.dot(hh, wrl_ref[...], preferred_element_type=F32)) + br_ref[...]
    lanef = lane.astype(F32)
    lg = jnp.where(lane < N_GROUPS, logits, -jnp.inf)
    gmax = jnp.max(lg, axis=-1, keepdims=True)
    gsum = jnp.sum(jnp.exp(lg - gmax), axis=-1, keepdims=True)
    gidx = jnp.min(jnp.where(lg == gmax, lanef, ROUTE_BIG), axis=-1, keepdims=True)
    e_lo = N_GROUPS + PER_GROUP * gidx
    le = jnp.where((lanef >= e_lo) & (lanef < e_lo + PER_GROUP), logits, -jnp.inf)
    m1 = jnp.max(le, axis=-1, keepdims=True)
    i1 = jnp.min(jnp.where(le == m1, lanef, ROUTE_BIG), axis=-1, keepdims=True)
    le2 = jnp.where(lanef == i1, -jnp.inf, le)
    m2 = jnp.max(le2, axis=-1, keepdims=True)
    i2 = jnp.min(jnp.where(le2 == m2, lanef, ROUTE_BIG), axis=-1, keepdims=True)
    esum = jnp.sum(jnp.exp(le - m1), axis=-1, keepdims=True)
    p1 = 1.0 / esum
    p2 = jnp.exp(m2 - m1) / esum
    psel = 1.0 / gsum
    w1 = p1 / (p1 + p2) * psel
    w2 = p2 / (p1 + p2) * psel

    oh1 = (lanef == (i1 - N_GROUPS)).astype(F32)
    oh2 = (lanef == (i2 - N_GROUPS)).astype(F32)
    oh = oh1 + oh2
    cnt = jnp.sum(oh, axis=0, keepdims=True)
    before = (_iota((tm, tm), 0) > _iota((tm, tm), 1)).astype(BF16)
    rank = jnp.dot(before, oh.astype(BF16), preferred_element_type=F32)
    chunks = jnp.floor((cnt + (CHUNK - 1)) * (1.0 / CHUNK))
    below = (_iota((LANES, LANES), 0) < _iota((LANES, LANES), 1)).astype(BF16)
    seg_lo = CHUNK * jnp.dot(jnp.broadcast_to(chunks, (8, LANES)).astype(BF16), below,
                             preferred_element_type=F32)[0:1, :]
    lr1 = jnp.sum(oh1 * (rank + seg_lo), axis=-1, keepdims=True)
    lr2 = jnp.sum(oh2 * (rank + seg_lo), axis=-1, keepdims=True)
    route = jnp.where(lane == RT_W1, w1, jnp.where(lane == RT_W2, w2, jnp.where(
        lane == RT_ROW1, lr1, jnp.where(lane == RT_ROW2, lr2, 0.0))))
    rt_ref[...] = route
    rtt_ref[0] = route.T[0:8, :]
    cnt_ref[0] = jnp.broadcast_to(cnt, (8, LANES))


def _outproj(x2, ys, oa, yc, wo, fg, gm, a2, s2, wrh, wrl, br, seq, tm):
    m = x2.shape[0]
    per_b = seq // tm
    nt = m // tm
    row = lambda w: pl.BlockSpec((tm, w), lambda i: (i, 0))
    bvec = pl.BlockSpec((1, 1, D_MODEL), lambda i: (i // per_b, 0, 0))
    full = lambda shape: pl.BlockSpec(shape, lambda i: (0,) * len(shape))
    return pl.pallas_call(
        _outproj_kernel,
        grid=(nt,),
        in_specs=[
            row(D_MODEL), row(SSD_WIDTH), row(ATTN_WIDTH), row(CONV_WIDTH),
            full((D_MODEL, D_MODEL)), full((1, ATTN_WIDTH)), bvec, bvec, bvec,
            full((D_MODEL, LANES)), full((D_MODEL, LANES)), full((1, LANES)),
        ],
        out_specs=[row(D_MODEL), row(D_MODEL), row(LANES),
                   pl.BlockSpec((1, 8, tm), lambda i: (i, 0, 0)),
                   pl.BlockSpec((1, 8, LANES), lambda i: (i, 0, 0))],
        out_shape=[jax.ShapeDtypeStruct((m, D_MODEL), F32), jax.ShapeDtypeStruct((m, D_MODEL), BF16),
                   jax.ShapeDtypeStruct((m, LANES), F32), jax.ShapeDtypeStruct((nt, 8, tm), F32),
                   jax.ShapeDtypeStruct((nt, 8, LANES), F32)],
        compiler_params=_cparams(("arbitrary",)),
        name="out_proj_router",
    )(x2, ys, oa, yc, wo, fg, gm, a2, s2, wrh, wrl, br)


def _local_rows(tm):
    return -(-(2 * tm + (CHUNK - 1) * N_EXPERTS) // LANES) * LANES


def _wait_chunks(n, n_max, copy_of_size):
    for bit in range(n_max.bit_length()):
        @pl.when((lax.shift_right_logical(n, bit) & 1) == 1)
        def _():
            copy_of_size(CHUNK << bit).wait()


def _dispatch_kernel(lo_ref, gb_ref, n8_ref, tot_ref, ts_ref, tn_ref, h_ref, rtt_ref, xs_hbm,
                     xloc, zeros, sem, zsem, *, lr):
    t = pl.program_id(0)
    nt = pl.num_programs(0)
    slot = t % 2
    tm = h_ref.shape[0]

    def seg_copy(sl, src, dst):
        return pltpu.make_async_copy(xloc.at[sl, pl.ds(src, CHUNK), :], xs_hbm.at[pl.ds(dst, CHUNK), :],
                                     sem.at[sl])

    def wait_tile(tile, sl):
        _wait_chunks(tot_ref[tile], lr // CHUNK, lambda size: pltpu.make_async_copy(
            xloc.at[sl, pl.ds(0, size), :], xs_hbm.at[pl.ds(0, size), :], sem.at[sl]))

    @pl.when(t >= 2)
    def _():
        wait_tile(t - 2, slot)

    rows = _iota((lr, tm), 0).astype(F32)
    perm = jnp.where(rows == rtt_ref[0, RT_ROW1:RT_ROW1 + 1, :], 1.0,
                     jnp.where(rows == rtt_ref[0, RT_ROW2:RT_ROW2 + 1, :], 1.0, 0.0))
    xloc[slot] = jnp.dot(perm.astype(BF16), h_ref[...], preferred_element_type=F32)

    def per_expert(e, c):
        idx = t * N_EXPERTS + e
        src0 = lo_ref[idx] * CHUNK
        dst0 = gb_ref[idx] * CHUNK

        def per_chunk(k, c2):
            seg_copy(slot, pl.multiple_of(src0 + k * CHUNK, CHUNK),
                     pl.multiple_of(dst0 + k * CHUNK, CHUNK)).start()
            return c2
        lax.fori_loop(0, n8_ref[idx], per_chunk, 0)
        return c
    lax.fori_loop(0, N_EXPERTS, per_expert, 0)

    @pl.when(t == nt - 1)
    def _():
        zeros[...] = jnp.zeros(zeros.shape, F32)

        def zero_copy(dst):
            return pltpu.make_async_copy(zeros, xs_hbm.at[pl.ds(dst, CHUNK), :], zsem.at[0])

        def fill(e, c):
            dst0 = ts_ref[e] * CHUNK

            def one(k, c2):
                zero_copy(pl.multiple_of(dst0 + k * CHUNK, CHUNK)).start()
                return c2
            lax.fori_loop(0, tn_ref[e], one, 0)
            return c
        lax.fori_loop(0, N_EXPERTS + 1, fill, 0)

        def drain(e, c):
            def one(k, c2):
                zero_copy(0).wait()
                return c2
            lax.fori_loop(0, tn_ref[e], one, 0)
            return c
        lax.fori_loop(0, N_EXPERTS + 1, drain, 0)

        @pl.when(t >= 1)
        def _():
            wait_tile(t - 1, 1 - slot)
        wait_tile(t, slot)


def _dispatch(plan, h2, rtt, n_rows, tm):
    m = h2.shape[0]
    lr = _local_rows(tm)
    grid_spec = pltpu.PrefetchScalarGridSpec(
        num_scalar_prefetch=6,
        grid=(m // tm,),
        in_specs=[pl.BlockSpec((tm, D_MODEL), lambda t, *_: (t, 0)),
                  pl.BlockSpec((1, 8, tm), lambda t, *_: (t, 0, 0))],
        out_specs=pl.BlockSpec(memory_space=pl.ANY),
        scratch_shapes=[pltpu.VMEM((2, lr, D_MODEL), F32), pltpu.VMEM((CHUNK, D_MODEL), F32),
                        pltpu.SemaphoreType.DMA((2,)), pltpu.SemaphoreType.DMA((1,))],
    )
    return pl.pallas_call(
        functools.partial(_dispatch_kernel, lr=lr),
        grid_spec=grid_spec,
        out_shape=jax.ShapeDtypeStruct((n_rows, D_MODEL), F32),
        compiler_params=_cparams(("arbitrary",)),
        name="moe_dispatch",
    )(plan["lo8"], plan["gb8"], plan["n8"], plan["tot8"], plan["ts8"], plan["tn8"], h2, rtt)


def _expert_kernel(be_ref, nu_ref, x_ref, wg_ref, wu_ref, wd_ref, y_ref, wgb, wub, wdb):
    i = pl.program_id(0)

    @pl.when(i < nu_ref[0])
    def _():
        prev_e = be_ref[jnp.maximum(i - 1, 0)]

        @pl.when((i == 0) | (be_ref[i] != prev_e))
        def _():
            wgb[...] = wg_ref[0, 0].astype(BF16)
            wub[...] = wu_ref[0, 0].astype(BF16)
            wdb[...] = wd_ref[0, 0].astype(BF16)

        x = x_ref[...].astype(BF16)
        hid = _silu(jnp.dot(x, wgb[...], preferred_element_type=F32)) * jnp.dot(
            x, wub[...], preferred_element_type=F32)
        y_ref[...] = jnp.dot(hid.astype(BF16), wdb[...], preferred_element_type=F32)

    @pl.when(i >= nu_ref[0])
    def _():
        y_ref[...] = jnp.zeros(y_ref.shape, F32)


def _experts(block_e, n_used, xs, wg, wu, wd, layer):
    n_rows = xs.shape[0]
    wspec = lambda shape: pl.BlockSpec((1, 1) + shape, lambda i, be, nu: (layer, be[i], 0, 0))
    grid_spec = pltpu.PrefetchScalarGridSpec(
        num_scalar_prefetch=2,
        grid=(n_rows // MOE_BLOCK,),
        in_specs=[
            pl.BlockSpec((MOE_BLOCK, D_MODEL), lambda i, be, nu: (i, 0)),
            wspec((D_MODEL, D_EXPERT)), wspec((D_MODEL, D_EXPERT)), wspec((D_EXPERT, D_MODEL)),
        ],
        out_specs=pl.BlockSpec((MOE_BLOCK, D_MODEL), lambda i, be, nu: (i, 0)),
        scratch_shapes=[
            pltpu.VMEM((D_MODEL, D_EXPERT), BF16),
            pltpu.VMEM((D_MODEL, D_EXPERT), BF16),
            pltpu.VMEM((D_EXPERT, D_MODEL), BF16),
        ],
    )
    return pl.pallas_call(
        _expert_kernel,
        grid_spec=grid_spec,
        out_shape=jax.ShapeDtypeStruct((n_rows, D_MODEL), F32),
        compiler_params=_cparams(("arbitrary",)),
        name="moe_experts",
    )(block_e, n_used, xs, wg, wu, wd)


def _combine_kernel(lo_ref, gb_ref, n8_ref, tot_ref, x_ref, rt_ref, gf_ref, fg_ref, ys_hbm, o_ref,
                    yloc, sem, *, lr, final):
    t = pl.program_id(0)
    nt = pl.num_programs(0)
    slot = t % 2
    tm = x_ref.shape[0]

    def fetch(tile, sl):
        def per_expert(e, c):
            idx = tile * N_EXPERTS + e
            dst0 = lo_ref[idx] * CHUNK
            src0 = gb_ref[idx] * CHUNK

            def per_chunk(k, c2):
                pltpu.make_async_copy(ys_hbm.at[pl.ds(pl.multiple_of(src0 + k * CHUNK, CHUNK), CHUNK), :],
                                      yloc.at[sl, pl.ds(pl.multiple_of(dst0 + k * CHUNK, CHUNK), CHUNK), :],
                                      sem.at[sl]).start()
                return c2
            lax.fori_loop(0, n8_ref[idx], per_chunk, 0)
            return c
        lax.fori_loop(0, N_EXPERTS, per_expert, 0)

    @pl.when(t == 0)
    def _():
        yloc[...] = jnp.zeros(yloc.shape, F32)
        fetch(0, 0)

    @pl.when(t + 1 < nt)
    def _():
        fetch(t + 1, 1 - slot)

    _wait_chunks(tot_ref[t], lr // CHUNK, lambda size: pltpu.make_async_copy(
        ys_hbm.at[pl.ds(0, size), :], yloc.at[slot, pl.ds(0, size), :], sem.at[slot]))

    rt = rt_ref[...]
    ysb = yloc[slot].astype(BF16)
    cols = _iota((tm, lr), 1).astype(F32)
    pick1 = jnp.where(cols == rt[:, RT_ROW1:RT_ROW1 + 1], 1.0, 0.0).astype(BF16)
    pick2 = jnp.where(cols == rt[:, RT_ROW2:RT_ROW2 + 1], 1.0, 0.0).astype(BF16)
    y1 = jnp.dot(pick1, ysb, preferred_element_type=F32)
    y2 = jnp.dot(pick2, ysb, preferred_element_type=F32)
    moe = rt[:, RT_W1:RT_W1 + 1] * y1 + rt[:, RT_W2:RT_W2 + 1] * y2
    x = x_ref[...] + gf_ref[0] * moe
    if final:
        ms = jnp.mean(x * x, axis=-1, keepdims=True)
        x = x * lax.rsqrt(ms + EPS) * fg_ref[...]
    o_ref[...] = x


def _combine(plan, x2, ys, rt, gf, fg, seq, tm, final):
    m = x2.shape[0]
    per_b = seq // tm
    lr = _local_rows(tm)
    grid_spec = pltpu.PrefetchScalarGridSpec(
        num_scalar_prefetch=4,
        grid=(m // tm,),
        in_specs=[
            pl.BlockSpec((tm, D_MODEL), lambda i, *_: (i, 0)),
            pl.BlockSpec((tm, LANES), lambda i, *_: (i, 0)),
            pl.BlockSpec((1, 1, D_MODEL), lambda i, *_: (i // per_b, 0, 0)),
            pl.BlockSpec((1, D_MODEL), lambda i, *_: (0, 0)),
            pl.BlockSpec(memory_space=pl.ANY),
        ],
        out_specs=pl.BlockSpec((tm, D_MODEL), lambda i, *_: (i, 0)),
        scratch_shapes=[pltpu.VMEM((2, lr, D_MODEL), F32), pltpu.SemaphoreType.DMA((2,))],
    )
    return pl.pallas_call(
        functools.partial(_combine_kernel, lr=lr, final=final),
        grid_spec=grid_spec,
        out_shape=jax.ShapeDtypeStruct((m, D_MODEL), F32),
        compiler_params=_cparams(("arbitrary",)),
        name="moe_combine",
    )(plan["lo8"], plan["gb8"], plan["n8"], plan["tot8"], x2, rt, gf, fg, ys)


def _moe_plan(c, n_blocks):
    i32 = jnp.int32
    blk8 = MOE_BLOCK // CHUNK
    c8 = (c + CHUNK - 1) // CHUNK
    lo8 = jnp.cumsum(c8, axis=1) - c8
    per_e = jnp.sum(c8, axis=0)
    pad8 = (per_e + blk8 - 1) // blk8 * blk8
    end8 = jnp.cumsum(pad8)
    start8 = end8 - pad8
    gb8 = start8[None, :] + jnp.cumsum(c8, axis=0) - c8
    blk_start8 = jnp.arange(n_blocks, dtype=i32) * blk8
    block_e = jnp.minimum(jnp.sum(end8[None, :] <= blk_start8[:, None], axis=1), N_EXPERTS - 1)
    return {
        "lo8": lo8.reshape(-1).astype(i32), "gb8": gb8.reshape(-1).astype(i32),
        "n8": c8.reshape(-1).astype(i32), "tot8": jnp.sum(c8, axis=1).astype(i32),
        "ts8": jnp.concatenate([start8 + per_e, end8[-1:]]).astype(i32),
        "tn8": jnp.concatenate([pad8 - per_e, n_blocks * blk8 - end8[-1:]]).astype(i32),
        "block_e": block_e.astype(i32), "n_used": (end8[-1:] // blk8).astype(i32),
    }


def _w_in_sections(w):
    d_dt = SSD_WIDTH + XBC_WIDTH
    d_q = d_dt + SSD_HEADS
    d_f = d_q + 3 * ATTN_WIDTH
    d_ga = d_f + ATTN_HEADS
    small = jnp.concatenate([w[..., d_dt:d_q], w[..., d_f:d_ga],
                             jnp.zeros(w.shape[:-1] + (LANES - SSD_HEADS - ATTN_HEADS,), w.dtype)], axis=-1)
    sections = [w[..., SSD_WIDTH:d_dt], w[..., :SSD_WIDTH], w[..., d_q:d_f], w[..., d_ga:], small]
    return [sec.astype(BF16) for sec in sections]


def kernel(x, c, ada_w, ada_b, norm_mix_g, w_in, ssd_conv_w, ssd_conv_b, ssd_dt_bias, ssd_a_log,
           ssd_d, ssd_norm_g, fox_f_bias, fox_norm_g, cm_conv_w, cm_conv_b, cm_ln_g, cm_ln_b, w_out,
           norm_ffn_g, w_router_group, b_router_group, w_router_expert, b_router_expert, w_gate,
           w_up, w_down, final_norm_g):
    bsz, seq, d = x.shape
    m = bsz * seq
    tm = min(512, seq)
    tt = min(256, seq)
    n_blocks = -(-(2 * m + (CHUNK - 1) * (m // tm) * N_EXPERTS) // MOE_BLOCK) + N_EXPERTS

    mod = _modulation(c, ada_w, ada_b)
    x2 = x.reshape(m, d)
    w_in_sections = _w_in_sections(w_in)
    for l in range(DEPTH):
        sh_m, sc_m, g_m, sh_f, sc_f, g_f = [v.reshape(bsz, 1, d) for v in jnp.split(mod[l], 6, axis=-1)]
        a_m = norm_mix_g[l][None, None, :] * (1.0 + sc_m)
        proj = _inproj(x2, a_m, sh_m, w_in_sections, l, seq, tm)
        y_ssd = _ssd(proj, ssd_conv_w[l], ssd_conv_b[l], ssd_dt_bias[l], ssd_a_log[l],
                     ssd_d[l], ssd_norm_g[l], bsz, seq, tt)
        qa, ka, va, aux = _foxprep(proj, fox_f_bias[l], bsz, seq)
        o_att = _fox(qa, ka, va, aux)
        y_cnv = _conformer(proj, cm_conv_w[l], cm_conv_b[l], cm_ln_g[l], cm_ln_b[l], bsz, seq, tt)

        w_r = jnp.concatenate([w_router_group[l], w_router_expert[l],
                               jnp.zeros((d, LANES - N_GROUPS - N_EXPERTS), F32)], axis=1)
        w_rh = w_r.astype(BF16)
        w_rl = (w_r - w_rh.astype(F32)).astype(BF16)
        b_r = jnp.concatenate([b_router_group[l], b_router_expert[l],
                               jnp.zeros((LANES - N_GROUPS - N_EXPERTS,), F32)]).reshape(1, LANES)
        a_f = norm_ffn_g[l][None, None, :] * (1.0 + sc_f)
        x2, h2, rt, rtt, cnt = _outproj(x2, y_ssd, o_att, y_cnv, w_out[l].astype(BF16),
                                        fox_norm_g[l].reshape(1, -1), g_m, a_f, sh_f, w_rh, w_rl, b_r,
                                        seq, tm)
        plan = _moe_plan(cnt[:, 0, :N_EXPERTS].astype(jnp.int32), n_blocks)
        xs = _dispatch(plan, h2, rtt, n_blocks * MOE_BLOCK, tm)
        ys = _experts(plan["block_e"], plan["n_used"], xs, w_gate, w_up, w_down, l)
        x2 = _combine(plan, x2, ys, rt, g_f, final_norm_g.reshape(1, d), seq, tm,
                      final=(l == DEPTH - 1))
    return x2.reshape(bsz, seq, d)
```

```python
import functools

import jax
import jax.numpy as jnp
import numpy as np
from jax import lax
from jax.experimental import pallas as pl
from jax.experimental.pallas import tpu as pltpu

F32 = jnp.float32
BF16 = jnp.bfloat16
HIGHEST = lax.Precision.HIGHEST

D_MODEL = 1024
DEPTH = 4
SSD_WIDTH = 512
SSD_HEADS = 8
SSD_HEAD_DIM = 64
SSD_STATE = 128
SSD_CONV = 4
SSD_CHUNK = 128
XBC_WIDTH = 1024
ATTN_WIDTH = 256
ATTN_HEADS = 4
ATTN_HEAD_DIM = 64
CONV_WIDTH = 256
CONV_KERNEL = 31
N_GROUPS = 4
PER_GROUP = 8
N_EXPERTS = 32
D_EXPERT = 512
MOE_BLOCK = 512
EPS = 1e-6

LANES = 128
COL_XBC = 0
COL_Z = 1024
COL_QKV = 1536
COL_GA = 2304
COL_GB = 2560
COL_SM = 2816
NP = 2944
SM_DT = 0
SM_F = 8

VMEM_LIMIT = 56 * 1024 * 1024


def _cparams(sem):
    return pltpu.CompilerParams(dimension_semantics=sem, vmem_limit_bytes=VMEM_LIMIT)


def _sigmoid(x):
    return 1.0 / (1.0 + jnp.exp(-x))


def _silu(x):
    return x * _sigmoid(x)


def _softplus(x):
    return jnp.maximum(x, 0.0) + jnp.log1p(jnp.exp(-jnp.abs(x)))


def _iota(shape, dim):
    return lax.broadcasted_iota(jnp.int32, shape, dim)


def _split_bf16x3(x):
    hi = x.astype(BF16)
    r1 = x - hi.astype(F32)
    mid = r1.astype(BF16)
    lo = (r1 - mid.astype(F32)).astype(BF16)
    return hi, mid, lo


def _dot_onehot_rhs(x, sel):
    return sum(jnp.dot(part, sel, preferred_element_type=F32) for part in _split_bf16x3(x))


def _dot_onehot_lhs(sel, x):
    return sum(jnp.dot(sel, part, preferred_element_type=F32) for part in _split_bf16x3(x))


def _mod_kernel(c_ref, w_ref, b_ref, o_ref):
    cond = _silu(c_ref[...])
    o_ref[0] = jnp.dot(cond, w_ref[0], precision=HIGHEST, preferred_element_type=F32) + b_ref[0]


def _modulation(c, ada_w, ada_b):
    bsz = c.shape[0]
    rows = 8
    cpad = jnp.zeros((rows, D_MODEL), F32).at[:bsz].set(c)
    tn = 1536
    n6 = 6 * D_MODEL
    out = pl.pallas_call(
        _mod_kernel,
        grid=(DEPTH, n6 // tn),
        in_specs=[
            pl.BlockSpec((rows, D_MODEL), lambda l, j: (0, 0)),
            pl.BlockSpec((1, D_MODEL, tn), lambda l, j: (l, 0, j)),
            pl.BlockSpec((1, 1, tn), lambda l, j: (l, 0, j)),
        ],
        out_specs=pl.BlockSpec((1, rows, tn), lambda l, j: (l, 0, j)),
        out_shape=jax.ShapeDtypeStruct((DEPTH, rows, n6), F32),
        compiler_params=_cparams(("arbitrary", "arbitrary")),
        name="adaln_mod",
    )(cpad, ada_w, ada_b.reshape(DEPTH, 1, n6))
    return out[:, :bsz]


def _inproj_kernel(x_ref, a_ref, s_ref, *refs):
    w_refs, o_ref = refs[:-1], refs[-1]
    x = x_ref[...]
    ms = jnp.mean(x * x, axis=-1, keepdims=True)
    h = (x * lax.rsqrt(ms + EPS) * a_ref[0] + s_ref[0]).astype(BF16)
    col = 0
    for w_ref in w_refs:
        width = w_ref.shape[-1]
        o_ref[:, col:col + width] = jnp.dot(h, w_ref[0], preferred_element_type=F32)
        col += width


def _inproj(x2, a, s, w_sections, layer, seq, tm):
    m = x2.shape[0]
    per_b = seq // tm
    assert sum(w.shape[-1] for w in w_sections) == NP
    return pl.pallas_call(
        _inproj_kernel,
        grid=(m // tm,),
        in_specs=[
            pl.BlockSpec((tm, D_MODEL), lambda i: (i, 0)),
            pl.BlockSpec((1, 1, D_MODEL), lambda i: (i // per_b, 0, 0)),
            pl.BlockSpec((1, 1, D_MODEL), lambda i: (i // per_b, 0, 0)),
        ] + [pl.BlockSpec((1, D_MODEL, w.shape[-1]), lambda i: (layer, 0, 0)) for w in w_sections],
        out_specs=pl.BlockSpec((tm, NP), lambda i: (i, 0)),
        out_shape=jax.ShapeDtypeStruct((m, NP), F32),
        compiler_params=_cparams(("arbitrary",)),
        name="in_proj",
    )(x2, a, s, *w_sections)


def _ssd_kernel(z_ref, xbc_ref, sm_ref, cw_ref, cb_ref, dtb_ref, alog_ref, dtbt_ref,
                alogt_ref, e_ref, dx_ref, ng_ref, y_ref, xpad, xc, prev, *, tt):
    t = pl.program_id(1)

    @pl.when(t == 0)
    def _():
        xpad[0:8, :] = jnp.zeros((8, XBC_WIDTH), F32)
        prev[...] = jnp.zeros(prev.shape, F32)

    xpad[8:8 + tt, :] = xbc_ref[...]
    acc = jnp.broadcast_to(cb_ref[...], (tt, XBC_WIDTH))
    for k in range(SSD_CONV):
        off = 8 - (SSD_CONV - 1) + k
        acc = acc + cw_ref[k:k + 1, :] * xpad[off:off + tt, :]
    xc[...] = _silu(acc)
    xpad[0:8, :] = xpad[tt:tt + 8, :]

    cl = SSD_CHUNK
    row = _iota((cl, cl), 0)
    col = _iota((cl, cl), 1)
    causal = row >= col
    tril = jnp.where(causal, 1.0, 0.0).astype(BF16)
    triu = jnp.where(row <= col, 1.0, 0.0).astype(BF16)
    lo = col < SSD_HEAD_DIM
    lane1 = _iota((1, LANES), 1)
    a_row = jnp.where(lane1 < SSD_HEADS, -jnp.exp(alog_ref[...]), 0.0)
    a_col = -jnp.exp(alogt_ref[...])
    expand = e_ref[...]

    def chunk(c, carry):
        r0 = pl.multiple_of(c * cl, cl)
        xs = xc[pl.ds(r0, cl), 0:SSD_WIDTH]
        bmat = xc[pl.ds(r0, cl), SSD_WIDTH:SSD_WIDTH + 2 * SSD_STATE]
        cmat = xc[pl.ds(r0, cl), SSD_WIDTH + 2 * SSD_STATE:XBC_WIDTH]
        sm = sm_ref[pl.ds(r0, cl), :]
        dt = _softplus(sm + dtb_ref[...])
        da = dt * a_row
        acs = _dot_onehot_lhs(tril, da)
        dtt = _softplus(sm.T[0:SSD_HEADS, :] + dtbt_ref[...])
        acst = _dot_onehot_rhs(dtt * a_col, triu)
        dt_x = _dot_onehot_rhs(dt, expand)
        acs_x = _dot_onehot_rhs(acs, expand)
        last = acs_x[cl - 1:cl, :]
        eacs_x = jnp.exp(acs_x)
        dte_x = jnp.exp(last - acs_x)
        cd_x = jnp.exp(last)
        xdt = xs * dt_x
        xdte = (xdt * dte_x).astype(BF16)
        zc = z_ref[pl.ds(r0, cl), :]
        for g in range(2):
            bg = bmat[:, g * SSD_STATE:(g + 1) * SSD_STATE]
            cg = cmat[:, g * SSD_STATE:(g + 1) * SSD_STATE].astype(BF16)
            bgt = bg.T.astype(BF16)
            cbm = jnp.dot(cg, bgt, preferred_element_type=F32)
            pair_out = []
            for j in range(2):
                p = 2 * g + j
                sl = slice(p * LANES, (p + 1) * LANES)
                xp = xdt[:, sl]
                yd = jnp.zeros((cl, LANES), F32)
                for half in range(2):
                    h = 2 * p + half
                    seg = acs[:, h:h + 1] - acst[h:h + 1, :]
                    dec = jnp.exp(jnp.where(causal, seg, -jnp.inf))
                    gm = (cbm * dec).astype(BF16)
                    own = lo if half == 0 else jnp.logical_not(lo)
                    xm = jnp.where(own, xp, 0.0).astype(BF16)
                    yd = yd + jnp.dot(gm, xm, preferred_element_type=F32)
                prev_p = prev[:, sl]
                yo = jnp.dot(cg, prev_p.astype(BF16), preferred_element_type=F32) * eacs_x[:, sl]
                st = jnp.dot(bgt, xdte[:, sl], preferred_element_type=F32)
                prev[:, sl] = prev_p * cd_x[:, sl] + st
                pair_out.append(yd + yo + xs[:, sl] * dx_ref[:, sl])
            gs = slice(g * 2 * LANES, (g + 1) * 2 * LANES)
            yg = jnp.concatenate(pair_out, axis=-1) * _silu(zc[:, gs])
            ms = jnp.mean(yg * yg, axis=-1, keepdims=True)
            y_ref[pl.ds(r0, cl), gs] = yg * lax.rsqrt(ms + EPS) * ng_ref[:, gs]
        return carry

    lax.fori_loop(0, tt // cl, chunk, 0)


def _ssd(proj, cw, cb, dtb, alog, dx, ng, bsz, seq, tt):
    m = bsz * seq
    nt = seq // tt
    pad = LANES - SSD_HEADS
    dtb_row = jnp.pad(dtb, (0, pad)).reshape(1, LANES)
    alog_row = jnp.pad(alog, (0, pad)).reshape(1, LANES)
    dtb_col = jnp.broadcast_to(dtb[:, None], (SSD_HEADS, SSD_CHUNK))
    alog_col = jnp.broadcast_to(alog[:, None], (SSD_HEADS, SSD_CHUNK))
    expand = np.zeros((LANES, SSD_WIDTH), np.float32)
    for h in range(SSD_HEADS):
        expand[h, h * SSD_HEAD_DIM:(h + 1) * SSD_HEAD_DIM] = 1.0
    dx_row = jnp.repeat(dx, SSD_HEAD_DIM).reshape(1, SSD_WIDTH)
    full = lambda shape: pl.BlockSpec(shape, lambda b, t: (0,) * len(shape))
    return pl.pallas_call(
        functools.partial(_ssd_kernel, tt=tt),
        grid=(bsz, nt),
        in_specs=[
            pl.BlockSpec((tt, SSD_WIDTH), lambda b, t: (b * nt + t, COL_Z // SSD_WIDTH)),
            pl.BlockSpec((tt, XBC_WIDTH), lambda b, t: (b * nt + t, COL_XBC // XBC_WIDTH + 0)),
            pl.BlockSpec((tt, LANES), lambda b, t: (b * nt + t, COL_SM // LANES)),
            full((SSD_CONV, XBC_WIDTH)),
            full((1, XBC_WIDTH)),
            full((1, LANES)),
            full((1, LANES)),
            full((SSD_HEADS, SSD_CHUNK)),
            full((SSD_HEADS, SSD_CHUNK)),
            full((LANES, SSD_WIDTH)),
            full((1, SSD_WIDTH)),
            full((1, SSD_WIDTH)),
        ],
        out_specs=pl.BlockSpec((tt, SSD_WIDTH), lambda b, t: (b * nt + t, 0)),
        out_shape=jax.ShapeDtypeStruct((m, SSD_WIDTH), F32),
        scratch_shapes=[
            pltpu.VMEM((tt + 8, XBC_WIDTH), F32),
            pltpu.VMEM((tt, XBC_WIDTH), F32),
            pltpu.VMEM((SSD_STATE, SSD_WIDTH), F32),
        ],
        compiler_params=_cparams(("arbitrary", "arbitrary")),
        name="ssd_scan",
    )(proj, proj, proj, cw, cb.reshape(1, XBC_WIDTH), dtb_row, alog_row, dtb_col, alog_col,
      jnp.asarray(expand, dtype=BF16), dx_row, ng.reshape(1, SSD_WIDTH))


LOG2E = 1.4426950408889634
ATT_BLOCK = 256
PRUNE_LOG2 = 140.0
FAST_GAP_LOG2 = 60.0
SHIFT_MARGIN_LOG2 = 1.0
AUX_CUMEND = 0
AUX_KMAX = 1


def _foxprep_kernel(qkv_ref, sm_ref, fb_ref, q_out, k_out, v_out, aux_out, carry, *, tt):
    t = pl.program_id(1)

    @pl.when(t == 0)
    def _():
        carry[...] = jnp.zeros(carry.shape, F32)
        aux_out[...] = jnp.zeros(aux_out.shape, F32)

    logit = sm_ref[...] + fb_ref[...]
    logf = -_softplus(-logit)
    row = _iota((tt, tt), 0)
    col = _iota((tt, tt), 1)
    tril = jnp.where(row >= col, 1.0, 0.0).astype(BF16)
    cum = _dot_onehot_lhs(tril, logf) + carry[...]
    carry[...] = cum[tt - 1:tt, :]

    lane = _iota((tt, LANES), 1)
    lane1 = _iota((1, LANES), 1)
    scale = ATTN_HEAD_DIM ** -0.5 * LOG2E
    for h in range(ATTN_HEADS):
        pair, half = h // 2, h % 2
        own = (lane < ATTN_HEAD_DIM) if half == 0 else (lane >= ATTN_HEAD_DIM)
        a0 = ATTN_HEAD_DIM * (1 - half)
        cs = jnp.broadcast_to(cum[:, SM_F + h:SM_F + h + 1], (tt, LANES)) * LOG2E
        hi = cs.astype(BF16).astype(F32)
        r1 = cs - hi
        mid = r1.astype(BF16).astype(F32)
        low = r1 - mid
        qp = qkv_ref[:, pair * LANES:(pair + 1) * LANES]
        kp = qkv_ref[:, ATTN_WIDTH + pair * LANES:ATTN_WIDTH + (pair + 1) * LANES]
        vp = qkv_ref[:, 2 * ATTN_WIDTH + pair * LANES:2 * ATTN_WIDTH + (pair + 1) * LANES]
        qa = jnp.where(lane == a0, hi, jnp.where(lane == a0 + 1, mid, jnp.where(
            lane == a0 + 2, low, jnp.where((lane >= a0 + 3) & (lane < a0 + 6), 1.0, 0.0))))
        ka = jnp.where(lane == a0 + 3, -hi, jnp.where(lane == a0 + 4, -mid, jnp.where(
            lane == a0 + 5, -low, jnp.where((lane >= a0) & (lane < a0 + 9), 1.0, 0.0))))
        q_out[0, h] = jnp.where(own, qp * scale, qa).astype(BF16)
        kb = jnp.where(own, kp, ka).astype(BF16)
        k_out[0, h] = kb
        v_out[0, h] = jnp.where(own, vp, jnp.where(lane == a0, 1.0, 0.0)).astype(BF16)
        kf = jnp.where(own, kb.astype(F32), 0.0)
        kn2 = jnp.max(jnp.sum(kf * kf, axis=-1, keepdims=True), axis=0, keepdims=True)
        here = lane1 == t
        aux_out[0, h, AUX_CUMEND:AUX_CUMEND + 1, :] = jnp.where(
            here, cs[tt - 1:tt, :], aux_out[0, h, AUX_CUMEND:AUX_CUMEND + 1, :])
        aux_out[0, h, AUX_KMAX:AUX_KMAX + 1, :] = jnp.where(
            here, jnp.sqrt(kn2), aux_out[0, h, AUX_KMAX:AUX_KMAX + 1, :])


def _foxprep(proj, fb, bsz, seq):
    tt = ATT_BLOCK
    nt = seq // tt
    assert nt <= LANES
    fb_row = jnp.zeros((1, LANES), F32).at[0, SM_F:SM_F + ATTN_HEADS].set(fb)
    shp = jax.ShapeDtypeStruct((bsz, ATTN_HEADS, seq, LANES), BF16)
    ospec = pl.BlockSpec((1, ATTN_HEADS, tt, LANES), lambda b, t: (b, 0, t, 0))
    return pl.pallas_call(
        functools.partial(_foxprep_kernel, tt=tt),
        grid=(bsz, nt),
        in_specs=[
            pl.BlockSpec((tt, 3 * ATTN_WIDTH), lambda b, t: (b * nt + t, COL_QKV // (3 * ATTN_WIDTH))),
            pl.BlockSpec((tt, LANES), lambda b, t: (b * nt + t, COL_SM // LANES)),
            pl.BlockSpec((1, LANES), lambda b, t: (0, 0)),
        ],
        out_specs=[ospec, ospec, ospec,
                   pl.BlockSpec((1, ATTN_HEADS, 8, LANES), lambda b, t: (b, 0, 0, 0))],
        out_shape=[shp, shp, shp, jax.ShapeDtypeStruct((bsz, ATTN_HEADS, 8, LANES), F32)],
        scratch_shapes=[pltpu.VMEM((1, LANES), F32)],
        compiler_params=_cparams(("arbitrary", "arbitrary")),
        name="fox_prep",
    )(proj, proj, fb_row)


def _fox_kernel(q_ref, k_ref, v_ref, aux_ref, o_ref, *, tq):
    i = pl.program_id(2)
    nt = (((1,), (1,)), ((), ()))
    lane = _iota((tq, LANES), 1)
    lane1 = _iota((1, LANES), 1)
    qs = (q_ref[0, 0], q_ref[0, 1])

    def step(hh, r0, m, acc, mask):
        kblk = k_ref[0, hh, pl.ds(r0, tq), :]
        vblk = v_ref[0, hh, pl.ds(r0, tq), :]
        s = lax.dot_general(qs[hh], kblk, nt, preferred_element_type=F32)
        if mask is not None:
            s = jnp.where(mask, s, -jnp.inf)
        m_new = jnp.maximum(m, jnp.max(s, axis=-1, keepdims=True))
        p = jnp.exp2(s - m_new)
        alpha = jnp.exp2(m - m_new)
        acc = acc * alpha + jnp.dot(p.astype(BF16), vblk, preferred_element_type=F32)
        return m_new, acc

    d0 = pl.multiple_of(i * tq, tq)
    diag = _iota((tq, tq), 0) >= _iota((tq, tq), 1)
    i_f = i.astype(F32)
    first = []
    gaps = []
    q_shift = []
    for hh in range(2):
        own = (lane < ATTN_HEAD_DIM) if hh == 0 else (lane >= ATTN_HEAD_DIM)
        a0 = ATTN_HEAD_DIM * (1 - hh)
        qf = qs[hh].astype(F32)
        qn = jnp.sqrt(jnp.sum(jnp.where(own, qf * qf, 0.0), axis=-1, keepdims=True))
        cum_t = jnp.sum(jnp.where((lane >= a0) & (lane < a0 + 3), qf, 0.0), axis=-1, keepdims=True)
        s_self = jnp.sum(qf * k_ref[0, hh, pl.ds(d0, tq), :].astype(F32), axis=-1, keepdims=True)
        aux = aux_ref[0, hh]
        cend = aux[AUX_CUMEND:AUX_CUMEND + 1, :]
        kmax = jnp.max(jnp.where(lane1 <= i, aux[AUX_KMAX:AUX_KMAX + 1, :], 0.0), axis=-1, keepdims=True)
        bound = qn * kmax
        slack = jnp.max(bound + cum_t - s_self, axis=0, keepdims=True)
        live = (lane1 < i) & (slack - cend > -PRUNE_LOG2)
        first.append(jnp.min(jnp.where(live, lane1.astype(F32), i_f)))
        s_hi, s_mid, s_lo = [part.astype(F32) for part in _split_bf16x3(bound + SHIFT_MARGIN_LOG2)]
        gaps.append(jnp.max(s_hi + s_mid + s_lo - s_self))
        q_shift.append(jnp.where(lane == a0 + 6, -s_hi, jnp.where(lane == a0 + 7, -s_mid, jnp.where(
            lane == a0 + 8, -s_lo, qf))).astype(BF16))
    j_first = jnp.minimum(first[0], first[1]).astype(jnp.int32)

    def fast(_):
        def blocks(c, r0, width, mask=None):
            out = []
            for hh in range(2):
                s = lax.dot_general(q_shift[hh], k_ref[0, hh, pl.ds(r0, width), :], nt,
                                    preferred_element_type=F32)
                if mask is not None:
                    s = jnp.where(mask, s, -jnp.inf)
                out.append(c[hh] + jnp.dot(jnp.exp2(s).astype(BF16), v_ref[0, hh, pl.ds(r0, width), :],
                                           preferred_element_type=F32))
            return tuple(out)

        zero = jnp.zeros((tq, LANES), F32)
        c = blocks((zero, zero), d0, tq, diag)
        n = i - j_first
        c = lax.cond((n & 1) == 1, lambda c: blocks(c, pl.multiple_of(j_first * tq, tq), tq),
                     lambda c: c, c)
        c = lax.cond((n & 2) == 2, lambda c: blocks(c, pl.multiple_of((j_first + (n & 1)) * tq, tq), 2 * tq),
                     lambda c: c, c)
        j0 = j_first + (n & 3)

        def body(jj, c):
            return blocks(c, pl.multiple_of((j0 + 4 * jj) * tq, tq), 4 * tq)
        return lax.fori_loop(0, lax.shift_right_logical(n, 2), body, c)

    def online(_):
        m_init = jnp.full((tq, 1), -jnp.inf, F32)
        acc_init = jnp.zeros((tq, LANES), F32)
        carry = step(0, d0, m_init, acc_init, diag) + step(1, d0, m_init, acc_init, diag)

        def body(j, c):
            r0 = pl.multiple_of(j * tq, tq)
            m0, a0_, m1, a1_ = c
            m0, a0_ = step(0, r0, m0, a0_, None)
            m1, a1_ = step(1, r0, m1, a1_, None)
            return m0, a0_, m1, a1_
        _, a0_, _, a1_ = lax.fori_loop(j_first, i, body, carry)
        return a0_, a1_

    acc0, acc1 = lax.cond(jnp.maximum(gaps[0], gaps[1]) <= FAST_GAP_LOG2, fast, online, None)
    den0 = jnp.sum(jnp.where(lane == ATTN_HEAD_DIM, acc0, 0.0), axis=-1, keepdims=True)
    den1 = jnp.sum(jnp.where(lane == 0, acc1, 0.0), axis=-1, keepdims=True)
    o_ref[...] = jnp.where(lane < ATTN_HEAD_DIM, acc0 / den0, acc1 / den1)


def _fox(qa, ka, va, aux):
    bsz, nh, seq, _ = qa.shape
    tq = ATT_BLOCK
    nq = seq // tq
    kv_spec = pl.BlockSpec((1, 2, seq, LANES), lambda b, p, i: (b, p, 0, 0))
    return pl.pallas_call(
        functools.partial(_fox_kernel, tq=tq),
        grid=(bsz, nh // 2, nq),
        in_specs=[pl.BlockSpec((1, 2, tq, LANES), lambda b, p, i: (b, p, i, 0)), kv_spec, kv_spec,
                  pl.BlockSpec((1, 2, 8, LANES), lambda b, p, i: (b, p, 0, 0))],
        out_specs=pl.BlockSpec((tq, LANES), lambda b, p, i: (b * nq + i, p)),
        out_shape=jax.ShapeDtypeStruct((bsz * seq, ATTN_WIDTH), F32),
        compiler_params=_cparams(("arbitrary", "arbitrary", "arbitrary")),
        name="fox_attn",
    )(qa, ka, va, aux)


CONF_HALO = 32
CONF_ROWS = 64


def _conf_kernel(ga_ref, gb_ref, w_ref, b_ref, lg_ref, lb_ref, y_ref, upad, *, tt):
    t = pl.program_id(1)

    @pl.when(t == 0)
    def _():
        upad[0:CONF_HALO, :] = jnp.zeros((CONF_HALO, CONV_WIDTH), F32)

    upad[CONF_HALO:CONF_HALO + tt, :] = ga_ref[...] * _sigmoid(gb_ref[...])
    base = CONF_HALO - (CONV_KERNEL - 1)
    for r in range(tt // CONF_ROWS):
        r0 = r * CONF_ROWS
        acc = jnp.broadcast_to(b_ref[...], (CONF_ROWS, CONV_WIDTH))
        for k in range(CONV_KERNEL):
            acc = acc + w_ref[k:k + 1, :] * upad[r0 + base + k:r0 + base + k + CONF_ROWS, :]
        mu = jnp.mean(acc, axis=-1, keepdims=True)
        cen = acc - mu
        var = jnp.mean(cen * cen, axis=-1, keepdims=True)
        y = cen * lax.rsqrt(var + EPS) * lg_ref[...] + lb_ref[...]
        y_ref[r0:r0 + CONF_ROWS, :] = _silu(y)
    upad[0:CONF_HALO, :] = upad[tt:tt + CONF_HALO, :]


def _conformer(proj, w, b, lg, lb, bsz, seq, tt):
    m = bsz * seq
    nt = seq // tt
    full = lambda shape: pl.BlockSpec(shape, lambda bb, t: (0,) * len(shape))
    return pl.pallas_call(
        functools.partial(_conf_kernel, tt=tt),
        grid=(bsz, nt),
        in_specs=[
            pl.BlockSpec((tt, CONV_WIDTH), lambda bb, t: (bb * nt + t, COL_GA // CONV_WIDTH)),
            pl.BlockSpec((tt, CONV_WIDTH), lambda bb, t: (bb * nt + t, COL_GB // CONV_WIDTH)),
            full((CONV_KERNEL, CONV_WIDTH)),
            full((1, CONV_WIDTH)),
            full((1, CONV_WIDTH)),
            full((1, CONV_WIDTH)),
        ],
        out_specs=pl.BlockSpec((tt, CONV_WIDTH), lambda bb, t: (bb * nt + t, 0)),
        out_shape=jax.ShapeDtypeStruct((m, CONV_WIDTH), F32),
        scratch_shapes=[pltpu.VMEM((tt + CONF_HALO, CONV_WIDTH), F32)],
        compiler_params=_cparams(("arbitrary", "arbitrary")),
        name="conformer_conv",
    )(proj, proj, w, b.reshape(1, -1), lg.reshape(1, -1), lb.reshape(1, -1))


ROUTE_BIG = 1e9
CHUNK = 8
RT_W1, RT_W2, RT_ROW1, RT_ROW2 = 0, 1, 2, 3


def _split3(x):
    hi = x.astype(BF16)
    lo = (x - hi.astype(F32)).astype(BF16)
    return hi, lo


def _outproj_kernel(x_ref, ys_ref, oa_ref, yc_ref, wo_ref, fg_ref, gm_ref, a2_ref, s2_ref,
                    wrh_ref, wrl_ref, br_ref, xn_ref, h2_ref, rt_ref, rtt_ref, cnt_ref):
    tm = x_ref.shape[0]
    lane = _iota((tm, LANES), 1)
    att = oa_ref[...]
    ms = jnp.mean(att * att, axis=-1, keepdims=True)
    att = att * lax.rsqrt(ms + EPS) * fg_ref[...]
    y = jnp.dot(ys_ref[...].astype(BF16), wo_ref[0:SSD_WIDTH, :], preferred_element_type=F32)
    y = y + jnp.dot(att.astype(BF16), wo_ref[SSD_WIDTH:SSD_WIDTH + ATTN_WIDTH, :],
                    preferred_element_type=F32)
    y = y + jnp.dot(yc_ref[...].astype(BF16), wo_ref[SSD_WIDTH + ATTN_WIDTH:, :],
                    preferred_element_type=F32)
    xn = x_ref[...] + gm_ref[0] * y
    xn_ref[...] = xn
    ms2 = jnp.mean(xn * xn, axis=-1, keepdims=True)
    h2 = xn * lax.rsqrt(ms2 + EPS) * a2_ref[0] + s2_ref[0]
    h2_ref[...] = h2.astype(BF16)

    hh, hl = _split3(h2)
    logits = (jnp.dot(hh, wrh_ref[...], preferred_element_type=F32)
              + jnp.dot(hl, wrh_ref[...], preferred_element_type=F32)
              + jnp.dot(hh, wrl_ref[...], preferred_element_type=F32)) + br_ref[...]
    lanef = lane.astype(F32)
    lg = jnp.where(lane < N_GROUPS, logits, -jnp.inf)
    gmax = jnp.max(lg, axis=-1, keepdims=True)
    gsum = jnp.sum(jnp.exp(lg - gmax), axis=-1, keepdims=True)
    gidx = jnp.min(jnp.where(lg == gmax, lanef, ROUTE_BIG), axis=-1, keepdims=True)
    e_lo = N_GROUPS + PER_GROUP * gidx
    le = jnp.where((lanef >= e_lo) & (lanef < e_lo + PER_GROUP), logits, -jnp.inf)
    m1 = jnp.max(le, axis=-1, keepdims=True)
    i1 = jnp.min(jnp.where(le == m1, lanef, ROUTE_BIG), axis=-1, keepdims=True)
    le2 = jnp.where(lanef == i1, -jnp.inf, le)
    m2 = jnp.max(le2, axis=-1, keepdims=True)
    i2 = jnp.min(jnp.where(le2 == m2, lanef, ROUTE_BIG), axis=-1, keepdims=True)
    esum = jnp.sum(jnp.exp(le - m1), axis=-1, keepdims=True)
    p1 = 1.0 / esum
    p2 = jnp.exp(m2 - m1) / esum
    psel = 1.0 / gsum
    w1 = p1 / (p1 + p2) * psel
    w2 = p2 / (p1 + p2) * psel

    oh1 = (lanef == (i1 - N_GROUPS)).astype(F32)
    oh2 = (lanef == (i2 - N_GROUPS)).astype(F32)
    oh = oh1 + oh2
    cnt = jnp.sum(oh, axis=0, keepdims=True)
    before = (_iota((tm, tm), 0) > _iota((tm, tm), 1)).astype(BF16)
    rank = jnp.dot(before, oh.astype(BF16), preferred_element_type=F32)
    chunks = jnp.floor((cnt + (CHUNK - 1)) * (1.0 / CHUNK))
    below = (_iota((LANES, LANES), 0) < _iota((LANES, LANES), 1)).astype(BF16)
    seg_lo = CHUNK * jnp.dot(jnp.broadcast_to(chunks, (8, LANES)).astype(BF16), below,
                             preferred_element_type=F32)[0:1, :]
    lr1 = jnp.sum(oh1 * (rank + seg_lo), axis=-1, keepdims=True)
    lr2 = jnp.sum(oh2 * (rank + seg_lo), axis=-1, keepdims=True)
    route = jnp.where(lane == RT_W1, w1, jnp.where(lane == RT_W2, w2, jnp.where(
        lane == RT_ROW1, lr1, jnp.where(lane == RT_ROW2, lr2, 0.0))))
    rt_ref[...] = route
    rtt_ref[0] = route.T[0:8, :]
    cnt_ref[0] = jnp.broadcast_to(cnt, (8, LANES))


def _outproj(x2, ys, oa, yc, wo, fg, gm, a2, s2, wrh, wrl, br, seq, tm):
    m = x2.shape[0]
    per_b = seq // tm
    nt = m // tm
    row = lambda w: pl.BlockSpec((tm, w), lambda i: (i, 0))
    bvec = pl.BlockSpec((1, 1, D_MODEL), lambda i: (i // per_b, 0, 0))
    full = lambda shape: pl.BlockSpec(shape, lambda i: (0,) * len(shape))
    return pl.pallas_call(
        _outproj_kernel,
        grid=(nt,),
        in_specs=[
            row(D_MODEL), row(SSD_WIDTH), row(ATTN_WIDTH), row(CONV_WIDTH),
            full((D_MODEL, D_MODEL)), full((1, ATTN_WIDTH)), bvec, bvec, bvec,
            full((D_MODEL, LANES)), full((D_MODEL, LANES)), full((1, LANES)),
        ],
        out_specs=[row(D_MODEL), row(D_MODEL), row(LANES),
                   pl.BlockSpec((1, 8, tm), lambda i: (i, 0, 0)),
                   pl.BlockSpec((1, 8, LANES), lambda i: (i, 0, 0))],
        out_shape=[jax.ShapeDtypeStruct((m, D_MODEL), F32), jax.ShapeDtypeStruct((m, D_MODEL), BF16),
                   jax.ShapeDtypeStruct((m, LANES), F32), jax.ShapeDtypeStruct((nt, 8, tm), F32),
                   jax.ShapeDtypeStruct((nt, 8, LANES), F32)],
        compiler_params=_cparams(("arbitrary",)),
        name="out_proj_router",
    )(x2, ys, oa, yc, wo, fg, gm, a2, s2, wrh, wrl, br)


def _local_rows(tm):
    return -(-(2 * tm + (CHUNK - 1) * N_EXPERTS) // LANES) * LANES


def _wait_chunks(n, n_max, copy_of_size):
    for bit in range(n_max.bit_length()):
        @pl.when((lax.shift_right_logical(n, bit) & 1) == 1)
        def _():
            copy_of_size(CHUNK << bit).wait()


def _dispatch_kernel(lo_ref, gb_ref, n8_ref, tot_ref, ts_ref, tn_ref, h_ref, rtt_ref, xs_hbm,
                     xloc, zeros, sem, zsem, *, lr):
    t = pl.program_id(0)
    nt = pl.num_programs(0)
    slot = t % 2
    tm = h_ref.shape[0]

    def seg_copy(sl, src, dst):
        return pltpu.make_async_copy(xloc.at[sl, pl.ds(src, CHUNK), :], xs_hbm.at[pl.ds(dst, CHUNK), :],
                                     sem.at[sl])

    def wait_tile(tile, sl):
        _wait_chunks(tot_ref[tile], lr // CHUNK, lambda size: pltpu.make_async_copy(
            xloc.at[sl, pl.ds(0, size), :], xs_hbm.at[pl.ds(0, size), :], sem.at[sl]))

    @pl.when(t >= 2)
    def _():
        wait_tile(t - 2, slot)

    rows = _iota((lr, tm), 0).astype(F32)
    perm = jnp.where(rows == rtt_ref[0, RT_ROW1:RT_ROW1 + 1, :], 1.0,
                     jnp.where(rows == rtt_ref[0, RT_ROW2:RT_ROW2 + 1, :], 1.0, 0.0))
    xloc[slot] = jnp.dot(perm.astype(BF16), h_ref[...], preferred_element_type=F32)

    def per_expert(e, c):
        idx = t * N_EXPERTS + e
        src0 = lo_ref[idx] * CHUNK
        dst0 = gb_ref[idx] * CHUNK

        def per_chunk(k, c2):
            seg_copy(slot, pl.multiple_of(src0 + k * CHUNK, CHUNK),
                     pl.multiple_of(dst0 + k * CHUNK, CHUNK)).start()
            return c2
        lax.fori_loop(0, n8_ref[idx], per_chunk, 0)
        return c
    lax.fori_loop(0, N_EXPERTS, per_expert, 0)

    @pl.when(t == nt - 1)
    def _():
        zeros[...] = jnp.zeros(zeros.shape, F32)

        def zero_copy(dst):
            return pltpu.make_async_copy(zeros, xs_hbm.at[pl.ds(dst, CHUNK), :], zsem.at[0])

        def fill(e, c):
            dst0 = ts_ref[e] * CHUNK

            def one(k, c2):
                zero_copy(pl.multiple_of(dst0 + k * CHUNK, CHUNK)).start()
                return c2
            lax.fori_loop(0, tn_ref[e], one, 0)
            return c
        lax.fori_loop(0, N_EXPERTS + 1, fill, 0)

        def drain(e, c):
            def one(k, c2):
                zero_copy(0).wait()
                return c2
            lax.fori_loop(0, tn_ref[e], one, 0)
            return c
        lax.fori_loop(0, N_EXPERTS + 1, drain, 0)

        @pl.when(t >= 1)
        def _():
            wait_tile(t - 1, 1 - slot)
        wait_tile(t, slot)


def _dispatch(plan, h2, rtt, n_rows, tm):
    m = h2.shape[0]
    lr = _local_rows(tm)
    grid_spec = pltpu.PrefetchScalarGridSpec(
        num_scalar_prefetch=6,
        grid=(m // tm,),
        in_specs=[pl.BlockSpec((tm, D_MODEL), lambda t, *_: (t, 0)),
                  pl.BlockSpec((1, 8, tm), lambda t, *_: (t, 0, 0))],
        out_specs=pl.BlockSpec(memory_space=pl.ANY),
        scratch_shapes=[pltpu.VMEM((2, lr, D_MODEL), F32), pltpu.VMEM((CHUNK, D_MODEL), F32),
                        pltpu.SemaphoreType.DMA((2,)), pltpu.SemaphoreType.DMA((1,))],
    )
    return pl.pallas_call(
        functools.partial(_dispatch_kernel, lr=lr),
        grid_spec=grid_spec,
        out_shape=jax.ShapeDtypeStruct((n_rows, D_MODEL), F32),
        compiler_params=_cparams(("arbitrary",)),
        name="moe_dispatch",
    )(plan["lo8"], plan["gb8"], plan["n8"], plan["tot8"], plan["ts8"], plan["tn8"], h2, rtt)


def _expert_kernel(be_ref, nu_ref, x_ref, wg_ref, wu_ref, wd_ref, y_ref, wgb, wub, wdb):
    i = pl.program_id(0)

    @pl.when(i < nu_ref[0])
    def _():
        prev_e = be_ref[jnp.maximum(i - 1, 0)]

        @pl.when((i == 0) | (be_ref[i] != prev_e))
        def _():
            wgb[...] = wg_ref[0, 0].astype(BF16)
            wub[...] = wu_ref[0, 0].astype(BF16)
            wdb[...] = wd_ref[0, 0].astype(BF16)

        x = x_ref[...].astype(BF16)
        hid = _silu(jnp.dot(x, wgb[...], preferred_element_type=F32)) * jnp.dot(
            x, wub[...], preferred_element_type=F32)
        y_ref[...] = jnp.dot(hid.astype(BF16), wdb[...], preferred_element_type=F32)

    @pl.when(i >= nu_ref[0])
    def _():
        y_ref[...] = jnp.zeros(y_ref.shape, F32)


def _experts(block_e, n_used, xs, wg, wu, wd, layer):
    n_rows = xs.shape[0]
    wspec = lambda shape: pl.BlockSpec((1, 1) + shape, lambda i, be, nu: (layer, be[i], 0, 0))
    grid_spec = pltpu.PrefetchScalarGridSpec(
        num_scalar_prefetch=2,
        grid=(n_rows // MOE_BLOCK,),
        in_specs=[
            pl.BlockSpec((MOE_BLOCK, D_MODEL), lambda i, be, nu: (i, 0)),
            wspec((D_MODEL, D_EXPERT)), wspec((D_MODEL, D_EXPERT)), wspec((D_EXPERT, D_MODEL)),
        ],
        out_specs=pl.BlockSpec((MOE_BLOCK, D_MODEL), lambda i, be, nu: (i, 0)),
        scratch_shapes=[
            pltpu.VMEM((D_MODEL, D_EXPERT), BF16),
            pltpu.VMEM((D_MODEL, D_EXPERT), BF16),
            pltpu.VMEM((D_EXPERT, D_MODEL), BF16),
        ],
    )
    return pl.pallas_call(
        _expert_kernel,
        grid_spec=grid_spec,
        out_shape=jax.ShapeDtypeStruct((n_rows, D_MODEL), F32),
        compiler_params=_cparams(("arbitrary",)),
        name="moe_experts",
    )(block_e, n_used, xs, wg, wu, wd)


def _combine_kernel(lo_ref, gb_ref, n8_ref, tot_ref, x_ref, rt_ref, gf_ref, fg_ref, ys_hbm, o_ref,
                    yloc, sem, *, lr, final):
    t = pl.program_id(0)
    nt = pl.num_programs(0)
    slot = t % 2
    tm = x_ref.shape[0]

    def fetch(tile, sl):
        def per_expert(e, c):
            idx = tile * N_EXPERTS + e
            dst0 = lo_ref[idx] * CHUNK
            src0 = gb_ref[idx] * CHUNK

            def per_chunk(k, c2):
                pltpu.make_async_copy(ys_hbm.at[pl.ds(pl.multiple_of(src0 + k * CHUNK, CHUNK), CHUNK), :],
                                      yloc.at[sl, pl.ds(pl.multiple_of(dst0 + k * CHUNK, CHUNK), CHUNK), :],
                                      sem.at[sl]).start()
                return c2
            lax.fori_loop(0, n8_ref[idx], per_chunk, 0)
            return c
        lax.fori_loop(0, N_EXPERTS, per_expert, 0)

    @pl.when(t == 0)
    def _():
        yloc[...] = jnp.zeros(yloc.shape, F32)
        fetch(0, 0)

    @pl.when(t + 1 < nt)
    def _():
        fetch(t + 1, 1 - slot)

    _wait_chunks(tot_ref[t], lr // CHUNK, lambda size: pltpu.make_async_copy(
        ys_hbm.at[pl.ds(0, size), :], yloc.at[slot, pl.ds(0, size), :], sem.at[slot]))

    rt = rt_ref[...]
    ysb = yloc[slot].astype(BF16)
    cols = _iota((tm, lr), 1).astype(F32)
    pick1 = jnp.where(cols == rt[:, RT_ROW1:RT_ROW1 + 1], 1.0, 0.0).astype(BF16)
    pick2 = jnp.where(cols == rt[:, RT_ROW2:RT_ROW2 + 1], 1.0, 0.0).astype(BF16)
    y1 = jnp.dot(pick1, ysb, preferred_element_type=F32)
    y2 = jnp.dot(pick2, ysb, preferred_element_type=F32)
    moe = rt[:, RT_W1:RT_W1 + 1] * y1 + rt[:, RT_W2:RT_W2 + 1] * y2
    x = x_ref[...] + gf_ref[0] * moe
    if final:
        ms = jnp.mean(x * x, axis=-1, keepdims=True)
        x = x * lax.rsqrt(ms + EPS) * fg_ref[...]
    o_ref[...] = x


def _combine(plan, x2, ys, rt, gf, fg, seq, tm, final):
    m = x2.shape[0]
    per_b = seq // tm
    lr = _local_rows(tm)
    grid_spec = pltpu.PrefetchScalarGridSpec(
        num_scalar_prefetch=4,
        grid=(m // tm,),
        in_specs=[
            pl.BlockSpec((tm, D_MODEL), lambda i, *_: (i, 0)),
            pl.BlockSpec((tm, LANES), lambda i, *_: (i, 0)),
            pl.BlockSpec((1, 1, D_MODEL), lambda i, *_: (i // per_b, 0, 0)),
            pl.BlockSpec((1, D_MODEL), lambda i, *_: (0, 0)),
            pl.BlockSpec(memory_space=pl.ANY),
        ],
        out_specs=pl.BlockSpec((tm, D_MODEL), lambda i, *_: (i, 0)),
        scratch_shapes=[pltpu.VMEM((2, lr, D_MODEL), F32), pltpu.SemaphoreType.DMA((2,))],
    )
    return pl.pallas_call(
        functools.partial(_combine_kernel, lr=lr, final=final),
        grid_spec=grid_spec,
        out_shape=jax.ShapeDtypeStruct((m, D_MODEL), F32),
        compiler_params=_cparams(("arbitrary",)),
        name="moe_combine",
    )(plan["lo8"], plan["gb8"], plan["n8"], plan["tot8"], x2, rt, gf, fg, ys)


def _moe_plan(c, n_blocks):
    i32 = jnp.int32
    blk8 = MOE_BLOCK // CHUNK
    c8 = (c + CHUNK - 1) // CHUNK
    lo8 = jnp.cumsum(c8, axis=1) - c8
    per_e = jnp.sum(c8, axis=0)
    pad8 = (per_e + blk8 - 1) // blk8 * blk8
    end8 = jnp.cumsum(pad8)
    start8 = end8 - pad8
    gb8 = start8[None, :] + jnp.cumsum(c8, axis=0) - c8
    blk_start8 = jnp.arange(n_blocks, dtype=i32) * blk8
    block_e = jnp.minimum(jnp.sum(end8[None, :] <= blk_start8[:, None], axis=1), N_EXPERTS - 1)
    return {
        "lo8": lo8.reshape(-1).astype(i32), "gb8": gb8.reshape(-1).astype(i32),
        "n8": c8.reshape(-1).astype(i32), "tot8": jnp.sum(c8, axis=1).astype(i32),
        "ts8": jnp.concatenate([start8 + per_e, end8[-1:]]).astype(i32),
        "tn8": jnp.concatenate([pad8 - per_e, n_blocks * blk8 - end8[-1:]]).astype(i32),
        "block_e": block_e.astype(i32), "n_used": (end8[-1:] // blk8).astype(i32),
    }


def _w_in_sections(w):
    d_dt = SSD_WIDTH + XBC_WIDTH
    d_q = d_dt + SSD_HEADS
    d_f = d_q + 3 * ATTN_WIDTH
    d_ga = d_f + ATTN_HEADS
    small = jnp.concatenate([w[..., d_dt:d_q], w[..., d_f:d_ga],
                             jnp.zeros(w.shape[:-1] + (LANES - SSD_HEADS - ATTN_HEADS,), w.dtype)], axis=-1)
    sections = [w[..., SSD_WIDTH:d_dt], w[..., :SSD_WIDTH], w[..., d_q:d_f], w[..., d_ga:], small]
    return [sec.astype(BF16) for sec in sections]


def kernel(x, c, ada_w, ada_b, norm_mix_g, w_in, ssd_conv_w, ssd_conv_b, ssd_dt_bias, ssd_a_log,
           ssd_d, ssd_norm_g, fox_f_bias, fox_norm_g, cm_conv_w, cm_conv_b, cm_ln_g, cm_ln_b, w_out,
           norm_ffn_g, w_router_group, b_router_group, w_router_expert, b_router_expert, w_gate,
           w_up, w_down, final_norm_g):
    bsz, seq, d = x.shape
    m = bsz * seq
    tm = min(512, seq)
    tt = min(256, seq)
    n_blocks = -(-(2 * m + (CHUNK - 1) * (m // tm) * N_EXPERTS) // MOE_BLOCK) + N_EXPERTS

    mod = _modulation(c, ada_w, ada_b)
    x2 = x.reshape(m, d)
    w_in_sections = _w_in_sections(w_in)
    for l in range(DEPTH):
        sh_m, sc_m, g_m, sh_f, sc_f, g_f = [v.reshape(bsz, 1, d) for v in jnp.split(mod[l], 6, axis=-1)]
        a_m = norm_mix_g[l][None, None, :] * (1.0 + sc_m)
        proj = _inproj(x2, a_m, sh_m, w_in_sections, l, seq, tm)
        y_ssd = _ssd(proj, ssd_conv_w[l], ssd_conv_b[l], ssd_dt_bias[l], ssd_a_log[l],
                     ssd_d[l], ssd_norm_g[l], bsz, seq, tt)
        qa, ka, va, aux = _foxprep(proj, fox_f_bias[l], bsz, seq)
        o_att = _fox(qa, ka, va, aux)
        y_cnv = _conformer(proj, cm_conv_w[l], cm_conv_b[l], cm_ln_g[l], cm_ln_b[l], bsz, seq, tt)

        w_r = jnp.concatenate([w_router_group[l], w_router_expert[l],
                               jnp.zeros((d, LANES - N_GROUPS - N_EXPERTS), F32)], axis=1)
        w_rh = w_r.astype(BF16)
        w_rl = (w_r - w_rh.astype(F32)).astype(BF16)
        b_r = jnp.concatenate([b_router_group[l], b_router_expert[l],
                               jnp.zeros((LANES - N_GROUPS - N_EXPERTS,), F32)]).reshape(1, LANES)
        a_f = norm_ffn_g[l][None, None, :] * (1.0 + sc_f)
        x2, h2, rt, rtt, cnt = _outproj(x2, y_ssd, o_att, y_cnv, w_out[l].astype(BF16),
                                        fox_norm_g[l].reshape(1, -1), g_m, a_f, sh_f, w_rh, w_rl, b_r,
                                        seq, tm)
        plan = _moe_plan(cnt[:, 0, :N_EXPERTS].astype(jnp.int32), n_blocks)
        xs = _dispatch(plan, h2, rtt, n_blocks * MOE_BLOCK, tm)
        ys = _experts(plan["block_e"], plan["n_used"], xs, w_gate, w_up, w_down, l)
        x2 = _combine(plan, x2, ys, rt, g_f, final_norm_g.reshape(1, d), seq, tm,
                      final=(l == DEPTH - 1))
    return x2.reshape(bsz, seq, d)
```

```python
import functools

import jax
import jax.numpy as jnp
import numpy as np
from jax import lax
from jax.experimental import pallas as pl
from jax.experimental.pallas import tpu as pltpu

F32 = jnp.float32
BF16 = jnp.bfloat16
HIGHEST = lax.Precision.HIGHEST

D_MODEL = 1024
DEPTH = 4
SSD_WIDTH = 512
SSD_HEADS = 8
SSD_HEAD_DIM = 64
SSD_STATE = 128
SSD_CONV = 4
SSD_CHUNK = 128
XBC_WIDTH = 1024
ATTN_WIDTH = 256
ATTN_HEADS = 4
ATTN_HEAD_DIM = 64
CONV_WIDTH = 256
CONV_KERNEL = 31
N_GROUPS = 4
PER_GROUP = 8
N_EXPERTS = 32
D_EXPERT = 512
MOE_BLOCK = 512
EPS = 1e-6

LANES = 128
SUBLANES = 8
COL_XBC = 0
COL_Z = 1024
COL_QKV = 1536
COL_GA = 2304
COL_GB = 2560
COL_SM = 2816
NP = 2944
SM_DT = 0
SM_F = 8

VMEM_LIMIT = 56 * 1024 * 1024


def _cparams(sem):
    return pltpu.CompilerParams(dimension_semantics=sem, vmem_limit_bytes=VMEM_LIMIT)


def _sigmoid(x):
    return 1.0 / (1.0 + jnp.exp(-x))


def _silu(x):
    return x * _sigmoid(x)


def _softplus(x):
    return jnp.maximum(x, 0.0) + jnp.log1p(jnp.exp(-jnp.abs(x)))


def _iota(shape, dim):
    return lax.broadcasted_iota(jnp.int32, shape, dim)


def _split_bf16x3(x):
    hi = x.astype(BF16)
    r1 = x - hi.astype(F32)
    mid = r1.astype(BF16)
    lo = (r1 - mid.astype(F32)).astype(BF16)
    return hi, mid, lo


def _dot_onehot_rhs(x, sel):
    return sum(jnp.dot(part, sel, preferred_element_type=F32) for part in _split_bf16x3(x))


def _dot_onehot_lhs(sel, x):
    return sum(jnp.dot(sel, part, preferred_element_type=F32) for part in _split_bf16x3(x))


def _mod_kernel(c_ref, w_ref, b_ref, o_ref):
    cond = _silu(c_ref[...])
    o_ref[0] = jnp.dot(cond, w_ref[0], precision=HIGHEST, preferred_element_type=F32) + b_ref[0]


def _modulation(c, ada_w, ada_b):
    bsz = c.shape[0]
    rows = 8
    cpad = jnp.zeros((rows, D_MODEL), F32).at[:bsz].set(c)
    tn = 1536
    n6 = 6 * D_MODEL
    out = pl.pallas_call(
        _mod_kernel,
        grid=(DEPTH, n6 // tn),
        in_specs=[
            pl.BlockSpec((rows, D_MODEL), lambda l, j: (0, 0)),
            pl.BlockSpec((1, D_MODEL, tn), lambda l, j: (l, 0, j)),
            pl.BlockSpec((1, 1, tn), lambda l, j: (l, 0, j)),
        ],
        out_specs=pl.BlockSpec((1, rows, tn), lambda l, j: (l, 0, j)),
        out_shape=jax.ShapeDtypeStruct((DEPTH, rows, n6), F32),
        compiler_params=_cparams(("arbitrary", "arbitrary")),
        name="adaln_mod",
    )(cpad, ada_w, ada_b.reshape(DEPTH, 1, n6))
    return out[:, :bsz]


def _inproj_kernel(x_ref, a_ref, s_ref, *refs):
    w_refs, o_ref = refs[:-1], refs[-1]
    x = x_ref[...]
    ms = jnp.mean(x * x, axis=-1, keepdims=True)
    h = (x * lax.rsqrt(ms + EPS) * a_ref[0] + s_ref[0]).astype(BF16)
    col = 0
    for w_ref in w_refs:
        width = w_ref.shape[-1]
        o_ref[:, col:col + width] = jnp.dot(h, w_ref[0], preferred_element_type=F32)
        col += width


def _inproj(x2, a, s, w_sections, layer, seq, tm):
    m = x2.shape[0]
    per_b = seq // tm
    assert sum(w.shape[-1] for w in w_sections) == NP
    return pl.pallas_call(
        _inproj_kernel,
        grid=(m // tm,),
        in_specs=[
            pl.BlockSpec((tm, D_MODEL), lambda i: (i, 0)),
            pl.BlockSpec((1, 1, D_MODEL), lambda i: (i // per_b, 0, 0)),
            pl.BlockSpec((1, 1, D_MODEL), lambda i: (i // per_b, 0, 0)),
        ] + [pl.BlockSpec((1, D_MODEL, w.shape[-1]), lambda i: (layer, 0, 0)) for w in w_sections],
        out_specs=pl.BlockSpec((tm, NP), lambda i: (i, 0)),
        out_shape=jax.ShapeDtypeStruct((m, NP), F32),
        compiler_params=_cparams(("arbitrary",)),
        name="in_proj",
    )(x2, a, s, *w_sections)


def _ssd_kernel(z_ref, xbc_ref, sm_ref, cw_ref, cb_ref, dtb_ref, alog_ref, dtbt_ref,
                alogt_ref, e_ref, dx_ref, ng_ref, y_ref, xpad, xc, prev, *, tt):
    t = pl.program_id(1)

    @pl.when(t == 0)
    def _():
        xpad[0:8, :] = jnp.zeros((8, XBC_WIDTH), F32)
        prev[...] = jnp.zeros(prev.shape, F32)

    xpad[8:8 + tt, :] = xbc_ref[...]
    acc = jnp.broadcast_to(cb_ref[...], (tt, XBC_WIDTH))
    for k in range(SSD_CONV):
        off = 8 - (SSD_CONV - 1) + k
        acc = acc + cw_ref[k:k + 1, :] * xpad[off:off + tt, :]
    xc[...] = _silu(acc)
    xpad[0:8, :] = xpad[tt:tt + 8, :]

    cl = SSD_CHUNK
    row = _iota((cl, cl), 0)
    col = _iota((cl, cl), 1)
    causal = row >= col
    tril = jnp.where(causal, 1.0, 0.0).astype(BF16)
    triu = jnp.where(row <= col, 1.0, 0.0).astype(BF16)
    lo = col < SSD_HEAD_DIM
    lane1 = _iota((1, LANES), 1)
    a_row = jnp.where(lane1 < SSD_HEADS, -jnp.exp(alog_ref[...]), 0.0)
    a_col = -jnp.exp(alogt_ref[...])
    expand = e_ref[...]

    def chunk(c, carry):
        r0 = c * cl
        xs = xc[pl.ds(r0, cl), 0:SSD_WIDTH]
        bmat = xc[pl.ds(r0, cl), SSD_WIDTH:SSD_WIDTH + 2 * SSD_STATE]
        cmat = xc[pl.ds(r0, cl), SSD_WIDTH + 2 * SSD_STATE:XBC_WIDTH]
        sm = sm_ref[pl.ds(r0, cl), :]
        dt = _softplus(sm + dtb_ref[...])
        da = dt * a_row
        acs = _dot_onehot_lhs(tril, da)
        dtt = _softplus(sm.T[0:SSD_HEADS, :] + dtbt_ref[...])
        acst = _dot_onehot_rhs(dtt * a_col, triu)
        dt_x = _dot_onehot_rhs(dt, expand)
        acs_x = _dot_onehot_rhs(acs, expand)
        last = acs_x[cl - 1:cl, :]
        eacs_x = jnp.exp(acs_x)
        dte_x = jnp.exp(last - acs_x)
        cd_x = jnp.exp(last)
        xdt = xs * dt_x
        xdte = (xdt * dte_x).astype(BF16)
        zc = z_ref[pl.ds(r0, cl), :]
        for g in range(2):
            bg = bmat[:, g * SSD_STATE:(g + 1) * SSD_STATE]
            cg = cmat[:, g * SSD_STATE:(g + 1) * SSD_STATE].astype(BF16)
            bgt = bg.T.astype(BF16)
            cbm = jnp.dot(cg, bgt, preferred_element_type=F32)
            pair_out = []
            for j in range(2):
                p = 2 * g + j
                sl = slice(p * LANES, (p + 1) * LANES)
                xp = xdt[:, sl]
                yd = jnp.zeros((cl, LANES), F32)
                for half in range(2):
                    h = 2 * p + half
                    seg = acs[:, h:h + 1] - acst[h:h + 1, :]
                    dec = jnp.exp(jnp.where(causal, seg, -jnp.inf))
                    gm = (cbm * dec).astype(BF16)
                    own = lo if half == 0 else jnp.logical_not(lo)
                    xm = jnp.where(own, xp, 0.0).astype(BF16)
                    yd = yd + jnp.dot(gm, xm, preferred_element_type=F32)
                prev_p = prev[:, sl]
                yo = jnp.dot(cg, prev_p.astype(BF16), preferred_element_type=F32) * eacs_x[:, sl]
                st = jnp.dot(bgt, xdte[:, sl], preferred_element_type=F32)
                prev[:, sl] = prev_p * cd_x[:, sl] + st
                pair_out.append(yd + yo + xs[:, sl] * dx_ref[:, sl])
            gs = slice(g * 2 * LANES, (g + 1) * 2 * LANES)
            yg = jnp.concatenate(pair_out, axis=-1) * _silu(zc[:, gs])
            ms = jnp.mean(yg * yg, axis=-1, keepdims=True)
            y_ref[pl.ds(r0, cl), gs] = yg * lax.rsqrt(ms + EPS) * ng_ref[:, gs]
        return carry

    for c in range(tt // cl):
        chunk(c, 0)


def _ssd(proj, cw, cb, dtb, alog, dx, ng, bsz, seq, tt):
    m = bsz * seq
    nt = seq // tt
    pad = LANES - SSD_HEADS
    dtb_row = jnp.pad(dtb, (0, pad)).reshape(1, LANES)
    alog_row = jnp.pad(alog, (0, pad)).reshape(1, LANES)
    dtb_col = jnp.broadcast_to(dtb[:, None], (SSD_HEADS, SSD_CHUNK))
    alog_col = jnp.broadcast_to(alog[:, None], (SSD_HEADS, SSD_CHUNK))
    expand = np.zeros((LANES, SSD_WIDTH), np.float32)
    for h in range(SSD_HEADS):
        expand[h, h * SSD_HEAD_DIM:(h + 1) * SSD_HEAD_DIM] = 1.0
    dx_row = jnp.repeat(dx, SSD_HEAD_DIM).reshape(1, SSD_WIDTH)
    full = lambda shape: pl.BlockSpec(shape, lambda b, t: (0,) * len(shape))
    return pl.pallas_call(
        functools.partial(_ssd_kernel, tt=tt),
        grid=(bsz, nt),
        in_specs=[
            pl.BlockSpec((tt, SSD_WIDTH), lambda b, t: (b * nt + t, COL_Z // SSD_WIDTH)),
            pl.BlockSpec((tt, XBC_WIDTH), lambda b, t: (b * nt + t, COL_XBC // XBC_WIDTH + 0)),
            pl.BlockSpec((tt, LANES), lambda b, t: (b * nt + t, COL_SM // LANES)),
            full((SSD_CONV, XBC_WIDTH)),
            full((1, XBC_WIDTH)),
            full((1, LANES)),
            full((1, LANES)),
            full((SSD_HEADS, SSD_CHUNK)),
            full((SSD_HEADS, SSD_CHUNK)),
            full((LANES, SSD_WIDTH)),
            full((1, SSD_WIDTH)),
            full((1, SSD_WIDTH)),
        ],
        out_specs=pl.BlockSpec((tt, SSD_WIDTH), lambda b, t: (b * nt + t, 0)),
        out_shape=jax.ShapeDtypeStruct((m, SSD_WIDTH), F32),
        scratch_shapes=[
            pltpu.VMEM((tt + 8, XBC_WIDTH), F32),
            pltpu.VMEM((tt, XBC_WIDTH), F32),
            pltpu.VMEM((SSD_STATE, SSD_WIDTH), F32),
        ],
        compiler_params=_cparams(("arbitrary", "arbitrary")),
        name="ssd_scan",
    )(proj, proj, proj, cw, cb.reshape(1, XBC_WIDTH), dtb_row, alog_row, dtb_col, alog_col,
      jnp.asarray(expand, dtype=BF16), dx_row, ng.reshape(1, SSD_WIDTH))


LOG2E = 1.4426950408889634
ATT_BLOCK = 256
PRUNE_LOG2 = 140.0
FAST_GAP_LOG2 = 60.0
SHIFT_MARGIN_LOG2 = 1.0
AUX_CUMEND = 0
AUX_KMAX = 1


def _foxprep_kernel(qkv_ref, sm_ref, fb_ref, q_out, k_out, v_out, aux_out, carry, *, tt):
    t = pl.program_id(1)

    @pl.when(t == 0)
    def _():
        carry[...] = jnp.zeros(carry.shape, F32)
        aux_out[...] = jnp.zeros(aux_out.shape, F32)

    logit = sm_ref[...] + fb_ref[...]
    logf = -_softplus(-logit)
    row = _iota((tt, tt), 0)
    col = _iota((tt, tt), 1)
    tril = jnp.where(row >= col, 1.0, 0.0).astype(BF16)
    cum = _dot_onehot_lhs(tril, logf) + carry[...]
    carry[...] = cum[tt - 1:tt, :]

    lane = _iota((tt, LANES), 1)
    lane1 = _iota((1, LANES), 1)
    scale = ATTN_HEAD_DIM ** -0.5 * LOG2E
    for h in range(ATTN_HEADS):
        pair, half = h // 2, h % 2
        own = (lane < ATTN_HEAD_DIM) if half == 0 else (lane >= ATTN_HEAD_DIM)
        a0 = ATTN_HEAD_DIM * (1 - half)
        cs = jnp.broadcast_to(cum[:, SM_F + h:SM_F + h + 1], (tt, LANES)) * LOG2E
        hi = cs.astype(BF16).astype(F32)
        r1 = cs - hi
        mid = r1.astype(BF16).astype(F32)
        low = r1 - mid
        qp = qkv_ref[:, pair * LANES:(pair + 1) * LANES]
        kp = qkv_ref[:, ATTN_WIDTH + pair * LANES:ATTN_WIDTH + (pair + 1) * LANES]
        vp = qkv_ref[:, 2 * ATTN_WIDTH + pair * LANES:2 * ATTN_WIDTH + (pair + 1) * LANES]
        qa = jnp.where(lane == a0, hi, jnp.where(lane == a0 + 1, mid, jnp.where(
            lane == a0 + 2, low, jnp.where((lane >= a0 + 3) & (lane < a0 + 6), 1.0, 0.0))))
        ka = jnp.where(lane == a0 + 3, -hi, jnp.where(lane == a0 + 4, -mid, jnp.where(
            lane == a0 + 5, -low, jnp.where((lane >= a0) & (lane < a0 + 9), 1.0, 0.0))))
        q_out[0, h] = jnp.where(own, qp * scale, qa).astype(BF16)
        kb = jnp.where(own, kp, ka).astype(BF16)
        k_out[0, h] = kb
        v_out[0, h] = jnp.where(own, vp, jnp.where(lane == a0, 1.0, 0.0)).astype(BF16)
        kf = jnp.where(own, kb.astype(F32), 0.0)
        kn2 = jnp.max(jnp.sum(kf * kf, axis=-1, keepdims=True), axis=0, keepdims=True)
        here = lane1 == t
        aux_out[0, h, AUX_CUMEND:AUX_CUMEND + 1, :] = jnp.where(
            here, cs[tt - 1:tt, :], aux_out[0, h, AUX_CUMEND:AUX_CUMEND + 1, :])
        aux_out[0, h, AUX_KMAX:AUX_KMAX + 1, :] = jnp.where(
            here, jnp.sqrt(kn2), aux_out[0, h, AUX_KMAX:AUX_KMAX + 1, :])


def _foxprep(proj, fb, bsz, seq):
    tt = ATT_BLOCK
    nt = seq // tt
    assert nt <= LANES
    fb_row = jnp.zeros((1, LANES), F32).at[0, SM_F:SM_F + ATTN_HEADS].set(fb)
    shp = jax.ShapeDtypeStruct((bsz, ATTN_HEADS, seq, LANES), BF16)
    ospec = pl.BlockSpec((1, ATTN_HEADS, tt, LANES), lambda b, t: (b, 0, t, 0))
    return pl.pallas_call(
        functools.partial(_foxprep_kernel, tt=tt),
        grid=(bsz, nt),
        in_specs=[
            pl.BlockSpec((tt, 3 * ATTN_WIDTH), lambda b, t: (b * nt + t, COL_QKV // (3 * ATTN_WIDTH))),
            pl.BlockSpec((tt, LANES), lambda b, t: (b * nt + t, COL_SM // LANES)),
            pl.BlockSpec((1, LANES), lambda b, t: (0, 0)),
        ],
        out_specs=[ospec, ospec, ospec,
                   pl.BlockSpec((1, ATTN_HEADS, 8, LANES), lambda b, t: (b, 0, 0, 0))],
        out_shape=[shp, shp, shp, jax.ShapeDtypeStruct((bsz, ATTN_HEADS, 8, LANES), F32)],
        scratch_shapes=[pltpu.VMEM((1, LANES), F32)],
        compiler_params=_cparams(("arbitrary", "arbitrary")),
        name="fox_prep",
    )(proj, proj, fb_row)


def _fox_kernel(q_ref, k_ref, v_ref, aux_ref, o_ref, *, tq):
    i = pl.program_id(2)
    nt = (((1,), (1,)), ((), ()))
    lane = _iota((tq, LANES), 1)
    lane1 = _iota((1, LANES), 1)
    qs = (q_ref[0, 0], q_ref[0, 1])

    def step(hh, r0, m, acc, mask):
        kblk = k_ref[0, hh, pl.ds(r0, tq), :]
        vblk = v_ref[0, hh, pl.ds(r0, tq), :]
        s = lax.dot_general(qs[hh], kblk, nt, preferred_element_type=F32)
        if mask is not None:
            s = jnp.where(mask, s, -jnp.inf)
        m_new = jnp.maximum(m, jnp.max(s, axis=-1, keepdims=True))
        p = jnp.exp2(s - m_new)
        alpha = jnp.exp2(m - m_new)
        acc = acc * alpha + jnp.dot(p.astype(BF16), vblk, preferred_element_type=F32)
        return m_new, acc

    d0 = pl.multiple_of(i * tq, tq)
    diag = _iota((tq, tq), 0) >= _iota((tq, tq), 1)
    i_f = i.astype(F32)
    first = []
    gaps = []
    q_shift = []
    for hh in range(2):
        own = (lane < ATTN_HEAD_DIM) if hh == 0 else (lane >= ATTN_HEAD_DIM)
        a0 = ATTN_HEAD_DIM * (1 - hh)
        qf = qs[hh].astype(F32)
        qn = jnp.sqrt(jnp.sum(jnp.where(own, qf * qf, 0.0), axis=-1, keepdims=True))
        cum_t = jnp.sum(jnp.where((lane >= a0) & (lane < a0 + 3), qf, 0.0), axis=-1, keepdims=True)
        s_self = jnp.sum(qf * k_ref[0, hh, pl.ds(d0, tq), :].astype(F32), axis=-1, keepdims=True)
        aux = aux_ref[0, hh]
        cend = aux[AUX_CUMEND:AUX_CUMEND + 1, :]
        kmax = jnp.max(jnp.where(lane1 <= i, aux[AUX_KMAX:AUX_KMAX + 1, :], 0.0), axis=-1, keepdims=True)
        bound = qn * kmax
        slack = jnp.max(bound + cum_t - s_self, axis=0, keepdims=True)
        live = (lane1 < i) & (slack - cend > -PRUNE_LOG2)
        first.append(jnp.min(jnp.where(live, lane1.astype(F32), i_f)))
        s_hi, s_mid, s_lo = [part.astype(F32) for part in _split_bf16x3(bound + SHIFT_MARGIN_LOG2)]
        gaps.append(jnp.max(s_hi + s_mid + s_lo - s_self))
        q_shift.append(jnp.where(lane == a0 + 6, -s_hi, jnp.where(lane == a0 + 7, -s_mid, jnp.where(
            lane == a0 + 8, -s_lo, qf))).astype(BF16))
    j_first = jnp.minimum(first[0], first[1]).astype(jnp.int32)

    def fast(_):
        def blocks(c, r0, width, mask=None):
            out = []
            for hh in range(2):
                s = lax.dot_general(q_shift[hh], k_ref[0, hh, pl.ds(r0, width), :], nt,
                                    preferred_element_type=F32)
                if mask is not None:
                    s = jnp.where(mask, s, -jnp.inf)
                out.append(c[hh] + jnp.dot(jnp.exp2(s).astype(BF16), v_ref[0, hh, pl.ds(r0, width), :],
                                           preferred_element_type=F32))
            return tuple(out)

        zero = jnp.zeros((tq, LANES), F32)
        c = blocks((zero, zero), d0, tq, diag)
        n = i - j_first
        c = lax.cond((n & 1) == 1, lambda c: blocks(c, pl.multiple_of(j_first * tq, tq), tq),
                     lambda c: c, c)
        c = lax.cond((n & 2) == 2, lambda c: blocks(c, pl.multiple_of((j_first + (n & 1)) * tq, tq), 2 * tq),
                     lambda c: c, c)
        j0 = j_first + (n & 3)

        def body(jj, c):
            return blocks(c, pl.multiple_of((j0 + 4 * jj) * tq, tq), 4 * tq)
        return lax.fori_loop(0, lax.shift_right_logical(n, 2), body, c)

    def online(_):
        m_init = jnp.full((tq, 1), -jnp.inf, F32)
        acc_init = jnp.zeros((tq, LANES), F32)
        carry = step(0, d0, m_init, acc_init, diag) + step(1, d0, m_init, acc_init, diag)

        def body(j, c):
            r0 = pl.multiple_of(j * tq, tq)
            m0, a0_, m1, a1_ = c
            m0, a0_ = step(0, r0, m0, a0_, None)
            m1, a1_ = step(1, r0, m1, a1_, None)
            return m0, a0_, m1, a1_
        _, a0_, _, a1_ = lax.fori_loop(j_first, i, body, carry)
        return a0_, a1_

    acc0, acc1 = lax.cond(jnp.maximum(gaps[0], gaps[1]) <= FAST_GAP_LOG2, fast, online, None)
    den0 = jnp.sum(jnp.where(lane == ATTN_HEAD_DIM, acc0, 0.0), axis=-1, keepdims=True)
    den1 = jnp.sum(jnp.where(lane == 0, acc1, 0.0), axis=-1, keepdims=True)
    o_ref[...] = jnp.where(lane < ATTN_HEAD_DIM, acc0 / den0, acc1 / den1)


def _fox(qa, ka, va, aux):
    bsz, nh, seq, _ = qa.shape
    tq = ATT_BLOCK
    nq = seq // tq
    kv_spec = pl.BlockSpec((1, 2, seq, LANES), lambda b, p, i: (b, p, 0, 0))
    return pl.pallas_call(
        functools.partial(_fox_kernel, tq=tq),
        grid=(bsz, nh // 2, nq),
        in_specs=[pl.BlockSpec((1, 2, tq, LANES), lambda b, p, i: (b, p, i, 0)), kv_spec, kv_spec,
                  pl.BlockSpec((1, 2, 8, LANES), lambda b, p, i: (b, p, 0, 0))],
        out_specs=pl.BlockSpec((tq, LANES), lambda b, p, i: (b * nq + i, p)),
        out_shape=jax.ShapeDtypeStruct((bsz * seq, ATTN_WIDTH), F32),
        compiler_params=_cparams(("arbitrary", "arbitrary", "arbitrary")),
        name="fox_attn",
    )(qa, ka, va, aux)


CONF_HALO = 32
CONF_ROWS = 64


def _conf_kernel(ga_ref, gb_ref, w_ref, b_ref, lg_ref, lb_ref, y_ref, upad, ush, *, tt):
    t = pl.program_id(1)

    @pl.when(t == 0)
    def _():
        upad[0:CONF_HALO, :] = jnp.zeros((CONF_HALO, CONV_WIDTH), F32)

    upad[CONF_HALO:CONF_HALO + tt, :] = ga_ref[...] * _sigmoid(gb_ref[...])
    base = CONF_HALO - (CONV_KERNEL - 1)
    for ph in range(1, SUBLANES):
        ush[ph - 1] = upad[ph:ph + ush.shape[1], :]
    for r in range(tt // CONF_ROWS):
        r0 = r * CONF_ROWS
        acc = jnp.broadcast_to(b_ref[...], (CONF_ROWS, CONV_WIDTH))
        for k in range(CONV_KERNEL):
            ph = (base + k) % SUBLANES
            lo = r0 + base + k - ph
            src = upad[lo:lo + CONF_ROWS, :] if ph == 0 else ush[ph - 1, lo:lo + CONF_ROWS, :]
            acc = acc + w_ref[k:k + 1, :] * src
        mu = jnp.mean(acc, axis=-1, keepdims=True)
        cen = acc - mu
        var = jnp.mean(cen * cen, axis=-1, keepdims=True)
        y = cen * lax.rsqrt(var + EPS) * lg_ref[...] + lb_ref[...]
        y_ref[r0:r0 + CONF_ROWS, :] = _silu(y)
    upad[0:CONF_HALO, :] = upad[tt:tt + CONF_HALO, :]


def _conformer(proj, w, b, lg, lb, bsz, seq, tt):
    m = bsz * seq
    nt = seq // tt
    full = lambda shape: pl.BlockSpec(shape, lambda bb, t: (0,) * len(shape))
    return pl.pallas_call(
        functools.partial(_conf_kernel, tt=tt),
        grid=(bsz, nt),
        in_specs=[
            pl.BlockSpec((tt, CONV_WIDTH), lambda bb, t: (bb * nt + t, COL_GA // CONV_WIDTH)),
            pl.BlockSpec((tt, CONV_WIDTH), lambda bb, t: (bb * nt + t, COL_GB // CONV_WIDTH)),
            full((CONV_KERNEL, CONV_WIDTH)),
            full((1, CONV_WIDTH)),
            full((1, CONV_WIDTH)),
            full((1, CONV_WIDTH)),
        ],
        out_specs=pl.BlockSpec((tt, CONV_WIDTH), lambda bb, t: (bb * nt + t, 0)),
        out_shape=jax.ShapeDtypeStruct((m, CONV_WIDTH), F32),
        scratch_shapes=[pltpu.VMEM((tt + CONF_HALO, CONV_WIDTH), F32),
                        pltpu.VMEM((SUBLANES - 1, tt + CONF_HALO - SUBLANES, CONV_WIDTH), F32)],
        compiler_params=_cparams(("arbitrary", "arbitrary")),
        name="conformer_conv",
    )(proj, proj, w, b.reshape(1, -1), lg.reshape(1, -1), lb.reshape(1, -1))


ROUTE_BIG = 1e9
CHUNK = 8
RT_W1, RT_W2, RT_ROW1, RT_ROW2 = 0, 1, 2, 3


def _split3(x):
    hi = x.astype(BF16)
    lo = (x - hi.astype(F32)).astype(BF16)
    return hi, lo


def _outproj_kernel(x_ref, ys_ref, oa_ref, yc_ref, wo_ref, fg_ref, gm_ref, a2_ref, s2_ref,
                    wr_ref, br_ref, xn_ref, h2_ref, rt_ref, rtt_ref, cnt_ref):
    tm = x_ref.shape[0]
    lane = _iota((tm, LANES), 1)
    att = oa_ref[...]
    ms = jnp.mean(att * att, axis=-1, keepdims=True)
    att = att * lax.rsqrt(ms + EPS) * fg_ref[...]
    y = jnp.dot(ys_ref[...].astype(BF16), wo_ref[0:SSD_WIDTH, :], preferred_element_type=F32)
    y = y + jnp.dot(att.astype(BF16), wo_ref[SSD_WIDTH:SSD_WIDTH + ATTN_WIDTH, :],
                    preferred_element_type=F32)
    y = y + jnp.dot(yc_ref[...].astype(BF16), wo_ref[SSD_WIDTH + ATTN_WIDTH:, :],
                    preferred_element_type=F32)
    xn = x_ref[...] + gm_ref[0] * y
    xn_ref[...] = xn
    ms2 = jnp.mean(xn * xn, axis=-1, keepdims=True)
    h2 = xn * lax.rsqrt(ms2 + EPS) * a2_ref[0] + s2_ref[0]
    h2_ref[...] = h2.astype(BF16)

    hh, hl = _split3(h2)
    part = jnp.dot(jnp.concatenate([hh, hl], axis=0), wr_ref[...], preferred_element_type=F32)
    logits = (part[:tm, :LANES] + part[:tm, LANES:] + part[tm:, :LANES] + part[tm:, LANES:]) + br_ref[...]
    lanef = lane.astype(F32)
    lg = jnp.where(lane < N_GROUPS, logits, -jnp.inf)
    gmax = jnp.max(lg, axis=-1, keepdims=True)
    gsum = jnp.sum(jnp.exp(lg - gmax), axis=-1, keepdims=True)
    gidx = jnp.min(jnp.where(lg == gmax, lanef, ROUTE_BIG), axis=-1, keepdims=True)
    e_lo = N_GROUPS + PER_GROUP * gidx
    le = jnp.where((lanef >= e_lo) & (lanef < e_lo + PER_GROUP), logits, -jnp.inf)
    m1 = jnp.max(le, axis=-1, keepdims=True)
    i1 = jnp.min(jnp.where(le == m1, lanef, ROUTE_BIG), axis=-1, keepdims=True)
    le2 = jnp.where(lanef == i1, -jnp.inf, le)
    m2 = jnp.max(le2, axis=-1, keepdims=True)
    i2 = jnp.min(jnp.where(le2 == m2, lanef, ROUTE_BIG), axis=-1, keepdims=True)
    esum = jnp.sum(jnp.exp(le - m1), axis=-1, keepdims=True)
    p1 = 1.0 / esum
    p2 = jnp.exp(m2 - m1) / esum
    psel = 1.0 / gsum
    w1 = p1 / (p1 + p2) * psel
    w2 = p2 / (p1 + p2) * psel

    oh1 = (lanef == (i1 - N_GROUPS)).astype(F32)
    oh2 = (lanef == (i2 - N_GROUPS)).astype(F32)
    oh = oh1 + oh2
    cnt = jnp.sum(oh, axis=0, keepdims=True)
    before = (_iota((tm, tm), 0) > _iota((tm, tm), 1)).astype(BF16)
    rank = jnp.dot(before, oh.astype(BF16), preferred_element_type=F32)
    chunks = jnp.floor((cnt + (CHUNK - 1)) * (1.0 / CHUNK))
    below = (_iota((LANES, LANES), 0) < _iota((LANES, LANES), 1)).astype(BF16)
    seg_lo = CHUNK * jnp.dot(jnp.broadcast_to(chunks, (8, LANES)).astype(BF16), below,
                             preferred_element_type=F32)[0:1, :]
    lr1 = jnp.sum(oh1 * (rank + seg_lo), axis=-1, keepdims=True)
    lr2 = jnp.sum(oh2 * (rank + seg_lo), axis=-1, keepdims=True)
    route = jnp.where(lane == RT_W1, w1, jnp.where(lane == RT_W2, w2, jnp.where(
        lane == RT_ROW1, lr1, jnp.where(lane == RT_ROW2, lr2, 0.0))))
    rt_ref[...] = route
    rtt_ref[0] = route.T[0:8, :]
    cnt_ref[0] = jnp.broadcast_to(cnt, (8, LANES))


def _outproj(x2, ys, oa, yc, wo, fg, gm, a2, s2, wr, br, seq, tm):
    m = x2.shape[0]
    per_b = seq // tm
    nt = m // tm
    row = lambda w: pl.BlockSpec((tm, w), lambda i: (i, 0))
    bvec = pl.BlockSpec((1, 1, D_MODEL), lambda i: (i // per_b, 0, 0))
    full = lambda shape: pl.BlockSpec(shape, lambda i: (0,) * len(shape))
    return pl.pallas_call(
        _outproj_kernel,
        grid=(nt,),
        in_specs=[
            row(D_MODEL), row(SSD_WIDTH), row(ATTN_WIDTH), row(CONV_WIDTH),
            full((D_MODEL, D_MODEL)), full((1, ATTN_WIDTH)), bvec, bvec, bvec,
            full((D_MODEL, 2 * LANES)), full((1, LANES)),
        ],
        out_specs=[row(D_MODEL), row(D_MODEL), row(LANES),
                   pl.BlockSpec((1, 8, tm), lambda i: (i, 0, 0)),
                   pl.BlockSpec((1, 8, LANES), lambda i: (i, 0, 0))],
        out_shape=[jax.ShapeDtypeStruct((m, D_MODEL), F32), jax.ShapeDtypeStruct((m, D_MODEL), BF16),
                   jax.ShapeDtypeStruct((m, LANES), F32), jax.ShapeDtypeStruct((nt, 8, tm), F32),
                   jax.ShapeDtypeStruct((nt, 8, LANES), F32)],
        compiler_params=_cparams(("arbitrary",)),
        name="out_proj_router",
    )(x2, ys, oa, yc, wo, fg, gm, a2, s2, wr, br)


def _local_rows(tm):
    return -(-(2 * tm + (CHUNK - 1) * N_EXPERTS) // LANES) * LANES


def _wait_chunks(n, n_max, copy_of_size):
    for bit in range(n_max.bit_length()):
        @pl.when((lax.shift_right_logical(n, bit) & 1) == 1)
        def _():
            copy_of_size(CHUNK << bit).wait()


def _dispatch_kernel(lo_ref, gb_ref, n8_ref, tot_ref, ts_ref, tn_ref, h_ref, rtt_ref, xs_hbm,
                     xloc, zeros, sem, zsem, *, lr):
    t = pl.program_id(0)
    nt = pl.num_programs(0)
    slot = t % 2
    tm = h_ref.shape[0]

    def seg_copy(sl, src, dst):
        return pltpu.make_async_copy(xloc.at[sl, pl.ds(src, CHUNK), :], xs_hbm.at[pl.ds(dst, CHUNK), :],
                                     sem.at[sl])

    def wait_tile(tile, sl):
        _wait_chunks(tot_ref[tile], lr // CHUNK, lambda size: pltpu.make_async_copy(
            xloc.at[sl, pl.ds(0, size), :], xs_hbm.at[pl.ds(0, size), :], sem.at[sl]))

    @pl.when(t >= 2)
    def _():
        wait_tile(t - 2, slot)

    rows = _iota((lr, tm), 0).astype(F32)
    perm = jnp.where(rows == rtt_ref[0, RT_ROW1:RT_ROW1 + 1, :], 1.0,
                     jnp.where(rows == rtt_ref[0, RT_ROW2:RT_ROW2 + 1, :], 1.0, 0.0))
    xloc[slot] = jnp.dot(perm.astype(BF16), h_ref[...], preferred_element_type=F32)

    def per_expert(e, c):
        idx = t * N_EXPERTS + e
        src0 = lo_ref[idx] * CHUNK
        dst0 = gb_ref[idx] * CHUNK

        def per_chunk(k, c2):
            seg_copy(slot, pl.multiple_of(src0 + k * CHUNK, CHUNK),
                     pl.multiple_of(dst0 + k * CHUNK, CHUNK)).start()
            return c2
        lax.fori_loop(0, n8_ref[idx], per_chunk, 0)
        return c
    lax.fori_loop(0, N_EXPERTS, per_expert, 0)

    @pl.when(t == nt - 1)
    def _():
        zeros[...] = jnp.zeros(zeros.shape, F32)

        def zero_copy(dst):
            return pltpu.make_async_copy(zeros, xs_hbm.at[pl.ds(dst, CHUNK), :], zsem.at[0])

        def fill(e, c):
            dst0 = ts_ref[e] * CHUNK

            def one(k, c2):
                zero_copy(pl.multiple_of(dst0 + k * CHUNK, CHUNK)).start()
                return c2
            lax.fori_loop(0, tn_ref[e], one, 0)
            return c
        lax.fori_loop(0, N_EXPERTS + 1, fill, 0)

        def drain(e, c):
            def one(k, c2):
                zero_copy(0).wait()
                return c2
            lax.fori_loop(0, tn_ref[e], one, 0)
            return c
        lax.fori_loop(0, N_EXPERTS + 1, drain, 0)

        @pl.when(t >= 1)
        def _():
            wait_tile(t - 1, 1 - slot)
        wait_tile(t, slot)


def _dispatch(plan, h2, rtt, n_rows, tm):
    m = h2.shape[0]
    lr = _local_rows(tm)
    grid_spec = pltpu.PrefetchScalarGridSpec(
        num_scalar_prefetch=6,
        grid=(m // tm,),
        in_specs=[pl.BlockSpec((tm, D_MODEL), lambda t, *_: (t, 0)),
                  pl.BlockSpec((1, 8, tm), lambda t, *_: (t, 0, 0))],
        out_specs=pl.BlockSpec(memory_space=pl.ANY),
        scratch_shapes=[pltpu.VMEM((2, lr, D_MODEL), F32), pltpu.VMEM((CHUNK, D_MODEL), F32),
                        pltpu.SemaphoreType.DMA((2,)), pltpu.SemaphoreType.DMA((1,))],
    )
    return pl.pallas_call(
        functools.partial(_dispatch_kernel, lr=lr),
        grid_spec=grid_spec,
        out_shape=jax.ShapeDtypeStruct((n_rows, D_MODEL), F32),
        compiler_params=_cparams(("arbitrary",)),
        name="moe_dispatch",
    )(plan["lo8"], plan["gb8"], plan["n8"], plan["tot8"], plan["ts8"], plan["tn8"], h2, rtt)


def _expert_kernel(be_ref, nu_ref, x_ref, wg_ref, wu_ref, wd_ref, y_ref, wgb, wub, wdb):
    i = pl.program_id(0)

    @pl.when(i < nu_ref[0])
    def _():
        prev_e = be_ref[jnp.maximum(i - 1, 0)]

        @pl.when((i == 0) | (be_ref[i] != prev_e))
        def _():
            wgb[...] = wg_ref[0, 0].astype(BF16)
            wub[...] = wu_ref[0, 0].astype(BF16)
            wdb[...] = wd_ref[0, 0].astype(BF16)

        x = x_ref[...].astype(BF16)
        hid = _silu(jnp.dot(x, wgb[...], preferred_element_type=F32)) * jnp.dot(
            x, wub[...], preferred_element_type=F32)
        y_ref[...] = jnp.dot(hid.astype(BF16), wdb[...], preferred_element_type=F32)

    @pl.when(i >= nu_ref[0])
    def _():
        y_ref[...] = jnp.zeros(y_ref.shape, F32)


def _experts(block_e, n_used, xs, wg, wu, wd, layer):
    n_rows = xs.shape[0]
    wspec = lambda shape: pl.BlockSpec((1, 1) + shape, lambda i, be, nu: (layer, be[i], 0, 0))
    grid_spec = pltpu.PrefetchScalarGridSpec(
        num_scalar_prefetch=2,
        grid=(n_rows // MOE_BLOCK,),
        in_specs=[
            pl.BlockSpec((MOE_BLOCK, D_MODEL), lambda i, be, nu: (i, 0)),
            wspec((D_MODEL, D_EXPERT)), wspec((D_MODEL, D_EXPERT)), wspec((D_EXPERT, D_MODEL)),
        ],
        out_specs=pl.BlockSpec((MOE_BLOCK, D_MODEL), lambda i, be, nu: (i, 0)),
        scratch_shapes=[
            pltpu.VMEM((D_MODEL, D_EXPERT), BF16),
            pltpu.VMEM((D_MODEL, D_EXPERT), BF16),
            pltpu.VMEM((D_EXPERT, D_MODEL), BF16),
        ],
    )
    return pl.pallas_call(
        _expert_kernel,
        grid_spec=grid_spec,
        out_shape=jax.ShapeDtypeStruct((n_rows, D_MODEL), F32),
        compiler_params=_cparams(("arbitrary",)),
        name="moe_experts",
    )(block_e, n_used, xs, wg, wu, wd)


def _combine_kernel(lo_ref, gb_ref, n8_ref, tot_ref, x_ref, rt_ref, rtt_ref, gf_ref, fg_ref, ys_hbm,
                    o_ref, yloc, sem, *, lr, final):
    t = pl.program_id(0)
    nt = pl.num_programs(0)
    slot = t % 2
    tm = x_ref.shape[0]

    def fetch(tile, sl):
        def per_expert(e, c):
            idx = tile * N_EXPERTS + e
            dst0 = lo_ref[idx] * CHUNK
            src0 = gb_ref[idx] * CHUNK

            def per_chunk(k, c2):
                pltpu.make_async_copy(ys_hbm.at[pl.ds(pl.multiple_of(src0 + k * CHUNK, CHUNK), CHUNK), :],
                                      yloc.at[sl, pl.ds(pl.multiple_of(dst0 + k * CHUNK, CHUNK), CHUNK), :],
                                      sem.at[sl]).start()
                return c2
            lax.fori_loop(0, n8_ref[idx], per_chunk, 0)
            return c
        lax.fori_loop(0, N_EXPERTS, per_expert, 0)

    @pl.when(t == 0)
    def _():
        yloc[...] = jnp.zeros(yloc.shape, F32)
        fetch(0, 0)

    @pl.when(t + 1 < nt)
    def _():
        fetch(t + 1, 1 - slot)

    _wait_chunks(tot_ref[t], lr // CHUNK, lambda size: pltpu.make_async_copy(
        ys_hbm.at[pl.ds(0, size), :], yloc.at[slot, pl.ds(0, size), :], sem.at[slot]))

    rows = _iota((lr, tm), 0).astype(F32)
    w_row = jnp.sum(jnp.where(rows == rtt_ref[0, RT_ROW1:RT_ROW1 + 1, :], rtt_ref[0, RT_W1:RT_W1 + 1, :],
                              jnp.where(rows == rtt_ref[0, RT_ROW2:RT_ROW2 + 1, :],
                                        rtt_ref[0, RT_W2:RT_W2 + 1, :], 0.0)), axis=-1, keepdims=True)
    ysw = (yloc[slot] * w_row).astype(BF16)
    rt = rt_ref[...]
    cols = _iota((tm, lr), 1).astype(F32)
    pick = jnp.where(cols == rt[:, RT_ROW1:RT_ROW1 + 1], 1.0,
                     jnp.where(cols == rt[:, RT_ROW2:RT_ROW2 + 1], 1.0, 0.0)).astype(BF16)
    x = x_ref[...] + gf_ref[0] * jnp.dot(pick, ysw, preferred_element_type=F32)
    if final:
        ms = jnp.mean(x * x, axis=-1, keepdims=True)
        x = x * lax.rsqrt(ms + EPS) * fg_ref[...]
    o_ref[...] = x


def _combine(plan, x2, ys, rt, rtt, gf, fg, seq, tm, final):
    m = x2.shape[0]
    per_b = seq // tm
    lr = _local_rows(tm)
    grid_spec = pltpu.PrefetchScalarGridSpec(
        num_scalar_prefetch=4,
        grid=(m // tm,),
        in_specs=[
            pl.BlockSpec((tm, D_MODEL), lambda i, *_: (i, 0)),
            pl.BlockSpec((tm, LANES), lambda i, *_: (i, 0)),
            pl.BlockSpec((1, 8, tm), lambda i, *_: (i, 0, 0)),
            pl.BlockSpec((1, 1, D_MODEL), lambda i, *_: (i // per_b, 0, 0)),
            pl.BlockSpec((1, D_MODEL), lambda i, *_: (0, 0)),
            pl.BlockSpec(memory_space=pl.ANY),
        ],
        out_specs=pl.BlockSpec((tm, D_MODEL), lambda i, *_: (i, 0)),
        scratch_shapes=[pltpu.VMEM((2, lr, D_MODEL), F32), pltpu.SemaphoreType.DMA((2,))],
    )
    return pl.pallas_call(
        functools.partial(_combine_kernel, lr=lr, final=final),
        grid_spec=grid_spec,
        out_shape=jax.ShapeDtypeStruct((m, D_MODEL), F32),
        compiler_params=_cparams(("arbitrary",)),
        name="moe_combine",
    )(plan["lo8"], plan["gb8"], plan["n8"], plan["tot8"], x2, rt, rtt, gf, fg, ys)


def _moe_plan(c, n_blocks):
    i32 = jnp.int32
    blk8 = MOE_BLOCK // CHUNK
    c8 = (c + CHUNK - 1) // CHUNK
    lo8 = jnp.cumsum(c8, axis=1) - c8
    per_e = jnp.sum(c8, axis=0)
    pad8 = (per_e + blk8 - 1) // blk8 * blk8
    end8 = jnp.cumsum(pad8)
    start8 = end8 - pad8
    gb8 = start8[None, :] + jnp.cumsum(c8, axis=0) - c8
    blk_start8 = jnp.arange(n_blocks, dtype=i32) * blk8
    block_e = jnp.minimum(jnp.sum(end8[None, :] <= blk_start8[:, None], axis=1), N_EXPERTS - 1)
    return {
        "lo8": lo8.reshape(-1).astype(i32), "gb8": gb8.reshape(-1).astype(i32),
        "n8": c8.reshape(-1).astype(i32), "tot8": jnp.sum(c8, axis=1).astype(i32),
        "ts8": jnp.concatenate([start8 + per_e, end8[-1:]]).astype(i32),
        "tn8": jnp.concatenate([pad8 - per_e, n_blocks * blk8 - end8[-1:]]).astype(i32),
        "block_e": block_e.astype(i32), "n_used": (end8[-1:] // blk8).astype(i32),
    }


def _w_in_sections(w):
    d_dt = SSD_WIDTH + XBC_WIDTH
    d_q = d_dt + SSD_HEADS
    d_f = d_q + 3 * ATTN_WIDTH
    d_ga = d_f + ATTN_HEADS
    small = jnp.concatenate([w[..., d_dt:d_q], w[..., d_f:d_ga],
                             jnp.zeros(w.shape[:-1] + (LANES - SSD_HEADS - ATTN_HEADS,), w.dtype)], axis=-1)
    sections = [w[..., SSD_WIDTH:d_dt], w[..., :SSD_WIDTH], w[..., d_q:d_f], w[..., d_ga:], small]
    return [sec.astype(BF16) for sec in sections]


def kernel(x, c, ada_w, ada_b, norm_mix_g, w_in, ssd_conv_w, ssd_conv_b, ssd_dt_bias, ssd_a_log,
           ssd_d, ssd_norm_g, fox_f_bias, fox_norm_g, cm_conv_w, cm_conv_b, cm_ln_g, cm_ln_b, w_out,
           norm_ffn_g, w_router_group, b_router_group, w_router_expert, b_router_expert, w_gate,
           w_up, w_down, final_norm_g):
    bsz, seq, d = x.shape
    m = bsz * seq
    tm = min(512, seq)
    tt = min(256, seq)
    n_blocks = -(-(2 * m + (CHUNK - 1) * (m // tm) * N_EXPERTS) // MOE_BLOCK) + N_EXPERTS

    mod = _modulation(c, ada_w, ada_b)
    x2 = x.reshape(m, d)
    w_in_sections = _w_in_sections(w_in)
    for l in range(DEPTH):
        sh_m, sc_m, g_m, sh_f, sc_f, g_f = [v.reshape(bsz, 1, d) for v in jnp.split(mod[l], 6, axis=-1)]
        a_m = norm_mix_g[l][None, None, :] * (1.0 + sc_m)
        proj = _inproj(x2, a_m, sh_m, w_in_sections, l, seq, tm)
        y_ssd = _ssd(proj, ssd_conv_w[l], ssd_conv_b[l], ssd_dt_bias[l], ssd_a_log[l],
                     ssd_d[l], ssd_norm_g[l], bsz, seq, tt)
        qa, ka, va, aux = _foxprep(proj, fox_f_bias[l], bsz, seq)
        o_att = _fox(qa, ka, va, aux)
        y_cnv = _conformer(proj, cm_conv_w[l], cm_conv_b[l], cm_ln_g[l], cm_ln_b[l], bsz, seq, tt)

        w_r = jnp.concatenate([w_router_group[l], w_router_expert[l],
                               jnp.zeros((d, LANES - N_GROUPS - N_EXPERTS), F32)], axis=1)
        w_rh = w_r.astype(BF16)
        w_r2 = jnp.concatenate([w_rh, (w_r - w_rh.astype(F32)).astype(BF16)], axis=1)
        b_r = jnp.concatenate([b_router_group[l], b_router_expert[l],
                               jnp.zeros((LANES - N_GROUPS - N_EXPERTS,), F32)]).reshape(1, LANES)
        a_f = norm_ffn_g[l][None, None, :] * (1.0 + sc_f)
        x2, h2, rt, rtt, cnt = _outproj(x2, y_ssd, o_att, y_cnv, w_out[l].astype(BF16),
                                        fox_norm_g[l].reshape(1, -1), g_m, a_f, sh_f, w_r2, b_r,
                                        seq, tm)
        plan = _moe_plan(cnt[:, 0, :N_EXPERTS].astype(jnp.int32), n_blocks)
        xs = _dispatch(plan, h2, rtt, n_blocks * MOE_BLOCK, tm)
        ys = _experts(plan["block_e"], plan["n_used"], xs, w_gate, w_up, w_down, l)
        x2 = _combine(plan, x2, ys, rt, rtt, g_f, final_norm_g.reshape(1, d), seq, tm,
                      final=(l == DEPTH - 1))
    return x2.reshape(bsz, seq, d)
```

```python
import functools

import jax
import jax.numpy as jnp
import numpy as np
from jax import lax
from jax.experimental import pallas as pl
from jax.experimental.pallas import tpu as pltpu

F32 = jnp.float32
BF16 = jnp.bfloat16
HIGHEST = lax.Precision.HIGHEST

D_MODEL = 1024
DEPTH = 4
SSD_WIDTH = 512
SSD_HEADS = 8
SSD_HEAD_DIM = 64
SSD_STATE = 128
SSD_CONV = 4
SSD_CHUNK = 128
XBC_WIDTH = 1024
ATTN_WIDTH = 256
ATTN_HEADS = 4
ATTN_HEAD_DIM = 64
CONV_WIDTH = 256
CONV_KERNEL = 31
N_GROUPS = 4
PER_GROUP = 8
N_EXPERTS = 32
D_EXPERT = 512
MOE_BLOCK = 512
EPS = 1e-6

LANES = 128
SUBLANES = 8
COL_XBC = 0
COL_Z = 1024
COL_QKV = 1536
COL_GA = 2304
COL_GB = 2560
COL_SM = 2816
NP = 2944
SM_DT = 0
SM_F = 8

VMEM_LIMIT = 56 * 1024 * 1024


def _cparams(sem):
    return pltpu.CompilerParams(dimension_semantics=sem, vmem_limit_bytes=VMEM_LIMIT)


def _sigmoid(x):
    return 1.0 / (1.0 + jnp.exp(-x))


def _silu(x):
    return x * _sigmoid(x)


def _softplus(x):
    return jnp.maximum(x, 0.0) + jnp.log1p(jnp.exp(-jnp.abs(x)))


def _iota(shape, dim):
    return lax.broadcasted_iota(jnp.int32, shape, dim)


def _split_bf16x3(x):
    hi = x.astype(BF16)
    r1 = x - hi.astype(F32)
    mid = r1.astype(BF16)
    lo = (r1 - mid.astype(F32)).astype(BF16)
    return hi, mid, lo


def _dot_onehot_rhs(x, sel):
    return sum(jnp.dot(part, sel, preferred_element_type=F32) for part in _split_bf16x3(x))


def _dot_onehot_lhs(sel, x):
    return sum(jnp.dot(sel, part, preferred_element_type=F32) for part in _split_bf16x3(x))


def _mod_kernel(c_ref, w_ref, b_ref, o_ref):
    cond = _silu(c_ref[...])
    o_ref[0] = jnp.dot(cond, w_ref[0], precision=HIGHEST, preferred_element_type=F32) + b_ref[0]


def _modulation(c, ada_w, ada_b):
    bsz = c.shape[0]
    rows = 8
    cpad = jnp.zeros((rows, D_MODEL), F32).at[:bsz].set(c)
    tn = 1536
    n6 = 6 * D_MODEL
    out = pl.pallas_call(
        _mod_kernel,
        grid=(DEPTH, n6 // tn),
        in_specs=[
            pl.BlockSpec((rows, D_MODEL), lambda l, j: (0, 0)),
            pl.BlockSpec((1, D_MODEL, tn), lambda l, j: (l, 0, j)),
            pl.BlockSpec((1, 1, tn), lambda l, j: (l, 0, j)),
        ],
        out_specs=pl.BlockSpec((1, rows, tn), lambda l, j: (l, 0, j)),
        out_shape=jax.ShapeDtypeStruct((DEPTH, rows, n6), F32),
        compiler_params=_cparams(("arbitrary", "arbitrary")),
        name="adaln_mod",
    )(cpad, ada_w, ada_b.reshape(DEPTH, 1, n6))
    return out[:, :bsz]


def _inproj_kernel(x_ref, a_ref, s_ref, *refs):
    w_refs, o_ref = refs[:-1], refs[-1]
    x = x_ref[...]
    ms = jnp.mean(x * x, axis=-1, keepdims=True)
    h = (x * lax.rsqrt(ms + EPS) * a_ref[0] + s_ref[0]).astype(BF16)
    col = 0
    for w_ref in w_refs:
        width = w_ref.shape[-1]
        o_ref[:, col:col + width] = jnp.dot(h, w_ref[0], preferred_element_type=F32)
        col += width


def _inproj(x2, a, s, w_sections, layer, seq, tm):
    m = x2.shape[0]
    per_b = seq // tm
    assert sum(w.shape[-1] for w in w_sections) == NP
    return pl.pallas_call(
        _inproj_kernel,
        grid=(m // tm,),
        in_specs=[
            pl.BlockSpec((tm, D_MODEL), lambda i: (i, 0)),
            pl.BlockSpec((1, 1, D_MODEL), lambda i: (i // per_b, 0, 0)),
            pl.BlockSpec((1, 1, D_MODEL), lambda i: (i // per_b, 0, 0)),
        ] + [pl.BlockSpec((1, D_MODEL, w.shape[-1]), lambda i: (layer, 0, 0)) for w in w_sections],
        out_specs=pl.BlockSpec((tm, NP), lambda i: (i, 0)),
        out_shape=jax.ShapeDtypeStruct((m, NP), F32),
        compiler_params=_cparams(("arbitrary",)),
        name="in_proj",
    )(x2, a, s, *w_sections)


def _ssd_kernel(z_ref, xbc_ref, sm_ref, cw_ref, cb_ref, dtb_ref, alog_ref, dtbt_ref,
                alogt_ref, e_ref, dx_ref, ng_ref, y_ref, xpad, xc, prev, *, tt):
    t = pl.program_id(1)

    @pl.when(t == 0)
    def _():
        xpad[0:8, :] = jnp.zeros((8, XBC_WIDTH), F32)
        prev[...] = jnp.zeros(prev.shape, F32)

    xpad[8:8 + tt, :] = xbc_ref[...]
    acc = jnp.broadcast_to(cb_ref[...], (tt, XBC_WIDTH))
    for k in range(SSD_CONV):
        off = 8 - (SSD_CONV - 1) + k
        acc = acc + cw_ref[k:k + 1, :] * xpad[off:off + tt, :]
    xc[...] = _silu(acc)
    xpad[0:8, :] = xpad[tt:tt + 8, :]

    cl = SSD_CHUNK
    row = _iota((cl, cl), 0)
    col = _iota((cl, cl), 1)
    causal = row >= col
    tril = jnp.where(causal, 1.0, 0.0).astype(BF16)
    triu = jnp.where(row <= col, 1.0, 0.0).astype(BF16)
    lo = col < SSD_HEAD_DIM
    lane1 = _iota((1, LANES), 1)
    a_row = jnp.where(lane1 < SSD_HEADS, -jnp.exp(alog_ref[...]), 0.0)
    a_col = -jnp.exp(alogt_ref[...])
    expand = e_ref[...]

    def chunk(c, carry):
        r0 = c * cl
        xs = xc[pl.ds(r0, cl), 0:SSD_WIDTH]
        bmat = xc[pl.ds(r0, cl), SSD_WIDTH:SSD_WIDTH + 2 * SSD_STATE]
        cmat = xc[pl.ds(r0, cl), SSD_WIDTH + 2 * SSD_STATE:XBC_WIDTH]
        sm = sm_ref[pl.ds(r0, cl), :]
        dt = _softplus(sm + dtb_ref[...])
        da = dt * a_row
        acs = _dot_onehot_lhs(tril, da)
        dtt = _softplus(sm.T[0:SSD_HEADS, :] + dtbt_ref[...])
        acst = _dot_onehot_rhs(dtt * a_col, triu)
        dt_x = _dot_onehot_rhs(dt, expand)
        acs_x = _dot_onehot_rhs(acs, expand)
        last = acs_x[cl - 1:cl, :]
        eacs_x = jnp.exp(acs_x)
        dte_x = jnp.exp(last - acs_x)
        cd_x = jnp.exp(last)
        xdt = xs * dt_x
        xdte = (xdt * dte_x).astype(BF16)
        zc = z_ref[pl.ds(r0, cl), :]
        for g in range(2):
            bg = bmat[:, g * SSD_STATE:(g + 1) * SSD_STATE]
            cg = cmat[:, g * SSD_STATE:(g + 1) * SSD_STATE].astype(BF16)
            bgt = bg.T.astype(BF16)
            cbm = jnp.dot(cg, bgt, preferred_element_type=F32)
            pair_out = []
            for j in range(2):
                p = 2 * g + j
                sl = slice(p * LANES, (p + 1) * LANES)
                xp = xdt[:, sl]
                yd = jnp.zeros((cl, LANES), F32)
                for half in range(2):
                    h = 2 * p + half
                    seg = acs[:, h:h + 1] - acst[h:h + 1, :]
                    dec = jnp.exp(jnp.where(causal, seg, -jnp.inf))
                    gm = (cbm * dec).astype(BF16)
                    own = lo if half == 0 else jnp.logical_not(lo)
                    xm = jnp.where(own, xp, 0.0).astype(BF16)
                    yd = yd + jnp.dot(gm, xm, preferred_element_type=F32)
                prev_p = prev[:, sl]
                yo = jnp.dot(cg, prev_p.astype(BF16), preferred_element_type=F32) * eacs_x[:, sl]
                st = jnp.dot(bgt, xdte[:, sl], preferred_element_type=F32)
                prev[:, sl] = prev_p * cd_x[:, sl] + st
                pair_out.append(yd + yo + xs[:, sl] * dx_ref[:, sl])
            gs = slice(g * 2 * LANES, (g + 1) * 2 * LANES)
            yg = jnp.concatenate(pair_out, axis=-1) * _silu(zc[:, gs])
            ms = jnp.mean(yg * yg, axis=-1, keepdims=True)
            y_ref[pl.ds(r0, cl), gs] = yg * lax.rsqrt(ms + EPS) * ng_ref[:, gs]
        return carry

    for c in range(tt // cl):
        chunk(c, 0)


def _ssd(proj, cw, cb, dtb, alog, dx, ng, bsz, seq, tt):
    m = bsz * seq
    nt = seq // tt
    pad = LANES - SSD_HEADS
    dtb_row = jnp.pad(dtb, (0, pad)).reshape(1, LANES)
    alog_row = jnp.pad(alog, (0, pad)).reshape(1, LANES)
    dtb_col = jnp.broadcast_to(dtb[:, None], (SSD_HEADS, SSD_CHUNK))
    alog_col = jnp.broadcast_to(alog[:, None], (SSD_HEADS, SSD_CHUNK))
    expand = np.zeros((LANES, SSD_WIDTH), np.float32)
    for h in range(SSD_HEADS):
        expand[h, h * SSD_HEAD_DIM:(h + 1) * SSD_HEAD_DIM] = 1.0
    dx_row = jnp.repeat(dx, SSD_HEAD_DIM).reshape(1, SSD_WIDTH)
    full = lambda shape: pl.BlockSpec(shape, lambda b, t: (0,) * len(shape))
    return pl.pallas_call(
        functools.partial(_ssd_kernel, tt=tt),
        grid=(bsz, nt),
        in_specs=[
            pl.BlockSpec((tt, SSD_WIDTH), lambda b, t: (b * nt + t, COL_Z // SSD_WIDTH)),
            pl.BlockSpec((tt, XBC_WIDTH), lambda b, t: (b * nt + t, COL_XBC // XBC_WIDTH + 0)),
            pl.BlockSpec((tt, LANES), lambda b, t: (b * nt + t, COL_SM // LANES)),
            full((SSD_CONV, XBC_WIDTH)),
            full((1, XBC_WIDTH)),
            full((1, LANES)),
            full((1, LANES)),
            full((SSD_HEADS, SSD_CHUNK)),
            full((SSD_HEADS, SSD_CHUNK)),
            full((LANES, SSD_WIDTH)),
            full((1, SSD_WIDTH)),
            full((1, SSD_WIDTH)),
        ],
        out_specs=pl.BlockSpec((tt, SSD_WIDTH), lambda b, t: (b * nt + t, 0)),
        out_shape=jax.ShapeDtypeStruct((m, SSD_WIDTH), F32),
        scratch_shapes=[
            pltpu.VMEM((tt + 8, XBC_WIDTH), F32),
            pltpu.VMEM((tt, XBC_WIDTH), F32),
            pltpu.VMEM((SSD_STATE, SSD_WIDTH), F32),
        ],
        compiler_params=_cparams(("arbitrary", "arbitrary")),
        name="ssd_scan",
    )(proj, proj, proj, cw, cb.reshape(1, XBC_WIDTH), dtb_row, alog_row, dtb_col, alog_col,
      jnp.asarray(expand, dtype=BF16), dx_row, ng.reshape(1, SSD_WIDTH))


LOG2E = 1.4426950408889634
ATT_BLOCK = 256
PRUNE_LOG2 = 140.0
FAST_GAP_LOG2 = 60.0
SHIFT_MARGIN_LOG2 = 1.0
AUX_CUMEND = 0
AUX_KMAX = 1


def _foxprep_kernel(qkv_ref, sm_ref, fb_ref, q_out, k_out, v_out, aux_out, carry, *, tt):
    t = pl.program_id(1)

    @pl.when(t == 0)
    def _():
        carry[...] = jnp.zeros(carry.shape, F32)
        aux_out[...] = jnp.zeros(aux_out.shape, F32)

    logit = sm_ref[...] + fb_ref[...]
    logf = -_softplus(-logit)
    row = _iota((tt, tt), 0)
    col = _iota((tt, tt), 1)
    tril = jnp.where(row >= col, 1.0, 0.0).astype(BF16)
    cum = _dot_onehot_lhs(tril, logf) + carry[...]
    carry[...] = cum[tt - 1:tt, :]

    lane = _iota((tt, LANES), 1)
    lane1 = _iota((1, LANES), 1)
    scale = ATTN_HEAD_DIM ** -0.5 * LOG2E
    for h in range(ATTN_HEADS):
        pair, half = h // 2, h % 2
        own = (lane < ATTN_HEAD_DIM) if half == 0 else (lane >= ATTN_HEAD_DIM)
        a0 = ATTN_HEAD_DIM * (1 - half)
        cs = jnp.broadcast_to(cum[:, SM_F + h:SM_F + h + 1], (tt, LANES)) * LOG2E
        hi = cs.astype(BF16).astype(F32)
        r1 = cs - hi
        mid = r1.astype(BF16).astype(F32)
        low = r1 - mid
        qp = qkv_ref[:, pair * LANES:(pair + 1) * LANES]
        kp = qkv_ref[:, ATTN_WIDTH + pair * LANES:ATTN_WIDTH + (pair + 1) * LANES]
        vp = qkv_ref[:, 2 * ATTN_WIDTH + pair * LANES:2 * ATTN_WIDTH + (pair + 1) * LANES]
        qa = jnp.where(lane == a0, hi, jnp.where(lane == a0 + 1, mid, jnp.where(
            lane == a0 + 2, low, jnp.where((lane >= a0 + 3) & (lane < a0 + 6), 1.0, 0.0))))
        ka = jnp.where(lane == a0 + 3, -hi, jnp.where(lane == a0 + 4, -mid, jnp.where(
            lane == a0 + 5, -low, jnp.where((lane >= a0) & (lane < a0 + 9), 1.0, 0.0))))
        q_out[0, h] = jnp.where(own, qp * scale, qa).astype(BF16)
        kb = jnp.where(own, kp, ka).astype(BF16)
        k_out[0, h] = kb
        v_out[0, h] = jnp.where(own, vp, jnp.where(lane == a0, 1.0, 0.0)).astype(BF16)
        kf = jnp.where(own, kb.astype(F32), 0.0)
        kn2 = jnp.max(jnp.sum(kf * kf, axis=-1, keepdims=True), axis=0, keepdims=True)
        here = lane1 == t
        aux_out[0, h, AUX_CUMEND:AUX_CUMEND + 1, :] = jnp.where(
            here, cs[tt - 1:tt, :], aux_out[0, h, AUX_CUMEND:AUX_CUMEND + 1, :])
        aux_out[0, h, AUX_KMAX:AUX_KMAX + 1, :] = jnp.where(
            here, jnp.sqrt(kn2), aux_out[0, h, AUX_KMAX:AUX_KMAX + 1, :])


def _foxprep(proj, fb, bsz, seq):
    tt = ATT_BLOCK
    nt = seq // tt
    assert nt <= LANES
    fb_row = jnp.zeros((1, LANES), F32).at[0, SM_F:SM_F + ATTN_HEADS].set(fb)
    shp = jax.ShapeDtypeStruct((bsz, ATTN_HEADS, seq, LANES), BF16)
    ospec = pl.BlockSpec((1, ATTN_HEADS, tt, LANES), lambda b, t: (b, 0, t, 0))
    return pl.pallas_call(
        functools.partial(_foxprep_kernel, tt=tt),
        grid=(bsz, nt),
        in_specs=[
            pl.BlockSpec((tt, 3 * ATTN_WIDTH), lambda b, t: (b * nt + t, COL_QKV // (3 * ATTN_WIDTH))),
            pl.BlockSpec((tt, LANES), lambda b, t: (b * nt + t, COL_SM // LANES)),
            pl.BlockSpec((1, LANES), lambda b, t: (0, 0)),
        ],
        out_specs=[ospec, ospec, ospec,
                   pl.BlockSpec((1, ATTN_HEADS, 8, LANES), lambda b, t: (b, 0, 0, 0))],
        out_shape=[shp, shp, shp, jax.ShapeDtypeStruct((bsz, ATTN_HEADS, 8, LANES), F32)],
        scratch_shapes=[pltpu.VMEM((1, LANES), F32)],
        compiler_params=_cparams(("arbitrary", "arbitrary")),
        name="fox_prep",
    )(proj, proj, fb_row)


def _fox_kernel(q_ref, k_ref, v_ref, aux_ref, o_ref, *, tq):
    i = pl.program_id(2)
    nt = (((1,), (1,)), ((), ()))
    lane = _iota((tq, LANES), 1)
    lane1 = _iota((1, LANES), 1)
    qs = (q_ref[0, 0], q_ref[0, 1])

    def step(hh, r0, m, acc, mask):
        kblk = k_ref[0, hh, pl.ds(r0, tq), :]
        vblk = v_ref[0, hh, pl.ds(r0, tq), :]
        s = lax.dot_general(qs[hh], kblk, nt, preferred_element_type=F32)
        if mask is not None:
            s = jnp.where(mask, s, -jnp.inf)
        m_new = jnp.maximum(m, jnp.max(s, axis=-1, keepdims=True))
        p = jnp.exp2(s - m_new)
        alpha = jnp.exp2(m - m_new)
        acc = acc * alpha + jnp.dot(p.astype(BF16), vblk, preferred_element_type=F32)
        return m_new, acc

    d0 = pl.multiple_of(i * tq, tq)
    diag = _iota((tq, tq), 0) >= _iota((tq, tq), 1)
    i_f = i.astype(F32)
    first = []
    gaps = []
    q_shift = []
    for hh in range(2):
        own = (lane < ATTN_HEAD_DIM) if hh == 0 else (lane >= ATTN_HEAD_DIM)
        a0 = ATTN_HEAD_DIM * (1 - hh)
        qf = qs[hh].astype(F32)
        qn = jnp.sqrt(jnp.sum(jnp.where(own, qf * qf, 0.0), axis=-1, keepdims=True))
        cum_t = jnp.sum(jnp.where((lane >= a0) & (lane < a0 + 3), qf, 0.0), axis=-1, keepdims=True)
        s_self = jnp.sum(qf * k_ref[0, hh, pl.ds(d0, tq), :].astype(F32), axis=-1, keepdims=True)
        aux = aux_ref[0, hh]
        cend = aux[AUX_CUMEND:AUX_CUMEND + 1, :]
        kmax = jnp.max(jnp.where(lane1 <= i, aux[AUX_KMAX:AUX_KMAX + 1, :], 0.0), axis=-1, keepdims=True)
        bound = qn * kmax
        slack = jnp.max(bound + cum_t - s_self, axis=0, keepdims=True)
        live = (lane1 < i) & (slack - cend > -PRUNE_LOG2)
        first.append(jnp.min(jnp.where(live, lane1.astype(F32), i_f)))
        s_hi, s_mid, s_lo = [part.astype(F32) for part in _split_bf16x3(bound + SHIFT_MARGIN_LOG2)]
        gaps.append(jnp.max(s_hi + s_mid + s_lo - s_self))
        q_shift.append(jnp.where(lane == a0 + 6, -s_hi, jnp.where(lane == a0 + 7, -s_mid, jnp.where(
            lane == a0 + 8, -s_lo, qf))).astype(BF16))
    j_first = jnp.minimum(first[0], first[1]).astype(jnp.int32)

    def fast(_):
        def blocks(c, r0, width, mask=None):
            out = []
            for hh in range(2):
                s = lax.dot_general(q_shift[hh], k_ref[0, hh, pl.ds(r0, width), :], nt,
                                    preferred_element_type=F32)
                if mask is not None:
                    s = jnp.where(mask, s, -jnp.inf)
                out.append(c[hh] + jnp.dot(jnp.exp2(s).astype(BF16), v_ref[0, hh, pl.ds(r0, width), :],
                                           preferred_element_type=F32))
            return tuple(out)

        zero = jnp.zeros((tq, LANES), F32)
        c = blocks((zero, zero), d0, tq, diag)
        n = i - j_first
        c = lax.cond((n & 1) == 1, lambda c: blocks(c, pl.multiple_of(j_first * tq, tq), tq),
                     lambda c: c, c)
        c = lax.cond((n & 2) == 2, lambda c: blocks(c, pl.multiple_of((j_first + (n & 1)) * tq, tq), 2 * tq),
                     lambda c: c, c)
        j0 = j_first + (n & 3)

        def body(jj, c):
            return blocks(c, pl.multiple_of((j0 + 4 * jj) * tq, tq), 4 * tq)
        return lax.fori_loop(0, lax.shift_right_logical(n, 2), body, c)

    def online(_):
        m_init = jnp.full((tq, 1), -jnp.inf, F32)
        acc_init = jnp.zeros((tq, LANES), F32)
        carry = step(0, d0, m_init, acc_init, diag) + step(1, d0, m_init, acc_init, diag)

        def body(j, c):
            r0 = pl.multiple_of(j * tq, tq)
            m0, a0_, m1, a1_ = c
            m0, a0_ = step(0, r0, m0, a0_, None)
            m1, a1_ = step(1, r0, m1, a1_, None)
            return m0, a0_, m1, a1_
        _, a0_, _, a1_ = lax.fori_loop(j_first, i, body, carry)
        return a0_, a1_

    acc0, acc1 = lax.cond(jnp.maximum(gaps[0], gaps[1]) <= FAST_GAP_LOG2, fast, online, None)
    den0 = jnp.sum(jnp.where(lane == ATTN_HEAD_DIM, acc0, 0.0), axis=-1, keepdims=True)
    den1 = jnp.sum(jnp.where(lane == 0, acc1, 0.0), axis=-1, keepdims=True)
    o_ref[...] = jnp.where(lane < ATTN_HEAD_DIM, acc0 / den0, acc1 / den1)


def _fox(qa, ka, va, aux):
    bsz, nh, seq, _ = qa.shape
    tq = ATT_BLOCK
    nq = seq // tq
    kv_spec = pl.BlockSpec((1, 2, seq, LANES), lambda b, p, i: (b, p, 0, 0))
    return pl.pallas_call(
        functools.partial(_fox_kernel, tq=tq),
        grid=(bsz, nh // 2, nq),
        in_specs=[pl.BlockSpec((1, 2, tq, LANES), lambda b, p, i: (b, p, i, 0)), kv_spec, kv_spec,
                  pl.BlockSpec((1, 2, 8, LANES), lambda b, p, i: (b, p, 0, 0))],
        out_specs=pl.BlockSpec((tq, LANES), lambda b, p, i: (b * nq + i, p)),
        out_shape=jax.ShapeDtypeStruct((bsz * seq, ATTN_WIDTH), F32),
        compiler_params=_cparams(("arbitrary", "arbitrary", "arbitrary")),
        name="fox_attn",
    )(qa, ka, va, aux)


CONF_HALO = 32
CONF_ROWS = 64


def _conf_kernel(ga_ref, gb_ref, w_ref, b_ref, lg_ref, lb_ref, y_ref, upad, ush, *, tt):
    t = pl.program_id(1)

    @pl.when(t == 0)
    def _():
        upad[0:CONF_HALO, :] = jnp.zeros((CONF_HALO, CONV_WIDTH), F32)

    upad[CONF_HALO:CONF_HALO + tt, :] = ga_ref[...] * _sigmoid(gb_ref[...])
    base = CONF_HALO - (CONV_KERNEL - 1)
    for ph in range(1, SUBLANES):
        ush[ph - 1] = upad[ph:ph + ush.shape[1], :]
    for r in range(tt // CONF_ROWS):
        r0 = r * CONF_ROWS
        acc = jnp.broadcast_to(b_ref[...], (CONF_ROWS, CONV_WIDTH))
        for k in range(CONV_KERNEL):
            ph = (base + k) % SUBLANES
            lo = r0 + base + k - ph
            src = upad[lo:lo + CONF_ROWS, :] if ph == 0 else ush[ph - 1, lo:lo + CONF_ROWS, :]
            acc = acc + w_ref[k:k + 1, :] * src
        mu = jnp.mean(acc, axis=-1, keepdims=True)
        cen = acc - mu
        var = jnp.mean(cen * cen, axis=-1, keepdims=True)
        y = cen * lax.rsqrt(var + EPS) * lg_ref[...] + lb_ref[...]
        y_ref[r0:r0 + CONF_ROWS, :] = _silu(y)
    upad[0:CONF_HALO, :] = upad[tt:tt + CONF_HALO, :]


def _conformer(proj, w, b, lg, lb, bsz, seq, tt):
    m = bsz * seq
    nt = seq // tt
    full = lambda shape: pl.BlockSpec(shape, lambda bb, t: (0,) * len(shape))
    return pl.pallas_call(
        functools.partial(_conf_kernel, tt=tt),
        grid=(bsz, nt),
        in_specs=[
            pl.BlockSpec((tt, CONV_WIDTH), lambda bb, t: (bb * nt + t, COL_GA // CONV_WIDTH)),
            pl.BlockSpec((tt, CONV_WIDTH), lambda bb, t: (bb * nt + t, COL_GB // CONV_WIDTH)),
            full((CONV_KERNEL, CONV_WIDTH)),
            full((1, CONV_WIDTH)),
            full((1, CONV_WIDTH)),
            full((1, CONV_WIDTH)),
        ],
        out_specs=pl.BlockSpec((tt, CONV_WIDTH), lambda bb, t: (bb * nt + t, 0)),
        out_shape=jax.ShapeDtypeStruct((m, CONV_WIDTH), F32),
        scratch_shapes=[pltpu.VMEM((tt + CONF_HALO, CONV_WIDTH), F32),
                        pltpu.VMEM((SUBLANES - 1, tt + CONF_HALO - SUBLANES, CONV_WIDTH), F32)],
        compiler_params=_cparams(("arbitrary", "arbitrary")),
        name="conformer_conv",
    )(proj, proj, w, b.reshape(1, -1), lg.reshape(1, -1), lb.reshape(1, -1))


ROUTE_BIG = 1e9
CHUNK = 8
XS_FEAT = D_MODEL // 2
XS_WIDTH = XS_FEAT + LANES
RT_W1, RT_W2, RT_ROW1, RT_ROW2 = 0, 1, 2, 3


def _split3(x):
    hi = x.astype(BF16)
    lo = (x - hi.astype(F32)).astype(BF16)
    return hi, lo


def _outproj_kernel(x_ref, ys_ref, oa_ref, yc_ref, wo_ref, fg_ref, gm_ref, a2_ref, s2_ref,
                    wr_ref, br_ref, xn_ref, h2_ref, rt_ref, rtt_ref, cnt_ref):
    tm = x_ref.shape[0]
    lane = _iota((tm, LANES), 1)
    att = oa_ref[...]
    ms = jnp.mean(att * att, axis=-1, keepdims=True)
    att = att * lax.rsqrt(ms + EPS) * fg_ref[...]
    y = jnp.dot(ys_ref[...].astype(BF16), wo_ref[0:SSD_WIDTH, :], preferred_element_type=F32)
    y = y + jnp.dot(att.astype(BF16), wo_ref[SSD_WIDTH:SSD_WIDTH + ATTN_WIDTH, :],
                    preferred_element_type=F32)
    y = y + jnp.dot(yc_ref[...].astype(BF16), wo_ref[SSD_WIDTH + ATTN_WIDTH:, :],
                    preferred_element_type=F32)
    xn = x_ref[...] + gm_ref[0] * y
    xn_ref[...] = xn
    ms2 = jnp.mean(xn * xn, axis=-1, keepdims=True)
    h2 = xn * lax.rsqrt(ms2 + EPS) * a2_ref[0] + s2_ref[0]
    h2_ref[...] = h2.astype(BF16)

    hh, hl = _split3(h2)
    part = jnp.dot(jnp.concatenate([hh, hl], axis=0), wr_ref[...], preferred_element_type=F32)
    logits = (part[:tm, :LANES] + part[:tm, LANES:] + part[tm:, :LANES] + part[tm:, LANES:]) + br_ref[...]
    lanef = lane.astype(F32)
    lg = jnp.where(lane < N_GROUPS, logits, -jnp.inf)
    gmax = jnp.max(lg, axis=-1, keepdims=True)
    gsum = jnp.sum(jnp.exp(lg - gmax), axis=-1, keepdims=True)
    gidx = jnp.min(jnp.where(lg == gmax, lanef, ROUTE_BIG), axis=-1, keepdims=True)
    e_lo = N_GROUPS + PER_GROUP * gidx
    le = jnp.where((lanef >= e_lo) & (lanef < e_lo + PER_GROUP), logits, -jnp.inf)
    m1 = jnp.max(le, axis=-1, keepdims=True)
    i1 = jnp.min(jnp.where(le == m1, lanef, ROUTE_BIG), axis=-1, keepdims=True)
    le2 = jnp.where(lanef == i1, -jnp.inf, le)
    m2 = jnp.max(le2, axis=-1, keepdims=True)
    i2 = jnp.min(jnp.where(le2 == m2, lanef, ROUTE_BIG), axis=-1, keepdims=True)
    esum = jnp.sum(jnp.exp(le - m1), axis=-1, keepdims=True)
    p1 = 1.0 / esum
    p2 = jnp.exp(m2 - m1) / esum
    psel = 1.0 / gsum
    w1 = p1 / (p1 + p2) * psel
    w2 = p2 / (p1 + p2) * psel

    oh1 = (lanef == (i1 - N_GROUPS)).astype(F32)
    oh2 = (lanef == (i2 - N_GROUPS)).astype(F32)
    oh = oh1 + oh2
    cnt = jnp.sum(oh, axis=0, keepdims=True)
    before = (_iota((tm, tm), 0) > _iota((tm, tm), 1)).astype(BF16)
    rank = jnp.dot(before, oh.astype(BF16), preferred_element_type=F32)
    chunks = jnp.floor((cnt + (CHUNK - 1)) * (1.0 / CHUNK))
    below = (_iota((LANES, LANES), 0) < _iota((LANES, LANES), 1)).astype(BF16)
    seg_lo = CHUNK * jnp.dot(jnp.broadcast_to(chunks, (8, LANES)).astype(BF16), below,
                             preferred_element_type=F32)[0:1, :]
    lr1 = jnp.sum(oh1 * (rank + seg_lo), axis=-1, keepdims=True)
    lr2 = jnp.sum(oh2 * (rank + seg_lo), axis=-1, keepdims=True)
    route = jnp.where(lane == RT_W1, w1, jnp.where(lane == RT_W2, w2, jnp.where(
        lane == RT_ROW1, lr1, jnp.where(lane == RT_ROW2, lr2, 0.0))))
    rt_ref[...] = route
    rtt_ref[0] = route.T[0:8, :]
    cnt_ref[0] = jnp.broadcast_to(cnt, (8, LANES))


def _outproj(x2, ys, oa, yc, wo, fg, gm, a2, s2, wr, br, seq, tm):
    m = x2.shape[0]
    per_b = seq // tm
    nt = m // tm
    row = lambda w: pl.BlockSpec((tm, w), lambda i: (i, 0))
    bvec = pl.BlockSpec((1, 1, D_MODEL), lambda i: (i // per_b, 0, 0))
    full = lambda shape: pl.BlockSpec(shape, lambda i: (0,) * len(shape))
    return pl.pallas_call(
        _outproj_kernel,
        grid=(nt,),
        in_specs=[
            row(D_MODEL), row(SSD_WIDTH), row(ATTN_WIDTH), row(CONV_WIDTH),
            full((D_MODEL, D_MODEL)), full((1, ATTN_WIDTH)), bvec, bvec, bvec,
            full((D_MODEL, 2 * LANES)), full((1, LANES)),
        ],
        out_specs=[row(D_MODEL), row(D_MODEL), row(LANES),
                   pl.BlockSpec((1, 8, tm), lambda i: (i, 0, 0)),
                   pl.BlockSpec((1, 8, LANES), lambda i: (i, 0, 0))],
        out_shape=[jax.ShapeDtypeStruct((m, D_MODEL), F32), jax.ShapeDtypeStruct((m, D_MODEL), BF16),
                   jax.ShapeDtypeStruct((m, LANES), F32), jax.ShapeDtypeStruct((nt, 8, tm), F32),
                   jax.ShapeDtypeStruct((nt, 8, LANES), F32)],
        compiler_params=_cparams(("arbitrary",)),
        name="out_proj_router",
    )(x2, ys, oa, yc, wo, fg, gm, a2, s2, wr, br)


def _local_rows(tm):
    return -(-(2 * tm + (CHUNK - 1) * N_EXPERTS) // LANES) * LANES


def _pack_halves(x):
    w = x.shape[-1] // 2
    return pltpu.bitcast(x[:, :w], jnp.uint32) | lax.shift_right_logical(
        pltpu.bitcast(x[:, w:], jnp.uint32), jnp.uint32(16))


def _unpack_halves(p):
    hi = pltpu.bitcast(p & jnp.uint32(0xFFFF0000), F32)
    lo = pltpu.bitcast(lax.shift_left(p, jnp.uint32(16)), F32)
    return jnp.concatenate([hi, lo], axis=-1).astype(BF16)


def _wait_chunks(n, n_max, copy_of_size):
    for bit in range(n_max.bit_length()):
        @pl.when((lax.shift_right_logical(n, bit) & 1) == 1)
        def _():
            copy_of_size(CHUNK << bit).wait()


def _dispatch_kernel(lo_ref, gb_ref, n8_ref, tot_ref, ts_ref, tn_ref, h_ref, rtt_ref, xs_hbm,
                     xloc, zeros, sem, zsem, *, lr):
    t = pl.program_id(0)
    nt = pl.num_programs(0)
    slot = t % 2
    tm = h_ref.shape[0]

    def seg_copy(sl, src, dst):
        return pltpu.make_async_copy(xloc.at[sl, pl.ds(src, CHUNK), :], xs_hbm.at[pl.ds(dst, CHUNK), :],
                                     sem.at[sl])

    def wait_tile(tile, sl):
        _wait_chunks(tot_ref[tile], lr // CHUNK, lambda size: pltpu.make_async_copy(
            xloc.at[sl, pl.ds(0, size), :], xs_hbm.at[pl.ds(0, size), :], sem.at[sl]))

    @pl.when(t >= 2)
    def _():
        wait_tile(t - 2, slot)

    rows = _iota((lr, tm), 0).astype(F32)
    is1 = rows == rtt_ref[0, RT_ROW1:RT_ROW1 + 1, :]
    is2 = rows == rtt_ref[0, RT_ROW2:RT_ROW2 + 1, :]
    perm = jnp.where(is1, 1.0, jnp.where(is2, 1.0, 0.0))
    xloc[slot, :, 0:XS_FEAT] = _pack_halves(
        jnp.dot(perm.astype(BF16), h_ref[...], preferred_element_type=F32))
    w_row = jnp.sum(jnp.where(is1, rtt_ref[0, RT_W1:RT_W1 + 1, :],
                              jnp.where(is2, rtt_ref[0, RT_W2:RT_W2 + 1, :], 0.0)), axis=-1, keepdims=True)
    xloc[slot, :, XS_FEAT:XS_WIDTH] = jnp.where(
        _iota((lr, LANES), 1) == 0, pltpu.bitcast(jnp.broadcast_to(w_row, (lr, LANES)), jnp.uint32),
        jnp.uint32(0))

    def per_expert(e, c):
        idx = t * N_EXPERTS + e
        src0 = lo_ref[idx] * CHUNK
        dst0 = gb_ref[idx] * CHUNK

        def per_chunk(k, c2):
            seg_copy(slot, pl.multiple_of(src0 + k * CHUNK, CHUNK),
                     pl.multiple_of(dst0 + k * CHUNK, CHUNK)).start()
            return c2
        lax.fori_loop(0, n8_ref[idx], per_chunk, 0)
        return c
    lax.fori_loop(0, N_EXPERTS, per_expert, 0)

    @pl.when(t == nt - 1)
    def _():
        zeros[...] = jnp.zeros(zeros.shape, jnp.uint32)

        def zero_copy(dst):
            return pltpu.make_async_copy(zeros, xs_hbm.at[pl.ds(dst, CHUNK), :], zsem.at[0])

        def fill(e, c):
            dst0 = ts_ref[e] * CHUNK

            def one(k, c2):
                zero_copy(pl.multiple_of(dst0 + k * CHUNK, CHUNK)).start()
                return c2
            lax.fori_loop(0, tn_ref[e], one, 0)
            return c
        lax.fori_loop(0, N_EXPERTS + 1, fill, 0)

        def drain(e, c):
            def one(k, c2):
                zero_copy(0).wait()
                return c2
            lax.fori_loop(0, tn_ref[e], one, 0)
            return c
        lax.fori_loop(0, N_EXPERTS + 1, drain, 0)

        @pl.when(t >= 1)
        def _():
            wait_tile(t - 1, 1 - slot)
        wait_tile(t, slot)


def _dispatch(plan, h2, rtt, n_rows, tm):
    m = h2.shape[0]
    lr = _local_rows(tm)
    grid_spec = pltpu.PrefetchScalarGridSpec(
        num_scalar_prefetch=6,
        grid=(m // tm,),
        in_specs=[pl.BlockSpec((tm, D_MODEL), lambda t, *_: (t, 0)),
                  pl.BlockSpec((1, 8, tm), lambda t, *_: (t, 0, 0))],
        out_specs=pl.BlockSpec(memory_space=pl.ANY),
        scratch_shapes=[pltpu.VMEM((2, lr, XS_WIDTH), jnp.uint32),
                        pltpu.VMEM((CHUNK, XS_WIDTH), jnp.uint32),
                        pltpu.SemaphoreType.DMA((2,)), pltpu.SemaphoreType.DMA((1,))],
    )
    return pl.pallas_call(
        functools.partial(_dispatch_kernel, lr=lr),
        grid_spec=grid_spec,
        out_shape=jax.ShapeDtypeStruct((n_rows, XS_WIDTH), jnp.uint32),
        compiler_params=_cparams(("arbitrary",)),
        name="moe_dispatch",
    )(plan["lo8"], plan["gb8"], plan["n8"], plan["tot8"], plan["ts8"], plan["tn8"], h2, rtt)


def _expert_kernel(be_ref, nu_ref, x_ref, wg_ref, wu_ref, wd_ref, y_ref, wgb, wub, wdb):
    i = pl.program_id(0)

    @pl.when(i < nu_ref[0])
    def _():
        prev_e = be_ref[jnp.maximum(i - 1, 0)]

        @pl.when((i == 0) | (be_ref[i] != prev_e))
        def _():
            wgb[...] = wg_ref[0, 0].astype(BF16)
            wub[...] = wu_ref[0, 0].astype(BF16)
            wdb[...] = wd_ref[0, 0].astype(BF16)

        x = _unpack_halves(x_ref[:, 0:XS_FEAT])
        w_row = pltpu.bitcast(x_ref[:, XS_FEAT:XS_FEAT + 1], F32)
        hid = _silu(jnp.dot(x, wgb[...], preferred_element_type=F32)) * jnp.dot(
            x, wub[...], preferred_element_type=F32)
        y = jnp.dot(hid.astype(BF16), wdb[...], preferred_element_type=F32)
        y_ref[...] = _pack_halves((y * w_row).astype(BF16).astype(F32))

    @pl.when(i >= nu_ref[0])
    def _():
        y_ref[...] = jnp.zeros(y_ref.shape, jnp.uint32)


def _experts(block_e, n_used, xs, wg, wu, wd, layer):
    n_rows = xs.shape[0]
    wspec = lambda shape: pl.BlockSpec((1, 1) + shape, lambda i, be, nu: (layer, be[i], 0, 0))
    grid_spec = pltpu.PrefetchScalarGridSpec(
        num_scalar_prefetch=2,
        grid=(n_rows // MOE_BLOCK,),
        in_specs=[
            pl.BlockSpec((MOE_BLOCK, XS_WIDTH), lambda i, be, nu: (i, 0)),
            wspec((D_MODEL, D_EXPERT)), wspec((D_MODEL, D_EXPERT)), wspec((D_EXPERT, D_MODEL)),
        ],
        out_specs=pl.BlockSpec((MOE_BLOCK, XS_FEAT), lambda i, be, nu: (i, 0)),
        scratch_shapes=[
            pltpu.VMEM((D_MODEL, D_EXPERT), BF16),
            pltpu.VMEM((D_MODEL, D_EXPERT), BF16),
            pltpu.VMEM((D_EXPERT, D_MODEL), BF16),
        ],
    )
    return pl.pallas_call(
        _expert_kernel,
        grid_spec=grid_spec,
        out_shape=jax.ShapeDtypeStruct((n_rows, XS_FEAT), jnp.uint32),
        compiler_params=_cparams(("arbitrary",)),
        name="moe_experts",
    )(block_e, n_used, xs, wg, wu, wd)


def _combine_kernel(lo_ref, gb_ref, n8_ref, tot_ref, x_ref, rt_ref, gf_ref, fg_ref, ys_hbm,
                    o_ref, yloc, sem, *, lr, final):
    t = pl.program_id(0)
    nt = pl.num_programs(0)
    slot = t % 2
    tm = x_ref.shape[0]

    def fetch(tile, sl):
        def per_expert(e, c):
            idx = tile * N_EXPERTS + e
            dst0 = lo_ref[idx] * CHUNK
            src0 = gb_ref[idx] * CHUNK

            def per_chunk(k, c2):
                pltpu.make_async_copy(ys_hbm.at[pl.ds(pl.multiple_of(src0 + k * CHUNK, CHUNK), CHUNK), :],
                                      yloc.at[sl, pl.ds(pl.multiple_of(dst0 + k * CHUNK, CHUNK), CHUNK), :],
                                      sem.at[sl]).start()
                return c2
            lax.fori_loop(0, n8_ref[idx], per_chunk, 0)
            return c
        lax.fori_loop(0, N_EXPERTS, per_expert, 0)

    @pl.when(t == 0)
    def _():
        yloc[...] = jnp.zeros(yloc.shape, jnp.uint32)
        fetch(0, 0)

    @pl.when(t + 1 < nt)
    def _():
        fetch(t + 1, 1 - slot)

    _wait_chunks(tot_ref[t], lr // CHUNK, lambda size: pltpu.make_async_copy(
        ys_hbm.at[pl.ds(0, size), :], yloc.at[slot, pl.ds(0, size), :], sem.at[slot]))

    ysw = _unpack_halves(yloc[slot])
    rt = rt_ref[...]
    cols = _iota((tm, lr), 1).astype(F32)
    pick = jnp.where(cols == rt[:, RT_ROW1:RT_ROW1 + 1], 1.0,
                     jnp.where(cols == rt[:, RT_ROW2:RT_ROW2 + 1], 1.0, 0.0)).astype(BF16)
    x = x_ref[...] + gf_ref[0] * jnp.dot(pick, ysw, preferred_element_type=F32)
    if final:
        ms = jnp.mean(x * x, axis=-1, keepdims=True)
        x = x * lax.rsqrt(ms + EPS) * fg_ref[...]
    o_ref[...] = x


def _combine(plan, x2, ys, rt, gf, fg, seq, tm, final):
    m = x2.shape[0]
    per_b = seq // tm
    lr = _local_rows(tm)
    grid_spec = pltpu.PrefetchScalarGridSpec(
        num_scalar_prefetch=4,
        grid=(m // tm,),
        in_specs=[
            pl.BlockSpec((tm, D_MODEL), lambda i, *_: (i, 0)),
            pl.BlockSpec((tm, LANES), lambda i, *_: (i, 0)),
            pl.BlockSpec((1, 1, D_MODEL), lambda i, *_: (i // per_b, 0, 0)),
            pl.BlockSpec((1, D_MODEL), lambda i, *_: (0, 0)),
            pl.BlockSpec(memory_space=pl.ANY),
        ],
        out_specs=pl.BlockSpec((tm, D_MODEL), lambda i, *_: (i, 0)),
        scratch_shapes=[pltpu.VMEM((2, lr, XS_FEAT), jnp.uint32), pltpu.SemaphoreType.DMA((2,))],
    )
    return pl.pallas_call(
        functools.partial(_combine_kernel, lr=lr, final=final),
        grid_spec=grid_spec,
        out_shape=jax.ShapeDtypeStruct((m, D_MODEL), F32),
        compiler_params=_cparams(("arbitrary",)),
        name="moe_combine",
    )(plan["lo8"], plan["gb8"], plan["n8"], plan["tot8"], x2, rt, gf, fg, ys)


def _moe_plan(c, n_blocks):
    i32 = jnp.int32
    blk8 = MOE_BLOCK // CHUNK
    c8 = (c + CHUNK - 1) // CHUNK
    lo8 = jnp.cumsum(c8, axis=1) - c8
    per_e = jnp.sum(c8, axis=0)
    pad8 = (per_e + blk8 - 1) // blk8 * blk8
    end8 = jnp.cumsum(pad8)
    start8 = end8 - pad8
    gb8 = start8[None, :] + jnp.cumsum(c8, axis=0) - c8
    blk_start8 = jnp.arange(n_blocks, dtype=i32) * blk8
    block_e = jnp.minimum(jnp.sum(end8[None, :] <= blk_start8[:, None], axis=1), N_EXPERTS - 1)
    return {
        "lo8": lo8.reshape(-1).astype(i32), "gb8": gb8.reshape(-1).astype(i32),
        "n8": c8.reshape(-1).astype(i32), "tot8": jnp.sum(c8, axis=1).astype(i32),
        "ts8": jnp.concatenate([start8 + per_e, end8[-1:]]).astype(i32),
        "tn8": jnp.concatenate([pad8 - per_e, n_blocks * blk8 - end8[-1:]]).astype(i32),
        "block_e": block_e.astype(i32), "n_used": (end8[-1:] // blk8).astype(i32),
    }


def _w_in_sections(w):
    d_dt = SSD_WIDTH + XBC_WIDTH
    d_q = d_dt + SSD_HEADS
    d_f = d_q + 3 * ATTN_WIDTH
    d_ga = d_f + ATTN_HEADS
    small = jnp.concatenate([w[..., d_dt:d_q], w[..., d_f:d_ga],
                             jnp.zeros(w.shape[:-1] + (LANES - SSD_HEADS - ATTN_HEADS,), w.dtype)], axis=-1)
    sections = [w[..., SSD_WIDTH:d_dt], w[..., :SSD_WIDTH], w[..., d_q:d_f], w[..., d_ga:], small]
    return [sec.astype(BF16) for sec in sections]


def kernel(x, c, ada_w, ada_b, norm_mix_g, w_in, ssd_conv_w, ssd_conv_b, ssd_dt_bias, ssd_a_log,
           ssd_d, ssd_norm_g, fox_f_bias, fox_norm_g, cm_conv_w, cm_conv_b, cm_ln_g, cm_ln_b, w_out,
           norm_ffn_g, w_router_group, b_router_group, w_router_expert, b_router_expert, w_gate,
           w_up, w_down, final_norm_g):
    bsz, seq, d = x.shape
    m = bsz * seq
    tm = min(512, seq)
    tt = min(256, seq)
    n_blocks = -(-(2 * m + (CHUNK - 1) * (m // tm) * N_EXPERTS) // MOE_BLOCK) + N_EXPERTS

    mod = _modulation(c, ada_w, ada_b)
    x2 = x.reshape(m, d)
    w_in_sections = _w_in_sections(w_in)
    for l in range(DEPTH):
        sh_m, sc_m, g_m, sh_f, sc_f, g_f = [v.reshape(bsz, 1, d) for v in jnp.split(mod[l], 6, axis=-1)]
        a_m = norm_mix_g[l][None, None, :] * (1.0 + sc_m)
        proj = _inproj(x2, a_m, sh_m, w_in_sections, l, seq, tm)
        y_ssd = _ssd(proj, ssd_conv_w[l], ssd_conv_b[l], ssd_dt_bias[l], ssd_a_log[l],
                     ssd_d[l], ssd_norm_g[l], bsz, seq, tt)
        qa, ka, va, aux = _foxprep(proj, fox_f_bias[l], bsz, seq)
        o_att = _fox(qa, ka, va, aux)
        y_cnv = _conformer(proj, cm_conv_w[l], cm_conv_b[l], cm_ln_g[l], cm_ln_b[l], bsz, seq, tt)

        w_r = jnp.concatenate([w_router_group[l], w_router_expert[l],
                               jnp.zeros((d, LANES - N_GROUPS - N_EXPERTS), F32)], axis=1)
        w_rh = w_r.astype(BF16)
        w_r2 = jnp.concatenate([w_rh, (w_r - w_rh.astype(F32)).astype(BF16)], axis=1)
        b_r = jnp.concatenate([b_router_group[l], b_router_expert[l],
                               jnp.zeros((LANES - N_GROUPS - N_EXPERTS,), F32)]).reshape(1, LANES)
        a_f = norm_ffn_g[l][None, None, :] * (1.0 + sc_f)
        x2, h2, rt, rtt, cnt = _outproj(x2, y_ssd, o_att, y_cnv, w_out[l].astype(BF16),
                                        fox_norm_g[l].reshape(1, -1), g_m, a_f, sh_f, w_r2, b_r,
                                        seq, tm)
        plan = _moe_plan(cnt[:, 0, :N_EXPERTS].astype(jnp.int32), n_blocks)
        xs = _dispatch(plan, h2, rtt, n_blocks * MOE_BLOCK, tm)
        ys = _experts(plan["block_e"], plan["n_used"], xs, w_gate, w_up, w_down, l)
        x2 = _combine(plan, x2, ys, rt, g_f, final_norm_g.reshape(1, d), seq, tm,
                      final=(l == DEPTH - 1))
    return x2.reshape(bsz, seq, d)
```

```python
import functools

import jax
import jax.numpy as jnp
import numpy as np
from jax import lax
from jax.experimental import pallas as pl
from jax.experimental.pallas import tpu as pltpu

F32 = jnp.float32
BF16 = jnp.bfloat16
HIGHEST = lax.Precision.HIGHEST

D_MODEL = 1024
DEPTH = 4
SSD_WIDTH = 512
SSD_HEADS = 8
SSD_HEAD_DIM = 64
SSD_STATE = 128
SSD_CONV = 4
SSD_CHUNK = 128
XBC_WIDTH = 1024
ATTN_WIDTH = 256
ATTN_HEADS = 4
ATTN_HEAD_DIM = 64
CONV_WIDTH = 256
CONV_KERNEL = 31
N_GROUPS = 4
PER_GROUP = 8
N_EXPERTS = 32
D_EXPERT = 512
MOE_BLOCK = 512
EPS = 1e-6

LANES = 128
SUBLANES = 8
COL_XBC = 0
COL_Z = 1024
COL_QKV = 1536
COL_GA = 2304
COL_GB = 2560
COL_SM = 2816
NP = 2944
SM_DT = 0
SM_F = 8

VMEM_LIMIT = 56 * 1024 * 1024


def _cparams(sem):
    return pltpu.CompilerParams(dimension_semantics=sem, vmem_limit_bytes=VMEM_LIMIT)


def _sigmoid(x):
    return 1.0 / (1.0 + jnp.exp(-x))


def _silu(x):
    return x * _sigmoid(x)


def _softplus(x):
    return jnp.maximum(x, 0.0) + jnp.log1p(jnp.exp(-jnp.abs(x)))


def _iota(shape, dim):
    return lax.broadcasted_iota(jnp.int32, shape, dim)


def _split_bf16x3(x):
    hi = x.astype(BF16)
    r1 = x - hi.astype(F32)
    mid = r1.astype(BF16)
    lo = (r1 - mid.astype(F32)).astype(BF16)
    return hi, mid, lo


def _dot_onehot_rhs(x, sel):
    return sum(jnp.dot(part, sel, preferred_element_type=F32) for part in _split_bf16x3(x))


def _dot_onehot_lhs(sel, x):
    return sum(jnp.dot(sel, part, preferred_element_type=F32) for part in _split_bf16x3(x))


def _mod_kernel(c_ref, w_ref, b_ref, o_ref):
    cond = _silu(c_ref[...])
    o_ref[0] = jnp.dot(cond, w_ref[0], precision=HIGHEST, preferred_element_type=F32) + b_ref[0]


def _modulation(c, ada_w, ada_b):
    bsz = c.shape[0]
    rows = 8
    cpad = jnp.zeros((rows, D_MODEL), F32).at[:bsz].set(c)
    tn = 1536
    n6 = 6 * D_MODEL
    out = pl.pallas_call(
        _mod_kernel,
        grid=(DEPTH, n6 // tn),
        in_specs=[
            pl.BlockSpec((rows, D_MODEL), lambda l, j: (0, 0)),
            pl.BlockSpec((1, D_MODEL, tn), lambda l, j: (l, 0, j)),
            pl.BlockSpec((1, 1, tn), lambda l, j: (l, 0, j)),
        ],
        out_specs=pl.BlockSpec((1, rows, tn), lambda l, j: (l, 0, j)),
        out_shape=jax.ShapeDtypeStruct((DEPTH, rows, n6), F32),
        compiler_params=_cparams(("arbitrary", "arbitrary")),
        name="adaln_mod",
    )(cpad, ada_w, ada_b.reshape(DEPTH, 1, n6))
    return out[:, :bsz]


def _inproj_kernel(x_ref, a_ref, s_ref, *refs):
    w_refs, o_ref = refs[:-1], refs[-1]
    x = x_ref[...]
    ms = jnp.mean(x * x, axis=-1, keepdims=True)
    h = (x * lax.rsqrt(ms + EPS) * a_ref[0] + s_ref[0]).astype(BF16)
    col = 0
    for w_ref in w_refs:
        width = w_ref.shape[-1]
        o_ref[:, col:col + width] = jnp.dot(h, w_ref[0], preferred_element_type=F32)
        col += width


def _inproj(x2, a, s, w_sections, layer, seq, tm):
    m = x2.shape[0]
    per_b = seq // tm
    assert sum(w.shape[-1] for w in w_sections) == NP
    return pl.pallas_call(
        _inproj_kernel,
        grid=(m // tm,),
        in_specs=[
            pl.BlockSpec((tm, D_MODEL), lambda i: (i, 0)),
            pl.BlockSpec((1, 1, D_MODEL), lambda i: (i // per_b, 0, 0)),
            pl.BlockSpec((1, 1, D_MODEL), lambda i: (i // per_b, 0, 0)),
        ] + [pl.BlockSpec((1, D_MODEL, w.shape[-1]), lambda i: (layer, 0, 0)) for w in w_sections],
        out_specs=pl.BlockSpec((tm, NP), lambda i: (i, 0)),
        out_shape=jax.ShapeDtypeStruct((m, NP), F32),
        compiler_params=_cparams(("arbitrary",)),
        name="in_proj",
    )(x2, a, s, *w_sections)


def _ssd_kernel(z_ref, xbc_ref, sm_ref, cw_ref, cb_ref, dtb_ref, alog_ref, dtbt_ref,
                alogt_ref, e_ref, dx_ref, ng_ref, y_ref, xpad, xc, prev, *, tt):
    t = pl.program_id(1)

    @pl.when(t == 0)
    def _():
        xpad[0:8, :] = jnp.zeros((8, XBC_WIDTH), F32)
        prev[...] = jnp.zeros(prev.shape, F32)

    xpad[8:8 + tt, :] = xbc_ref[...]
    acc = jnp.broadcast_to(cb_ref[...], (tt, XBC_WIDTH))
    for k in range(SSD_CONV):
        off = 8 - (SSD_CONV - 1) + k
        acc = acc + cw_ref[k:k + 1, :] * xpad[off:off + tt, :]
    xc[...] = _silu(acc)
    xpad[0:8, :] = xpad[tt:tt + 8, :]

    cl = SSD_CHUNK
    row = _iota((cl, cl), 0)
    col = _iota((cl, cl), 1)
    causal = row >= col
    tril = jnp.where(causal, 1.0, 0.0).astype(BF16)
    triu = jnp.where(row <= col, 1.0, 0.0).astype(BF16)
    lo = col < SSD_HEAD_DIM
    lane1 = _iota((1, LANES), 1)
    a_row = jnp.where(lane1 < SSD_HEADS, -jnp.exp(alog_ref[...]), 0.0)
    a_col = -jnp.exp(alogt_ref[...])
    expand = e_ref[...]

    def chunk(c, carry):
        r0 = c * cl
        xs = xc[pl.ds(r0, cl), 0:SSD_WIDTH]
        bmat = xc[pl.ds(r0, cl), SSD_WIDTH:SSD_WIDTH + 2 * SSD_STATE]
        cmat = xc[pl.ds(r0, cl), SSD_WIDTH + 2 * SSD_STATE:XBC_WIDTH]
        sm = sm_ref[pl.ds(r0, cl), :]
        dt = _softplus(sm + dtb_ref[...])
        da = dt * a_row
        acs = _dot_onehot_lhs(tril, da)
        dtt = _softplus(sm.T[0:SSD_HEADS, :] + dtbt_ref[...])
        acst = _dot_onehot_rhs(dtt * a_col, triu)
        dt_x = _dot_onehot_rhs(dt, expand)
        acs_x = _dot_onehot_rhs(acs, expand)
        last = acs_x[cl - 1:cl, :]
        eacs_x = jnp.exp(acs_x)
        dte_x = jnp.exp(last - acs_x)
        cd_x = jnp.exp(last)
        xdt = xs * dt_x
        xdte = (xdt * dte_x).astype(BF16)
        zc = z_ref[pl.ds(r0, cl), :]
        for g in range(2):
            bg = bmat[:, g * SSD_STATE:(g + 1) * SSD_STATE]
            cg = cmat[:, g * SSD_STATE:(g + 1) * SSD_STATE].astype(BF16)
            bgt = bg.T.astype(BF16)
            cbm = jnp.dot(cg, bgt, preferred_element_type=F32)
            pair_out = []
            for j in range(2):
                p = 2 * g + j
                sl = slice(p * LANES, (p + 1) * LANES)
                xp = xdt[:, sl]
                yd = jnp.zeros((cl, LANES), F32)
                for half in range(2):
                    h = 2 * p + half
                    seg = acs[:, h:h + 1] - acst[h:h + 1, :]
                    dec = jnp.exp(jnp.where(causal, seg, -jnp.inf))
                    gm = (cbm * dec).astype(BF16)
                    own = lo if half == 0 else jnp.logical_not(lo)
                    xm = jnp.where(own, xp, 0.0).astype(BF16)
                    yd = yd + jnp.dot(gm, xm, preferred_element_type=F32)
                prev_p = prev[:, sl]
                yo = jnp.dot(cg, prev_p.astype(BF16), preferred_element_type=F32) * eacs_x[:, sl]
                st = jnp.dot(bgt, xdte[:, sl], preferred_element_type=F32)
                prev[:, sl] = prev_p * cd_x[:, sl] + st
                pair_out.append(yd + yo + xs[:, sl] * dx_ref[:, sl])
            gs = slice(g * 2 * LANES, (g + 1) * 2 * LANES)
            yg = jnp.concatenate(pair_out, axis=-1) * _silu(zc[:, gs])
            ms = jnp.mean(yg * yg, axis=-1, keepdims=True)
            y_ref[pl.ds(r0, cl), gs] = yg * lax.rsqrt(ms + EPS) * ng_ref[:, gs]
        return carry

    for c in range(tt // cl):
        chunk(c, 0)


def _ssd(proj, cw, cb, dtb, alog, dx, ng, bsz, seq, tt):
    m = bsz * seq
    nt = seq // tt
    pad = LANES - SSD_HEADS
    dtb_row = jnp.pad(dtb, (0, pad)).reshape(1, LANES)
    alog_row = jnp.pad(alog, (0, pad)).reshape(1, LANES)
    dtb_col = jnp.broadcast_to(dtb[:, None], (SSD_HEADS, SSD_CHUNK))
    alog_col = jnp.broadcast_to(alog[:, None], (SSD_HEADS, SSD_CHUNK))
    expand = np.zeros((LANES, SSD_WIDTH), np.float32)
    for h in range(SSD_HEADS):
        expand[h, h * SSD_HEAD_DIM:(h + 1) * SSD_HEAD_DIM] = 1.0
    dx_row = jnp.repeat(dx, SSD_HEAD_DIM).reshape(1, SSD_WIDTH)
    full = lambda shape: pl.BlockSpec(shape, lambda b, t: (0,) * len(shape))
    return pl.pallas_call(
        functools.partial(_ssd_kernel, tt=tt),
        grid=(bsz, nt),
        in_specs=[
            pl.BlockSpec((tt, SSD_WIDTH), lambda b, t: (b * nt + t, COL_Z // SSD_WIDTH)),
            pl.BlockSpec((tt, XBC_WIDTH), lambda b, t: (b * nt + t, COL_XBC // XBC_WIDTH + 0)),
            pl.BlockSpec((tt, LANES), lambda b, t: (b * nt + t, COL_SM // LANES)),
            full((SSD_CONV, XBC_WIDTH)),
            full((1, XBC_WIDTH)),
            full((1, LANES)),
            full((1, LANES)),
            full((SSD_HEADS, SSD_CHUNK)),
            full((SSD_HEADS, SSD_CHUNK)),
            full((LANES, SSD_WIDTH)),
            full((1, SSD_WIDTH)),
            full((1, SSD_WIDTH)),
        ],
        out_specs=pl.BlockSpec((tt, SSD_WIDTH), lambda b, t: (b * nt + t, 0)),
        out_shape=jax.ShapeDtypeStruct((m, SSD_WIDTH), F32),
        scratch_shapes=[
            pltpu.VMEM((tt + 8, XBC_WIDTH), F32),
            pltpu.VMEM((tt, XBC_WIDTH), F32),
            pltpu.VMEM((SSD_STATE, SSD_WIDTH), F32),
        ],
        compiler_params=_cparams(("arbitrary", "arbitrary")),
        name="ssd_scan",
    )(proj, proj, proj, cw, cb.reshape(1, XBC_WIDTH), dtb_row, alog_row, dtb_col, alog_col,
      jnp.asarray(expand, dtype=BF16), dx_row, ng.reshape(1, SSD_WIDTH))


LOG2E = 1.4426950408889634
ATT_BLOCK = 256
ATT_TQ = 512
PRUNE_LOG2 = 140.0
FAST_GAP_LOG2 = 60.0
SHIFT_MARGIN_LOG2 = 1.0
AUX_CUMEND = 0
AUX_KMAX = 1


def _foxprep_kernel(qkv_ref, sm_ref, fb_ref, q_out, k_out, v_out, aux_out, carry, *, tt):
    t = pl.program_id(1)

    @pl.when(t == 0)
    def _():
        carry[...] = jnp.zeros(carry.shape, F32)
        aux_out[...] = jnp.zeros(aux_out.shape, F32)

    logit = sm_ref[...] + fb_ref[...]
    logf = -_softplus(-logit)
    row = _iota((tt, tt), 0)
    col = _iota((tt, tt), 1)
    tril = jnp.where(row >= col, 1.0, 0.0).astype(BF16)
    cum = _dot_onehot_lhs(tril, logf) + carry[...]
    carry[...] = cum[tt - 1:tt, :]

    lane = _iota((tt, LANES), 1)
    lane1 = _iota((1, LANES), 1)
    scale = ATTN_HEAD_DIM ** -0.5 * LOG2E
    for h in range(ATTN_HEADS):
        pair, half = h // 2, h % 2
        own = (lane < ATTN_HEAD_DIM) if half == 0 else (lane >= ATTN_HEAD_DIM)
        a0 = ATTN_HEAD_DIM * (1 - half)
        cs = jnp.broadcast_to(cum[:, SM_F + h:SM_F + h + 1], (tt, LANES)) * LOG2E
        hi = cs.astype(BF16).astype(F32)
        r1 = cs - hi
        mid = r1.astype(BF16).astype(F32)
        low = r1 - mid
        qp = qkv_ref[:, pair * LANES:(pair + 1) * LANES]
        kp = qkv_ref[:, ATTN_WIDTH + pair * LANES:ATTN_WIDTH + (pair + 1) * LANES]
        vp = qkv_ref[:, 2 * ATTN_WIDTH + pair * LANES:2 * ATTN_WIDTH + (pair + 1) * LANES]
        qa = jnp.where(lane == a0, hi, jnp.where(lane == a0 + 1, mid, jnp.where(
            lane == a0 + 2, low, jnp.where((lane >= a0 + 3) & (lane < a0 + 6), 1.0, 0.0))))
        ka = jnp.where(lane == a0 + 3, -hi, jnp.where(lane == a0 + 4, -mid, jnp.where(
            lane == a0 + 5, -low, jnp.where((lane >= a0) & (lane < a0 + 9), 1.0, 0.0))))
        q_out[0, h] = jnp.where(own, qp * scale, qa).astype(BF16)
        kb = jnp.where(own, kp, ka).astype(BF16)
        k_out[0, h] = kb
        v_out[0, h] = jnp.where(own, vp, jnp.where(lane == a0, 1.0, 0.0)).astype(BF16)
        kf = jnp.where(own, kb.astype(F32), 0.0)
        kn2 = jnp.max(jnp.sum(kf * kf, axis=-1, keepdims=True), axis=0, keepdims=True)
        here = lane1 == t
        aux_out[0, h, AUX_CUMEND:AUX_CUMEND + 1, :] = jnp.where(
            here, cs[tt - 1:tt, :], aux_out[0, h, AUX_CUMEND:AUX_CUMEND + 1, :])
        aux_out[0, h, AUX_KMAX:AUX_KMAX + 1, :] = jnp.where(
            here, jnp.sqrt(kn2), aux_out[0, h, AUX_KMAX:AUX_KMAX + 1, :])


def _foxprep(proj, fb, bsz, seq):
    tt = ATT_BLOCK
    nt = seq // tt
    assert nt <= LANES
    fb_row = jnp.zeros((1, LANES), F32).at[0, SM_F:SM_F + ATTN_HEADS].set(fb)
    shp = jax.ShapeDtypeStruct((bsz, ATTN_HEADS, seq, LANES), BF16)
    ospec = pl.BlockSpec((1, ATTN_HEADS, tt, LANES), lambda b, t: (b, 0, t, 0))
    return pl.pallas_call(
        functools.partial(_foxprep_kernel, tt=tt),
        grid=(bsz, nt),
        in_specs=[
            pl.BlockSpec((tt, 3 * ATTN_WIDTH), lambda b, t: (b * nt + t, COL_QKV // (3 * ATTN_WIDTH))),
            pl.BlockSpec((tt, LANES), lambda b, t: (b * nt + t, COL_SM // LANES)),
            pl.BlockSpec((1, LANES), lambda b, t: (0, 0)),
        ],
        out_specs=[ospec, ospec, ospec,
                   pl.BlockSpec((1, ATTN_HEADS, 8, LANES), lambda b, t: (b, 0, 0, 0))],
        out_shape=[shp, shp, shp, jax.ShapeDtypeStruct((bsz, ATTN_HEADS, 8, LANES), F32)],
        scratch_shapes=[pltpu.VMEM((1, LANES), F32)],
        compiler_params=_cparams(("arbitrary", "arbitrary")),
        name="fox_prep",
    )(proj, proj, fb_row)


def _fox_kernel(q_ref, k_ref, v_ref, aux_ref, o_ref, *, tq, kb):
    i = pl.program_id(2)
    ib = i * (tq // kb)
    nt = (((1,), (1,)), ((), ()))
    lane = _iota((tq, LANES), 1)
    lane1 = _iota((1, LANES), 1)
    qs = (q_ref[0, 0], q_ref[0, 1])

    def step(hh, r0, width, m, acc, mask):
        kblk = k_ref[0, hh, pl.ds(r0, width), :]
        vblk = v_ref[0, hh, pl.ds(r0, width), :]
        s = lax.dot_general(qs[hh], kblk, nt, preferred_element_type=F32)
        if mask is not None:
            s = jnp.where(mask, s, -jnp.inf)
        m_new = jnp.maximum(m, jnp.max(s, axis=-1, keepdims=True))
        p = jnp.exp2(s - m_new)
        alpha = jnp.exp2(m - m_new)
        acc = acc * alpha + jnp.dot(p.astype(BF16), vblk, preferred_element_type=F32)
        return m_new, acc

    d0 = pl.multiple_of(i * tq, tq)
    diag = _iota((tq, tq), 0) >= _iota((tq, tq), 1)
    ib_f = ib.astype(F32)
    first = []
    gaps = []
    q_shift = []
    for hh in range(2):
        own = (lane < ATTN_HEAD_DIM) if hh == 0 else (lane >= ATTN_HEAD_DIM)
        a0 = ATTN_HEAD_DIM * (1 - hh)
        qf = qs[hh].astype(F32)
        qn = jnp.sqrt(jnp.sum(jnp.where(own, qf * qf, 0.0), axis=-1, keepdims=True))
        cum_t = jnp.sum(jnp.where((lane >= a0) & (lane < a0 + 3), qf, 0.0), axis=-1, keepdims=True)
        s_self = jnp.sum(qf * k_ref[0, hh, pl.ds(d0, tq), :].astype(F32), axis=-1, keepdims=True)
        aux = aux_ref[0, hh]
        cend = aux[AUX_CUMEND:AUX_CUMEND + 1, :]
        kmax = jnp.max(jnp.where(lane1 < ib + tq // kb, aux[AUX_KMAX:AUX_KMAX + 1, :], 0.0), axis=-1,
                       keepdims=True)
        bound = qn * kmax
        slack = jnp.max(bound + cum_t - s_self, axis=0, keepdims=True)
        live = (lane1 < ib) & (slack - cend > -PRUNE_LOG2)
        first.append(jnp.min(jnp.where(live, lane1.astype(F32), ib_f)))
        s_hi, s_mid, s_lo = [part.astype(F32) for part in _split_bf16x3(bound + SHIFT_MARGIN_LOG2)]
        gaps.append(jnp.max(s_hi + s_mid + s_lo - s_self))
        q_shift.append(jnp.where(lane == a0 + 6, -s_hi, jnp.where(lane == a0 + 7, -s_mid, jnp.where(
            lane == a0 + 8, -s_lo, qf))).astype(BF16))
    j_first = jnp.minimum(first[0], first[1]).astype(jnp.int32)

    def fast(_):
        def blocks(c, r0, width, mask=None):
            out = []
            for hh in range(2):
                s = lax.dot_general(q_shift[hh], k_ref[0, hh, pl.ds(r0, width), :], nt,
                                    preferred_element_type=F32)
                if mask is not None:
                    s = jnp.where(mask, s, -jnp.inf)
                out.append(c[hh] + jnp.dot(jnp.exp2(s).astype(BF16), v_ref[0, hh, pl.ds(r0, width), :],
                                           preferred_element_type=F32))
            return tuple(out)

        zero = jnp.zeros((tq, LANES), F32)
        c = blocks((zero, zero), d0, tq, diag)
        n = ib - j_first
        c = lax.cond((n & 1) == 1, lambda c: blocks(c, pl.multiple_of(j_first * kb, kb), kb),
                     lambda c: c, c)
        c = lax.cond((n & 2) == 2, lambda c: blocks(c, pl.multiple_of((j_first + (n & 1)) * kb, kb), 2 * kb),
                     lambda c: c, c)
        j0 = j_first + (n & 3)

        def body(jj, c):
            return blocks(c, pl.multiple_of((j0 + 4 * jj) * kb, kb), 4 * kb)
        return lax.fori_loop(0, lax.shift_right_logical(n, 2), body, c)

    def online(_):
        m_init = jnp.full((tq, 1), -jnp.inf, F32)
        acc_init = jnp.zeros((tq, LANES), F32)
        carry = step(0, d0, tq, m_init, acc_init, diag) + step(1, d0, tq, m_init, acc_init, diag)

        def body(j, c):
            r0 = pl.multiple_of(j * kb, kb)
            m0, a0_, m1, a1_ = c
            m0, a0_ = step(0, r0, kb, m0, a0_, None)
            m1, a1_ = step(1, r0, kb, m1, a1_, None)
            return m0, a0_, m1, a1_
        _, a0_, _, a1_ = lax.fori_loop(j_first, ib, body, carry)
        return a0_, a1_

    acc0, acc1 = lax.cond(jnp.maximum(gaps[0], gaps[1]) <= FAST_GAP_LOG2, fast, online, None)
    den0 = jnp.sum(jnp.where(lane == ATTN_HEAD_DIM, acc0, 0.0), axis=-1, keepdims=True)
    den1 = jnp.sum(jnp.where(lane == 0, acc1, 0.0), axis=-1, keepdims=True)
    o_ref[...] = jnp.where(lane < ATTN_HEAD_DIM, acc0 / den0, acc1 / den1)


def _fox(qa, ka, va, aux):
    bsz, nh, seq, _ = qa.shape
    tq = min(ATT_TQ, seq)
    nq = seq // tq
    kv_spec = pl.BlockSpec((1, 2, seq, LANES), lambda b, p, i: (b, p, 0, 0))
    return pl.pallas_call(
        functools.partial(_fox_kernel, tq=tq, kb=ATT_BLOCK),
        grid=(bsz, nh // 2, nq),
        in_specs=[pl.BlockSpec((1, 2, tq, LANES), lambda b, p, i: (b, p, i, 0)), kv_spec, kv_spec,
                  pl.BlockSpec((1, 2, 8, LANES), lambda b, p, i: (b, p, 0, 0))],
        out_specs=pl.BlockSpec((tq, LANES), lambda b, p, i: (b * nq + i, p)),
        out_shape=jax.ShapeDtypeStruct((bsz * seq, ATTN_WIDTH), F32),
        compiler_params=_cparams(("arbitrary", "arbitrary", "arbitrary")),
        name="fox_attn",
    )(qa, ka, va, aux)


CONF_HALO = 32
CONF_ROWS = 64


def _conf_kernel(ga_ref, gb_ref, w_ref, b_ref, lg_ref, lb_ref, y_ref, upad, ush, *, tt):
    t = pl.program_id(1)

    @pl.when(t == 0)
    def _():
        upad[0:CONF_HALO, :] = jnp.zeros((CONF_HALO, CONV_WIDTH), F32)

    upad[CONF_HALO:CONF_HALO + tt, :] = ga_ref[...] * _sigmoid(gb_ref[...])
    base = CONF_HALO - (CONV_KERNEL - 1)
    for ph in range(1, SUBLANES):
        ush[ph - 1] = upad[ph:ph + ush.shape[1], :]
    for r in range(tt // CONF_ROWS):
        r0 = r * CONF_ROWS
        acc = jnp.broadcast_to(b_ref[...], (CONF_ROWS, CONV_WIDTH))
        for k in range(CONV_KERNEL):
            ph = (base + k) % SUBLANES
            lo = r0 + base + k - ph
            src = upad[lo:lo + CONF_ROWS, :] if ph == 0 else ush[ph - 1, lo:lo + CONF_ROWS, :]
            acc = acc + w_ref[k:k + 1, :] * src
        mu = jnp.mean(acc, axis=-1, keepdims=True)
        cen = acc - mu
        var = jnp.mean(cen * cen, axis=-1, keepdims=True)
        y = cen * lax.rsqrt(var + EPS) * lg_ref[...] + lb_ref[...]
        y_ref[r0:r0 + CONF_ROWS, :] = _silu(y)
    upad[0:CONF_HALO, :] = upad[tt:tt + CONF_HALO, :]


def _conformer(proj, w, b, lg, lb, bsz, seq, tt):
    m = bsz * seq
    nt = seq // tt
    full = lambda shape: pl.BlockSpec(shape, lambda bb, t: (0,) * len(shape))
    return pl.pallas_call(
        functools.partial(_conf_kernel, tt=tt),
        grid=(bsz, nt),
        in_specs=[
            pl.BlockSpec((tt, CONV_WIDTH), lambda bb, t: (bb * nt + t, COL_GA // CONV_WIDTH)),
            pl.BlockSpec((tt, CONV_WIDTH), lambda bb, t: (bb * nt + t, COL_GB // CONV_WIDTH)),
            full((CONV_KERNEL, CONV_WIDTH)),
            full((1, CONV_WIDTH)),
            full((1, CONV_WIDTH)),
            full((1, CONV_WIDTH)),
        ],
        out_specs=pl.BlockSpec((tt, CONV_WIDTH), lambda bb, t: (bb * nt + t, 0)),
        out_shape=jax.ShapeDtypeStruct((m, CONV_WIDTH), F32),
        scratch_shapes=[pltpu.VMEM((tt + CONF_HALO, CONV_WIDTH), F32),
                        pltpu.VMEM((SUBLANES - 1, tt + CONF_HALO - SUBLANES, CONV_WIDTH), F32)],
        compiler_params=_cparams(("arbitrary", "arbitrary")),
        name="conformer_conv",
    )(proj, proj, w, b.reshape(1, -1), lg.reshape(1, -1), lb.reshape(1, -1))


ROUTE_BIG = 1e9
CHUNK = 8
XS_FEAT = D_MODEL // 2
XS_WIDTH = XS_FEAT + LANES
RT_W1, RT_W2, RT_ROW1, RT_ROW2 = 0, 1, 2, 3


def _split3(x):
    hi = x.astype(BF16)
    lo = (x - hi.astype(F32)).astype(BF16)
    return hi, lo


def _outproj_kernel(x_ref, ys_ref, oa_ref, yc_ref, wo_ref, fg_ref, gm_ref, a2_ref, s2_ref,
                    wr_ref, br_ref, xn_ref, h2_ref, rt_ref, rtt_ref, cnt_ref):
    tm = x_ref.shape[0]
    lane = _iota((tm, LANES), 1)
    att = oa_ref[...]
    ms = jnp.mean(att * att, axis=-1, keepdims=True)
    att = att * lax.rsqrt(ms + EPS) * fg_ref[...]
    y = jnp.dot(ys_ref[...].astype(BF16), wo_ref[0:SSD_WIDTH, :], preferred_element_type=F32)
    y = y + jnp.dot(att.astype(BF16), wo_ref[SSD_WIDTH:SSD_WIDTH + ATTN_WIDTH, :],
                    preferred_element_type=F32)
    y = y + jnp.dot(yc_ref[...].astype(BF16), wo_ref[SSD_WIDTH + ATTN_WIDTH:, :],
                    preferred_element_type=F32)
    xn = x_ref[...] + gm_ref[0] * y
    xn_ref[...] = xn
    ms2 = jnp.mean(xn * xn, axis=-1, keepdims=True)
    h2 = xn * lax.rsqrt(ms2 + EPS) * a2_ref[0] + s2_ref[0]
    h2_ref[...] = h2.astype(BF16)

    hh, hl = _split3(h2)
    part = jnp.dot(jnp.concatenate([hh, hl], axis=0), wr_ref[...], preferred_element_type=F32)
    logits = (part[:tm, :LANES] + part[:tm, LANES:] + part[tm:, :LANES] + part[tm:, LANES:]) + br_ref[...]
    lanef = lane.astype(F32)
    lg = jnp.where(lane < N_GROUPS, logits, -jnp.inf)
    gmax = jnp.max(lg, axis=-1, keepdims=True)
    gsum = jnp.sum(jnp.exp(lg - gmax), axis=-1, keepdims=True)
    gidx = jnp.min(jnp.where(lg == gmax, lanef, ROUTE_BIG), axis=-1, keepdims=True)
    e_lo = N_GROUPS + PER_GROUP * gidx
    le = jnp.where((lanef >= e_lo) & (lanef < e_lo + PER_GROUP), logits, -jnp.inf)
    m1 = jnp.max(le, axis=-1, keepdims=True)
    i1 = jnp.min(jnp.where(le == m1, lanef, ROUTE_BIG), axis=-1, keepdims=True)
    le2 = jnp.where(lanef == i1, -jnp.inf, le)
    m2 = jnp.max(le2, axis=-1, keepdims=True)
    i2 = jnp.min(jnp.where(le2 == m2, lanef, ROUTE_BIG), axis=-1, keepdims=True)
    esum = jnp.sum(jnp.exp(le - m1), axis=-1, keepdims=True)
    p1 = 1.0 / esum
    p2 = jnp.exp(m2 - m1) / esum
    psel = 1.0 / gsum
    w1 = p1 / (p1 + p2) * psel
    w2 = p2 / (p1 + p2) * psel

    oh1 = (lanef == (i1 - N_GROUPS)).astype(F32)
    oh2 = (lanef == (i2 - N_GROUPS)).astype(F32)
    oh = oh1 + oh2
    cnt = jnp.sum(oh, axis=0, keepdims=True)
    before = (_iota((tm, tm), 0) > _iota((tm, tm), 1)).astype(BF16)
    rank = jnp.dot(before, oh.astype(BF16), preferred_element_type=F32)
    chunks = jnp.floor((cnt + (CHUNK - 1)) * (1.0 / CHUNK))
    below = (_iota((LANES, LANES), 0) < _iota((LANES, LANES), 1)).astype(BF16)
    seg_lo = CHUNK * jnp.dot(jnp.broadcast_to(chunks, (8, LANES)).astype(BF16), below,
                             preferred_element_type=F32)[0:1, :]
    lr1 = jnp.sum(oh1 * (rank + seg_lo), axis=-1, keepdims=True)
    lr2 = jnp.sum(oh2 * (rank + seg_lo), axis=-1, keepdims=True)
    route = jnp.where(lane == RT_W1, w1, jnp.where(lane == RT_W2, w2, jnp.where(
        lane == RT_ROW1, lr1, jnp.where(lane == RT_ROW2, lr2, 0.0))))
    rt_ref[...] = route
    rtt_ref[0] = route.T[0:8, :]
    cnt_ref[0] = jnp.broadcast_to(cnt, (8, LANES))


def _outproj(x2, ys, oa, yc, wo, fg, gm, a2, s2, wr, br, seq, tm):
    m = x2.shape[0]
    per_b = seq // tm
    nt = m // tm
    row = lambda w: pl.BlockSpec((tm, w), lambda i: (i, 0))
    bvec = pl.BlockSpec((1, 1, D_MODEL), lambda i: (i // per_b, 0, 0))
    full = lambda shape: pl.BlockSpec(shape, lambda i: (0,) * len(shape))
    return pl.pallas_call(
        _outproj_kernel,
        grid=(nt,),
        in_specs=[
            row(D_MODEL), row(SSD_WIDTH), row(ATTN_WIDTH), row(CONV_WIDTH),
            full((D_MODEL, D_MODEL)), full((1, ATTN_WIDTH)), bvec, bvec, bvec,
            full((D_MODEL, 2 * LANES)), full((1, LANES)),
        ],
        out_specs=[row(D_MODEL), row(D_MODEL), row(LANES),
                   pl.BlockSpec((1, 8, tm), lambda i: (i, 0, 0)),
                   pl.BlockSpec((1, 8, LANES), lambda i: (i, 0, 0))],
        out_shape=[jax.ShapeDtypeStruct((m, D_MODEL), F32), jax.ShapeDtypeStruct((m, D_MODEL), BF16),
                   jax.ShapeDtypeStruct((m, LANES), F32), jax.ShapeDtypeStruct((nt, 8, tm), F32),
                   jax.ShapeDtypeStruct((nt, 8, LANES), F32)],
        compiler_params=_cparams(("arbitrary",)),
        name="out_proj_router",
    )(x2, ys, oa, yc, wo, fg, gm, a2, s2, wr, br)


def _local_rows(tm):
    return -(-(2 * tm + (CHUNK - 1) * N_EXPERTS) // LANES) * LANES


def _pack_halves(x):
    w = x.shape[-1] // 2
    return pltpu.bitcast(x[:, :w], jnp.uint32) | lax.shift_right_logical(
        pltpu.bitcast(x[:, w:], jnp.uint32), jnp.uint32(16))


def _unpack_halves(p):
    hi = pltpu.bitcast(p & jnp.uint32(0xFFFF0000), F32)
    lo = pltpu.bitcast(lax.shift_left(p, jnp.uint32(16)), F32)
    return jnp.concatenate([hi, lo], axis=-1).astype(BF16)


def _wait_chunks(n, n_max, copy_of_size):
    for bit in range(n_max.bit_length()):
        @pl.when((lax.shift_right_logical(n, bit) & 1) == 1)
        def _():
            copy_of_size(CHUNK << bit).wait()


def _dispatch_kernel(lo_ref, gb_ref, n8_ref, tot_ref, ts_ref, tn_ref, h_ref, rtt_ref, xs_hbm,
                     xloc, zeros, sem, zsem, *, lr):
    t = pl.program_id(0)
    nt = pl.num_programs(0)
    slot = t % 2
    tm = h_ref.shape[0]

    def seg_copy(sl, src, dst):
        return pltpu.make_async_copy(xloc.at[sl, pl.ds(src, CHUNK), :], xs_hbm.at[pl.ds(dst, CHUNK), :],
                                     sem.at[sl])

    def wait_tile(tile, sl):
        _wait_chunks(tot_ref[tile], lr // CHUNK, lambda size: pltpu.make_async_copy(
            xloc.at[sl, pl.ds(0, size), :], xs_hbm.at[pl.ds(0, size), :], sem.at[sl]))

    @pl.when(t >= 2)
    def _():
        wait_tile(t - 2, slot)

    rows = _iota((lr, tm), 0).astype(F32)
    is1 = rows == rtt_ref[0, RT_ROW1:RT_ROW1 + 1, :]
    is2 = rows == rtt_ref[0, RT_ROW2:RT_ROW2 + 1, :]
    perm = jnp.where(is1, 1.0, jnp.where(is2, 1.0, 0.0))
    xloc[slot, :, 0:XS_FEAT] = _pack_halves(
        jnp.dot(perm.astype(BF16), h_ref[...], preferred_element_type=F32))
    w_row = jnp.sum(jnp.where(is1, rtt_ref[0, RT_W1:RT_W1 + 1, :],
                              jnp.where(is2, rtt_ref[0, RT_W2:RT_W2 + 1, :], 0.0)), axis=-1, keepdims=True)
    xloc[slot, :, XS_FEAT:XS_WIDTH] = jnp.where(
        _iota((lr, LANES), 1) == 0, pltpu.bitcast(jnp.broadcast_to(w_row, (lr, LANES)), jnp.uint32),
        jnp.uint32(0))

    def per_expert(e, c):
        idx = t * N_EXPERTS + e
        src0 = lo_ref[idx] * CHUNK
        dst0 = gb_ref[idx] * CHUNK

        def per_chunk(k, c2):
            seg_copy(slot, pl.multiple_of(src0 + k * CHUNK, CHUNK),
                     pl.multiple_of(dst0 + k * CHUNK, CHUNK)).start()
            return c2
        lax.fori_loop(0, n8_ref[idx], per_chunk, 0)
        return c
    lax.fori_loop(0, N_EXPERTS, per_expert, 0)

    @pl.when(t == nt - 1)
    def _():
        zeros[...] = jnp.zeros(zeros.shape, jnp.uint32)

        def zero_copy(dst):
            return pltpu.make_async_copy(zeros, xs_hbm.at[pl.ds(dst, CHUNK), :], zsem.at[0])

        def fill(e, c):
            dst0 = ts_ref[e] * CHUNK

            def one(k, c2):
                zero_copy(pl.multiple_of(dst0 + k * CHUNK, CHUNK)).start()
                return c2
            lax.fori_loop(0, tn_ref[e], one, 0)
            return c
        lax.fori_loop(0, N_EXPERTS + 1, fill, 0)

        def drain(e, c):
            def one(k, c2):
                zero_copy(0).wait()
                return c2
            lax.fori_loop(0, tn_ref[e], one, 0)
            return c
        lax.fori_loop(0, N_EXPERTS + 1, drain, 0)

        @pl.when(t >= 1)
        def _():
            wait_tile(t - 1, 1 - slot)
        wait_tile(t, slot)


def _dispatch(plan, h2, rtt, n_rows, tm):
    m = h2.shape[0]
    lr = _local_rows(tm)
    grid_spec = pltpu.PrefetchScalarGridSpec(
        num_scalar_prefetch=6,
        grid=(m // tm,),
        in_specs=[pl.BlockSpec((tm, D_MODEL), lambda t, *_: (t, 0)),
                  pl.BlockSpec((1, 8, tm), lambda t, *_: (t, 0, 0))],
        out_specs=pl.BlockSpec(memory_space=pl.ANY),
        scratch_shapes=[pltpu.VMEM((2, lr, XS_WIDTH), jnp.uint32),
                        pltpu.VMEM((CHUNK, XS_WIDTH), jnp.uint32),
                        pltpu.SemaphoreType.DMA((2,)), pltpu.SemaphoreType.DMA((1,))],
    )
    return pl.pallas_call(
        functools.partial(_dispatch_kernel, lr=lr),
        grid_spec=grid_spec,
        out_shape=jax.ShapeDtypeStruct((n_rows, XS_WIDTH), jnp.uint32),
        compiler_params=_cparams(("arbitrary",)),
        name="moe_dispatch",
    )(plan["lo8"], plan["gb8"], plan["n8"], plan["tot8"], plan["ts8"], plan["tn8"], h2, rtt)


def _expert_kernel(be_ref, nu_ref, x_ref, wg_ref, wu_ref, wd_ref, y_ref, wgb, wub, wdb):
    i = pl.program_id(0)

    @pl.when(i < nu_ref[0])
    def _():
        prev_e = be_ref[jnp.maximum(i - 1, 0)]

        @pl.when((i == 0) | (be_ref[i] != prev_e))
        def _():
            wgb[...] = wg_ref[0, 0].astype(BF16)
            wub[...] = wu_ref[0, 0].astype(BF16)
            wdb[...] = wd_ref[0, 0].astype(BF16)

        x = _unpack_halves(x_ref[:, 0:XS_FEAT])
        w_row = pltpu.bitcast(x_ref[:, XS_FEAT:XS_FEAT + 1], F32)
        hid = _silu(jnp.dot(x, wgb[...], preferred_element_type=F32)) * jnp.dot(
            x, wub[...], preferred_element_type=F32)
        y = jnp.dot(hid.astype(BF16), wdb[...], preferred_element_type=F32)
        y_ref[...] = _pack_halves((y * w_row).astype(BF16).astype(F32))

    @pl.when(i >= nu_ref[0])
    def _():
        y_ref[...] = jnp.zeros(y_ref.shape, jnp.uint32)


def _experts(block_e, n_used, xs, wg, wu, wd, layer):
    n_rows = xs.shape[0]
    wspec = lambda shape: pl.BlockSpec((1, 1) + shape, lambda i, be, nu: (layer, be[i], 0, 0))
    grid_spec = pltpu.PrefetchScalarGridSpec(
        num_scalar_prefetch=2,
        grid=(n_rows // MOE_BLOCK,),
        in_specs=[
            pl.BlockSpec((MOE_BLOCK, XS_WIDTH), lambda i, be, nu: (i, 0)),
            wspec((D_MODEL, D_EXPERT)), wspec((D_MODEL, D_EXPERT)), wspec((D_EXPERT, D_MODEL)),
        ],
        out_specs=pl.BlockSpec((MOE_BLOCK, XS_FEAT), lambda i, be, nu: (i, 0)),
        scratch_shapes=[
            pltpu.VMEM((D_MODEL, D_EXPERT), BF16),
            pltpu.VMEM((D_MODEL, D_EXPERT), BF16),
            pltpu.VMEM((D_EXPERT, D_MODEL), BF16),
        ],
    )
    return pl.pallas_call(
        _expert_kernel,
        grid_spec=grid_spec,
        out_shape=jax.ShapeDtypeStruct((n_rows, XS_FEAT), jnp.uint32),
        compiler_params=_cparams(("arbitrary",)),
        name="moe_experts",
    )(block_e, n_used, xs, wg, wu, wd)


def _combine_kernel(lo_ref, gb_ref, n8_ref, tot_ref, x_ref, rt_ref, gf_ref, fg_ref, ys_hbm,
                    o_ref, yloc, sem, *, lr, final):
    t = pl.program_id(0)
    nt = pl.num_programs(0)
    slot = t % 2
    tm = x_ref.shape[0]

    def fetch(tile, sl):
        def per_expert(e, c):
            idx = tile * N_EXPERTS + e
            dst0 = lo_ref[idx] * CHUNK
            src0 = gb_ref[idx] * CHUNK

            def per_chunk(k, c2):
                pltpu.make_async_copy(ys_hbm.at[pl.ds(pl.multiple_of(src0 + k * CHUNK, CHUNK), CHUNK), :],
                                      yloc.at[sl, pl.ds(pl.multiple_of(dst0 + k * CHUNK, CHUNK), CHUNK), :],
                                      sem.at[sl]).start()
                return c2
            lax.fori_loop(0, n8_ref[idx], per_chunk, 0)
            return c
        lax.fori_loop(0, N_EXPERTS, per_expert, 0)

    @pl.when(t == 0)
    def _():
        yloc[...] = jnp.zeros(yloc.shape, jnp.uint32)
        fetch(0, 0)

    @pl.when(t + 1 < nt)
    def _():
        fetch(t + 1, 1 - slot)

    _wait_chunks(tot_ref[t], lr // CHUNK, lambda size: pltpu.make_async_copy(
        ys_hbm.at[pl.ds(0, size), :], yloc.at[slot, pl.ds(0, size), :], sem.at[slot]))

    ysw = _unpack_halves(yloc[slot])
    rt = rt_ref[...]
    cols = _iota((tm, lr), 1).astype(F32)
    pick = jnp.where(cols == rt[:, RT_ROW1:RT_ROW1 + 1], 1.0,
                     jnp.where(cols == rt[:, RT_ROW2:RT_ROW2 + 1], 1.0, 0.0)).astype(BF16)
    x = x_ref[...] + gf_ref[0] * jnp.dot(pick, ysw, preferred_element_type=F32)
    if final:
        ms = jnp.mean(x * x, axis=-1, keepdims=True)
        x = x * lax.rsqrt(ms + EPS) * fg_ref[...]
    o_ref[...] = x


def _combine(plan, x2, ys, rt, gf, fg, seq, tm, final):
    m = x2.shape[0]
    per_b = seq // tm
    lr = _local_rows(tm)
    grid_spec = pltpu.PrefetchScalarGridSpec(
        num_scalar_prefetch=4,
        grid=(m // tm,),
        in_specs=[
            pl.BlockSpec((tm, D_MODEL), lambda i, *_: (i, 0)),
            pl.BlockSpec((tm, LANES), lambda i, *_: (i, 0)),
            pl.BlockSpec((1, 1, D_MODEL), lambda i, *_: (i // per_b, 0, 0)),
            pl.BlockSpec((1, D_MODEL), lambda i, *_: (0, 0)),
            pl.BlockSpec(memory_space=pl.ANY),
        ],
        out_specs=pl.BlockSpec((tm, D_MODEL), lambda i, *_: (i, 0)),
        scratch_shapes=[pltpu.VMEM((2, lr, XS_FEAT), jnp.uint32), pltpu.SemaphoreType.DMA((2,))],
    )
    return pl.pallas_call(
        functools.partial(_combine_kernel, lr=lr, final=final),
        grid_spec=grid_spec,
        out_shape=jax.ShapeDtypeStruct((m, D_MODEL), F32),
        compiler_params=_cparams(("arbitrary",)),
        name="moe_combine",
    )(plan["lo8"], plan["gb8"], plan["n8"], plan["tot8"], x2, rt, gf, fg, ys)


def _moe_plan(c, n_blocks):
    i32 = jnp.int32
    blk8 = MOE_BLOCK // CHUNK
    c8 = (c + CHUNK - 1) // CHUNK
    lo8 = jnp.cumsum(c8, axis=1) - c8
    per_e = jnp.sum(c8, axis=0)
    pad8 = (per_e + blk8 - 1) // blk8 * blk8
    end8 = jnp.cumsum(pad8)
    start8 = end8 - pad8
    gb8 = start8[None, :] + jnp.cumsum(c8, axis=0) - c8
    blk_start8 = jnp.arange(n_blocks, dtype=i32) * blk8
    block_e = jnp.minimum(jnp.sum(end8[None, :] <= blk_start8[:, None], axis=1), N_EXPERTS - 1)
    return {
        "lo8": lo8.reshape(-1).astype(i32), "gb8": gb8.reshape(-1).astype(i32),
        "n8": c8.reshape(-1).astype(i32), "tot8": jnp.sum(c8, axis=1).astype(i32),
        "ts8": jnp.concatenate([start8 + per_e, end8[-1:]]).astype(i32),
        "tn8": jnp.concatenate([pad8 - per_e, n_blocks * blk8 - end8[-1:]]).astype(i32),
        "block_e": block_e.astype(i32), "n_used": (end8[-1:] // blk8).astype(i32),
    }


def _w_in_sections(w):
    d_dt = SSD_WIDTH + XBC_WIDTH
    d_q = d_dt + SSD_HEADS
    d_f = d_q + 3 * ATTN_WIDTH
    d_ga = d_f + ATTN_HEADS
    small = jnp.concatenate([w[..., d_dt:d_q], w[..., d_f:d_ga],
                             jnp.zeros(w.shape[:-1] + (LANES - SSD_HEADS - ATTN_HEADS,), w.dtype)], axis=-1)
    sections = [w[..., SSD_WIDTH:d_dt], w[..., :SSD_WIDTH], w[..., d_q:d_f], w[..., d_ga:], small]
    return [sec.astype(BF16) for sec in sections]


def kernel(x, c, ada_w, ada_b, norm_mix_g, w_in, ssd_conv_w, ssd_conv_b, ssd_dt_bias, ssd_a_log,
           ssd_d, ssd_norm_g, fox_f_bias, fox_norm_g, cm_conv_w, cm_conv_b, cm_ln_g, cm_ln_b, w_out,
           norm_ffn_g, w_router_group, b_router_group, w_router_expert, b_router_expert, w_gate,
           w_up, w_down, final_norm_g):
    bsz, seq, d = x.shape
    m = bsz * seq
    tm = min(512, seq)
    tt = min(256, seq)
    n_blocks = -(-(2 * m + (CHUNK - 1) * (m // tm) * N_EXPERTS) // MOE_BLOCK) + N_EXPERTS

    mod = _modulation(c, ada_w, ada_b)
    x2 = x.reshape(m, d)
    w_in_sections = _w_in_sections(w_in)
    for l in range(DEPTH):
        sh_m, sc_m, g_m, sh_f, sc_f, g_f = [v.reshape(bsz, 1, d) for v in jnp.split(mod[l], 6, axis=-1)]
        a_m = norm_mix_g[l][None, None, :] * (1.0 + sc_m)
        proj = _inproj(x2, a_m, sh_m, w_in_sections, l, seq, tm)
        y_ssd = _ssd(proj, ssd_conv_w[l], ssd_conv_b[l], ssd_dt_bias[l], ssd_a_log[l],
                     ssd_d[l], ssd_norm_g[l], bsz, seq, tt)
        qa, ka, va, aux = _foxprep(proj, fox_f_bias[l], bsz, seq)
        o_att = _fox(qa, ka, va, aux)
        y_cnv = _conformer(proj, cm_conv_w[l], cm_conv_b[l], cm_ln_g[l], cm_ln_b[l], bsz, seq, tt)

        w_r = jnp.concatenate([w_router_group[l], w_router_expert[l],
                               jnp.zeros((d, LANES - N_GROUPS - N_EXPERTS), F32)], axis=1)
        w_rh = w_r.astype(BF16)
        w_r2 = jnp.concatenate([w_rh, (w_r - w_rh.astype(F32)).astype(BF16)], axis=1)
        b_r = jnp.concatenate([b_router_group[l], b_router_expert[l],
                               jnp.zeros((LANES - N_GROUPS - N_EXPERTS,), F32)]).reshape(1, LANES)
        a_f = norm_ffn_g[l][None, None, :] * (1.0 + sc_f)
        x2, h2, rt, rtt, cnt = _outproj(x2, y_ssd, o_att, y_cnv, w_out[l].astype(BF16),
                                        fox_norm_g[l].reshape(1, -1), g_m, a_f, sh_f, w_r2, b_r,
                                        seq, tm)
        plan = _moe_plan(cnt[:, 0, :N_EXPERTS].astype(jnp.int32), n_blocks)
        xs = _dispatch(plan, h2, rtt, n_blocks * MOE_BLOCK, tm)
        ys = _experts(plan["block_e"], plan["n_used"], xs, w_gate, w_up, w_down, l)
        x2 = _combine(plan, x2, ys, rt, g_f, final_norm_g.reshape(1, d), seq, tm,
                      final=(l == DEPTH - 1))
    return x2.reshape(bsz, seq, d)
```

```python
import functools

import jax
import jax.numpy as jnp
import numpy as np
from jax import lax
from jax.experimental import pallas as pl
from jax.experimental.pallas import tpu as pltpu

F32 = jnp.float32
BF16 = jnp.bfloat16
HIGHEST = lax.Precision.HIGHEST

D_MODEL = 1024
DEPTH = 4
SSD_WIDTH = 512
SSD_HEADS = 8
SSD_HEAD_DIM = 64
SSD_STATE = 128
SSD_CONV = 4
SSD_CHUNK = 128
XBC_WIDTH = 1024
ATTN_WIDTH = 256
ATTN_HEADS = 4
ATTN_HEAD_DIM = 64
CONV_WIDTH = 256
CONV_KERNEL = 31
N_GROUPS = 4
PER_GROUP = 8
N_EXPERTS = 32
D_EXPERT = 512
MOE_BLOCK = 512
EPS = 1e-6

LANES = 128
SUBLANES = 8
COL_XBC = 0
COL_Z = 1024
COL_QKV = 1536
COL_GA = 2304
COL_GB = 2560
COL_SM = 2816
NP = 2944
SM_DT = 0
SM_F = 8

VMEM_LIMIT = 56 * 1024 * 1024


def _cparams(sem):
    return pltpu.CompilerParams(dimension_semantics=sem, vmem_limit_bytes=VMEM_LIMIT)


def _sigmoid(x):
    return 1.0 / (1.0 + jnp.exp(-x))


def _silu(x):
    return x * _sigmoid(x)


def _softplus(x):
    return jnp.maximum(x, 0.0) + jnp.log1p(jnp.exp(-jnp.abs(x)))


def _iota(shape, dim):
    return lax.broadcasted_iota(jnp.int32, shape, dim)


def _split_bf16x3(x):
    hi = x.astype(BF16)
    r1 = x - hi.astype(F32)
    mid = r1.astype(BF16)
    lo = (r1 - mid.astype(F32)).astype(BF16)
    return hi, mid, lo


def _dot_onehot_rhs(x, sel):
    return sum(jnp.dot(part, sel, preferred_element_type=F32) for part in _split_bf16x3(x))


def _dot_onehot_lhs(sel, x):
    return sum(jnp.dot(sel, part, preferred_element_type=F32) for part in _split_bf16x3(x))


def _mod_kernel(c_ref, w_ref, b_ref, o_ref):
    cond = _silu(c_ref[...])
    o_ref[0] = jnp.dot(cond, w_ref[0], precision=HIGHEST, preferred_element_type=F32) + b_ref[0]


def _modulation(c, ada_w, ada_b):
    bsz = c.shape[0]
    rows = 8
    cpad = jnp.zeros((rows, D_MODEL), F32).at[:bsz].set(c)
    tn = 1536
    n6 = 6 * D_MODEL
    out = pl.pallas_call(
        _mod_kernel,
        grid=(DEPTH, n6 // tn),
        in_specs=[
            pl.BlockSpec((rows, D_MODEL), lambda l, j: (0, 0)),
            pl.BlockSpec((1, D_MODEL, tn), lambda l, j: (l, 0, j)),
            pl.BlockSpec((1, 1, tn), lambda l, j: (l, 0, j)),
        ],
        out_specs=pl.BlockSpec((1, rows, tn), lambda l, j: (l, 0, j)),
        out_shape=jax.ShapeDtypeStruct((DEPTH, rows, n6), F32),
        compiler_params=_cparams(("arbitrary", "arbitrary")),
        name="adaln_mod",
    )(cpad, ada_w, ada_b.reshape(DEPTH, 1, n6))
    return out[:, :bsz]


def _inproj_kernel(x_ref, a_ref, s_ref, *refs):
    w_refs, o_ref = refs[:-1], refs[-1]
    x = x_ref[...]
    ms = jnp.mean(x * x, axis=-1, keepdims=True)
    h = (x * lax.rsqrt(ms + EPS) * a_ref[0] + s_ref[0]).astype(BF16)
    col = 0
    for w_ref in w_refs:
        width = w_ref.shape[-1]
        o_ref[:, col:col + width] = jnp.dot(h, w_ref[0], preferred_element_type=F32)
        col += width


def _inproj(x2, a, s, w_sections, layer, seq, tm):
    m = x2.shape[0]
    per_b = seq // tm
    assert sum(w.shape[-1] for w in w_sections) == NP
    return pl.pallas_call(
        _inproj_kernel,
        grid=(m // tm,),
        in_specs=[
            pl.BlockSpec((tm, D_MODEL), lambda i: (i, 0)),
            pl.BlockSpec((1, 1, D_MODEL), lambda i: (i // per_b, 0, 0)),
            pl.BlockSpec((1, 1, D_MODEL), lambda i: (i // per_b, 0, 0)),
        ] + [pl.BlockSpec((1, D_MODEL, w.shape[-1]), lambda i: (layer, 0, 0)) for w in w_sections],
        out_specs=pl.BlockSpec((tm, NP), lambda i: (i, 0)),
        out_shape=jax.ShapeDtypeStruct((m, NP), F32),
        compiler_params=_cparams(("arbitrary",)),
        name="in_proj",
    )(x2, a, s, *w_sections)


def _ssd_kernel(z_ref, xbc_ref, sm_ref, cw_ref, cb_ref, dtb_ref, alog_ref, dtbt_ref,
                alogt_ref, e_ref, dx_ref, ng_ref, y_ref, xpad, xc, prev, *, tt):
    t = pl.program_id(1)

    @pl.when(t == 0)
    def _():
        xpad[0:8, :] = jnp.zeros((8, XBC_WIDTH), F32)
        prev[...] = jnp.zeros(prev.shape, F32)

    xpad[8:8 + tt, :] = xbc_ref[...]
    acc = jnp.broadcast_to(cb_ref[...], (tt, XBC_WIDTH))
    for k in range(SSD_CONV):
        off = 8 - (SSD_CONV - 1) + k
        acc = acc + cw_ref[k:k + 1, :] * xpad[off:off + tt, :]
    xc[...] = _silu(acc)
    xpad[0:8, :] = xpad[tt:tt + 8, :]

    cl = SSD_CHUNK
    row = _iota((cl, cl), 0)
    col = _iota((cl, cl), 1)
    causal = row >= col
    tril = jnp.where(causal, 1.0, 0.0).astype(BF16)
    triu = jnp.where(row <= col, 1.0, 0.0).astype(BF16)
    lo = col < SSD_HEAD_DIM
    lane1 = _iota((1, LANES), 1)
    a_row = jnp.where(lane1 < SSD_HEADS, -jnp.exp(alog_ref[...]), 0.0)
    a_col = -jnp.exp(alogt_ref[...])
    expand = e_ref[...]

    def chunk(c, carry):
        r0 = c * cl
        xs = xc[pl.ds(r0, cl), 0:SSD_WIDTH]
        bmat = xc[pl.ds(r0, cl), SSD_WIDTH:SSD_WIDTH + 2 * SSD_STATE]
        cmat = xc[pl.ds(r0, cl), SSD_WIDTH + 2 * SSD_STATE:XBC_WIDTH]
        sm = sm_ref[pl.ds(r0, cl), :]
        dt = _softplus(sm + dtb_ref[...])
        da = dt * a_row
        acs = _dot_onehot_lhs(tril, da)
        dtt = _softplus(sm.T[0:SSD_HEADS, :] + dtbt_ref[...])
        acst = _dot_onehot_rhs(dtt * a_col, triu)
        dt_x = _dot_onehot_rhs(dt, expand)
        acs_x = _dot_onehot_rhs(acs, expand)
        last = acs_x[cl - 1:cl, :]
        eacs_x = jnp.exp(acs_x)
        dte_x = jnp.exp(last - acs_x)
        cd_x = jnp.exp(last)
        xdt = xs * dt_x
        xdte = (xdt * dte_x).astype(BF16)
        zc = z_ref[pl.ds(r0, cl), :]
        for g in range(2):
            bg = bmat[:, g * SSD_STATE:(g + 1) * SSD_STATE]
            cg = cmat[:, g * SSD_STATE:(g + 1) * SSD_STATE].astype(BF16)
            bgt = bg.T.astype(BF16)
            cbm = jnp.dot(cg, bgt, preferred_element_type=F32)
            pair_out = []
            for j in range(2):
                p = 2 * g + j
                sl = slice(p * LANES, (p + 1) * LANES)
                xp = xdt[:, sl]
                yd = jnp.zeros((cl, LANES), F32)
                for half in range(2):
                    h = 2 * p + half
                    seg = acs[:, h:h + 1] - acst[h:h + 1, :]
                    dec = jnp.exp(jnp.where(causal, seg, -jnp.inf))
                    gm = (cbm * dec).astype(BF16)
                    own = lo if half == 0 else jnp.logical_not(lo)
                    xm = jnp.where(own, xp, 0.0).astype(BF16)
                    yd = yd + jnp.dot(gm, xm, preferred_element_type=F32)
                prev_p = prev[:, sl]
                yo = jnp.dot(cg, prev_p.astype(BF16), preferred_element_type=F32) * eacs_x[:, sl]
                st = jnp.dot(bgt, xdte[:, sl], preferred_element_type=F32)
                prev[:, sl] = prev_p * cd_x[:, sl] + st
                pair_out.append(yd + yo + xs[:, sl] * dx_ref[:, sl])
            gs = slice(g * 2 * LANES, (g + 1) * 2 * LANES)
            yg = jnp.concatenate(pair_out, axis=-1) * _silu(zc[:, gs])
            ms = jnp.mean(yg * yg, axis=-1, keepdims=True)
            y_ref[pl.ds(r0, cl), gs] = yg * lax.rsqrt(ms + EPS) * ng_ref[:, gs]
        return carry

    for c in range(tt // cl):
        chunk(c, 0)


def _ssd(proj, cw, cb, dtb, alog, dx, ng, bsz, seq, tt):
    m = bsz * seq
    nt = seq // tt
    pad = LANES - SSD_HEADS
    dtb_row = jnp.pad(dtb, (0, pad)).reshape(1, LANES)
    alog_row = jnp.pad(alog, (0, pad)).reshape(1, LANES)
    dtb_col = jnp.broadcast_to(dtb[:, None], (SSD_HEADS, SSD_CHUNK))
    alog_col = jnp.broadcast_to(alog[:, None], (SSD_HEADS, SSD_CHUNK))
    expand = np.zeros((LANES, SSD_WIDTH), np.float32)
    for h in range(SSD_HEADS):
        expand[h, h * SSD_HEAD_DIM:(h + 1) * SSD_HEAD_DIM] = 1.0
    dx_row = jnp.repeat(dx, SSD_HEAD_DIM).reshape(1, SSD_WIDTH)
    full = lambda shape: pl.BlockSpec(shape, lambda b, t: (0,) * len(shape))
    return pl.pallas_call(
        functools.partial(_ssd_kernel, tt=tt),
        grid=(bsz, nt),
        in_specs=[
            pl.BlockSpec((tt, SSD_WIDTH), lambda b, t: (b * nt + t, COL_Z // SSD_WIDTH)),
            pl.BlockSpec((tt, XBC_WIDTH), lambda b, t: (b * nt + t, COL_XBC // XBC_WIDTH + 0)),
            pl.BlockSpec((tt, LANES), lambda b, t: (b * nt + t, COL_SM // LANES)),
            full((SSD_CONV, XBC_WIDTH)),
            full((1, XBC_WIDTH)),
            full((1, LANES)),
            full((1, LANES)),
            full((SSD_HEADS, SSD_CHUNK)),
            full((SSD_HEADS, SSD_CHUNK)),
            full((LANES, SSD_WIDTH)),
            full((1, SSD_WIDTH)),
            full((1, SSD_WIDTH)),
        ],
        out_specs=pl.BlockSpec((tt, SSD_WIDTH), lambda b, t: (b * nt + t, 0)),
        out_shape=jax.ShapeDtypeStruct((m, SSD_WIDTH), F32),
        scratch_shapes=[
            pltpu.VMEM((tt + 8, XBC_WIDTH), F32),
            pltpu.VMEM((tt, XBC_WIDTH), F32),
            pltpu.VMEM((SSD_STATE, SSD_WIDTH), F32),
        ],
        compiler_params=_cparams(("arbitrary", "arbitrary")),
        name="ssd_scan",
    )(proj, proj, proj, cw, cb.reshape(1, XBC_WIDTH), dtb_row, alog_row, dtb_col, alog_col,
      jnp.asarray(expand, dtype=BF16), dx_row, ng.reshape(1, SSD_WIDTH))


LOG2E = 1.4426950408889634
ATT_BLOCK = 256
ATT_TQ = 512
PRUNE_LOG2 = 140.0
FAST_GAP_LOG2 = 60.0
SHIFT_MARGIN_LOG2 = 1.0
AUX_CUMEND = 0
AUX_KMAX = 1
ST_QNORM, ST_CUM, ST_SELF = 0, 1, 2


def _foxprep_kernel(qkv_ref, sm_ref, fb_ref, q_out, k_out, v_out, aux_out, stat_out, carry, *, tt):
    t = pl.program_id(1)

    @pl.when(t == 0)
    def _():
        carry[...] = jnp.zeros(carry.shape, F32)
        aux_out[...] = jnp.zeros(aux_out.shape, F32)

    logit = sm_ref[...] + fb_ref[...]
    logf = -_softplus(-logit)
    row = _iota((tt, tt), 0)
    col = _iota((tt, tt), 1)
    tril = jnp.where(row >= col, 1.0, 0.0).astype(BF16)
    cum = _dot_onehot_lhs(tril, logf) + carry[...]
    carry[...] = cum[tt - 1:tt, :]

    lane = _iota((tt, LANES), 1)
    lane1 = _iota((1, LANES), 1)
    scale = ATTN_HEAD_DIM ** -0.5 * LOG2E
    for h in range(ATTN_HEADS):
        pair, half = h // 2, h % 2
        own = (lane < ATTN_HEAD_DIM) if half == 0 else (lane >= ATTN_HEAD_DIM)
        a0 = ATTN_HEAD_DIM * (1 - half)
        cs = jnp.broadcast_to(cum[:, SM_F + h:SM_F + h + 1], (tt, LANES)) * LOG2E
        hi = cs.astype(BF16).astype(F32)
        r1 = cs - hi
        mid = r1.astype(BF16).astype(F32)
        low = r1 - mid
        qp = qkv_ref[:, pair * LANES:(pair + 1) * LANES]
        kp = qkv_ref[:, ATTN_WIDTH + pair * LANES:ATTN_WIDTH + (pair + 1) * LANES]
        vp = qkv_ref[:, 2 * ATTN_WIDTH + pair * LANES:2 * ATTN_WIDTH + (pair + 1) * LANES]
        qa = jnp.where(lane == a0, hi, jnp.where(lane == a0 + 1, mid, jnp.where(
            lane == a0 + 2, low, jnp.where((lane >= a0 + 3) & (lane < a0 + 6), 1.0, 0.0))))
        ka = jnp.where(lane == a0 + 3, -hi, jnp.where(lane == a0 + 4, -mid, jnp.where(
            lane == a0 + 5, -low, jnp.where((lane >= a0) & (lane < a0 + 9), 1.0, 0.0))))
        qb = jnp.where(own, qp * scale, qa).astype(BF16)
        q_out[0, h] = qb
        kb = jnp.where(own, kp, ka).astype(BF16)
        k_out[0, h] = kb
        qf = jnp.where(own, qb.astype(F32), 0.0)
        qn = jnp.sqrt(jnp.sum(qf * qf, axis=-1, keepdims=True))
        s_self = jnp.sum(qf * kb.astype(F32), axis=-1, keepdims=True)
        stat = jnp.where(lane == ST_QNORM, qn, jnp.where(lane == ST_CUM, hi + mid + low, jnp.where(
            lane == ST_SELF, s_self, 0.0)))
        stat_out[0, h] = stat.T[0:8, :]
        v_out[0, h] = jnp.where(own, vp, jnp.where(lane == a0, 1.0, 0.0)).astype(BF16)
        kf = jnp.where(own, kb.astype(F32), 0.0)
        kn2 = jnp.max(jnp.sum(kf * kf, axis=-1, keepdims=True), axis=0, keepdims=True)
        here = lane1 == t
        aux_out[0, h, AUX_CUMEND:AUX_CUMEND + 1, :] = jnp.where(
            here, cs[tt - 1:tt, :], aux_out[0, h, AUX_CUMEND:AUX_CUMEND + 1, :])
        aux_out[0, h, AUX_KMAX:AUX_KMAX + 1, :] = jnp.where(
            here, jnp.sqrt(kn2), aux_out[0, h, AUX_KMAX:AUX_KMAX + 1, :])


def _foxprep(proj, fb, bsz, seq):
    tt = ATT_BLOCK
    nt = seq // tt
    assert nt <= LANES
    fb_row = jnp.zeros((1, LANES), F32).at[0, SM_F:SM_F + ATTN_HEADS].set(fb)
    shp = jax.ShapeDtypeStruct((bsz, ATTN_HEADS, seq, LANES), BF16)
    ospec = pl.BlockSpec((1, ATTN_HEADS, tt, LANES), lambda b, t: (b, 0, t, 0))
    return pl.pallas_call(
        functools.partial(_foxprep_kernel, tt=tt),
        grid=(bsz, nt),
        in_specs=[
            pl.BlockSpec((tt, 3 * ATTN_WIDTH), lambda b, t: (b * nt + t, COL_QKV // (3 * ATTN_WIDTH))),
            pl.BlockSpec((tt, LANES), lambda b, t: (b * nt + t, COL_SM // LANES)),
            pl.BlockSpec((1, LANES), lambda b, t: (0, 0)),
        ],
        out_specs=[ospec, ospec, ospec,
                   pl.BlockSpec((1, ATTN_HEADS, 8, LANES), lambda b, t: (b, 0, 0, 0)),
                   pl.BlockSpec((1, ATTN_HEADS, 8, tt), lambda b, t: (b, 0, 0, t))],
        out_shape=[shp, shp, shp, jax.ShapeDtypeStruct((bsz, ATTN_HEADS, 8, LANES), F32),
                   jax.ShapeDtypeStruct((bsz, ATTN_HEADS, 8, seq), F32)],
        scratch_shapes=[pltpu.VMEM((1, LANES), F32)],
        compiler_params=_cparams(("arbitrary", "arbitrary")),
        name="fox_prep",
    )(proj, proj, fb_row)


def _fox_kernel(q_ref, k_ref, v_ref, aux_ref, stat_ref, o_ref, *, tq, kb):
    i = pl.program_id(2)
    ib = i * (tq // kb)
    nt = (((1,), (1,)), ((), ()))
    lane = _iota((tq, LANES), 1)
    lane1 = _iota((1, LANES), 1)
    qs = (q_ref[0, 0], q_ref[0, 1])

    def step(hh, r0, width, m, acc, mask):
        kblk = k_ref[0, hh, pl.ds(r0, width), :]
        vblk = v_ref[0, hh, pl.ds(r0, width), :]
        s = lax.dot_general(qs[hh], kblk, nt, preferred_element_type=F32)
        if mask is not None:
            s = jnp.where(mask, s, -jnp.inf)
        m_new = jnp.maximum(m, jnp.max(s, axis=-1, keepdims=True))
        p = jnp.exp2(s - m_new)
        alpha = jnp.exp2(m - m_new)
        acc = acc * alpha + jnp.dot(p.astype(BF16), vblk, preferred_element_type=F32)
        return m_new, acc

    d0 = pl.multiple_of(i * tq, tq)
    diag = _iota((tq, tq), 0) >= _iota((tq, tq), 1)
    ib_f = ib.astype(F32)
    first = []
    gaps = []
    q_shift = []
    for hh in range(2):
        a0 = ATTN_HEAD_DIM * (1 - hh)
        st = stat_ref[0, hh]
        qn = st[ST_QNORM:ST_QNORM + 1, :]
        cum_t = st[ST_CUM:ST_CUM + 1, :]
        s_self = st[ST_SELF:ST_SELF + 1, :]
        aux = aux_ref[0, hh]
        cend = aux[AUX_CUMEND:AUX_CUMEND + 1, :]
        kmax = jnp.max(jnp.where(lane1 < ib + tq // kb, aux[AUX_KMAX:AUX_KMAX + 1, :], 0.0), axis=-1,
                       keepdims=True)
        bound = qn * kmax
        slack = jnp.max(bound + cum_t - s_self, axis=-1, keepdims=True)
        live = (lane1 < ib) & (slack - cend > -PRUNE_LOG2)
        first.append(jnp.min(jnp.where(live, lane1.astype(F32), ib_f)))
        parts = _split_bf16x3(bound + SHIFT_MARGIN_LOG2)
        gaps.append(jnp.max(sum(part.astype(F32) for part in parts) - s_self))
        terms = jnp.concatenate([-part.astype(F32) for part in parts] + [jnp.zeros((13, tq), F32)],
                                axis=0).astype(BF16)
        place = jnp.where(_iota((16, LANES), 1) == _iota((16, LANES), 0) + (a0 + 6), 1.0,
                          0.0).astype(BF16)
        q_shift.append(qs[hh] + lax.dot_general(terms, place, (((0,), (0,)), ((), ())),
                                                preferred_element_type=F32).astype(BF16))
    j_first = jnp.minimum(first[0], first[1]).astype(jnp.int32)

    def fast(_):
        def blocks(c, r0, width, mask=None):
            out = []
            for hh in range(2):
                s = lax.dot_general(q_shift[hh], k_ref[0, hh, pl.ds(r0, width), :], nt,
                                    preferred_element_type=F32)
                if mask is not None:
                    s = jnp.where(mask, s, -jnp.inf)
                out.append(c[hh] + jnp.dot(jnp.exp2(s).astype(BF16), v_ref[0, hh, pl.ds(r0, width), :],
                                           preferred_element_type=F32))
            return tuple(out)

        zero = jnp.zeros((tq, LANES), F32)
        c = blocks((zero, zero), d0, tq, diag)
        n = ib - j_first
        c = lax.cond((n & 1) == 1, lambda c: blocks(c, pl.multiple_of(j_first * kb, kb), kb),
                     lambda c: c, c)
        c = lax.cond((n & 2) == 2, lambda c: blocks(c, pl.multiple_of((j_first + (n & 1)) * kb, kb), 2 * kb),
                     lambda c: c, c)
        j0 = j_first + (n & 3)

        def body(jj, c):
            return blocks(c, pl.multiple_of((j0 + 4 * jj) * kb, kb), 4 * kb)
        return lax.fori_loop(0, lax.shift_right_logical(n, 2), body, c)

    def online(_):
        m_init = jnp.full((tq, 1), -jnp.inf, F32)
        acc_init = jnp.zeros((tq, LANES), F32)
        carry = step(0, d0, tq, m_init, acc_init, diag) + step(1, d0, tq, m_init, acc_init, diag)

        def body(j, c):
            r0 = pl.multiple_of(j * kb, kb)
            m0, a0_, m1, a1_ = c
            m0, a0_ = step(0, r0, kb, m0, a0_, None)
            m1, a1_ = step(1, r0, kb, m1, a1_, None)
            return m0, a0_, m1, a1_
        _, a0_, _, a1_ = lax.fori_loop(j_first, ib, body, carry)
        return a0_, a1_

    acc0, acc1 = lax.cond(jnp.maximum(gaps[0], gaps[1]) <= FAST_GAP_LOG2, fast, online, None)
    den0 = jnp.sum(jnp.where(lane == ATTN_HEAD_DIM, acc0, 0.0), axis=-1, keepdims=True)
    den1 = jnp.sum(jnp.where(lane == 0, acc1, 0.0), axis=-1, keepdims=True)
    o_ref[...] = jnp.where(lane < ATTN_HEAD_DIM, acc0 / den0, acc1 / den1)


def _fox(qa, ka, va, aux, stat):
    bsz, nh, seq, _ = qa.shape
    tq = min(ATT_TQ, seq)
    nq = seq // tq
    kv_spec = pl.BlockSpec((1, 2, seq, LANES), lambda b, p, i: (b, p, 0, 0))
    return pl.pallas_call(
        functools.partial(_fox_kernel, tq=tq, kb=ATT_BLOCK),
        grid=(bsz, nh // 2, nq),
        in_specs=[pl.BlockSpec((1, 2, tq, LANES), lambda b, p, i: (b, p, i, 0)), kv_spec, kv_spec,
                  pl.BlockSpec((1, 2, 8, LANES), lambda b, p, i: (b, p, 0, 0)),
                  pl.BlockSpec((1, 2, 8, tq), lambda b, p, i: (b, p, 0, i))],
        out_specs=pl.BlockSpec((tq, LANES), lambda b, p, i: (b * nq + i, p)),
        out_shape=jax.ShapeDtypeStruct((bsz * seq, ATTN_WIDTH), F32),
        compiler_params=_cparams(("arbitrary", "arbitrary", "arbitrary")),
        name="fox_attn",
    )(qa, ka, va, aux, stat)


CONF_HALO = 32
CONF_ROWS = 64


def _conf_kernel(ga_ref, gb_ref, w_ref, b_ref, lg_ref, lb_ref, y_ref, upad, ush, *, tt):
    t = pl.program_id(1)

    @pl.when(t == 0)
    def _():
        upad[0:CONF_HALO, :] = jnp.zeros((CONF_HALO, CONV_WIDTH), F32)

    upad[CONF_HALO:CONF_HALO + tt, :] = ga_ref[...] * _sigmoid(gb_ref[...])
    base = CONF_HALO - (CONV_KERNEL - 1)
    for ph in range(1, SUBLANES):
        ush[ph - 1] = upad[ph:ph + ush.shape[1], :]
    for r in range(tt // CONF_ROWS):
        r0 = r * CONF_ROWS
        acc = jnp.broadcast_to(b_ref[...], (CONF_ROWS, CONV_WIDTH))
        for k in range(CONV_KERNEL):
            ph = (base + k) % SUBLANES
            lo = r0 + base + k - ph
            src = upad[lo:lo + CONF_ROWS, :] if ph == 0 else ush[ph - 1, lo:lo + CONF_ROWS, :]
            acc = acc + w_ref[k:k + 1, :] * src
        mu = jnp.mean(acc, axis=-1, keepdims=True)
        cen = acc - mu
        var = jnp.mean(cen * cen, axis=-1, keepdims=True)
        y = cen * lax.rsqrt(var + EPS) * lg_ref[...] + lb_ref[...]
        y_ref[r0:r0 + CONF_ROWS, :] = _silu(y)
    upad[0:CONF_HALO, :] = upad[tt:tt + CONF_HALO, :]


def _conformer(proj, w, b, lg, lb, bsz, seq, tt):
    m = bsz * seq
    nt = seq // tt
    full = lambda shape: pl.BlockSpec(shape, lambda bb, t: (0,) * len(shape))
    return pl.pallas_call(
        functools.partial(_conf_kernel, tt=tt),
        grid=(bsz, nt),
        in_specs=[
            pl.BlockSpec((tt, CONV_WIDTH), lambda bb, t: (bb * nt + t, COL_GA // CONV_WIDTH)),
            pl.BlockSpec((tt, CONV_WIDTH), lambda bb, t: (bb * nt + t, COL_GB // CONV_WIDTH)),
            full((CONV_KERNEL, CONV_WIDTH)),
            full((1, CONV_WIDTH)),
            full((1, CONV_WIDTH)),
            full((1, CONV_WIDTH)),
        ],
        out_specs=pl.BlockSpec((tt, CONV_WIDTH), lambda bb, t: (bb * nt + t, 0)),
        out_shape=jax.ShapeDtypeStruct((m, CONV_WIDTH), F32),
        scratch_shapes=[pltpu.VMEM((tt + CONF_HALO, CONV_WIDTH), F32),
                        pltpu.VMEM((SUBLANES - 1, tt + CONF_HALO - SUBLANES, CONV_WIDTH), F32)],
        compiler_params=_cparams(("arbitrary", "arbitrary")),
        name="conformer_conv",
    )(proj, proj, w, b.reshape(1, -1), lg.reshape(1, -1), lb.reshape(1, -1))


ROUTE_BIG = 1e9
CHUNK = 8
XS_FEAT = D_MODEL // 2
XS_WIDTH = XS_FEAT + LANES
RT_W1, RT_W2, RT_ROW1, RT_ROW2 = 0, 1, 2, 3


def _split3(x):
    hi = x.astype(BF16)
    lo = (x - hi.astype(F32)).astype(BF16)
    return hi, lo


def _outproj_kernel(x_ref, ys_ref, oa_ref, yc_ref, wo_ref, fg_ref, gm_ref, a2_ref, s2_ref,
                    wr_ref, br_ref, xn_ref, h2_ref, rt_ref, rtt_ref, cnt_ref):
    tm = x_ref.shape[0]
    lane = _iota((tm, LANES), 1)
    att = oa_ref[...]
    ms = jnp.mean(att * att, axis=-1, keepdims=True)
    att = att * lax.rsqrt(ms + EPS) * fg_ref[...]
    y = jnp.dot(ys_ref[...].astype(BF16), wo_ref[0:SSD_WIDTH, :], preferred_element_type=F32)
    y = y + jnp.dot(att.astype(BF16), wo_ref[SSD_WIDTH:SSD_WIDTH + ATTN_WIDTH, :],
                    preferred_element_type=F32)
    y = y + jnp.dot(yc_ref[...].astype(BF16), wo_ref[SSD_WIDTH + ATTN_WIDTH:, :],
                    preferred_element_type=F32)
    xn = x_ref[...] + gm_ref[0] * y
    xn_ref[...] = xn
    ms2 = jnp.mean(xn * xn, axis=-1, keepdims=True)
    h2 = xn * lax.rsqrt(ms2 + EPS) * a2_ref[0] + s2_ref[0]
    h2_ref[...] = h2.astype(BF16)

    hh, hl = _split3(h2)
    part = jnp.dot(jnp.concatenate([hh, hl], axis=0), wr_ref[...], preferred_element_type=F32)
    logits = (part[:tm, :LANES] + part[:tm, LANES:] + part[tm:, :LANES] + part[tm:, LANES:]) + br_ref[...]
    lanef = lane.astype(F32)
    lg = jnp.where(lane < N_GROUPS, logits, -jnp.inf)
    gmax = jnp.max(lg, axis=-1, keepdims=True)
    gsum = jnp.sum(jnp.exp(lg - gmax), axis=-1, keepdims=True)
    gidx = jnp.min(jnp.where(lg == gmax, lanef, ROUTE_BIG), axis=-1, keepdims=True)
    e_lo = N_GROUPS + PER_GROUP * gidx
    le = jnp.where((lanef >= e_lo) & (lanef < e_lo + PER_GROUP), logits, -jnp.inf)
    m1 = jnp.max(le, axis=-1, keepdims=True)
    i1 = jnp.min(jnp.where(le == m1, lanef, ROUTE_BIG), axis=-1, keepdims=True)
    le2 = jnp.where(lanef == i1, -jnp.inf, le)
    m2 = jnp.max(le2, axis=-1, keepdims=True)
    i2 = jnp.min(jnp.where(le2 == m2, lanef, ROUTE_BIG), axis=-1, keepdims=True)
    esum = jnp.sum(jnp.exp(le - m1), axis=-1, keepdims=True)
    p1 = 1.0 / esum
    p2 = jnp.exp(m2 - m1) / esum
    psel = 1.0 / gsum
    w1 = p1 / (p1 + p2) * psel
    w2 = p2 / (p1 + p2) * psel

    oh1 = (lanef == (i1 - N_GROUPS)).astype(F32)
    oh2 = (lanef == (i2 - N_GROUPS)).astype(F32)
    oh = oh1 + oh2
    cnt = jnp.sum(oh, axis=0, keepdims=True)
    before = (_iota((tm, tm), 0) > _iota((tm, tm), 1)).astype(BF16)
    rank = jnp.dot(before, oh.astype(BF16), preferred_element_type=F32)
    chunks = jnp.floor((cnt + (CHUNK - 1)) * (1.0 / CHUNK))
    below = (_iota((LANES, LANES), 0) < _iota((LANES, LANES), 1)).astype(BF16)
    seg_lo = CHUNK * jnp.dot(jnp.broadcast_to(chunks, (8, LANES)).astype(BF16), below,
                             preferred_element_type=F32)[0:1, :]
    lr1 = jnp.sum(oh1 * (rank + seg_lo), axis=-1, keepdims=True)
    lr2 = jnp.sum(oh2 * (rank + seg_lo), axis=-1, keepdims=True)
    route = jnp.where(lane == RT_W1, w1, jnp.where(lane == RT_W2, w2, jnp.where(
        lane == RT_ROW1, lr1, jnp.where(lane == RT_ROW2, lr2, 0.0))))
    rt_ref[...] = route
    rtt_ref[0] = route.T[0:8, :]
    cnt_ref[0] = jnp.broadcast_to(cnt, (8, LANES))


def _outproj(x2, ys, oa, yc, wo, fg, gm, a2, s2, wr, br, seq, tm):
    m = x2.shape[0]
    per_b = seq // tm
    nt = m // tm
    row = lambda w: pl.BlockSpec((tm, w), lambda i: (i, 0))
    bvec = pl.BlockSpec((1, 1, D_MODEL), lambda i: (i // per_b, 0, 0))
    full = lambda shape: pl.BlockSpec(shape, lambda i: (0,) * len(shape))
    return pl.pallas_call(
        _outproj_kernel,
        grid=(nt,),
        in_specs=[
            row(D_MODEL), row(SSD_WIDTH), row(ATTN_WIDTH), row(CONV_WIDTH),
            full((D_MODEL, D_MODEL)), full((1, ATTN_WIDTH)), bvec, bvec, bvec,
            full((D_MODEL, 2 * LANES)), full((1, LANES)),
        ],
        out_specs=[row(D_MODEL), row(D_MODEL), row(LANES),
                   pl.BlockSpec((1, 8, tm), lambda i: (i, 0, 0)),
                   pl.BlockSpec((1, 8, LANES), lambda i: (i, 0, 0))],
        out_shape=[jax.ShapeDtypeStruct((m, D_MODEL), F32), jax.ShapeDtypeStruct((m, D_MODEL), BF16),
                   jax.ShapeDtypeStruct((m, LANES), F32), jax.ShapeDtypeStruct((nt, 8, tm), F32),
                   jax.ShapeDtypeStruct((nt, 8, LANES), F32)],
        compiler_params=_cparams(("arbitrary",)),
        name="out_proj_router",
    )(x2, ys, oa, yc, wo, fg, gm, a2, s2, wr, br)


def _local_rows(tm):
    return -(-(2 * tm + (CHUNK - 1) * N_EXPERTS) // LANES) * LANES


def _pack_halves(x):
    w = x.shape[-1] // 2
    return pltpu.bitcast(x[:, :w], jnp.uint32) | lax.shift_right_logical(
        pltpu.bitcast(x[:, w:], jnp.uint32), jnp.uint32(16))


def _unpack_halves(p):
    hi = pltpu.bitcast(p & jnp.uint32(0xFFFF0000), F32)
    lo = pltpu.bitcast(lax.shift_left(p, jnp.uint32(16)), F32)
    return jnp.concatenate([hi, lo], axis=-1).astype(BF16)


def _wait_chunks(n, n_max, copy_of_size):
    for bit in range(n_max.bit_length()):
        @pl.when((lax.shift_right_logical(n, bit) & 1) == 1)
        def _():
            copy_of_size(CHUNK << bit).wait()


def _dispatch_kernel(lo_ref, gb_ref, n8_ref, tot_ref, ts_ref, tn_ref, h_ref, rtt_ref, xs_hbm,
                     xloc, zeros, sem, zsem, *, lr):
    t = pl.program_id(0)
    nt = pl.num_programs(0)
    slot = t % 2
    tm = h_ref.shape[0]

    def seg_copy(sl, src, dst):
        return pltpu.make_async_copy(xloc.at[sl, pl.ds(src, CHUNK), :], xs_hbm.at[pl.ds(dst, CHUNK), :],
                                     sem.at[sl])

    def wait_tile(tile, sl):
        _wait_chunks(tot_ref[tile], lr // CHUNK, lambda size: pltpu.make_async_copy(
            xloc.at[sl, pl.ds(0, size), :], xs_hbm.at[pl.ds(0, size), :], sem.at[sl]))

    @pl.when(t >= 2)
    def _():
        wait_tile(t - 2, slot)

    rows = _iota((lr, tm), 0).astype(F32)
    is1 = rows == rtt_ref[0, RT_ROW1:RT_ROW1 + 1, :]
    is2 = rows == rtt_ref[0, RT_ROW2:RT_ROW2 + 1, :]
    perm = jnp.where(is1, 1.0, jnp.where(is2, 1.0, 0.0))
    xloc[slot, :, 0:XS_FEAT] = _pack_halves(
        jnp.dot(perm.astype(BF16), h_ref[...], preferred_element_type=F32))
    w_row = jnp.sum(jnp.where(is1, rtt_ref[0, RT_W1:RT_W1 + 1, :],
                              jnp.where(is2, rtt_ref[0, RT_W2:RT_W2 + 1, :], 0.0)), axis=-1, keepdims=True)
    xloc[slot, :, XS_FEAT:XS_WIDTH] = jnp.where(
        _iota((lr, LANES), 1) == 0, pltpu.bitcast(jnp.broadcast_to(w_row, (lr, LANES)), jnp.uint32),
        jnp.uint32(0))

    def per_expert(e, c):
        idx = t * N_EXPERTS + e
        src0 = lo_ref[idx] * CHUNK
        dst0 = gb_ref[idx] * CHUNK

        def per_chunk(k, c2):
            seg_copy(slot, pl.multiple_of(src0 + k * CHUNK, CHUNK),
                     pl.multiple_of(dst0 + k * CHUNK, CHUNK)).start()
            return c2
        lax.fori_loop(0, n8_ref[idx], per_chunk, 0)
        return c
    lax.fori_loop(0, N_EXPERTS, per_expert, 0)

    @pl.when(t == nt - 1)
    def _():
        zeros[...] = jnp.zeros(zeros.shape, jnp.uint32)

        def zero_copy(dst):
            return pltpu.make_async_copy(zeros, xs_hbm.at[pl.ds(dst, CHUNK), :], zsem.at[0])

        def fill(e, c):
            dst0 = ts_ref[e] * CHUNK

            def one(k, c2):
                zero_copy(pl.multiple_of(dst0 + k * CHUNK, CHUNK)).start()
                return c2
            lax.fori_loop(0, tn_ref[e], one, 0)
            return c
        lax.fori_loop(0, N_EXPERTS + 1, fill, 0)

        def drain(e, c):
            def one(k, c2):
                zero_copy(0).wait()
                return c2
            lax.fori_loop(0, tn_ref[e], one, 0)
            return c
        lax.fori_loop(0, N_EXPERTS + 1, drain, 0)

        @pl.when(t >= 1)
        def _():
            wait_tile(t - 1, 1 - slot)
        wait_tile(t, slot)


def _dispatch(plan, h2, rtt, n_rows, tm):
    m = h2.shape[0]
    lr = _local_rows(tm)
    grid_spec = pltpu.PrefetchScalarGridSpec(
        num_scalar_prefetch=6,
        grid=(m // tm,),
        in_specs=[pl.BlockSpec((tm, D_MODEL), lambda t, *_: (t, 0)),
                  pl.BlockSpec((1, 8, tm), lambda t, *_: (t, 0, 0))],
        out_specs=pl.BlockSpec(memory_space=pl.ANY),
        scratch_shapes=[pltpu.VMEM((2, lr, XS_WIDTH), jnp.uint32),
                        pltpu.VMEM((CHUNK, XS_WIDTH), jnp.uint32),
                        pltpu.SemaphoreType.DMA((2,)), pltpu.SemaphoreType.DMA((1,))],
    )
    return pl.pallas_call(
        functools.partial(_dispatch_kernel, lr=lr),
        grid_spec=grid_spec,
        out_shape=jax.ShapeDtypeStruct((n_rows, XS_WIDTH), jnp.uint32),
        compiler_params=_cparams(("arbitrary",)),
        name="moe_dispatch",
    )(plan["lo8"], plan["gb8"], plan["n8"], plan["tot8"], plan["ts8"], plan["tn8"], h2, rtt)


def _expert_kernel(be_ref, nu_ref, x_ref, wg_ref, wu_ref, wd_ref, y_ref, wgub, wdb):
    i = pl.program_id(0)

    @pl.when(i < nu_ref[0])
    def _():
        prev_e = be_ref[jnp.maximum(i - 1, 0)]

        @pl.when((i == 0) | (be_ref[i] != prev_e))
        def _():
            wgub[:, 0:D_EXPERT] = wg_ref[0, 0].astype(BF16)
            wgub[:, D_EXPERT:2 * D_EXPERT] = wu_ref[0, 0].astype(BF16)
            wdb[...] = wd_ref[0, 0].astype(BF16)

        x = _unpack_halves(x_ref[:, 0:XS_FEAT])
        w_row = pltpu.bitcast(x_ref[:, XS_FEAT:XS_FEAT + 1], F32)
        gu = jnp.dot(x, wgub[...], preferred_element_type=F32)
        hid = _silu(gu[:, 0:D_EXPERT]) * gu[:, D_EXPERT:2 * D_EXPERT]
        y = jnp.dot(hid.astype(BF16), wdb[...], preferred_element_type=F32)
        y_ref[...] = _pack_halves((y * w_row).astype(BF16).astype(F32))

    @pl.when(i >= nu_ref[0])
    def _():
        y_ref[...] = jnp.zeros(y_ref.shape, jnp.uint32)


def _experts(block_e, n_used, xs, wg, wu, wd, layer):
    n_rows = xs.shape[0]
    wspec = lambda shape: pl.BlockSpec((1, 1) + shape, lambda i, be, nu: (layer, be[i], 0, 0))
    grid_spec = pltpu.PrefetchScalarGridSpec(
        num_scalar_prefetch=2,
        grid=(n_rows // MOE_BLOCK,),
        in_specs=[
            pl.BlockSpec((MOE_BLOCK, XS_WIDTH), lambda i, be, nu: (i, 0)),
            wspec((D_MODEL, D_EXPERT)), wspec((D_MODEL, D_EXPERT)), wspec((D_EXPERT, D_MODEL)),
        ],
        out_specs=pl.BlockSpec((MOE_BLOCK, XS_FEAT), lambda i, be, nu: (i, 0)),
        scratch_shapes=[
            pltpu.VMEM((D_MODEL, 2 * D_EXPERT), BF16),
            pltpu.VMEM((D_EXPERT, D_MODEL), BF16),
        ],
    )
    return pl.pallas_call(
        _expert_kernel,
        grid_spec=grid_spec,
        out_shape=jax.ShapeDtypeStruct((n_rows, XS_FEAT), jnp.uint32),
        compiler_params=_cparams(("arbitrary",)),
        name="moe_experts",
    )(block_e, n_used, xs, wg, wu, wd)


def _combine_kernel(lo_ref, gb_ref, n8_ref, tot_ref, x_ref, rt_ref, gf_ref, fg_ref, ys_hbm,
                    o_ref, yloc, sem, *, lr, final):
    t = pl.program_id(0)
    nt = pl.num_programs(0)
    slot = t % 2
    tm = x_ref.shape[0]

    def fetch(tile, sl):
        def per_expert(e, c):
            idx = tile * N_EXPERTS + e
            dst0 = lo_ref[idx] * CHUNK
            src0 = gb_ref[idx] * CHUNK

            def per_chunk(k, c2):
                pltpu.make_async_copy(ys_hbm.at[pl.ds(pl.multiple_of(src0 + k * CHUNK, CHUNK), CHUNK), :],
                                      yloc.at[sl, pl.ds(pl.multiple_of(dst0 + k * CHUNK, CHUNK), CHUNK), :],
                                      sem.at[sl]).start()
                return c2
            lax.fori_loop(0, n8_ref[idx], per_chunk, 0)
            return c
        lax.fori_loop(0, N_EXPERTS, per_expert, 0)

    @pl.when(t == 0)
    def _():
        yloc[...] = jnp.zeros(yloc.shape, jnp.uint32)
        fetch(0, 0)

    @pl.when(t + 1 < nt)
    def _():
        fetch(t + 1, 1 - slot)

    _wait_chunks(tot_ref[t], lr // CHUNK, lambda size: pltpu.make_async_copy(
        ys_hbm.at[pl.ds(0, size), :], yloc.at[slot, pl.ds(0, size), :], sem.at[slot]))

    ysw = _unpack_halves(yloc[slot])
    rt = rt_ref[...]
    cols = _iota((tm, lr), 1).astype(F32)
    pick = jnp.where(cols == rt[:, RT_ROW1:RT_ROW1 + 1], 1.0,
                     jnp.where(cols == rt[:, RT_ROW2:RT_ROW2 + 1], 1.0, 0.0)).astype(BF16)
    x = x_ref[...] + gf_ref[0] * jnp.dot(pick, ysw, preferred_element_type=F32)
    if final:
        ms = jnp.mean(x * x, axis=-1, keepdims=True)
        x = x * lax.rsqrt(ms + EPS) * fg_ref[...]
    o_ref[...] = x


def _combine(plan, x2, ys, rt, gf, fg, seq, tm, final):
    m = x2.shape[0]
    per_b = seq // tm
    lr = _local_rows(tm)
    grid_spec = pltpu.PrefetchScalarGridSpec(
        num_scalar_prefetch=4,
        grid=(m // tm,),
        in_specs=[
            pl.BlockSpec((tm, D_MODEL), lambda i, *_: (i, 0)),
            pl.BlockSpec((tm, LANES), lambda i, *_: (i, 0)),
            pl.BlockSpec((1, 1, D_MODEL), lambda i, *_: (i // per_b, 0, 0)),
            pl.BlockSpec((1, D_MODEL), lambda i, *_: (0, 0)),
            pl.BlockSpec(memory_space=pl.ANY),
        ],
        out_specs=pl.BlockSpec((tm, D_MODEL), lambda i, *_: (i, 0)),
        scratch_shapes=[pltpu.VMEM((2, lr, XS_FEAT), jnp.uint32), pltpu.SemaphoreType.DMA((2,))],
    )
    return pl.pallas_call(
        functools.partial(_combine_kernel, lr=lr, final=final),
        grid_spec=grid_spec,
        out_shape=jax.ShapeDtypeStruct((m, D_MODEL), F32),
        compiler_params=_cparams(("arbitrary",)),
        name="moe_combine",
    )(plan["lo8"], plan["gb8"], plan["n8"], plan["tot8"], x2, rt, gf, fg, ys)


def _moe_plan(c, n_blocks):
    i32 = jnp.int32
    blk8 = MOE_BLOCK // CHUNK
    c8 = (c + CHUNK - 1) // CHUNK
    lo8 = jnp.cumsum(c8, axis=1) - c8
    per_e = jnp.sum(c8, axis=0)
    pad8 = (per_e + blk8 - 1) // blk8 * blk8
    end8 = jnp.cumsum(pad8)
    start8 = end8 - pad8
    gb8 = start8[None, :] + jnp.cumsum(c8, axis=0) - c8
    blk_start8 = jnp.arange(n_blocks, dtype=i32) * blk8
    block_e = jnp.minimum(jnp.sum(end8[None, :] <= blk_start8[:, None], axis=1), N_EXPERTS - 1)
    return {
        "lo8": lo8.reshape(-1).astype(i32), "gb8": gb8.reshape(-1).astype(i32),
        "n8": c8.reshape(-1).astype(i32), "tot8": jnp.sum(c8, axis=1).astype(i32),
        "ts8": jnp.concatenate([start8 + per_e, end8[-1:]]).astype(i32),
        "tn8": jnp.concatenate([pad8 - per_e, n_blocks * blk8 - end8[-1:]]).astype(i32),
        "block_e": block_e.astype(i32), "n_used": (end8[-1:] // blk8).astype(i32),
    }


def _w_in_sections(w):
    d_dt = SSD_WIDTH + XBC_WIDTH
    d_q = d_dt + SSD_HEADS
    d_f = d_q + 3 * ATTN_WIDTH
    d_ga = d_f + ATTN_HEADS
    small = jnp.concatenate([w[..., d_dt:d_q], w[..., d_f:d_ga],
                             jnp.zeros(w.shape[:-1] + (LANES - SSD_HEADS - ATTN_HEADS,), w.dtype)], axis=-1)
    sections = [w[..., SSD_WIDTH:d_dt], w[..., :SSD_WIDTH], w[..., d_q:d_f], w[..., d_ga:], small]
    return [sec.astype(BF16) for sec in sections]


def kernel(x, c, ada_w, ada_b, norm_mix_g, w_in, ssd_conv_w, ssd_conv_b, ssd_dt_bias, ssd_a_log,
           ssd_d, ssd_norm_g, fox_f_bias, fox_norm_g, cm_conv_w, cm_conv_b, cm_ln_g, cm_ln_b, w_out,
           norm_ffn_g, w_router_group, b_router_group, w_router_expert, b_router_expert, w_gate,
           w_up, w_down, final_norm_g):
    bsz, seq, d = x.shape
    m = bsz * seq
    tm = min(512, seq)
    tt = min(256, seq)
    n_blocks = -(-(2 * m + (CHUNK - 1) * (m // tm) * N_EXPERTS) // MOE_BLOCK) + N_EXPERTS

    mod = _modulation(c, ada_w, ada_b)
    x2 = x.reshape(m, d)
    w_in_sections = _w_in_sections(w_in)
    for l in range(DEPTH):
        sh_m, sc_m, g_m, sh_f, sc_f, g_f = [v.reshape(bsz, 1, d) for v in jnp.split(mod[l], 6, axis=-1)]
        a_m = norm_mix_g[l][None, None, :] * (1.0 + sc_m)
        proj = _inproj(x2, a_m, sh_m, w_in_sections, l, seq, tm)
        y_ssd = _ssd(proj, ssd_conv_w[l], ssd_conv_b[l], ssd_dt_bias[l], ssd_a_log[l],
                     ssd_d[l], ssd_norm_g[l], bsz, seq, tt)
        qa, ka, va, aux, stat = _foxprep(proj, fox_f_bias[l], bsz, seq)
        o_att = _fox(qa, ka, va, aux, stat)
        y_cnv = _conformer(proj, cm_conv_w[l], cm_conv_b[l], cm_ln_g[l], cm_ln_b[l], bsz, seq, tt)

        w_r = jnp.concatenate([w_router_group[l], w_router_expert[l],
                               jnp.zeros((d, LANES - N_GROUPS - N_EXPERTS), F32)], axis=1)
        w_rh = w_r.astype(BF16)
        w_r2 = jnp.concatenate([w_rh, (w_r - w_rh.astype(F32)).astype(BF16)], axis=1)
        b_r = jnp.concatenate([b_router_group[l], b_router_expert[l],
                               jnp.zeros((LANES - N_GROUPS - N_EXPERTS,), F32)]).reshape(1, LANES)
        a_f = norm_ffn_g[l][None, None, :] * (1.0 + sc_f)
        x2, h2, rt, rtt, cnt = _outproj(x2, y_ssd, o_att, y_cnv, w_out[l].astype(BF16),
                                        fox_norm_g[l].reshape(1, -1), g_m, a_f, sh_f, w_r2, b_r,
                                        seq, tm)
        plan = _moe_plan(cnt[:, 0, :N_EXPERTS].astype(jnp.int32), n_blocks)
        xs = _dispatch(plan, h2, rtt, n_blocks * MOE_BLOCK, tm)
        ys = _experts(plan["block_e"], plan["n_used"], xs, w_gate, w_up, w_down, l)
        x2 = _combine(plan, x2, ys, rt, g_f, final_norm_g.reshape(1, d), seq, tm,
                      final=(l == DEPTH - 1))
    return x2.reshape(bsz, seq, d)
```

```python
import functools

import jax
import jax.numpy as jnp
import numpy as np
from jax import lax
from jax.experimental import pallas as pl
from jax.experimental.pallas import tpu as pltpu

F32 = jnp.float32
BF16 = jnp.bfloat16
HIGHEST = lax.Precision.HIGHEST

D_MODEL = 1024
DEPTH = 4
SSD_WIDTH = 512
SSD_HEADS = 8
SSD_HEAD_DIM = 64
SSD_STATE = 128
SSD_CONV = 4
SSD_CHUNK = 128
XBC_WIDTH = 1024
ATTN_WIDTH = 256
ATTN_HEADS = 4
ATTN_HEAD_DIM = 64
CONV_WIDTH = 256
CONV_KERNEL = 31
N_GROUPS = 4
PER_GROUP = 8
N_EXPERTS = 32
D_EXPERT = 512
MOE_BLOCK = 512
EPS = 1e-6

LANES = 128
SUBLANES = 8
COL_XBC = 0
COL_Z = 1024
COL_QKV = 1536
COL_GA = 2304
COL_GB = 2560
COL_SM = 2816
NP = 2944
SM_DT = 0
SM_F = 8

VMEM_LIMIT = 56 * 1024 * 1024


def _cparams(sem):
    return pltpu.CompilerParams(dimension_semantics=sem, vmem_limit_bytes=VMEM_LIMIT)


def _sigmoid(x):
    return 1.0 / (1.0 + jnp.exp(-x))


def _silu(x):
    return x * _sigmoid(x)


def _softplus(x):
    return jnp.maximum(x, 0.0) + jnp.log1p(jnp.exp(-jnp.abs(x)))


def _iota(shape, dim):
    return lax.broadcasted_iota(jnp.int32, shape, dim)


def _split_bf16x3(x):
    hi = x.astype(BF16)
    r1 = x - hi.astype(F32)
    mid = r1.astype(BF16)
    lo = (r1 - mid.astype(F32)).astype(BF16)
    return hi, mid, lo


def _dot_onehot_rhs(x, sel):
    return sum(jnp.dot(part, sel, preferred_element_type=F32) for part in _split_bf16x3(x))


def _dot_onehot_lhs(sel, x):
    return sum(jnp.dot(sel, part, preferred_element_type=F32) for part in _split_bf16x3(x))


def _mod_kernel(c_ref, w_ref, b_ref, o_ref):
    cond = _silu(c_ref[...])
    o_ref[0] = jnp.dot(cond, w_ref[0], precision=HIGHEST, preferred_element_type=F32) + b_ref[0]


def _modulation(c, ada_w, ada_b):
    bsz = c.shape[0]
    rows = 8
    cpad = jnp.zeros((rows, D_MODEL), F32).at[:bsz].set(c)
    tn = 1536
    n6 = 6 * D_MODEL
    out = pl.pallas_call(
        _mod_kernel,
        grid=(DEPTH, n6 // tn),
        in_specs=[
            pl.BlockSpec((rows, D_MODEL), lambda l, j: (0, 0)),
            pl.BlockSpec((1, D_MODEL, tn), lambda l, j: (l, 0, j)),
            pl.BlockSpec((1, 1, tn), lambda l, j: (l, 0, j)),
        ],
        out_specs=pl.BlockSpec((1, rows, tn), lambda l, j: (l, 0, j)),
        out_shape=jax.ShapeDtypeStruct((DEPTH, rows, n6), F32),
        compiler_params=_cparams(("arbitrary", "arbitrary")),
        name="adaln_mod",
    )(cpad, ada_w, ada_b.reshape(DEPTH, 1, n6))
    return out[:, :bsz]


def _inproj_kernel(x_ref, a_ref, s_ref, *refs):
    w_refs, o_ref = refs[:-1], refs[-1]
    x = x_ref[...]
    ms = jnp.mean(x * x, axis=-1, keepdims=True)
    h = (x * lax.rsqrt(ms + EPS) * a_ref[0] + s_ref[0]).astype(BF16)
    col = 0
    for w_ref in w_refs:
        width = w_ref.shape[-1]
        o_ref[:, col:col + width] = jnp.dot(h, w_ref[0], preferred_element_type=F32)
        col += width


def _inproj(x2, a, s, w_sections, layer, seq, tm):
    m = x2.shape[0]
    per_b = seq // tm
    assert sum(w.shape[-1] for w in w_sections) == NP
    return pl.pallas_call(
        _inproj_kernel,
        grid=(m // tm,),
        in_specs=[
            pl.BlockSpec((tm, D_MODEL), lambda i: (i, 0)),
            pl.BlockSpec((1, 1, D_MODEL), lambda i: (i // per_b, 0, 0)),
            pl.BlockSpec((1, 1, D_MODEL), lambda i: (i // per_b, 0, 0)),
        ] + [pl.BlockSpec((1, D_MODEL, w.shape[-1]), lambda i: (layer, 0, 0)) for w in w_sections],
        out_specs=pl.BlockSpec((tm, NP), lambda i: (i, 0)),
        out_shape=jax.ShapeDtypeStruct((m, NP), F32),
        compiler_params=_cparams(("arbitrary",)),
        name="in_proj",
    )(x2, a, s, *w_sections)


def _ssd_kernel(z_ref, xbc_ref, sm_ref, cw_ref, cb_ref, dtb_ref, alog_ref, dtbt_ref,
                alogt_ref, e_ref, dx_ref, ng_ref, y_ref, xpad, xc, prev, *, tt):
    t = pl.program_id(1)

    @pl.when(t == 0)
    def _():
        xpad[0:8, :] = jnp.zeros((8, XBC_WIDTH), F32)
        prev[...] = jnp.zeros(prev.shape, F32)

    xpad[8:8 + tt, :] = xbc_ref[...]
    acc = jnp.broadcast_to(cb_ref[...], (tt, XBC_WIDTH))
    for k in range(SSD_CONV):
        off = 8 - (SSD_CONV - 1) + k
        acc = acc + cw_ref[k:k + 1, :] * xpad[off:off + tt, :]
    xc[...] = _silu(acc)
    xpad[0:8, :] = xpad[tt:tt + 8, :]

    cl = SSD_CHUNK
    row = _iota((cl, cl), 0)
    col = _iota((cl, cl), 1)
    causal = row >= col
    tril = jnp.where(causal, 1.0, 0.0).astype(BF16)
    triu = jnp.where(row <= col, 1.0, 0.0).astype(BF16)
    lo = col < SSD_HEAD_DIM
    lane1 = _iota((1, LANES), 1)
    a_row = jnp.where(lane1 < SSD_HEADS, -jnp.exp(alog_ref[...]), 0.0)
    a_col = -jnp.exp(alogt_ref[...])
    expand = e_ref[...]

    def chunk(c, carry):
        r0 = c * cl
        xs = xc[pl.ds(r0, cl), 0:SSD_WIDTH]
        bmat = xc[pl.ds(r0, cl), SSD_WIDTH:SSD_WIDTH + 2 * SSD_STATE]
        cmat = xc[pl.ds(r0, cl), SSD_WIDTH + 2 * SSD_STATE:XBC_WIDTH]
        sm = sm_ref[pl.ds(r0, cl), :]
        dt = _softplus(sm + dtb_ref[...])
        da = dt * a_row
        acs = _dot_onehot_lhs(tril, da)
        dtt = _softplus(sm.T[0:SSD_HEADS, :] + dtbt_ref[...])
        acst = _dot_onehot_rhs(dtt * a_col, triu)
        dt_x = _dot_onehot_rhs(dt, expand)
        acs_x = _dot_onehot_rhs(acs, expand)
        last = acs_x[cl - 1:cl, :]
        eacs_x = jnp.exp(acs_x)
        dte_x = jnp.exp(last - acs_x)
        cd_x = jnp.exp(last)
        xdt = xs * dt_x
        xdte = (xdt * dte_x).astype(BF16)
        zc = z_ref[pl.ds(r0, cl), :]
        for g in range(2):
            bg = bmat[:, g * SSD_STATE:(g + 1) * SSD_STATE]
            cg = cmat[:, g * SSD_STATE:(g + 1) * SSD_STATE].astype(BF16)
            bgt = bg.T.astype(BF16)
            cbm = jnp.dot(cg, bgt, preferred_element_type=F32)
            pair_out = []
            for j in range(2):
                p = 2 * g + j
                sl = slice(p * LANES, (p + 1) * LANES)
                xp = xdt[:, sl]
                yd = jnp.zeros((cl, LANES), F32)
                for half in range(2):
                    h = 2 * p + half
                    seg = acs[:, h:h + 1] - acst[h:h + 1, :]
                    dec = jnp.exp(jnp.where(causal, seg, -jnp.inf))
                    gm = (cbm * dec).astype(BF16)
                    own = lo if half == 0 else jnp.logical_not(lo)
                    xm = jnp.where(own, xp, 0.0).astype(BF16)
                    yd = yd + jnp.dot(gm, xm, preferred_element_type=F32)
                prev_p = prev[:, sl]
                yo = jnp.dot(cg, prev_p.astype(BF16), preferred_element_type=F32) * eacs_x[:, sl]
                st = jnp.dot(bgt, xdte[:, sl], preferred_element_type=F32)
                prev[:, sl] = prev_p * cd_x[:, sl] + st
                pair_out.append(yd + yo + xs[:, sl] * dx_ref[:, sl])
            gs = slice(g * 2 * LANES, (g + 1) * 2 * LANES)
            yg = jnp.concatenate(pair_out, axis=-1) * _silu(zc[:, gs])
            ms = jnp.mean(yg * yg, axis=-1, keepdims=True)
            y_ref[pl.ds(r0, cl), gs] = yg * lax.rsqrt(ms + EPS) * ng_ref[:, gs]
        return carry

    for c in range(tt // cl):
        chunk(c, 0)


def _ssd(proj, cw, cb, dtb, alog, dx, ng, bsz, seq, tt):
    m = bsz * seq
    nt = seq // tt
    pad = LANES - SSD_HEADS
    dtb_row = jnp.pad(dtb, (0, pad)).reshape(1, LANES)
    alog_row = jnp.pad(alog, (0, pad)).reshape(1, LANES)
    dtb_col = jnp.broadcast_to(dtb[:, None], (SSD_HEADS, SSD_CHUNK))
    alog_col = jnp.broadcast_to(alog[:, None], (SSD_HEADS, SSD_CHUNK))
    expand = np.zeros((LANES, SSD_WIDTH), np.float32)
    for h in range(SSD_HEADS):
        expand[h, h * SSD_HEAD_DIM:(h + 1) * SSD_HEAD_DIM] = 1.0
    dx_row = jnp.repeat(dx, SSD_HEAD_DIM).reshape(1, SSD_WIDTH)
    full = lambda shape: pl.BlockSpec(shape, lambda b, t: (0,) * len(shape))
    return pl.pallas_call(
        functools.partial(_ssd_kernel, tt=tt),
        grid=(bsz, nt),
        in_specs=[
            pl.BlockSpec((tt, SSD_WIDTH), lambda b, t: (b * nt + t, COL_Z // SSD_WIDTH)),
            pl.BlockSpec((tt, XBC_WIDTH), lambda b, t: (b * nt + t, COL_XBC // XBC_WIDTH + 0)),
            pl.BlockSpec((tt, LANES), lambda b, t: (b * nt + t, COL_SM // LANES)),
            full((SSD_CONV, XBC_WIDTH)),
            full((1, XBC_WIDTH)),
            full((1, LANES)),
            full((1, LANES)),
            full((SSD_HEADS, SSD_CHUNK)),
            full((SSD_HEADS, SSD_CHUNK)),
            full((LANES, SSD_WIDTH)),
            full((1, SSD_WIDTH)),
            full((1, SSD_WIDTH)),
        ],
        out_specs=pl.BlockSpec((tt, SSD_WIDTH), lambda b, t: (b * nt + t, 0)),
        out_shape=jax.ShapeDtypeStruct((m, SSD_WIDTH), F32),
        scratch_shapes=[
            pltpu.VMEM((tt + 8, XBC_WIDTH), F32),
            pltpu.VMEM((tt, XBC_WIDTH), F32),
            pltpu.VMEM((SSD_STATE, SSD_WIDTH), F32),
        ],
        compiler_params=_cparams(("arbitrary", "arbitrary")),
        name="ssd_scan",
    )(proj, proj, proj, cw, cb.reshape(1, XBC_WIDTH), dtb_row, alog_row, dtb_col, alog_col,
      jnp.asarray(expand, dtype=BF16), dx_row, ng.reshape(1, SSD_WIDTH))


LOG2E = 1.4426950408889634
ATT_BLOCK = 256
ATT_TQ = 512
PRUNE_LOG2 = 140.0
FAST_GAP_LOG2 = 60.0
SHIFT_MARGIN_LOG2 = 1.0
AUX_CUMEND = 0
AUX_KMAX = 1
ST_QNORM, ST_CUM, ST_SELF = 0, 1, 2


def _foxprep_kernel(qkv_ref, sm_ref, fb_ref, q_out, k_out, v_out, aux_out, stat_out, carry, *, tt):
    t = pl.program_id(1)

    @pl.when(t == 0)
    def _():
        carry[...] = jnp.zeros(carry.shape, F32)
        aux_out[...] = jnp.zeros(aux_out.shape, F32)

    logit = sm_ref[...] + fb_ref[...]
    logf = -_softplus(-logit)
    row = _iota((tt, tt), 0)
    col = _iota((tt, tt), 1)
    tril = jnp.where(row >= col, 1.0, 0.0).astype(BF16)
    cum = _dot_onehot_lhs(tril, logf) + carry[...]
    carry[...] = cum[tt - 1:tt, :]

    lane = _iota((tt, LANES), 1)
    lane1 = _iota((1, LANES), 1)
    scale = ATTN_HEAD_DIM ** -0.5 * LOG2E
    for h in range(ATTN_HEADS):
        pair, half = h // 2, h % 2
        own = (lane < ATTN_HEAD_DIM) if half == 0 else (lane >= ATTN_HEAD_DIM)
        a0 = ATTN_HEAD_DIM * (1 - half)
        cs = jnp.broadcast_to(cum[:, SM_F + h:SM_F + h + 1], (tt, LANES)) * LOG2E
        hi = cs.astype(BF16).astype(F32)
        r1 = cs - hi
        mid = r1.astype(BF16).astype(F32)
        low = r1 - mid
        qp = qkv_ref[:, pair * LANES:(pair + 1) * LANES]
        kp = qkv_ref[:, ATTN_WIDTH + pair * LANES:ATTN_WIDTH + (pair + 1) * LANES]
        vp = qkv_ref[:, 2 * ATTN_WIDTH + pair * LANES:2 * ATTN_WIDTH + (pair + 1) * LANES]
        qa = jnp.where(lane == a0, hi, jnp.where(lane == a0 + 1, mid, jnp.where(
            lane == a0 + 2, low, jnp.where((lane >= a0 + 3) & (lane < a0 + 6), 1.0, 0.0))))
        ka = jnp.where(lane == a0 + 3, -hi, jnp.where(lane == a0 + 4, -mid, jnp.where(
            lane == a0 + 5, -low, jnp.where((lane >= a0) & (lane < a0 + 9), 1.0, 0.0))))
        qb = jnp.where(own, qp * scale, qa).astype(BF16)
        q_out[0, h] = qb
        kb = jnp.where(own, kp, ka).astype(BF16)
        k_out[0, h] = kb
        qf = jnp.where(own, qb.astype(F32), 0.0)
        qn = jnp.sqrt(jnp.sum(qf * qf, axis=-1, keepdims=True))
        s_self = jnp.sum(qf * kb.astype(F32), axis=-1, keepdims=True)
        stat = jnp.where(lane == ST_QNORM, qn, jnp.where(lane == ST_CUM, hi + mid + low, jnp.where(
            lane == ST_SELF, s_self, 0.0)))
        stat_out[0, h] = stat.T[0:8, :]
        v_out[0, h] = jnp.where(own, vp, jnp.where(lane == a0, 1.0, 0.0)).astype(BF16)
        kf = jnp.where(own, kb.astype(F32), 0.0)
        kn2 = jnp.max(jnp.sum(kf * kf, axis=-1, keepdims=True), axis=0, keepdims=True)
        here = lane1 == t
        aux_out[0, h, AUX_CUMEND:AUX_CUMEND + 1, :] = jnp.where(
            here, cs[tt - 1:tt, :], aux_out[0, h, AUX_CUMEND:AUX_CUMEND + 1, :])
        aux_out[0, h, AUX_KMAX:AUX_KMAX + 1, :] = jnp.where(
            here, jnp.sqrt(kn2), aux_out[0, h, AUX_KMAX:AUX_KMAX + 1, :])


def _foxprep(proj, fb, bsz, seq):
    tt = ATT_BLOCK
    nt = seq // tt
    assert nt <= LANES
    fb_row = jnp.zeros((1, LANES), F32).at[0, SM_F:SM_F + ATTN_HEADS].set(fb)
    shp = jax.ShapeDtypeStruct((bsz, ATTN_HEADS, seq, LANES), BF16)
    ospec = pl.BlockSpec((1, ATTN_HEADS, tt, LANES), lambda b, t: (b, 0, t, 0))
    return pl.pallas_call(
        functools.partial(_foxprep_kernel, tt=tt),
        grid=(bsz, nt),
        in_specs=[
            pl.BlockSpec((tt, 3 * ATTN_WIDTH), lambda b, t: (b * nt + t, COL_QKV // (3 * ATTN_WIDTH))),
            pl.BlockSpec((tt, LANES), lambda b, t: (b * nt + t, COL_SM // LANES)),
            pl.BlockSpec((1, LANES), lambda b, t: (0, 0)),
        ],
        out_specs=[ospec, ospec, ospec,
                   pl.BlockSpec((1, ATTN_HEADS, 8, LANES), lambda b, t: (b, 0, 0, 0)),
                   pl.BlockSpec((1, ATTN_HEADS, 8, tt), lambda b, t: (b, 0, 0, t))],
        out_shape=[shp, shp, shp, jax.ShapeDtypeStruct((bsz, ATTN_HEADS, 8, LANES), F32),
                   jax.ShapeDtypeStruct((bsz, ATTN_HEADS, 8, seq), F32)],
        scratch_shapes=[pltpu.VMEM((1, LANES), F32)],
        compiler_params=_cparams(("arbitrary", "arbitrary")),
        name="fox_prep",
    )(proj, proj, fb_row)


def _fox_kernel(q_ref, k_ref, v_ref, aux_ref, stat_ref, o_ref, *, tq, kb):
    i = pl.program_id(2)
    ib = i * (tq // kb)
    nt = (((1,), (1,)), ((), ()))
    lane = _iota((tq, LANES), 1)
    lane1 = _iota((1, LANES), 1)
    qs = (q_ref[0, 0], q_ref[0, 1])

    def step(hh, r0, width, m, acc, mask):
        kblk = k_ref[0, hh, pl.ds(r0, width), :]
        vblk = v_ref[0, hh, pl.ds(r0, width), :]
        s = lax.dot_general(qs[hh], kblk, nt, preferred_element_type=F32)
        if mask is not None:
            s = jnp.where(mask, s, -jnp.inf)
        m_new = jnp.maximum(m, jnp.max(s, axis=-1, keepdims=True))
        p = jnp.exp2(s - m_new)
        alpha = jnp.exp2(m - m_new)
        acc = acc * alpha + jnp.dot(p.astype(BF16), vblk, preferred_element_type=F32)
        return m_new, acc

    d0 = pl.multiple_of(i * tq, tq)
    diag = _iota((tq, tq), 0) >= _iota((tq, tq), 1)
    ib_f = ib.astype(F32)
    first = []
    gaps = []
    q_shift = []
    for hh in range(2):
        a0 = ATTN_HEAD_DIM * (1 - hh)
        st = stat_ref[0, hh]
        qn = st[ST_QNORM:ST_QNORM + 1, :]
        cum_t = st[ST_CUM:ST_CUM + 1, :]
        s_self = st[ST_SELF:ST_SELF + 1, :]
        aux = aux_ref[0, hh]
        cend = aux[AUX_CUMEND:AUX_CUMEND + 1, :]
        kmax = jnp.max(jnp.where(lane1 < ib + tq // kb, aux[AUX_KMAX:AUX_KMAX + 1, :], 0.0), axis=-1,
                       keepdims=True)
        bound = qn * kmax
        slack = jnp.max(bound + cum_t - s_self, axis=-1, keepdims=True)
        live = (lane1 < ib) & (slack - cend > -PRUNE_LOG2)
        first.append(jnp.min(jnp.where(live, lane1.astype(F32), ib_f)))
        parts = _split_bf16x3(bound + SHIFT_MARGIN_LOG2)
        gaps.append(jnp.max(sum(part.astype(F32) for part in parts) - s_self))
        terms = jnp.concatenate([-part.astype(F32) for part in parts] + [jnp.zeros((13, tq), F32)],
                                axis=0).astype(BF16)
        place = jnp.where(_iota((16, LANES), 1) == _iota((16, LANES), 0) + (a0 + 6), 1.0,
                          0.0).astype(BF16)
        q_shift.append(qs[hh] + lax.dot_general(terms, place, (((0,), (0,)), ((), ())),
                                                preferred_element_type=F32).astype(BF16))
    j_first = jnp.minimum(first[0], first[1]).astype(jnp.int32)

    def fast(_):
        def blocks(c, r0, width, mask=None):
            out = []
            for hh in range(2):
                s = lax.dot_general(q_shift[hh], k_ref[0, hh, pl.ds(r0, width), :], nt,
                                    preferred_element_type=F32)
                if mask is not None:
                    s = jnp.where(mask, s, -jnp.inf)
                out.append(c[hh] + jnp.dot(jnp.exp2(s).astype(BF16), v_ref[0, hh, pl.ds(r0, width), :],
                                           preferred_element_type=F32))
            return tuple(out)

        zero = jnp.zeros((tq, LANES), F32)
        c = blocks((zero, zero), d0, tq, diag)
        j_lo = j_first - ((ib - j_first) & 1)
        n = ib - j_lo
        c = lax.cond((n & 2) == 2, lambda c: blocks(c, pl.multiple_of(j_lo * kb, kb), 2 * kb),
                     lambda c: c, c)
        j0 = j_lo + (n & 2)

        def body(jj, c):
            return blocks(c, pl.multiple_of((j0 + 4 * jj) * kb, kb), 4 * kb)
        return lax.fori_loop(0, lax.shift_right_logical(n, 2), body, c)

    def online(_):
        m_init = jnp.full((tq, 1), -jnp.inf, F32)
        acc_init = jnp.zeros((tq, LANES), F32)
        carry = step(0, d0, tq, m_init, acc_init, diag) + step(1, d0, tq, m_init, acc_init, diag)

        def body(j, c):
            r0 = pl.multiple_of(j * kb, kb)
            m0, a0_, m1, a1_ = c
            m0, a0_ = step(0, r0, kb, m0, a0_, None)
            m1, a1_ = step(1, r0, kb, m1, a1_, None)
            return m0, a0_, m1, a1_
        _, a0_, _, a1_ = lax.fori_loop(j_first, ib, body, carry)
        return a0_, a1_

    acc0, acc1 = lax.cond(jnp.maximum(gaps[0], gaps[1]) <= FAST_GAP_LOG2, fast, online, None)
    den0 = jnp.sum(jnp.where(lane == ATTN_HEAD_DIM, acc0, 0.0), axis=-1, keepdims=True)
    den1 = jnp.sum(jnp.where(lane == 0, acc1, 0.0), axis=-1, keepdims=True)
    o_ref[...] = jnp.where(lane < ATTN_HEAD_DIM, acc0 / den0, acc1 / den1)


def _fox(qa, ka, va, aux, stat):
    bsz, nh, seq, _ = qa.shape
    tq = min(ATT_TQ, seq)
    assert (tq // ATT_BLOCK) % 2 == 0
    nq = seq // tq
    kv_spec = pl.BlockSpec((1, 2, seq, LANES), lambda b, p, i: (b, p, 0, 0))
    return pl.pallas_call(
        functools.partial(_fox_kernel, tq=tq, kb=ATT_BLOCK),
        grid=(bsz, nh // 2, nq),
        in_specs=[pl.BlockSpec((1, 2, tq, LANES), lambda b, p, i: (b, p, i, 0)), kv_spec, kv_spec,
                  pl.BlockSpec((1, 2, 8, LANES), lambda b, p, i: (b, p, 0, 0)),
                  pl.BlockSpec((1, 2, 8, tq), lambda b, p, i: (b, p, 0, i))],
        out_specs=pl.BlockSpec((tq, LANES), lambda b, p, i: (b * nq + i, p)),
        out_shape=jax.ShapeDtypeStruct((bsz * seq, ATTN_WIDTH), F32),
        compiler_params=_cparams(("arbitrary", "arbitrary", "arbitrary")),
        name="fox_attn",
    )(qa, ka, va, aux, stat)


CONF_HALO = 32
CONF_ROWS = 64


def _conf_kernel(ga_ref, gb_ref, w_ref, b_ref, lg_ref, lb_ref, y_ref, upad, ush, *, tt):
    t = pl.program_id(1)

    @pl.when(t == 0)
    def _():
        upad[0:CONF_HALO, :] = jnp.zeros((CONF_HALO, CONV_WIDTH), F32)

    upad[CONF_HALO:CONF_HALO + tt, :] = ga_ref[...] * _sigmoid(gb_ref[...])
    base = CONF_HALO - (CONV_KERNEL - 1)
    for ph in range(1, SUBLANES):
        ush[ph - 1] = upad[ph:ph + ush.shape[1], :]
    for r in range(tt // CONF_ROWS):
        r0 = r * CONF_ROWS
        acc = jnp.broadcast_to(b_ref[...], (CONF_ROWS, CONV_WIDTH))
        for k in range(CONV_KERNEL):
            ph = (base + k) % SUBLANES
            lo = r0 + base + k - ph
            src = upad[lo:lo + CONF_ROWS, :] if ph == 0 else ush[ph - 1, lo:lo + CONF_ROWS, :]
            acc = acc + w_ref[k:k + 1, :] * src
        mu = jnp.mean(acc, axis=-1, keepdims=True)
        cen = acc - mu
        var = jnp.mean(cen * cen, axis=-1, keepdims=True)
        y = cen * lax.rsqrt(var + EPS) * lg_ref[...] + lb_ref[...]
        y_ref[r0:r0 + CONF_ROWS, :] = _silu(y)
    upad[0:CONF_HALO, :] = upad[tt:tt + CONF_HALO, :]


def _conformer(proj, w, b, lg, lb, bsz, seq, tt):
    m = bsz * seq
    nt = seq // tt
    full = lambda shape: pl.BlockSpec(shape, lambda bb, t: (0,) * len(shape))
    return pl.pallas_call(
        functools.partial(_conf_kernel, tt=tt),
        grid=(bsz, nt),
        in_specs=[
            pl.BlockSpec((tt, CONV_WIDTH), lambda bb, t: (bb * nt + t, COL_GA // CONV_WIDTH)),
            pl.BlockSpec((tt, CONV_WIDTH), lambda bb, t: (bb * nt + t, COL_GB // CONV_WIDTH)),
            full((CONV_KERNEL, CONV_WIDTH)),
            full((1, CONV_WIDTH)),
            full((1, CONV_WIDTH)),
            full((1, CONV_WIDTH)),
        ],
        out_specs=pl.BlockSpec((tt, CONV_WIDTH), lambda bb, t: (bb * nt + t, 0)),
        out_shape=jax.ShapeDtypeStruct((m, CONV_WIDTH), F32),
        scratch_shapes=[pltpu.VMEM((tt + CONF_HALO, CONV_WIDTH), F32),
                        pltpu.VMEM((SUBLANES - 1, tt + CONF_HALO - SUBLANES, CONV_WIDTH), F32)],
        compiler_params=_cparams(("arbitrary", "arbitrary")),
        name="conformer_conv",
    )(proj, proj, w, b.reshape(1, -1), lg.reshape(1, -1), lb.reshape(1, -1))


ROUTE_BIG = 1e9
CHUNK = 8
XS_FEAT = D_MODEL // 2
XS_WIDTH = XS_FEAT + LANES
RT_W1, RT_W2, RT_ROW1, RT_ROW2 = 0, 1, 2, 3


def _split3(x):
    hi = x.astype(BF16)
    lo = (x - hi.astype(F32)).astype(BF16)
    return hi, lo


def _outproj_kernel(x_ref, ys_ref, oa_ref, yc_ref, wo_ref, fg_ref, gm_ref, a2_ref, s2_ref,
                    wr_ref, br_ref, xn_ref, h2_ref, rt_ref, rtt_ref, cnt_ref):
    tm = x_ref.shape[0]
    lane = _iota((tm, LANES), 1)
    att = oa_ref[...]
    ms = jnp.mean(att * att, axis=-1, keepdims=True)
    att = att * lax.rsqrt(ms + EPS) * fg_ref[...]
    y = jnp.dot(ys_ref[...].astype(BF16), wo_ref[0:SSD_WIDTH, :], preferred_element_type=F32)
    y = y + jnp.dot(att.astype(BF16), wo_ref[SSD_WIDTH:SSD_WIDTH + ATTN_WIDTH, :],
                    preferred_element_type=F32)
    y = y + jnp.dot(yc_ref[...].astype(BF16), wo_ref[SSD_WIDTH + ATTN_WIDTH:, :],
                    preferred_element_type=F32)
    xn = x_ref[...] + gm_ref[0] * y
    xn_ref[...] = xn
    ms2 = jnp.mean(xn * xn, axis=-1, keepdims=True)
    h2 = xn * lax.rsqrt(ms2 + EPS) * a2_ref[0] + s2_ref[0]
    h2_ref[...] = h2.astype(BF16)

    hh, hl = _split3(h2)
    part = jnp.dot(jnp.concatenate([hh, hl], axis=0), wr_ref[...], preferred_element_type=F32)
    logits = (part[:tm, :LANES] + part[:tm, LANES:] + part[tm:, :LANES] + part[tm:, LANES:]) + br_ref[...]
    lanef = lane.astype(F32)
    lg = jnp.where(lane < N_GROUPS, logits, -jnp.inf)
    gmax = jnp.max(lg, axis=-1, keepdims=True)
    gsum = jnp.sum(jnp.exp(lg - gmax), axis=-1, keepdims=True)
    gidx = jnp.min(jnp.where(lg == gmax, lanef, ROUTE_BIG), axis=-1, keepdims=True)
    e_lo = N_GROUPS + PER_GROUP * gidx
    le = jnp.where((lanef >= e_lo) & (lanef < e_lo + PER_GROUP), logits, -jnp.inf)
    m1 = jnp.max(le, axis=-1, keepdims=True)
    i1 = jnp.min(jnp.where(le == m1, lanef, ROUTE_BIG), axis=-1, keepdims=True)
    le2 = jnp.where(lanef == i1, -jnp.inf, le)
    m2 = jnp.max(le2, axis=-1, keepdims=True)
    i2 = jnp.min(jnp.where(le2 == m2, lanef, ROUTE_BIG), axis=-1, keepdims=True)
    esum = jnp.sum(jnp.exp(le - m1), axis=-1, keepdims=True)
    p1 = 1.0 / esum
    p2 = jnp.exp(m2 - m1) / esum
    psel = 1.0 / gsum
    w1 = p1 / (p1 + p2) * psel
    w2 = p2 / (p1 + p2) * psel

    oh1 = (lanef == (i1 - N_GROUPS)).astype(F32)
    oh2 = (lanef == (i2 - N_GROUPS)).astype(F32)
    oh = oh1 + oh2
    cnt = jnp.sum(oh, axis=0, keepdims=True)
    before = (_iota((tm, tm), 0) > _iota((tm, tm), 1)).astype(BF16)
    rank = jnp.dot(before, oh.astype(BF16), preferred_element_type=F32)
    chunks = jnp.floor((cnt + (CHUNK - 1)) * (1.0 / CHUNK))
    below = (_iota((LANES, LANES), 0) < _iota((LANES, LANES), 1)).astype(BF16)
    seg_lo = CHUNK * jnp.dot(jnp.broadcast_to(chunks, (8, LANES)).astype(BF16), below,
                             preferred_element_type=F32)[0:1, :]
    lr1 = jnp.sum(oh1 * (rank + seg_lo), axis=-1, keepdims=True)
    lr2 = jnp.sum(oh2 * (rank + seg_lo), axis=-1, keepdims=True)
    route = jnp.where(lane == RT_W1, w1, jnp.where(lane == RT_W2, w2, jnp.where(
        lane == RT_ROW1, lr1, jnp.where(lane == RT_ROW2, lr2, 0.0))))
    rt_ref[...] = route
    rtt_ref[0] = route.T[0:8, :]
    cnt_ref[0] = jnp.broadcast_to(cnt, (8, LANES))


def _outproj(x2, ys, oa, yc, wo, fg, gm, a2, s2, wr, br, seq, tm):
    m = x2.shape[0]
    per_b = seq // tm
    nt = m // tm
    row = lambda w: pl.BlockSpec((tm, w), lambda i: (i, 0))
    bvec = pl.BlockSpec((1, 1, D_MODEL), lambda i: (i // per_b, 0, 0))
    full = lambda shape: pl.BlockSpec(shape, lambda i: (0,) * len(shape))
    return pl.pallas_call(
        _outproj_kernel,
        grid=(nt,),
        in_specs=[
            row(D_MODEL), row(SSD_WIDTH), row(ATTN_WIDTH), row(CONV_WIDTH),
            full((D_MODEL, D_MODEL)), full((1, ATTN_WIDTH)), bvec, bvec, bvec,
            full((D_MODEL, 2 * LANES)), full((1, LANES)),
        ],
        out_specs=[row(D_MODEL), row(D_MODEL), row(LANES),
                   pl.BlockSpec((1, 8, tm), lambda i: (i, 0, 0)),
                   pl.BlockSpec((1, 8, LANES), lambda i: (i, 0, 0))],
        out_shape=[jax.ShapeDtypeStruct((m, D_MODEL), F32), jax.ShapeDtypeStruct((m, D_MODEL), BF16),
                   jax.ShapeDtypeStruct((m, LANES), F32), jax.ShapeDtypeStruct((nt, 8, tm), F32),
                   jax.ShapeDtypeStruct((nt, 8, LANES), F32)],
        compiler_params=_cparams(("arbitrary",)),
        name="out_proj_router",
    )(x2, ys, oa, yc, wo, fg, gm, a2, s2, wr, br)


def _local_rows(tm):
    return -(-(2 * tm + (CHUNK - 1) * N_EXPERTS) // LANES) * LANES


def _pack_halves(x):
    w = x.shape[-1] // 2
    return pltpu.bitcast(x[:, :w], jnp.uint32) | lax.shift_right_logical(
        pltpu.bitcast(x[:, w:], jnp.uint32), jnp.uint32(16))


def _unpack_halves(p):
    hi = pltpu.bitcast(p & jnp.uint32(0xFFFF0000), F32)
    lo = pltpu.bitcast(lax.shift_left(p, jnp.uint32(16)), F32)
    return jnp.concatenate([hi, lo], axis=-1).astype(BF16)


def _wait_chunks(n, n_max, copy_of_size):
    for bit in range(n_max.bit_length()):
        @pl.when((lax.shift_right_logical(n, bit) & 1) == 1)
        def _():
            copy_of_size(CHUNK << bit).wait()


def _dispatch_kernel(lo_ref, gb_ref, n8_ref, tot_ref, ts_ref, tn_ref, h_ref, rtt_ref, xs_hbm,
                     xloc, zeros, sem, zsem, *, lr):
    t = pl.program_id(0)
    nt = pl.num_programs(0)
    slot = t % 2
    tm = h_ref.shape[0]

    def seg_copy(sl, src, dst):
        return pltpu.make_async_copy(xloc.at[sl, pl.ds(src, CHUNK), :], xs_hbm.at[pl.ds(dst, CHUNK), :],
                                     sem.at[sl])

    def wait_tile(tile, sl):
        _wait_chunks(tot_ref[tile], lr // CHUNK, lambda size: pltpu.make_async_copy(
            xloc.at[sl, pl.ds(0, size), :], xs_hbm.at[pl.ds(0, size), :], sem.at[sl]))

    @pl.when(t >= 2)
    def _():
        wait_tile(t - 2, slot)

    rows = _iota((lr, tm), 0).astype(F32)
    is1 = rows == rtt_ref[0, RT_ROW1:RT_ROW1 + 1, :]
    is2 = rows == rtt_ref[0, RT_ROW2:RT_ROW2 + 1, :]
    perm = jnp.where(is1, 1.0, jnp.where(is2, 1.0, 0.0))
    xloc[slot, :, 0:XS_FEAT] = _pack_halves(
        jnp.dot(perm.astype(BF16), h_ref[...], preferred_element_type=F32))
    w_row = jnp.sum(jnp.where(is1, rtt_ref[0, RT_W1:RT_W1 + 1, :],
                              jnp.where(is2, rtt_ref[0, RT_W2:RT_W2 + 1, :], 0.0)), axis=-1, keepdims=True)
    xloc[slot, :, XS_FEAT:XS_WIDTH] = jnp.where(
        _iota((lr, LANES), 1) == 0, pltpu.bitcast(jnp.broadcast_to(w_row, (lr, LANES)), jnp.uint32),
        jnp.uint32(0))

    def per_expert(e, c):
        idx = t * N_EXPERTS + e
        src0 = lo_ref[idx] * CHUNK
        dst0 = gb_ref[idx] * CHUNK

        def per_chunk(k, c2):
            seg_copy(slot, pl.multiple_of(src0 + k * CHUNK, CHUNK),
                     pl.multiple_of(dst0 + k * CHUNK, CHUNK)).start()
            return c2
        lax.fori_loop(0, n8_ref[idx], per_chunk, 0)
        return c
    lax.fori_loop(0, N_EXPERTS, per_expert, 0)

    @pl.when(t == nt - 1)
    def _():
        zeros[...] = jnp.zeros(zeros.shape, jnp.uint32)

        def zero_copy(dst):
            return pltpu.make_async_copy(zeros, xs_hbm.at[pl.ds(dst, CHUNK), :], zsem.at[0])

        def fill(e, c):
            dst0 = ts_ref[e] * CHUNK

            def one(k, c2):
                zero_copy(pl.multiple_of(dst0 + k * CHUNK, CHUNK)).start()
                return c2
            lax.fori_loop(0, tn_ref[e], one, 0)
            return c
        lax.fori_loop(0, N_EXPERTS + 1, fill, 0)

        def drain(e, c):
            def one(k, c2):
                zero_copy(0).wait()
                return c2
            lax.fori_loop(0, tn_ref[e], one, 0)
            return c
        lax.fori_loop(0, N_EXPERTS + 1, drain, 0)

        @pl.when(t >= 1)
        def _():
            wait_tile(t - 1, 1 - slot)
        wait_tile(t, slot)


def _dispatch(plan, h2, rtt, n_rows, tm):
    m = h2.shape[0]
    lr = _local_rows(tm)
    grid_spec = pltpu.PrefetchScalarGridSpec(
        num_scalar_prefetch=6,
        grid=(m // tm,),
        in_specs=[pl.BlockSpec((tm, D_MODEL), lambda t, *_: (t, 0)),
                  pl.BlockSpec((1, 8, tm), lambda t, *_: (t, 0, 0))],
        out_specs=pl.BlockSpec(memory_space=pl.ANY),
        scratch_shapes=[pltpu.VMEM((2, lr, XS_WIDTH), jnp.uint32),
                        pltpu.VMEM((CHUNK, XS_WIDTH), jnp.uint32),
                        pltpu.SemaphoreType.DMA((2,)), pltpu.SemaphoreType.DMA((1,))],
    )
    return pl.pallas_call(
        functools.partial(_dispatch_kernel, lr=lr),
        grid_spec=grid_spec,
        out_shape=jax.ShapeDtypeStruct((n_rows, XS_WIDTH), jnp.uint32),
        compiler_params=_cparams(("arbitrary",)),
        name="moe_dispatch",
    )(plan["lo8"], plan["gb8"], plan["n8"], plan["tot8"], plan["ts8"], plan["tn8"], h2, rtt)


def _expert_kernel(be_ref, nu_ref, x_ref, wg_ref, wu_ref, wd_ref, y_ref, wgub, wdb):
    i = pl.program_id(0)

    @pl.when(i < nu_ref[0])
    def _():
        prev_e = be_ref[jnp.maximum(i - 1, 0)]

        @pl.when((i == 0) | (be_ref[i] != prev_e))
        def _():
            wgub[:, 0:D_EXPERT] = wg_ref[0, 0].astype(BF16)
            wgub[:, D_EXPERT:2 * D_EXPERT] = wu_ref[0, 0].astype(BF16)
            wdb[...] = wd_ref[0, 0].astype(BF16)

        x = _unpack_halves(x_ref[:, 0:XS_FEAT])
        w_row = pltpu.bitcast(x_ref[:, XS_FEAT:XS_FEAT + 1], F32)
        gu = jnp.dot(x, wgub[...], preferred_element_type=F32)
        hid = _silu(gu[:, 0:D_EXPERT]) * gu[:, D_EXPERT:2 * D_EXPERT]
        y = jnp.dot(hid.astype(BF16), wdb[...], preferred_element_type=F32)
        y_ref[...] = _pack_halves((y * w_row).astype(BF16).astype(F32))

    @pl.when(i >= nu_ref[0])
    def _():
        y_ref[...] = jnp.zeros(y_ref.shape, jnp.uint32)


def _experts(block_e, n_used, xs, wg, wu, wd, layer):
    n_rows = xs.shape[0]
    wspec = lambda shape: pl.BlockSpec((1, 1) + shape, lambda i, be, nu: (layer, be[i], 0, 0))
    grid_spec = pltpu.PrefetchScalarGridSpec(
        num_scalar_prefetch=2,
        grid=(n_rows // MOE_BLOCK,),
        in_specs=[
            pl.BlockSpec((MOE_BLOCK, XS_WIDTH), lambda i, be, nu: (i, 0)),
            wspec((D_MODEL, D_EXPERT)), wspec((D_MODEL, D_EXPERT)), wspec((D_EXPERT, D_MODEL)),
        ],
        out_specs=pl.BlockSpec((MOE_BLOCK, XS_FEAT), lambda i, be, nu: (i, 0)),
        scratch_shapes=[
            pltpu.VMEM((D_MODEL, 2 * D_EXPERT), BF16),
            pltpu.VMEM((D_EXPERT, D_MODEL), BF16),
        ],
    )
    return pl.pallas_call(
        _expert_kernel,
        grid_spec=grid_spec,
        out_shape=jax.ShapeDtypeStruct((n_rows, XS_FEAT), jnp.uint32),
        compiler_params=_cparams(("arbitrary",)),
        name="moe_experts",
    )(block_e, n_used, xs, wg, wu, wd)


def _combine_kernel(lo_ref, gb_ref, n8_ref, tot_ref, x_ref, rt_ref, gf_ref, fg_ref, ys_hbm,
                    o_ref, yloc, sem, *, lr, final):
    t = pl.program_id(0)
    nt = pl.num_programs(0)
    slot = t % 2
    tm = x_ref.shape[0]

    def fetch(tile, sl):
        def per_expert(e, c):
            idx = tile * N_EXPERTS + e
            dst0 = lo_ref[idx] * CHUNK
            src0 = gb_ref[idx] * CHUNK

            def per_chunk(k, c2):
                pltpu.make_async_copy(ys_hbm.at[pl.ds(pl.multiple_of(src0 + k * CHUNK, CHUNK), CHUNK), :],
                                      yloc.at[sl, pl.ds(pl.multiple_of(dst0 + k * CHUNK, CHUNK), CHUNK), :],
                                      sem.at[sl]).start()
                return c2
            lax.fori_loop(0, n8_ref[idx], per_chunk, 0)
            return c
        lax.fori_loop(0, N_EXPERTS, per_expert, 0)

    @pl.when(t == 0)
    def _():
        yloc[...] = jnp.zeros(yloc.shape, jnp.uint32)
        fetch(0, 0)

    @pl.when(t + 1 < nt)
    def _():
        fetch(t + 1, 1 - slot)

    _wait_chunks(tot_ref[t], lr // CHUNK, lambda size: pltpu.make_async_copy(
        ys_hbm.at[pl.ds(0, size), :], yloc.at[slot, pl.ds(0, size), :], sem.at[slot]))

    ysw = _unpack_halves(yloc[slot])
    rt = rt_ref[...]
    cols = _iota((tm, lr), 1).astype(F32)
    pick = jnp.where(cols == rt[:, RT_ROW1:RT_ROW1 + 1], 1.0,
                     jnp.where(cols == rt[:, RT_ROW2:RT_ROW2 + 1], 1.0, 0.0)).astype(BF16)
    x = x_ref[...] + gf_ref[0] * jnp.dot(pick, ysw, preferred_element_type=F32)
    if final:
        ms = jnp.mean(x * x, axis=-1, keepdims=True)
        x = x * lax.rsqrt(ms + EPS) * fg_ref[...]
    o_ref[...] = x


def _combine(plan, x2, ys, rt, gf, fg, seq, tm, final):
    m = x2.shape[0]
    per_b = seq // tm
    lr = _local_rows(tm)
    grid_spec = pltpu.PrefetchScalarGridSpec(
        num_scalar_prefetch=4,
        grid=(m // tm,),
        in_specs=[
            pl.BlockSpec((tm, D_MODEL), lambda i, *_: (i, 0)),
            pl.BlockSpec((tm, LANES), lambda i, *_: (i, 0)),
            pl.BlockSpec((1, 1, D_MODEL), lambda i, *_: (i // per_b, 0, 0)),
            pl.BlockSpec((1, D_MODEL), lambda i, *_: (0, 0)),
            pl.BlockSpec(memory_space=pl.ANY),
        ],
        out_specs=pl.BlockSpec((tm, D_MODEL), lambda i, *_: (i, 0)),
        scratch_shapes=[pltpu.VMEM((2, lr, XS_FEAT), jnp.uint32), pltpu.SemaphoreType.DMA((2,))],
    )
    return pl.pallas_call(
        functools.partial(_combine_kernel, lr=lr, final=final),
        grid_spec=grid_spec,
        out_shape=jax.ShapeDtypeStruct((m, D_MODEL), F32),
        compiler_params=_cparams(("arbitrary",)),
        name="moe_combine",
    )(plan["lo8"], plan["gb8"], plan["n8"], plan["tot8"], x2, rt, gf, fg, ys)


def _moe_plan(c, n_blocks):
    i32 = jnp.int32
    blk8 = MOE_BLOCK // CHUNK
    c8 = (c + CHUNK - 1) // CHUNK
    lo8 = jnp.cumsum(c8, axis=1) - c8
    per_e = jnp.sum(c8, axis=0)
    pad8 = (per_e + blk8 - 1) // blk8 * blk8
    end8 = jnp.cumsum(pad8)
    start8 = end8 - pad8
    gb8 = start8[None, :] + jnp.cumsum(c8, axis=0) - c8
    blk_start8 = jnp.arange(n_blocks, dtype=i32) * blk8
    block_e = jnp.minimum(jnp.sum(end8[None, :] <= blk_start8[:, None], axis=1), N_EXPERTS - 1)
    return {
        "lo8": lo8.reshape(-1).astype(i32), "gb8": gb8.reshape(-1).astype(i32),
        "n8": c8.reshape(-1).astype(i32), "tot8": jnp.sum(c8, axis=1).astype(i32),
        "ts8": jnp.concatenate([start8 + per_e, end8[-1:]]).astype(i32),
        "tn8": jnp.concatenate([pad8 - per_e, n_blocks * blk8 - end8[-1:]]).astype(i32),
        "block_e": block_e.astype(i32), "n_used": (end8[-1:] // blk8).astype(i32),
    }


def _w_in_sections(w):
    d_dt = SSD_WIDTH + XBC_WIDTH
    d_q = d_dt + SSD_HEADS
    d_f = d_q + 3 * ATTN_WIDTH
    d_ga = d_f + ATTN_HEADS
    small = jnp.concatenate([w[..., d_dt:d_q], w[..., d_f:d_ga],
                             jnp.zeros(w.shape[:-1] + (LANES - SSD_HEADS - ATTN_HEADS,), w.dtype)], axis=-1)
    sections = [w[..., SSD_WIDTH:d_dt], w[..., :SSD_WIDTH], w[..., d_q:d_f], w[..., d_ga:], small]
    return [sec.astype(BF16) for sec in sections]


def kernel(x, c, ada_w, ada_b, norm_mix_g, w_in, ssd_conv_w, ssd_conv_b, ssd_dt_bias, ssd_a_log,
           ssd_d, ssd_norm_g, fox_f_bias, fox_norm_g, cm_conv_w, cm_conv_b, cm_ln_g, cm_ln_b, w_out,
           norm_ffn_g, w_router_group, b_router_group, w_router_expert, b_router_expert, w_gate,
           w_up, w_down, final_norm_g):
    bsz, seq, d = x.shape
    m = bsz * seq
    tm = min(512, seq)
    tt = min(512, seq)
    n_blocks = -(-(2 * m + (CHUNK - 1) * (m // tm) * N_EXPERTS) // MOE_BLOCK) + N_EXPERTS

    mod = _modulation(c, ada_w, ada_b)
    x2 = x.reshape(m, d)
    w_in_sections = _w_in_sections(w_in)
    for l in range(DEPTH):
        sh_m, sc_m, g_m, sh_f, sc_f, g_f = [v.reshape(bsz, 1, d) for v in jnp.split(mod[l], 6, axis=-1)]
        a_m = norm_mix_g[l][None, None, :] * (1.0 + sc_m)
        proj = _inproj(x2, a_m, sh_m, w_in_sections, l, seq, tm)
        y_ssd = _ssd(proj, ssd_conv_w[l], ssd_conv_b[l], ssd_dt_bias[l], ssd_a_log[l],
                     ssd_d[l], ssd_norm_g[l], bsz, seq, tt)
        qa, ka, va, aux, stat = _foxprep(proj, fox_f_bias[l], bsz, seq)
        o_att = _fox(qa, ka, va, aux, stat)
        y_cnv = _conformer(proj, cm_conv_w[l], cm_conv_b[l], cm_ln_g[l], cm_ln_b[l], bsz, seq, tt)

        w_r = jnp.concatenate([w_router_group[l], w_router_expert[l],
                               jnp.zeros((d, LANES - N_GROUPS - N_EXPERTS), F32)], axis=1)
        w_rh = w_r.astype(BF16)
        w_r2 = jnp.concatenate([w_rh, (w_r - w_rh.astype(F32)).astype(BF16)], axis=1)
        b_r = jnp.concatenate([b_router_group[l], b_router_expert[l],
                               jnp.zeros((LANES - N_GROUPS - N_EXPERTS,), F32)]).reshape(1, LANES)
        a_f = norm_ffn_g[l][None, None, :] * (1.0 + sc_f)
        x2, h2, rt, rtt, cnt = _outproj(x2, y_ssd, o_att, y_cnv, w_out[l].astype(BF16),
                                        fox_norm_g[l].reshape(1, -1), g_m, a_f, sh_f, w_r2, b_r,
                                        seq, tm)
        plan = _moe_plan(cnt[:, 0, :N_EXPERTS].astype(jnp.int32), n_blocks)
        xs = _dispatch(plan, h2, rtt, n_blocks * MOE_BLOCK, tm)
        ys = _experts(plan["block_e"], plan["n_used"], xs, w_gate, w_up, w_down, l)
        x2 = _combine(plan, x2, ys, rt, g_f, final_norm_g.reshape(1, d), seq, tm,
                      final=(l == DEPTH - 1))
    return x2.reshape(bsz, seq, d)
```

```python
import functools

import jax
import jax.numpy as jnp
import numpy as np
from jax import lax
from jax.experimental import pallas as pl
from jax.experimental.pallas import tpu as pltpu

F32 = jnp.float32
BF16 = jnp.bfloat16
HIGHEST = lax.Precision.HIGHEST

D_MODEL = 1024
DEPTH = 4
SSD_WIDTH = 512
SSD_HEADS = 8
SSD_HEAD_DIM = 64
SSD_STATE = 128
SSD_CONV = 4
SSD_CHUNK = 128
XBC_WIDTH = 1024
ATTN_WIDTH = 256
ATTN_HEADS = 4
ATTN_HEAD_DIM = 64
CONV_WIDTH = 256
CONV_KERNEL = 31
N_GROUPS = 4
PER_GROUP = 8
N_EXPERTS = 32
D_EXPERT = 512
MOE_BLOCK = 512
EPS = 1e-6

LANES = 128
SUBLANES = 8
COL_XBC = 0
COL_Z = 1024
COL_QKV = 1536
COL_GA = 2304
COL_GB = 2560
COL_SM = 2816
NP = 2944
SM_F = 8

VMEM_LIMIT = 56 * 1024 * 1024


def _cparams(sem):
    return pltpu.CompilerParams(dimension_semantics=sem, vmem_limit_bytes=VMEM_LIMIT)


def _sigmoid(x):
    return 1.0 / (1.0 + jnp.exp(-x))


def _silu(x):
    return x * _sigmoid(x)


def _softplus(x):
    return jnp.maximum(x, 0.0) + jnp.log1p(jnp.exp(-jnp.abs(x)))


def _iota(shape, dim):
    return lax.broadcasted_iota(jnp.int32, shape, dim)


def _split_bf16x3(x):
    hi = x.astype(BF16)
    r1 = x - hi.astype(F32)
    mid = r1.astype(BF16)
    lo = (r1 - mid.astype(F32)).astype(BF16)
    return hi, mid, lo


def _dot_onehot_rhs(x, sel):
    return sum(jnp.dot(part, sel, preferred_element_type=F32) for part in _split_bf16x3(x))


def _dot_onehot_lhs(sel, x):
    return sum(jnp.dot(sel, part, preferred_element_type=F32) for part in _split_bf16x3(x))


def _mod_kernel(c_ref, w_ref, b_ref, o_ref):
    cond = _silu(c_ref[...])
    o_ref[0] = jnp.dot(cond, w_ref[0], precision=HIGHEST, preferred_element_type=F32) + b_ref[0]


def _modulation(c, ada_w, ada_b):
    bsz = c.shape[0]
    rows = 8
    cpad = jnp.zeros((rows, D_MODEL), F32).at[:bsz].set(c)
    tn = 1536
    n6 = 6 * D_MODEL
    out = pl.pallas_call(
        _mod_kernel,
        grid=(DEPTH, n6 // tn),
        in_specs=[
            pl.BlockSpec((rows, D_MODEL), lambda l, j: (0, 0)),
            pl.BlockSpec((1, D_MODEL, tn), lambda l, j: (l, 0, j)),
            pl.BlockSpec((1, 1, tn), lambda l, j: (l, 0, j)),
        ],
        out_specs=pl.BlockSpec((1, rows, tn), lambda l, j: (l, 0, j)),
        out_shape=jax.ShapeDtypeStruct((DEPTH, rows, n6), F32),
        compiler_params=_cparams(("arbitrary", "arbitrary")),
        name="adaln_mod",
    )(cpad, ada_w, ada_b.reshape(DEPTH, 1, n6))
    return out[:, :bsz]


def _inproj_kernel(x_ref, a_ref, s_ref, *refs):
    w_refs, o_ref = refs[:-1], refs[-1]
    x = x_ref[...]
    ms = jnp.mean(x * x, axis=-1, keepdims=True)
    h = (x * lax.rsqrt(ms + EPS) * a_ref[0] + s_ref[0]).astype(BF16)
    col = 0
    for w_ref in w_refs:
        width = w_ref.shape[-1]
        o_ref[:, col:col + width] = jnp.dot(h, w_ref[0], preferred_element_type=F32)
        col += width


def _inproj(x2, a, s, w_sections, layer, seq, tm):
    m = x2.shape[0]
    per_b = seq // tm
    assert sum(w.shape[-1] for w in w_sections) == NP
    return pl.pallas_call(
        _inproj_kernel,
        grid=(m // tm,),
        in_specs=[
            pl.BlockSpec((tm, D_MODEL), lambda i: (i, 0)),
            pl.BlockSpec((1, 1, D_MODEL), lambda i: (i // per_b, 0, 0)),
            pl.BlockSpec((1, 1, D_MODEL), lambda i: (i // per_b, 0, 0)),
        ] + [pl.BlockSpec((1, D_MODEL, w.shape[-1]), lambda i: (layer, 0, 0)) for w in w_sections],
        out_specs=pl.BlockSpec((tm, NP), lambda i: (i, 0)),
        out_shape=jax.ShapeDtypeStruct((m, NP), F32),
        compiler_params=_cparams(("arbitrary",)),
        name="in_proj",
    )(x2, a, s, *w_sections)


def _ssd_kernel(z_ref, xbc_ref, sm_ref, cw_ref, cb_ref, dtb_ref, alog_ref, dtbt_ref,
                alogt_ref, e_ref, dx_ref, ng_ref, y_ref, xpad, xc, prev, *, tt):
    t = pl.program_id(1)

    @pl.when(t == 0)
    def _():
        xpad[0:8, :] = jnp.zeros((8, XBC_WIDTH), F32)
        prev[...] = jnp.zeros(prev.shape, F32)

    xpad[8:8 + tt, :] = xbc_ref[...]
    acc = jnp.broadcast_to(cb_ref[...], (tt, XBC_WIDTH))
    for k in range(SSD_CONV):
        off = 8 - (SSD_CONV - 1) + k
        acc = acc + cw_ref[k:k + 1, :] * xpad[off:off + tt, :]
    xc[...] = _silu(acc)
    xpad[0:8, :] = xpad[tt:tt + 8, :]

    cl = SSD_CHUNK
    row = _iota((cl, cl), 0)
    col = _iota((cl, cl), 1)
    causal = row >= col
    tril = jnp.where(causal, 1.0, 0.0).astype(BF16)
    triu = jnp.where(row <= col, 1.0, 0.0).astype(BF16)
    lo = col < SSD_HEAD_DIM
    lane1 = _iota((1, LANES), 1)
    a_row = jnp.where(lane1 < SSD_HEADS, -jnp.exp(alog_ref[...]), 0.0)
    a_col = -jnp.exp(alogt_ref[...])
    expand = e_ref[...]

    def chunk(c, carry):
        r0 = c * cl
        xs = xc[pl.ds(r0, cl), 0:SSD_WIDTH]
        bmat = xc[pl.ds(r0, cl), SSD_WIDTH:SSD_WIDTH + 2 * SSD_STATE]
        cmat = xc[pl.ds(r0, cl), SSD_WIDTH + 2 * SSD_STATE:XBC_WIDTH]
        sm = sm_ref[pl.ds(r0, cl), :]
        dt = _softplus(sm + dtb_ref[...])
        da = dt * a_row
        acs = _dot_onehot_lhs(tril, da)
        dtt = _softplus(sm.T[0:SSD_HEADS, :] + dtbt_ref[...])
        acst = _dot_onehot_rhs(dtt * a_col, triu)
        dt_x = _dot_onehot_rhs(dt, expand)
        acs_x = _dot_onehot_rhs(acs, expand)
        last = acs_x[cl - 1:cl, :]
        eacs_x = jnp.exp(acs_x)
        dte_x = jnp.exp(last - acs_x)
        cd_x = jnp.exp(last)
        xdt = xs * dt_x
        xdte = (xdt * dte_x).astype(BF16)
        zc = z_ref[pl.ds(r0, cl), :]
        for g in range(2):
            bg = bmat[:, g * SSD_STATE:(g + 1) * SSD_STATE]
            cg = cmat[:, g * SSD_STATE:(g + 1) * SSD_STATE].astype(BF16)
            bgt = bg.T.astype(BF16)
            cbm = jnp.dot(cg, bgt, preferred_element_type=F32)
            pair_out = []
            for j in range(2):
                p = 2 * g + j
                sl = slice(p * LANES, (p + 1) * LANES)
                xp = xdt[:, sl]
                yd = jnp.zeros((cl, LANES), F32)
                for half in range(2):
                    h = 2 * p + half
                    seg = acs[:, h:h + 1] - acst[h:h + 1, :]
                    dec = jnp.exp(jnp.where(causal, seg, -jnp.inf))
                    gm = (cbm * dec).astype(BF16)
                    own = lo if half == 0 else jnp.logical_not(lo)
                    xm = jnp.where(own, xp, 0.0).astype(BF16)
                    yd = yd + jnp.dot(gm, xm, preferred_element_type=F32)
                prev_p = prev[:, sl]
                yo = jnp.dot(cg, prev_p.astype(BF16), preferred_element_type=F32) * eacs_x[:, sl]
                st = jnp.dot(bgt, xdte[:, sl], preferred_element_type=F32)
                prev[:, sl] = prev_p * cd_x[:, sl] + st
                pair_out.append(yd + yo + xs[:, sl] * dx_ref[:, sl])
            gs = slice(g * 2 * LANES, (g + 1) * 2 * LANES)
            yg = jnp.concatenate(pair_out, axis=-1) * _silu(zc[:, gs])
            ms = jnp.mean(yg * yg, axis=-1, keepdims=True)
            y_ref[pl.ds(r0, cl), gs] = yg * lax.rsqrt(ms + EPS) * ng_ref[:, gs]
        return carry

    for c in range(tt // cl):
        chunk(c, 0)


def _ssd(proj, cw, cb, dtb, alog, dx, ng, bsz, seq, tt):
    m = bsz * seq
    nt = seq // tt
    pad = LANES - SSD_HEADS
    dtb_row = jnp.pad(dtb, (0, pad)).reshape(1, LANES)
    alog_row = jnp.pad(alog, (0, pad)).reshape(1, LANES)
    dtb_col = jnp.broadcast_to(dtb[:, None], (SSD_HEADS, SSD_CHUNK))
    alog_col = jnp.broadcast_to(alog[:, None], (SSD_HEADS, SSD_CHUNK))
    expand = np.zeros((LANES, SSD_WIDTH), np.float32)
    for h in range(SSD_HEADS):
        expand[h, h * SSD_HEAD_DIM:(h + 1) * SSD_HEAD_DIM] = 1.0
    dx_row = jnp.repeat(dx, SSD_HEAD_DIM).reshape(1, SSD_WIDTH)
    full = lambda shape: pl.BlockSpec(shape, lambda b, t: (0,) * len(shape))
    return pl.pallas_call(
        functools.partial(_ssd_kernel, tt=tt),
        grid=(bsz, nt),
        in_specs=[
            pl.BlockSpec((tt, SSD_WIDTH), lambda b, t: (b * nt + t, COL_Z // SSD_WIDTH)),
            pl.BlockSpec((tt, XBC_WIDTH), lambda b, t: (b * nt + t, COL_XBC // XBC_WIDTH + 0)),
            pl.BlockSpec((tt, LANES), lambda b, t: (b * nt + t, COL_SM // LANES)),
            full((SSD_CONV, XBC_WIDTH)),
            full((1, XBC_WIDTH)),
            full((1, LANES)),
            full((1, LANES)),
            full((SSD_HEADS, SSD_CHUNK)),
            full((SSD_HEADS, SSD_CHUNK)),
            full((LANES, SSD_WIDTH)),
            full((1, SSD_WIDTH)),
            full((1, SSD_WIDTH)),
        ],
        out_specs=pl.BlockSpec((tt, SSD_WIDTH), lambda b, t: (b * nt + t, 0)),
        out_shape=jax.ShapeDtypeStruct((m, SSD_WIDTH), F32),
        scratch_shapes=[
            pltpu.VMEM((tt + 8, XBC_WIDTH), F32),
            pltpu.VMEM((tt, XBC_WIDTH), F32),
            pltpu.VMEM((SSD_STATE, SSD_WIDTH), F32),
        ],
        compiler_params=_cparams(("arbitrary", "arbitrary")),
        name="ssd_scan",
    )(proj, proj, proj, cw, cb.reshape(1, XBC_WIDTH), dtb_row, alog_row, dtb_col, alog_col,
      jnp.asarray(expand, dtype=BF16), dx_row, ng.reshape(1, SSD_WIDTH))


LOG2E = 1.4426950408889634
ATT_BLOCK = 256
ATT_TQ = 512
PRUNE_LOG2 = 128.0
FAST_GAP_LOG2 = 60.0
SHIFT_MARGIN_LOG2 = 1.0
AUX_CUMEND = 0
AUX_KMAX = 1
ST_QNORM, ST_CUM, ST_SELF = 0, 1, 2


def _foxprep_kernel(qkv_ref, sm_ref, fb_ref, q_out, k_out, v_out, aux_out, stat_out, carry, *, tt):
    t = pl.program_id(1)

    @pl.when(t == 0)
    def _():
        carry[...] = jnp.zeros(carry.shape, F32)
        aux_out[...] = jnp.zeros(aux_out.shape, F32)

    logit = sm_ref[...] + fb_ref[...]
    logf = -_softplus(-logit)
    row = _iota((tt, tt), 0)
    col = _iota((tt, tt), 1)
    tril = jnp.where(row >= col, 1.0, 0.0).astype(BF16)
    cum = _dot_onehot_lhs(tril, logf) + carry[...]
    carry[...] = cum[tt - 1:tt, :]

    lane = _iota((tt, LANES), 1)
    lane1 = _iota((1, LANES), 1)
    scale = ATTN_HEAD_DIM ** -0.5 * LOG2E
    for h in range(ATTN_HEADS):
        pair, half = h // 2, h % 2
        own = (lane < ATTN_HEAD_DIM) if half == 0 else (lane >= ATTN_HEAD_DIM)
        a0 = ATTN_HEAD_DIM * (1 - half)
        cs = jnp.broadcast_to(cum[:, SM_F + h:SM_F + h + 1], (tt, LANES)) * LOG2E
        hi = cs.astype(BF16).astype(F32)
        r1 = cs - hi
        mid = r1.astype(BF16).astype(F32)
        low = r1 - mid
        qp = qkv_ref[:, pair * LANES:(pair + 1) * LANES]
        kp = qkv_ref[:, ATTN_WIDTH + pair * LANES:ATTN_WIDTH + (pair + 1) * LANES]
        vp = qkv_ref[:, 2 * ATTN_WIDTH + pair * LANES:2 * ATTN_WIDTH + (pair + 1) * LANES]
        qa = jnp.where(lane == a0, hi, jnp.where(lane == a0 + 1, mid, jnp.where(
            lane == a0 + 2, low, jnp.where((lane >= a0 + 3) & (lane < a0 + 6), 1.0, 0.0))))
        ka = jnp.where(lane == a0 + 3, -hi, jnp.where(lane == a0 + 4, -mid, jnp.where(
            lane == a0 + 5, -low, jnp.where((lane >= a0) & (lane < a0 + 9), 1.0, 0.0))))
        qb = jnp.where(own, qp * scale, qa).astype(BF16)
        q_out[0, h] = qb
        kb = jnp.where(own, kp, ka).astype(BF16)
        k_out[0, h] = kb
        qf = jnp.where(own, qb.astype(F32), 0.0)
        qn = jnp.sqrt(jnp.sum(qf * qf, axis=-1, keepdims=True))
        s_self = jnp.sum(qf * kb.astype(F32), axis=-1, keepdims=True)
        stat = jnp.where(lane == ST_QNORM, qn, jnp.where(lane == ST_CUM, hi + mid + low, jnp.where(
            lane == ST_SELF, s_self, 0.0)))
        stat_out[0, h] = stat.T[0:8, :]
        v_out[0, h] = jnp.where(own, vp, jnp.where(lane == a0, 1.0, 0.0)).astype(BF16)
        kf = jnp.where(own, kb.astype(F32), 0.0)
        kn2_rows = jnp.sum(kf * kf, axis=-1, keepdims=True)
        for blk in range(tt // ATT_BLOCK):
            end = (blk + 1) * ATT_BLOCK
            kn2 = jnp.max(kn2_rows[end - ATT_BLOCK:end, :], axis=0, keepdims=True)
            here = lane1 == t * (tt // ATT_BLOCK) + blk
            aux_out[0, h, AUX_CUMEND:AUX_CUMEND + 1, :] = jnp.where(
                here, cs[end - 1:end, :], aux_out[0, h, AUX_CUMEND:AUX_CUMEND + 1, :])
            aux_out[0, h, AUX_KMAX:AUX_KMAX + 1, :] = jnp.where(
                here, jnp.sqrt(kn2), aux_out[0, h, AUX_KMAX:AUX_KMAX + 1, :])


def _foxprep(proj, fb, bsz, seq):
    tt = min(ATT_TQ, seq)
    nt = seq // tt
    assert tt % ATT_BLOCK == 0 and seq // ATT_BLOCK <= LANES
    fb_row = jnp.zeros((1, LANES), F32).at[0, SM_F:SM_F + ATTN_HEADS].set(fb)
    shp = jax.ShapeDtypeStruct((bsz, ATTN_HEADS, seq, LANES), BF16)
    ospec = pl.BlockSpec((1, ATTN_HEADS, tt, LANES), lambda b, t: (b, 0, t, 0))
    return pl.pallas_call(
        functools.partial(_foxprep_kernel, tt=tt),
        grid=(bsz, nt),
        in_specs=[
            pl.BlockSpec((tt, 3 * ATTN_WIDTH), lambda b, t: (b * nt + t, COL_QKV // (3 * ATTN_WIDTH))),
            pl.BlockSpec((tt, LANES), lambda b, t: (b * nt + t, COL_SM // LANES)),
            pl.BlockSpec((1, LANES), lambda b, t: (0, 0)),
        ],
        out_specs=[ospec, ospec, ospec,
                   pl.BlockSpec((1, ATTN_HEADS, 8, LANES), lambda b, t: (b, 0, 0, 0)),
                   pl.BlockSpec((1, ATTN_HEADS, 8, tt), lambda b, t: (b, 0, 0, t))],
        out_shape=[shp, shp, shp, jax.ShapeDtypeStruct((bsz, ATTN_HEADS, 8, LANES), F32),
                   jax.ShapeDtypeStruct((bsz, ATTN_HEADS, 8, seq), F32)],
        scratch_shapes=[pltpu.VMEM((1, LANES), F32)],
        compiler_params=_cparams(("arbitrary", "arbitrary")),
        name="fox_prep",
    )(proj, proj, fb_row)


def _fox_kernel(q_ref, k_ref, v_ref, aux_ref, stat_ref, o_ref, *, tq, kb):
    i = pl.program_id(2)
    ib = i * (tq // kb)
    nt = (((1,), (1,)), ((), ()))
    lane = _iota((tq, LANES), 1)
    lane1 = _iota((1, LANES), 1)
    qs = (q_ref[0, 0], q_ref[0, 1])

    def step(hh, r0, width, m, acc, mask):
        kblk = k_ref[0, hh, pl.ds(r0, width), :]
        vblk = v_ref[0, hh, pl.ds(r0, width), :]
        s = lax.dot_general(qs[hh], kblk, nt, preferred_element_type=F32)
        if mask is not None:
            s = jnp.where(mask, s, -jnp.inf)
        m_new = jnp.maximum(m, jnp.max(s, axis=-1, keepdims=True))
        p = jnp.exp2(s - m_new)
        alpha = jnp.exp2(m - m_new)
        acc = acc * alpha + jnp.dot(p.astype(BF16), vblk, preferred_element_type=F32)
        return m_new, acc

    d0 = pl.multiple_of(i * tq, tq)
    diag = _iota((tq, tq), 0) >= _iota((tq, tq), 1)
    ib_f = ib.astype(F32)
    first = []
    gaps = []
    q_shift = []
    for hh in range(2):
        a0 = ATTN_HEAD_DIM * (1 - hh)
        st = stat_ref[0, hh]
        qn = st[ST_QNORM:ST_QNORM + 1, :]
        cum_t = st[ST_CUM:ST_CUM + 1, :]
        s_self = st[ST_SELF:ST_SELF + 1, :]
        aux = aux_ref[0, hh]
        cend = aux[AUX_CUMEND:AUX_CUMEND + 1, :]
        kmax = jnp.max(jnp.where(lane1 < ib + tq // kb, aux[AUX_KMAX:AUX_KMAX + 1, :], 0.0), axis=-1,
                       keepdims=True)
        bound = qn * kmax
        slack = jnp.max(bound + cum_t - s_self, axis=-1, keepdims=True)
        live = (lane1 < ib) & (slack - cend > -PRUNE_LOG2)
        first.append(jnp.min(jnp.where(live, lane1.astype(F32), ib_f)))
        parts = _split_bf16x3(bound + SHIFT_MARGIN_LOG2)
        gaps.append(jnp.max(sum(part.astype(F32) for part in parts) - s_self))
        terms = jnp.concatenate([-part.astype(F32) for part in parts] + [jnp.zeros((13, tq), F32)],
                                axis=0).astype(BF16)
        place = jnp.where(_iota((16, LANES), 1) == _iota((16, LANES), 0) + (a0 + 6), 1.0,
                          0.0).astype(BF16)
        q_shift.append(qs[hh] + lax.dot_general(terms, place, (((0,), (0,)), ((), ())),
                                                preferred_element_type=F32).astype(BF16))
    j_first = jnp.minimum(first[0], first[1]).astype(jnp.int32)

    def fast(_):
        def blocks(c, r0, width, mask=None):
            out = []
            for hh in range(2):
                s = lax.dot_general(q_shift[hh], k_ref[0, hh, pl.ds(r0, width), :], nt,
                                    preferred_element_type=F32)
                if mask is not None:
                    s = jnp.where(mask, s, -jnp.inf)
                out.append(c[hh] + jnp.dot(jnp.exp2(s).astype(BF16), v_ref[0, hh, pl.ds(r0, width), :],
                                           preferred_element_type=F32))
            return tuple(out)

        zero = jnp.zeros((tq, LANES), F32)
        c = blocks((zero, zero), d0, tq, diag)
        j_lo = j_first - ((ib - j_first) & 1)
        n = ib - j_lo
        c = lax.cond((n & 2) == 2, lambda c: blocks(c, pl.multiple_of(j_lo * kb, kb), 2 * kb),
                     lambda c: c, c)
        j0 = j_lo + (n & 2)

        def body(jj, c):
            return blocks(c, pl.multiple_of((j0 + 4 * jj) * kb, kb), 4 * kb)
        return lax.fori_loop(0, lax.shift_right_logical(n, 2), body, c)

    def online(_):
        m_init = jnp.full((tq, 1), -jnp.inf, F32)
        acc_init = jnp.zeros((tq, LANES), F32)
        carry = step(0, d0, tq, m_init, acc_init, diag) + step(1, d0, tq, m_init, acc_init, diag)

        def body(j, c):
            r0 = pl.multiple_of(j * kb, kb)
            m0, a0_, m1, a1_ = c
            m0, a0_ = step(0, r0, kb, m0, a0_, None)
            m1, a1_ = step(1, r0, kb, m1, a1_, None)
            return m0, a0_, m1, a1_
        _, a0_, _, a1_ = lax.fori_loop(j_first, ib, body, carry)
        return a0_, a1_

    acc0, acc1 = lax.cond(jnp.maximum(gaps[0], gaps[1]) <= FAST_GAP_LOG2, fast, online, None)
    den0 = jnp.sum(jnp.where(lane == ATTN_HEAD_DIM, acc0, 0.0), axis=-1, keepdims=True)
    den1 = jnp.sum(jnp.where(lane == 0, acc1, 0.0), axis=-1, keepdims=True)
    o_ref[...] = jnp.where(lane < ATTN_HEAD_DIM, acc0 / den0, acc1 / den1)


def _fox(qa, ka, va, aux, stat):
    bsz, nh, seq, _ = qa.shape
    tq = min(ATT_TQ, seq)
    assert (tq // ATT_BLOCK) % 2 == 0
    nq = seq // tq
    kv_spec = pl.BlockSpec((1, 2, seq, LANES), lambda b, p, i: (b, p, 0, 0))
    return pl.pallas_call(
        functools.partial(_fox_kernel, tq=tq, kb=ATT_BLOCK),
        grid=(bsz, nh // 2, nq),
        in_specs=[pl.BlockSpec((1, 2, tq, LANES), lambda b, p, i: (b, p, i, 0)), kv_spec, kv_spec,
                  pl.BlockSpec((1, 2, 8, LANES), lambda b, p, i: (b, p, 0, 0)),
                  pl.BlockSpec((1, 2, 8, tq), lambda b, p, i: (b, p, 0, i))],
        out_specs=pl.BlockSpec((tq, LANES), lambda b, p, i: (b * nq + i, p)),
        out_shape=jax.ShapeDtypeStruct((bsz * seq, ATTN_WIDTH), F32),
        compiler_params=_cparams(("arbitrary", "arbitrary", "arbitrary")),
        name="fox_attn",
    )(qa, ka, va, aux, stat)


CONF_HALO = 32
CONF_ROWS = 64


def _conf_kernel(ga_ref, gb_ref, w_ref, b_ref, lg_ref, lb_ref, y_ref, upad, ush, *, tt):
    t = pl.program_id(1)

    @pl.when(t == 0)
    def _():
        upad[0:CONF_HALO, :] = jnp.zeros((CONF_HALO, CONV_WIDTH), F32)

    upad[CONF_HALO:CONF_HALO + tt, :] = ga_ref[...] * _sigmoid(gb_ref[...])
    base = CONF_HALO - (CONV_KERNEL - 1)
    for ph in range(1, SUBLANES):
        ush[ph - 1] = upad[ph:ph + ush.shape[1], :]
    for r in range(tt // CONF_ROWS):
        r0 = r * CONF_ROWS
        acc = jnp.broadcast_to(b_ref[...], (CONF_ROWS, CONV_WIDTH))
        for k in range(CONV_KERNEL):
            ph = (base + k) % SUBLANES
            lo = r0 + base + k - ph
            src = upad[lo:lo + CONF_ROWS, :] if ph == 0 else ush[ph - 1, lo:lo + CONF_ROWS, :]
            acc = acc + w_ref[k:k + 1, :] * src
        mu = jnp.mean(acc, axis=-1, keepdims=True)
        cen = acc - mu
        var = jnp.mean(cen * cen, axis=-1, keepdims=True)
        y = cen * lax.rsqrt(var + EPS) * lg_ref[...] + lb_ref[...]
        y_ref[r0:r0 + CONF_ROWS, :] = _silu(y)
    upad[0:CONF_HALO, :] = upad[tt:tt + CONF_HALO, :]


def _conformer(proj, w, b, lg, lb, bsz, seq, tt):
    m = bsz * seq
    nt = seq // tt
    full = lambda shape: pl.BlockSpec(shape, lambda bb, t: (0,) * len(shape))
    return pl.pallas_call(
        functools.partial(_conf_kernel, tt=tt),
        grid=(bsz, nt),
        in_specs=[
            pl.BlockSpec((tt, CONV_WIDTH), lambda bb, t: (bb * nt + t, COL_GA // CONV_WIDTH)),
            pl.BlockSpec((tt, CONV_WIDTH), lambda bb, t: (bb * nt + t, COL_GB // CONV_WIDTH)),
            full((CONV_KERNEL, CONV_WIDTH)),
            full((1, CONV_WIDTH)),
            full((1, CONV_WIDTH)),
            full((1, CONV_WIDTH)),
        ],
        out_specs=pl.BlockSpec((tt, CONV_WIDTH), lambda bb, t: (bb * nt + t, 0)),
        out_shape=jax.ShapeDtypeStruct((m, CONV_WIDTH), F32),
        scratch_shapes=[pltpu.VMEM((tt + CONF_HALO, CONV_WIDTH), F32),
                        pltpu.VMEM((SUBLANES - 1, tt + CONF_HALO - SUBLANES, CONV_WIDTH), F32)],
        compiler_params=_cparams(("arbitrary", "arbitrary")),
        name="conformer_conv",
    )(proj, proj, w, b.reshape(1, -1), lg.reshape(1, -1), lb.reshape(1, -1))


ROUTE_BIG = 1e9
CHUNK = 8
XS_FEAT = D_MODEL // 2
XS_WIDTH = XS_FEAT + LANES
RT_W1, RT_W2, RT_ROW1, RT_ROW2 = 0, 1, 2, 3


def _split3(x):
    hi = x.astype(BF16)
    lo = (x - hi.astype(F32)).astype(BF16)
    return hi, lo


def _outproj_kernel(x_ref, ys_ref, oa_ref, yc_ref, wo_ref, fg_ref, gm_ref, a2_ref, s2_ref,
                    wr_ref, br_ref, xn_ref, h2_ref, rt_ref, rtt_ref, cnt_ref):
    tm = x_ref.shape[0]
    lane = _iota((tm, LANES), 1)
    att = oa_ref[...]
    ms = jnp.mean(att * att, axis=-1, keepdims=True)
    att = att * lax.rsqrt(ms + EPS) * fg_ref[...]
    y = jnp.dot(ys_ref[...].astype(BF16), wo_ref[0:SSD_WIDTH, :], preferred_element_type=F32)
    y = y + jnp.dot(att.astype(BF16), wo_ref[SSD_WIDTH:SSD_WIDTH + ATTN_WIDTH, :],
                    preferred_element_type=F32)
    y = y + jnp.dot(yc_ref[...].astype(BF16), wo_ref[SSD_WIDTH + ATTN_WIDTH:, :],
                    preferred_element_type=F32)
    xn = x_ref[...] + gm_ref[0] * y
    xn_ref[...] = xn
    ms2 = jnp.mean(xn * xn, axis=-1, keepdims=True)
    h2 = xn * lax.rsqrt(ms2 + EPS) * a2_ref[0] + s2_ref[0]
    h2_ref[...] = h2.astype(BF16)

    hh, hl = _split3(h2)
    part = jnp.dot(jnp.concatenate([hh, hl], axis=0), wr_ref[...], preferred_element_type=F32)
    logits = (part[:tm, :LANES] + part[:tm, LANES:] + part[tm:, :LANES] + part[tm:, LANES:]) + br_ref[...]
    lanef = lane.astype(F32)
    lg = jnp.where(lane < N_GROUPS, logits, -jnp.inf)
    gmax = jnp.max(lg, axis=-1, keepdims=True)
    gsum = jnp.sum(jnp.exp(lg - gmax), axis=-1, keepdims=True)
    gidx = jnp.min(jnp.where(lg == gmax, lanef, ROUTE_BIG), axis=-1, keepdims=True)
    e_lo = N_GROUPS + PER_GROUP * gidx
    le = jnp.where((lanef >= e_lo) & (lanef < e_lo + PER_GROUP), logits, -jnp.inf)
    m1 = jnp.max(le, axis=-1, keepdims=True)
    i1 = jnp.min(jnp.where(le == m1, lanef, ROUTE_BIG), axis=-1, keepdims=True)
    le2 = jnp.where(lanef == i1, -jnp.inf, le)
    m2 = jnp.max(le2, axis=-1, keepdims=True)
    i2 = jnp.min(jnp.where(le2 == m2, lanef, ROUTE_BIG), axis=-1, keepdims=True)
    esum = jnp.sum(jnp.exp(le - m1), axis=-1, keepdims=True)
    p1 = 1.0 / esum
    p2 = jnp.exp(m2 - m1) / esum
    psel = 1.0 / gsum
    w1 = p1 / (p1 + p2) * psel
    w2 = p2 / (p1 + p2) * psel

    oh1 = (lanef == (i1 - N_GROUPS)).astype(F32)
    oh2 = (lanef == (i2 - N_GROUPS)).astype(F32)
    oh = oh1 + oh2
    cnt = jnp.sum(oh, axis=0, keepdims=True)
    before = (_iota((tm, tm), 0) > _iota((tm, tm), 1)).astype(BF16)
    rank = jnp.dot(before, oh.astype(BF16), preferred_element_type=F32)
    chunks = jnp.floor((cnt + (CHUNK - 1)) * (1.0 / CHUNK))
    below = (_iota((LANES, LANES), 0) < _iota((LANES, LANES), 1)).astype(BF16)
    seg_lo = CHUNK * jnp.dot(jnp.broadcast_to(chunks, (8, LANES)).astype(BF16), below,
                             preferred_element_type=F32)[0:1, :]
    lr1 = jnp.sum(oh1 * (rank + seg_lo), axis=-1, keepdims=True)
    lr2 = jnp.sum(oh2 * (rank + seg_lo), axis=-1, keepdims=True)
    route = jnp.where(lane == RT_W1, w1, jnp.where(lane == RT_W2, w2, jnp.where(
        lane == RT_ROW1, lr1, jnp.where(lane == RT_ROW2, lr2, 0.0))))
    rt_ref[...] = route
    rtt_ref[0] = route.T[0:8, :]
    cnt_ref[0] = jnp.broadcast_to(cnt, (8, LANES))


def _outproj(x2, ys, oa, yc, wo, fg, gm, a2, s2, wr, br, seq, tm):
    m = x2.shape[0]
    per_b = seq // tm
    nt = m // tm
    row = lambda w: pl.BlockSpec((tm, w), lambda i: (i, 0))
    bvec = pl.BlockSpec((1, 1, D_MODEL), lambda i: (i // per_b, 0, 0))
    full = lambda shape: pl.BlockSpec(shape, lambda i: (0,) * len(shape))
    return pl.pallas_call(
        _outproj_kernel,
        grid=(nt,),
        in_specs=[
            row(D_MODEL), row(SSD_WIDTH), row(ATTN_WIDTH), row(CONV_WIDTH),
            full((D_MODEL, D_MODEL)), full((1, ATTN_WIDTH)), bvec, bvec, bvec,
            full((D_MODEL, 2 * LANES)), full((1, LANES)),
        ],
        out_specs=[row(D_MODEL), row(D_MODEL), row(LANES),
                   pl.BlockSpec((1, 8, tm), lambda i: (i, 0, 0)),
                   pl.BlockSpec((1, 8, LANES), lambda i: (i, 0, 0))],
        out_shape=[jax.ShapeDtypeStruct((m, D_MODEL), F32), jax.ShapeDtypeStruct((m, D_MODEL), BF16),
                   jax.ShapeDtypeStruct((m, LANES), F32), jax.ShapeDtypeStruct((nt, 8, tm), F32),
                   jax.ShapeDtypeStruct((nt, 8, LANES), F32)],
        compiler_params=_cparams(("arbitrary",)),
        name="out_proj_router",
    )(x2, ys, oa, yc, wo, fg, gm, a2, s2, wr, br)


def _local_rows(tm):
    return -(-(2 * tm + (CHUNK - 1) * N_EXPERTS) // LANES) * LANES


def _pack_halves(x):
    w = x.shape[-1] // 2
    return pltpu.bitcast(x[:, :w], jnp.uint32) | lax.shift_right_logical(
        pltpu.bitcast(x[:, w:], jnp.uint32), jnp.uint32(16))


def _unpack_halves(p):
    hi = pltpu.bitcast(p & jnp.uint32(0xFFFF0000), F32)
    lo = pltpu.bitcast(lax.shift_left(p, jnp.uint32(16)), F32)
    return jnp.concatenate([hi, lo], axis=-1).astype(BF16)


def _wait_chunks(n, n_max, copy_of_size):
    for bit in range(n_max.bit_length()):
        @pl.when((lax.shift_right_logical(n, bit) & 1) == 1)
        def _():
            copy_of_size(CHUNK << bit).wait()


def _dispatch_kernel(lo_ref, gb_ref, n8_ref, tot_ref, ts_ref, tn_ref, h_ref, rtt_ref, xs_hbm,
                     xloc, zeros, sem, zsem, *, lr):
    t = pl.program_id(0)
    nt = pl.num_programs(0)
    slot = t % 2
    tm = h_ref.shape[0]

    def seg_copy(sl, src, dst):
        return pltpu.make_async_copy(xloc.at[sl, pl.ds(src, CHUNK), :], xs_hbm.at[pl.ds(dst, CHUNK), :],
                                     sem.at[sl])

    def wait_tile(tile, sl):
        _wait_chunks(tot_ref[tile], lr // CHUNK, lambda size: pltpu.make_async_copy(
            xloc.at[sl, pl.ds(0, size), :], xs_hbm.at[pl.ds(0, size), :], sem.at[sl]))

    @pl.when(t >= 2)
    def _():
        wait_tile(t - 2, slot)

    rows = _iota((lr, tm), 0).astype(F32)
    is1 = rows == rtt_ref[0, RT_ROW1:RT_ROW1 + 1, :]
    is2 = rows == rtt_ref[0, RT_ROW2:RT_ROW2 + 1, :]
    perm = jnp.where(is1, 1.0, jnp.where(is2, 1.0, 0.0))
    xloc[slot, :, 0:XS_FEAT] = _pack_halves(
        jnp.dot(perm.astype(BF16), h_ref[...], preferred_element_type=F32))
    w_row = jnp.sum(jnp.where(is1, rtt_ref[0, RT_W1:RT_W1 + 1, :],
                              jnp.where(is2, rtt_ref[0, RT_W2:RT_W2 + 1, :], 0.0)), axis=-1, keepdims=True)
    xloc[slot, :, XS_FEAT:XS_WIDTH] = jnp.where(
        _iota((lr, LANES), 1) == 0, pltpu.bitcast(jnp.broadcast_to(w_row, (lr, LANES)), jnp.uint32),
        jnp.uint32(0))

    def per_expert(e, c):
        idx = t * N_EXPERTS + e
        src0 = lo_ref[idx] * CHUNK
        dst0 = gb_ref[idx] * CHUNK

        def per_chunk(k, c2):
            seg_copy(slot, pl.multiple_of(src0 + k * CHUNK, CHUNK),
                     pl.multiple_of(dst0 + k * CHUNK, CHUNK)).start()
            return c2
        lax.fori_loop(0, n8_ref[idx], per_chunk, 0)
        return c
    lax.fori_loop(0, N_EXPERTS, per_expert, 0)

    @pl.when(t == nt - 1)
    def _():
        zeros[...] = jnp.zeros(zeros.shape, jnp.uint32)

        def zero_copy(dst):
            return pltpu.make_async_copy(zeros, xs_hbm.at[pl.ds(dst, CHUNK), :], zsem.at[0])

        def fill(e, c):
            dst0 = ts_ref[e] * CHUNK

            def one(k, c2):
                zero_copy(pl.multiple_of(dst0 + k * CHUNK, CHUNK)).start()
                return c2
            lax.fori_loop(0, tn_ref[e], one, 0)
            return c
        lax.fori_loop(0, N_EXPERTS + 1, fill, 0)

        def drain(e, c):
            def one(k, c2):
                zero_copy(0).wait()
                return c2
            lax.fori_loop(0, tn_ref[e], one, 0)
            return c
        lax.fori_loop(0, N_EXPERTS + 1, drain, 0)

        @pl.when(t >= 1)
        def _():
            wait_tile(t - 1, 1 - slot)
        wait_tile(t, slot)


def _dispatch(plan, h2, rtt, n_rows, tm):
    m = h2.shape[0]
    lr = _local_rows(tm)
    grid_spec = pltpu.PrefetchScalarGridSpec(
        num_scalar_prefetch=6,
        grid=(m // tm,),
        in_specs=[pl.BlockSpec((tm, D_MODEL), lambda t, *_: (t, 0)),
                  pl.BlockSpec((1, 8, tm), lambda t, *_: (t, 0, 0))],
        out_specs=pl.BlockSpec(memory_space=pl.ANY),
        scratch_shapes=[pltpu.VMEM((2, lr, XS_WIDTH), jnp.uint32),
                        pltpu.VMEM((CHUNK, XS_WIDTH), jnp.uint32),
                        pltpu.SemaphoreType.DMA((2,)), pltpu.SemaphoreType.DMA((1,))],
    )
    return pl.pallas_call(
        functools.partial(_dispatch_kernel, lr=lr),
        grid_spec=grid_spec,
        out_shape=jax.ShapeDtypeStruct((n_rows, XS_WIDTH), jnp.uint32),
        compiler_params=_cparams(("arbitrary",)),
        name="moe_dispatch",
    )(plan["lo8"], plan["gb8"], plan["n8"], plan["tot8"], plan["ts8"], plan["tn8"], h2, rtt)


def _expert_kernel(be_ref, nu_ref, x_ref, wg_ref, wu_ref, wd_ref, y_ref, wgub, wdb):
    i = pl.program_id(0)

    @pl.when(i < nu_ref[0])
    def _():
        prev_e = be_ref[jnp.maximum(i - 1, 0)]

        @pl.when((i == 0) | (be_ref[i] != prev_e))
        def _():
            wgub[:, 0:D_EXPERT] = wg_ref[0, 0].astype(BF16)
            wgub[:, D_EXPERT:2 * D_EXPERT] = wu_ref[0, 0].astype(BF16)
            wdb[...] = wd_ref[0, 0].astype(BF16)

        x = _unpack_halves(x_ref[:, 0:XS_FEAT])
        w_row = pltpu.bitcast(x_ref[:, XS_FEAT:XS_FEAT + 1], F32)
        gu = jnp.dot(x, wgub[...], preferred_element_type=F32)
        hid = _silu(gu[:, 0:D_EXPERT]) * gu[:, D_EXPERT:2 * D_EXPERT]
        y = jnp.dot(hid.astype(BF16), wdb[...], preferred_element_type=F32)
        y_ref[...] = _pack_halves((y * w_row).astype(BF16).astype(F32))

    @pl.when(i >= nu_ref[0])
    def _():
        y_ref[...] = jnp.zeros(y_ref.shape, jnp.uint32)


def _experts(block_e, n_used, xs, wg, wu, wd, layer):
    n_rows = xs.shape[0]
    wspec = lambda shape: pl.BlockSpec((1, 1) + shape, lambda i, be, nu: (layer, be[i], 0, 0))
    grid_spec = pltpu.PrefetchScalarGridSpec(
        num_scalar_prefetch=2,
        grid=(n_rows // MOE_BLOCK,),
        in_specs=[
            pl.BlockSpec((MOE_BLOCK, XS_WIDTH), lambda i, be, nu: (i, 0)),
            wspec((D_MODEL, D_EXPERT)), wspec((D_MODEL, D_EXPERT)), wspec((D_EXPERT, D_MODEL)),
        ],
        out_specs=pl.BlockSpec((MOE_BLOCK, XS_FEAT), lambda i, be, nu: (i, 0)),
        scratch_shapes=[
            pltpu.VMEM((D_MODEL, 2 * D_EXPERT), BF16),
            pltpu.VMEM((D_EXPERT, D_MODEL), BF16),
        ],
    )
    return pl.pallas_call(
        _expert_kernel,
        grid_spec=grid_spec,
        out_shape=jax.ShapeDtypeStruct((n_rows, XS_FEAT), jnp.uint32),
        compiler_params=_cparams(("arbitrary",)),
        name="moe_experts",
    )(block_e, n_used, xs, wg, wu, wd)


def _combine_kernel(lo_ref, gb_ref, n8_ref, tot_ref, x_ref, rt_ref, gf_ref, fg_ref, ys_hbm,
                    o_ref, yloc, sem, *, lr, final):
    t = pl.program_id(0)
    nt = pl.num_programs(0)
    slot = t % 2
    tm = x_ref.shape[0]

    def fetch(tile, sl):
        def per_expert(e, c):
            idx = tile * N_EXPERTS + e
            dst0 = lo_ref[idx] * CHUNK
            src0 = gb_ref[idx] * CHUNK

            def per_chunk(k, c2):
                pltpu.make_async_copy(ys_hbm.at[pl.ds(pl.multiple_of(src0 + k * CHUNK, CHUNK), CHUNK), :],
                                      yloc.at[sl, pl.ds(pl.multiple_of(dst0 + k * CHUNK, CHUNK), CHUNK), :],
                                      sem.at[sl]).start()
                return c2
            lax.fori_loop(0, n8_ref[idx], per_chunk, 0)
            return c
        lax.fori_loop(0, N_EXPERTS, per_expert, 0)

    @pl.when(t == 0)
    def _():
        yloc[...] = jnp.zeros(yloc.shape, jnp.uint32)
        fetch(0, 0)

    @pl.when(t + 1 < nt)
    def _():
        fetch(t + 1, 1 - slot)

    _wait_chunks(tot_ref[t], lr // CHUNK, lambda size: pltpu.make_async_copy(
        ys_hbm.at[pl.ds(0, size), :], yloc.at[slot, pl.ds(0, size), :], sem.at[slot]))

    ysw = _unpack_halves(yloc[slot])
    rt = rt_ref[...]
    cols = _iota((tm, lr), 1).astype(F32)
    pick = jnp.where(cols == rt[:, RT_ROW1:RT_ROW1 + 1], 1.0,
                     jnp.where(cols == rt[:, RT_ROW2:RT_ROW2 + 1], 1.0, 0.0)).astype(BF16)
    x = x_ref[...] + gf_ref[0] * jnp.dot(pick, ysw, preferred_element_type=F32)
    if final:
        ms = jnp.mean(x * x, axis=-1, keepdims=True)
        x = x * lax.rsqrt(ms + EPS) * fg_ref[...]
    o_ref[...] = x


def _combine(plan, x2, ys, rt, gf, fg, seq, tm, final):
    m = x2.shape[0]
    per_b = seq // tm
    lr = _local_rows(tm)
    grid_spec = pltpu.PrefetchScalarGridSpec(
        num_scalar_prefetch=4,
        grid=(m // tm,),
        in_specs=[
            pl.BlockSpec((tm, D_MODEL), lambda i, *_: (i, 0)),
            pl.BlockSpec((tm, LANES), lambda i, *_: (i, 0)),
            pl.BlockSpec((1, 1, D_MODEL), lambda i, *_: (i // per_b, 0, 0)),
            pl.BlockSpec((1, D_MODEL), lambda i, *_: (0, 0)),
            pl.BlockSpec(memory_space=pl.ANY),
        ],
        out_specs=pl.BlockSpec((tm, D_MODEL), lambda i, *_: (i, 0)),
        scratch_shapes=[pltpu.VMEM((2, lr, XS_FEAT), jnp.uint32), pltpu.SemaphoreType.DMA((2,))],
    )
    return pl.pallas_call(
        functools.partial(_combine_kernel, lr=lr, final=final),
        grid_spec=grid_spec,
        out_shape=jax.ShapeDtypeStruct((m, D_MODEL), F32),
        compiler_params=_cparams(("arbitrary",)),
        name="moe_combine",
    )(plan["lo8"], plan["gb8"], plan["n8"], plan["tot8"], x2, rt, gf, fg, ys)


def _moe_plan(c, n_blocks):
    i32 = jnp.int32
    blk8 = MOE_BLOCK // CHUNK
    c8 = (c + CHUNK - 1) // CHUNK
    lo8 = jnp.cumsum(c8, axis=1) - c8
    per_e = jnp.sum(c8, axis=0)
    pad8 = (per_e + blk8 - 1) // blk8 * blk8
    end8 = jnp.cumsum(pad8)
    start8 = end8 - pad8
    gb8 = start8[None, :] + jnp.cumsum(c8, axis=0) - c8
    blk_start8 = jnp.arange(n_blocks, dtype=i32) * blk8
    block_e = jnp.minimum(jnp.sum(end8[None, :] <= blk_start8[:, None], axis=1), N_EXPERTS - 1)
    return {
        "lo8": lo8.reshape(-1).astype(i32), "gb8": gb8.reshape(-1).astype(i32),
        "n8": c8.reshape(-1).astype(i32), "tot8": jnp.sum(c8, axis=1).astype(i32),
        "ts8": jnp.concatenate([start8 + per_e, end8[-1:]]).astype(i32),
        "tn8": jnp.concatenate([pad8 - per_e, n_blocks * blk8 - end8[-1:]]).astype(i32),
        "block_e": block_e.astype(i32), "n_used": (end8[-1:] // blk8).astype(i32),
    }


def _w_in_sections(w):
    d_dt = SSD_WIDTH + XBC_WIDTH
    d_q = d_dt + SSD_HEADS
    d_f = d_q + 3 * ATTN_WIDTH
    d_ga = d_f + ATTN_HEADS
    small = jnp.concatenate([w[..., d_dt:d_q], w[..., d_f:d_ga],
                             jnp.zeros(w.shape[:-1] + (LANES - SSD_HEADS - ATTN_HEADS,), w.dtype)], axis=-1)
    sections = [w[..., SSD_WIDTH:d_dt], w[..., :SSD_WIDTH], w[..., d_q:d_f], w[..., d_ga:], small]
    return [sec.astype(BF16) for sec in sections]


def kernel(x, c, ada_w, ada_b, norm_mix_g, w_in, ssd_conv_w, ssd_conv_b, ssd_dt_bias, ssd_a_log,
           ssd_d, ssd_norm_g, fox_f_bias, fox_norm_g, cm_conv_w, cm_conv_b, cm_ln_g, cm_ln_b, w_out,
           norm_ffn_g, w_router_group, b_router_group, w_router_expert, b_router_expert, w_gate,
           w_up, w_down, final_norm_g):
    bsz, seq, d = x.shape
    m = bsz * seq
    tm = min(512, seq)
    tt = min(512, seq)
    n_blocks = -(-(2 * m + (CHUNK - 1) * (m // tm) * N_EXPERTS) // MOE_BLOCK) + N_EXPERTS

    mod = _modulation(c, ada_w, ada_b)
    x2 = x.reshape(m, d)
    w_in_sections = _w_in_sections(w_in)
    for l in range(DEPTH):
        sh_m, sc_m, g_m, sh_f, sc_f, g_f = [v.reshape(bsz, 1, d) for v in jnp.split(mod[l], 6, axis=-1)]
        a_m = norm_mix_g[l][None, None, :] * (1.0 + sc_m)
        proj = _inproj(x2, a_m, sh_m, w_in_sections, l, seq, tm)
        y_ssd = _ssd(proj, ssd_conv_w[l], ssd_conv_b[l], ssd_dt_bias[l], ssd_a_log[l],
                     ssd_d[l], ssd_norm_g[l], bsz, seq, tt)
        qa, ka, va, aux, stat = _foxprep(proj, fox_f_bias[l], bsz, seq)
        o_att = _fox(qa, ka, va, aux, stat)
        y_cnv = _conformer(proj, cm_conv_w[l], cm_conv_b[l], cm_ln_g[l], cm_ln_b[l], bsz, seq, tt)

        w_r = jnp.concatenate([w_router_group[l], w_router_expert[l],
                               jnp.zeros((d, LANES - N_GROUPS - N_EXPERTS), F32)], axis=1)
        w_rh = w_r.astype(BF16)
        w_r2 = jnp.concatenate([w_rh, (w_r - w_rh.astype(F32)).astype(BF16)], axis=1)
        b_r = jnp.concatenate([b_router_group[l], b_router_expert[l],
                               jnp.zeros((LANES - N_GROUPS - N_EXPERTS,), F32)]).reshape(1, LANES)
        a_f = norm_ffn_g[l][None, None, :] * (1.0 + sc_f)
        x2, h2, rt, rtt, cnt = _outproj(x2, y_ssd, o_att, y_cnv, w_out[l].astype(BF16),
                                        fox_norm_g[l].reshape(1, -1), g_m, a_f, sh_f, w_r2, b_r,
                                        seq, tm)
        plan = _moe_plan(cnt[:, 0, :N_EXPERTS].astype(jnp.int32), n_blocks)
        xs = _dispatch(plan, h2, rtt, n_blocks * MOE_BLOCK, tm)
        ys = _experts(plan["block_e"], plan["n_used"], xs, w_gate, w_up, w_down, l)
        x2 = _combine(plan, x2, ys, rt, g_f, final_norm_g.reshape(1, d), seq, tm,
                      final=(l == DEPTH - 1))
    return x2.reshape(bsz, seq, d)
```

```python
import functools

import jax
import jax.numpy as jnp
import numpy as np
from jax import lax
from jax.experimental import pallas as pl
from jax.experimental.pallas import tpu as pltpu

F32 = jnp.float32
BF16 = jnp.bfloat16
HIGHEST = lax.Precision.HIGHEST

D_MODEL = 1024
DEPTH = 4
SSD_WIDTH = 512
SSD_HEADS = 8
SSD_HEAD_DIM = 64
SSD_STATE = 128
SSD_CONV = 4
SSD_CHUNK = 128
XBC_WIDTH = 1024
ATTN_WIDTH = 256
ATTN_HEADS = 4
ATTN_HEAD_DIM = 64
CONV_WIDTH = 256
CONV_KERNEL = 31
N_GROUPS = 4
PER_GROUP = 8
N_EXPERTS = 32
D_EXPERT = 512
MOE_BLOCK = 512
EPS = 1e-6

LANES = 128
SUBLANES = 8
COL_XBC = 0
COL_Z = 1024
COL_QKV = 1536
COL_GA = 2304
COL_GB = 2560
COL_SM = 2816
NP = 2944
SM_F = 8

VMEM_LIMIT = 56 * 1024 * 1024


def _cparams(sem):
    return pltpu.CompilerParams(dimension_semantics=sem, vmem_limit_bytes=VMEM_LIMIT)


def _sigmoid(x):
    return 1.0 / (1.0 + jnp.exp(-x))


def _silu(x):
    return x * _sigmoid(x)


def _softplus(x):
    return jnp.maximum(x, 0.0) + jnp.log1p(jnp.exp(-jnp.abs(x)))


def _iota(shape, dim):
    return lax.broadcasted_iota(jnp.int32, shape, dim)


def _split_bf16x3(x):
    hi = x.astype(BF16)
    r1 = x - hi.astype(F32)
    mid = r1.astype(BF16)
    lo = (r1 - mid.astype(F32)).astype(BF16)
    return hi, mid, lo


def _dot_onehot_rhs(x, sel):
    return sum(jnp.dot(part, sel, preferred_element_type=F32) for part in _split_bf16x3(x))


def _dot_onehot_lhs(sel, x):
    return sum(jnp.dot(sel, part, preferred_element_type=F32) for part in _split_bf16x3(x))


def _mod_kernel(c_ref, w_ref, b_ref, o_ref):
    cond = _silu(c_ref[...])
    o_ref[0] = jnp.dot(cond, w_ref[0], precision=HIGHEST, preferred_element_type=F32) + b_ref[0]


def _modulation(c, ada_w, ada_b):
    bsz = c.shape[0]
    rows = 8
    cpad = jnp.zeros((rows, D_MODEL), F32).at[:bsz].set(c)
    tn = 1536
    n6 = 6 * D_MODEL
    out = pl.pallas_call(
        _mod_kernel,
        grid=(DEPTH, n6 // tn),
        in_specs=[
            pl.BlockSpec((rows, D_MODEL), lambda l, j: (0, 0)),
            pl.BlockSpec((1, D_MODEL, tn), lambda l, j: (l, 0, j)),
            pl.BlockSpec((1, 1, tn), lambda l, j: (l, 0, j)),
        ],
        out_specs=pl.BlockSpec((1, rows, tn), lambda l, j: (l, 0, j)),
        out_shape=jax.ShapeDtypeStruct((DEPTH, rows, n6), F32),
        compiler_params=_cparams(("arbitrary", "arbitrary")),
        name="adaln_mod",
    )(cpad, ada_w, ada_b.reshape(DEPTH, 1, n6))
    return out[:, :bsz]


def _inproj_kernel(x_ref, a_ref, s_ref, *refs):
    w_refs, o_ref = refs[:-1], refs[-1]
    x = x_ref[...]
    ms = jnp.mean(x * x, axis=-1, keepdims=True)
    h = (x * lax.rsqrt(ms + EPS) * a_ref[0] + s_ref[0]).astype(BF16)
    col = 0
    for w_ref in w_refs:
        width = w_ref.shape[-1]
        o_ref[:, col:col + width] = jnp.dot(h, w_ref[0], preferred_element_type=F32)
        col += width


def _inproj(x2, a, s, w_sections, layer, seq, tm):
    m = x2.shape[0]
    per_b = seq // tm
    assert sum(w.shape[-1] for w in w_sections) == NP
    return pl.pallas_call(
        _inproj_kernel,
        grid=(m // tm,),
        in_specs=[
            pl.BlockSpec((tm, D_MODEL), lambda i: (i, 0)),
            pl.BlockSpec((1, 1, D_MODEL), lambda i: (i // per_b, 0, 0)),
            pl.BlockSpec((1, 1, D_MODEL), lambda i: (i // per_b, 0, 0)),
        ] + [pl.BlockSpec((1, D_MODEL, w.shape[-1]), lambda i: (layer, 0, 0)) for w in w_sections],
        out_specs=pl.BlockSpec((tm, NP), lambda i: (i, 0)),
        out_shape=jax.ShapeDtypeStruct((m, NP), F32),
        compiler_params=_cparams(("arbitrary",)),
        name="in_proj",
    )(x2, a, s, *w_sections)


def _ssd_kernel(z_ref, xbc_ref, sm_ref, cw_ref, cb_ref, dtb_ref, alog_ref, dtbt_ref,
                alogt_ref, e_ref, dx_ref, ng_ref, y_ref, xpad, xc, prev, *, tt):
    t = pl.program_id(1)

    @pl.when(t == 0)
    def _():
        xpad[0:8, :] = jnp.zeros((8, XBC_WIDTH), F32)
        prev[...] = jnp.zeros(prev.shape, F32)

    xpad[8:8 + tt, :] = xbc_ref[...]
    acc = jnp.broadcast_to(cb_ref[...], (tt, XBC_WIDTH))
    for k in range(SSD_CONV):
        off = 8 - (SSD_CONV - 1) + k
        acc = acc + cw_ref[k:k + 1, :] * xpad[off:off + tt, :]
    xc[...] = _silu(acc)
    xpad[0:8, :] = xpad[tt:tt + 8, :]

    cl = SSD_CHUNK
    row = _iota((cl, cl), 0)
    col = _iota((cl, cl), 1)
    causal = row >= col
    tril = jnp.where(causal, 1.0, 0.0).astype(BF16)
    triu = jnp.where(row <= col, 1.0, 0.0).astype(BF16)
    lo = col < SSD_HEAD_DIM
    lane1 = _iota((1, LANES), 1)
    a_row = jnp.where(lane1 < SSD_HEADS, -jnp.exp(alog_ref[...]), 0.0)
    a_col = -jnp.exp(alogt_ref[...])
    expand = e_ref[...]

    def chunk(c, carry):
        r0 = c * cl
        xs = xc[pl.ds(r0, cl), 0:SSD_WIDTH]
        bmat = xc[pl.ds(r0, cl), SSD_WIDTH:SSD_WIDTH + 2 * SSD_STATE]
        cmat = xc[pl.ds(r0, cl), SSD_WIDTH + 2 * SSD_STATE:XBC_WIDTH]
        sm = sm_ref[pl.ds(r0, cl), :]
        dt = _softplus(sm + dtb_ref[...])
        da = dt * a_row
        acs = _dot_onehot_lhs(tril, da)
        dtt = _softplus(sm.T[0:SSD_HEADS, :] + dtbt_ref[...])
        acst = _dot_onehot_rhs(dtt * a_col, triu)
        dt_x = _dot_onehot_rhs(dt, expand)
        acs_x = _dot_onehot_rhs(acs, expand)
        last = acs_x[cl - 1:cl, :]
        eacs_x = jnp.exp(acs_x)
        dte_x = jnp.exp(last - acs_x)
        cd_x = jnp.exp(last)
        xdt = xs * dt_x
        xdte = (xdt * dte_x).astype(BF16)
        zc = z_ref[pl.ds(r0, cl), :]
        for g in range(2):
            bg = bmat[:, g * SSD_STATE:(g + 1) * SSD_STATE]
            cg = cmat[:, g * SSD_STATE:(g + 1) * SSD_STATE].astype(BF16)
            bgt = bg.T.astype(BF16)
            cbm = jnp.dot(cg, bgt, preferred_element_type=F32)
            gs = slice(g * 2 * LANES, (g + 1) * 2 * LANES)
            prev_g = prev[:, gs]
            yo_g = jnp.dot(cg, prev_g.astype(BF16), preferred_element_type=F32) * eacs_x[:, gs]
            prev[:, gs] = prev_g * cd_x[:, gs] + jnp.dot(bgt, xdte[:, gs], preferred_element_type=F32)
            pair_out = []
            for j in range(2):
                p = 2 * g + j
                sl = slice(p * LANES, (p + 1) * LANES)
                xp = xdt[:, sl]
                gms, xms = [], []
                for half in range(2):
                    h = 2 * p + half
                    seg = acs[:, h:h + 1] - acst[h:h + 1, :]
                    dec = jnp.exp(jnp.where(causal, seg, -jnp.inf))
                    gms.append((cbm * dec).astype(BF16))
                    own = lo if half == 0 else jnp.logical_not(lo)
                    xms.append(jnp.where(own, xp, 0.0).astype(BF16))
                yd = jnp.dot(jnp.concatenate(gms, axis=1), jnp.concatenate(xms, axis=0),
                             preferred_element_type=F32)
                pair_out.append(yd + yo_g[:, j * LANES:(j + 1) * LANES] + xs[:, sl] * dx_ref[:, sl])
            yg =jnp.concatenate(pair_out, axis=-1) * _silu(zc[:, gs])
            ms = jnp.mean(yg * yg, axis=-1, keepdims=True)
            y_ref[pl.ds(r0, cl), gs] = yg * lax.rsqrt(ms + EPS) * ng_ref[:, gs]
        return carry

    for c in range(tt // cl):
        chunk(c, 0)


def _ssd(proj, cw, cb, dtb, alog, dx, ng, bsz, seq, tt):
    m = bsz * seq
    nt = seq // tt
    pad = LANES - SSD_HEADS
    dtb_row = jnp.pad(dtb, (0, pad)).reshape(1, LANES)
    alog_row = jnp.pad(alog, (0, pad)).reshape(1, LANES)
    dtb_col = jnp.broadcast_to(dtb[:, None], (SSD_HEADS, SSD_CHUNK))
    alog_col = jnp.broadcast_to(alog[:, None], (SSD_HEADS, SSD_CHUNK))
    expand = np.zeros((LANES, SSD_WIDTH), np.float32)
    for h in range(SSD_HEADS):
        expand[h, h * SSD_HEAD_DIM:(h + 1) * SSD_HEAD_DIM] = 1.0
    dx_row = jnp.repeat(dx, SSD_HEAD_DIM).reshape(1, SSD_WIDTH)
    full = lambda shape: pl.BlockSpec(shape, lambda b, t: (0,) * len(shape))
    return pl.pallas_call(
        functools.partial(_ssd_kernel, tt=tt),
        grid=(bsz, nt),
        in_specs=[
            pl.BlockSpec((tt, SSD_WIDTH), lambda b, t: (b * nt + t, COL_Z // SSD_WIDTH)),
            pl.BlockSpec((tt, XBC_WIDTH), lambda b, t: (b * nt + t, COL_XBC // XBC_WIDTH + 0)),
            pl.BlockSpec((tt, LANES), lambda b, t: (b * nt + t, COL_SM // LANES)),
            full((SSD_CONV, XBC_WIDTH)),
            full((1, XBC_WIDTH)),
            full((1, LANES)),
            full((1, LANES)),
            full((SSD_HEADS, SSD_CHUNK)),
            full((SSD_HEADS, SSD_CHUNK)),
            full((LANES, SSD_WIDTH)),
            full((1, SSD_WIDTH)),
            full((1, SSD_WIDTH)),
        ],
        out_specs=pl.BlockSpec((tt, SSD_WIDTH), lambda b, t: (b * nt + t, 0)),
        out_shape=jax.ShapeDtypeStruct((m, SSD_WIDTH), F32),
        scratch_shapes=[
            pltpu.VMEM((tt + 8, XBC_WIDTH), F32),
            pltpu.VMEM((tt, XBC_WIDTH), F32),
            pltpu.VMEM((SSD_STATE, SSD_WIDTH), F32),
        ],
        compiler_params=_cparams(("arbitrary", "arbitrary")),
        name="ssd_scan",
    )(proj, proj, proj, cw, cb.reshape(1, XBC_WIDTH), dtb_row, alog_row, dtb_col, alog_col,
      jnp.asarray(expand, dtype=BF16), dx_row, ng.reshape(1, SSD_WIDTH))


LOG2E = 1.4426950408889634
ATT_BLOCK = 256
ATT_TQ = 512
PRUNE_LOG2 = 128.0
FAST_GAP_LOG2 = 60.0
SHIFT_MARGIN_LOG2 = 1.0
AUX_CUMEND = 0
AUX_KMAX = 1
ST_QNORM, ST_CUM, ST_SELF = 0, 1, 2


def _foxprep_kernel(qkv_ref, sm_ref, fb_ref, q_out, k_out, v_out, aux_out, stat_out, carry, *, tt):
    t = pl.program_id(1)

    @pl.when(t == 0)
    def _():
        carry[...] = jnp.zeros(carry.shape, F32)
        aux_out[...] = jnp.zeros(aux_out.shape, F32)

    logit = sm_ref[...] + fb_ref[...]
    logf = -_softplus(-logit)
    row = _iota((tt, tt), 0)
    col = _iota((tt, tt), 1)
    tril = jnp.where(row >= col, 1.0, 0.0).astype(BF16)
    cum = _dot_onehot_lhs(tril, logf) + carry[...]
    carry[...] = cum[tt - 1:tt, :]

    lane = _iota((tt, LANES), 1)
    lane1 = _iota((1, LANES), 1)
    scale = ATTN_HEAD_DIM ** -0.5 * LOG2E
    for h in range(ATTN_HEADS):
        pair, half = h // 2, h % 2
        own = (lane < ATTN_HEAD_DIM) if half == 0 else (lane >= ATTN_HEAD_DIM)
        a0 = ATTN_HEAD_DIM * (1 - half)
        cs = jnp.broadcast_to(cum[:, SM_F + h:SM_F + h + 1], (tt, LANES)) * LOG2E
        hi = cs.astype(BF16).astype(F32)
        r1 = cs - hi
        mid = r1.astype(BF16).astype(F32)
        low = r1 - mid
        qp = qkv_ref[:, pair * LANES:(pair + 1) * LANES]
        kp = qkv_ref[:, ATTN_WIDTH + pair * LANES:ATTN_WIDTH + (pair + 1) * LANES]
        vp = qkv_ref[:, 2 * ATTN_WIDTH + pair * LANES:2 * ATTN_WIDTH + (pair + 1) * LANES]
        qa = jnp.where(lane == a0, hi, jnp.where(lane == a0 + 1, mid, jnp.where(
            lane == a0 + 2, low, jnp.where((lane >= a0 + 3) & (lane < a0 + 6), 1.0, 0.0))))
        ka = jnp.where(lane == a0 + 3, -hi, jnp.where(lane == a0 + 4, -mid, jnp.where(
            lane == a0 + 5, -low, jnp.where((lane >= a0) & (lane < a0 + 9), 1.0, 0.0))))
        qb = jnp.where(own, qp * scale, qa).astype(BF16)
        q_out[0, h] = qb
        kb = jnp.where(own, kp, ka).astype(BF16)
        k_out[0, h] = kb
        qf = jnp.where(own, qb.astype(F32), 0.0)
        qn = jnp.sqrt(jnp.sum(qf * qf, axis=-1, keepdims=True))
        s_self = jnp.sum(qf * kb.astype(F32), axis=-1, keepdims=True)
        stat = jnp.where(lane == ST_QNORM, qn, jnp.where(lane == ST_CUM, hi + mid + low, jnp.where(
            lane == ST_SELF, s_self, 0.0)))
        stat_out[0, h] = stat.T[0:8, :]
        v_out[0, h] = jnp.where(own, vp, jnp.where(lane == a0, 1.0, 0.0)).astype(BF16)
        kf = jnp.where(own, kb.astype(F32), 0.0)
        kn2_rows = jnp.sum(kf * kf, axis=-1, keepdims=True)
        for blk in range(tt // ATT_BLOCK):
            end = (blk + 1) * ATT_BLOCK
            kn2 = jnp.max(kn2_rows[end - ATT_BLOCK:end, :], axis=0, keepdims=True)
            here = lane1 == t * (tt // ATT_BLOCK) + blk
            aux_out[0, h, AUX_CUMEND:AUX_CUMEND + 1, :] = jnp.where(
                here, cs[end - 1:end, :], aux_out[0, h, AUX_CUMEND:AUX_CUMEND + 1, :])
            aux_out[0, h, AUX_KMAX:AUX_KMAX + 1, :] = jnp.where(
                here, jnp.sqrt(kn2), aux_out[0, h, AUX_KMAX:AUX_KMAX + 1, :])


def _foxprep(proj, fb, bsz, seq):
    tt = min(ATT_TQ, seq)
    nt = seq // tt
    assert tt % ATT_BLOCK == 0 and seq // ATT_BLOCK <= LANES
    fb_row = jnp.zeros((1, LANES), F32).at[0, SM_F:SM_F + ATTN_HEADS].set(fb)
    shp = jax.ShapeDtypeStruct((bsz, ATTN_HEADS, seq, LANES), BF16)
    ospec = pl.BlockSpec((1, ATTN_HEADS, tt, LANES), lambda b, t: (b, 0, t, 0))
    return pl.pallas_call(
        functools.partial(_foxprep_kernel, tt=tt),
        grid=(bsz, nt),
        in_specs=[
            pl.BlockSpec((tt, 3 * ATTN_WIDTH), lambda b, t: (b * nt + t, COL_QKV // (3 * ATTN_WIDTH))),
            pl.BlockSpec((tt, LANES), lambda b, t: (b * nt + t, COL_SM // LANES)),
            pl.BlockSpec((1, LANES), lambda b, t: (0, 0)),
        ],
        out_specs=[ospec, ospec, ospec,
                   pl.BlockSpec((1, ATTN_HEADS, 8, LANES), lambda b, t: (b, 0, 0, 0)),
                   pl.BlockSpec((1, ATTN_HEADS, 8, tt), lambda b, t: (b, 0, 0, t))],
        out_shape=[shp, shp, shp, jax.ShapeDtypeStruct((bsz, ATTN_HEADS, 8, LANES), F32),
                   jax.ShapeDtypeStruct((bsz, ATTN_HEADS, 8, seq), F32)],
        scratch_shapes=[pltpu.VMEM((1, LANES), F32)],
        compiler_params=_cparams(("arbitrary", "arbitrary")),
        name="fox_prep",
    )(proj, proj, fb_row)


def _fox_kernel(q_ref, k_ref, v_ref, aux_ref, stat_ref, o_ref, *, tq, kb):
    i = pl.program_id(2)
    ib = i * (tq // kb)
    nt = (((1,), (1,)), ((), ()))
    lane = _iota((tq, LANES), 1)
    lane1 = _iota((1, LANES), 1)
    qs = (q_ref[0, 0], q_ref[0, 1])

    def step(hh, r0, width, m, acc, mask):
        kblk = k_ref[0, hh, pl.ds(r0, width), :]
        vblk = v_ref[0, hh, pl.ds(r0, width), :]
        s = lax.dot_general(qs[hh], kblk, nt, preferred_element_type=F32)
        if mask is not None:
            s = jnp.where(mask, s, -jnp.inf)
        m_new = jnp.maximum(m, jnp.max(s, axis=-1, keepdims=True))
        p = jnp.exp2(s - m_new)
        alpha = jnp.exp2(m - m_new)
        acc = acc * alpha + jnp.dot(p.astype(BF16), vblk, preferred_element_type=F32)
        return m_new, acc

    d0 = pl.multiple_of(i * tq, tq)
    diag = _iota((tq, tq), 0) >= _iota((tq, tq), 1)
    ib_f = ib.astype(F32)
    first = []
    gaps = []
    q_shift = []
    for hh in range(2):
        a0 = ATTN_HEAD_DIM * (1 - hh)
        st = stat_ref[0, hh]
        qn = st[ST_QNORM:ST_QNORM + 1, :]
        cum_t = st[ST_CUM:ST_CUM + 1, :]
        s_self = st[ST_SELF:ST_SELF + 1, :]
        aux = aux_ref[0, hh]
        cend = aux[AUX_CUMEND:AUX_CUMEND + 1, :]
        kmax = jnp.max(jnp.where(lane1 < ib + tq // kb, aux[AUX_KMAX:AUX_KMAX + 1, :], 0.0), axis=-1,
                       keepdims=True)
        bound = qn * kmax
        slack = jnp.max(bound + cum_t - s_self, axis=-1, keepdims=True)
        live = (lane1 < ib) & (slack - cend > -PRUNE_LOG2)
        first.append(jnp.min(jnp.where(live, lane1.astype(F32), ib_f)))
        parts = _split_bf16x3(bound + SHIFT_MARGIN_LOG2)
        gaps.append(jnp.max(sum(part.astype(F32) for part in parts) - s_self))
        terms = jnp.concatenate([-part.astype(F32) for part in parts] + [jnp.zeros((13, tq), F32)],
                                axis=0).astype(BF16)
        place = jnp.where(_iota((16, LANES), 1) == _iota((16, LANES), 0) + (a0 + 6), 1.0,
                          0.0).astype(BF16)
        q_shift.append(qs[hh] + lax.dot_general(terms, place, (((0,), (0,)), ((), ())),
                                                preferred_element_type=F32).astype(BF16))
    j_first = jnp.minimum(first[0], first[1]).astype(jnp.int32)

    def fast(_):
        def blocks(c, r0, width, mask=None):
            out = []
            for hh in range(2):
                s = lax.dot_general(q_shift[hh], k_ref[0, hh, pl.ds(r0, width), :], nt,
                                    preferred_element_type=F32)
                if mask is not None:
                    s = jnp.where(mask, s, -jnp.inf)
                out.append(c[hh] + jnp.dot(jnp.exp2(s).astype(BF16), v_ref[0, hh, pl.ds(r0, width), :],
                                           preferred_element_type=F32))
            return tuple(out)

        zero = jnp.zeros((tq, LANES), F32)
        c = blocks((zero, zero), d0, tq, diag)
        j_lo = j_first - ((ib - j_first) & 1)
        n = ib - j_lo
        c = lax.cond((n & 2) == 2, lambda c: blocks(c, pl.multiple_of(j_lo * kb, kb), 2 * kb),
                     lambda c: c, c)
        j0 = j_lo + (n & 2)

        def body(jj, c):
            return blocks(c, pl.multiple_of((j0 + 4 * jj) * kb, kb), 4 * kb)
        return lax.fori_loop(0, lax.shift_right_logical(n, 2), body, c)

    def online(_):
        m_init = jnp.full((tq, 1), -jnp.inf, F32)
        acc_init = jnp.zeros((tq, LANES), F32)
        carry = step(0, d0, tq, m_init, acc_init, diag) + step(1, d0, tq, m_init, acc_init, diag)

        def body(j, c):
            r0 = pl.multiple_of(j * kb, kb)
            m0, a0_, m1, a1_ = c
            m0, a0_ = step(0, r0, kb, m0, a0_, None)
            m1, a1_ = step(1, r0, kb, m1, a1_, None)
            return m0, a0_, m1, a1_
        _, a0_, _, a1_ = lax.fori_loop(j_first, ib, body, carry)
        return a0_, a1_

    acc0, acc1 = lax.cond(jnp.maximum(gaps[0], gaps[1]) <= FAST_GAP_LOG2, fast, online, None)
    den0 = jnp.sum(jnp.where(lane == ATTN_HEAD_DIM, acc0, 0.0), axis=-1, keepdims=True)
    den1 = jnp.sum(jnp.where(lane == 0, acc1, 0.0), axis=-1, keepdims=True)
    o_ref[...] = jnp.where(lane < ATTN_HEAD_DIM, acc0 / den0, acc1 / den1)


def _fox(qa, ka, va, aux, stat):
    bsz, nh, seq, _ = qa.shape
    tq = min(ATT_TQ, seq)
    assert (tq // ATT_BLOCK) % 2 == 0
    nq = seq // tq
    kv_spec = pl.BlockSpec((1, 2, seq, LANES), lambda b, p, i: (b, p, 0, 0))
    return pl.pallas_call(
        functools.partial(_fox_kernel, tq=tq, kb=ATT_BLOCK),
        grid=(bsz, nh // 2, nq),
        in_specs=[pl.BlockSpec((1, 2, tq, LANES), lambda b, p, i: (b, p, i, 0)), kv_spec, kv_spec,
                  pl.BlockSpec((1, 2, 8, LANES), lambda b, p, i: (b, p, 0, 0)),
                  pl.BlockSpec((1, 2, 8, tq), lambda b, p, i: (b, p, 0, i))],
        out_specs=pl.BlockSpec((tq, LANES), lambda b, p, i: (b * nq + i, p)),
        out_shape=jax.ShapeDtypeStruct((bsz * seq, ATTN_WIDTH), F32),
        compiler_params=_cparams(("arbitrary", "arbitrary", "arbitrary")),
        name="fox_attn",
    )(qa, ka, va, aux, stat)


CONF_HALO = 32
CONF_ROWS = 64


def _conf_kernel(ga_ref, gb_ref, w_ref, b_ref, lg_ref, lb_ref, y_ref, upad, ush, *, tt):
    t = pl.program_id(1)

    @pl.when(t == 0)
    def _():
        upad[0:CONF_HALO, :] = jnp.zeros((CONF_HALO, CONV_WIDTH), F32)

    upad[CONF_HALO:CONF_HALO + tt, :] = ga_ref[...] * _sigmoid(gb_ref[...])
    base = CONF_HALO - (CONV_KERNEL - 1)
    for ph in range(1, SUBLANES):
        ush[ph - 1] = upad[ph:ph + ush.shape[1], :]
    for r in range(tt // CONF_ROWS):
        r0 = r * CONF_ROWS
        acc = jnp.broadcast_to(b_ref[...], (CONF_ROWS, CONV_WIDTH))
        for k in range(CONV_KERNEL):
            ph = (base + k) % SUBLANES
            lo = r0 + base + k - ph
            src = upad[lo:lo + CONF_ROWS, :] if ph == 0 else ush[ph - 1, lo:lo + CONF_ROWS, :]
            acc = acc + w_ref[k:k + 1, :] * src
        mu = jnp.mean(acc, axis=-1, keepdims=True)
        cen = acc - mu
        var = jnp.mean(cen * cen, axis=-1, keepdims=True)
        y = cen * lax.rsqrt(var + EPS) * lg_ref[...] + lb_ref[...]
        y_ref[r0:r0 + CONF_ROWS, :] = _silu(y)
    upad[0:CONF_HALO, :] = upad[tt:tt + CONF_HALO, :]


def _conformer(proj, w, b, lg, lb, bsz, seq, tt):
    m = bsz * seq
    nt = seq // tt
    full = lambda shape: pl.BlockSpec(shape, lambda bb, t: (0,) * len(shape))
    return pl.pallas_call(
        functools.partial(_conf_kernel, tt=tt),
        grid=(bsz, nt),
        in_specs=[
            pl.BlockSpec((tt, CONV_WIDTH), lambda bb, t: (bb * nt + t, COL_GA // CONV_WIDTH)),
            pl.BlockSpec((tt, CONV_WIDTH), lambda bb, t: (bb * nt + t, COL_GB // CONV_WIDTH)),
            full((CONV_KERNEL, CONV_WIDTH)),
            full((1, CONV_WIDTH)),
            full((1, CONV_WIDTH)),
            full((1, CONV_WIDTH)),
        ],
        out_specs=pl.BlockSpec((tt, CONV_WIDTH), lambda bb, t: (bb * nt + t, 0)),
        out_shape=jax.ShapeDtypeStruct((m, CONV_WIDTH), F32),
        scratch_shapes=[pltpu.VMEM((tt + CONF_HALO, CONV_WIDTH), F32),
                        pltpu.VMEM((SUBLANES - 1, tt + CONF_HALO - SUBLANES, CONV_WIDTH), F32)],
        compiler_params=_cparams(("arbitrary", "arbitrary")),
        name="conformer_conv",
    )(proj, proj, w, b.reshape(1, -1), lg.reshape(1, -1), lb.reshape(1, -1))


ROUTE_BIG = 1e9
CHUNK = 8
XS_FEAT = D_MODEL // 2
XS_WIDTH = XS_FEAT + LANES
RT_W1, RT_W2, RT_ROW1, RT_ROW2 = 0, 1, 2, 3


def _split3(x):
    hi = x.astype(BF16)
    lo = (x - hi.astype(F32)).astype(BF16)
    return hi, lo


def _outproj_kernel(x_ref, ys_ref, oa_ref, yc_ref, wo_ref, fg_ref, gm_ref, a2_ref, s2_ref,
                    wr_ref, br_ref, xn_ref, h2_ref, rt_ref, rtt_ref, cnt_ref):
    tm = x_ref.shape[0]
    lane = _iota((tm, LANES), 1)
    att = oa_ref[...]
    ms = jnp.mean(att * att, axis=-1, keepdims=True)
    att = att * lax.rsqrt(ms + EPS) * fg_ref[...]
    y = jnp.dot(ys_ref[...].astype(BF16), wo_ref[0:SSD_WIDTH, :], preferred_element_type=F32)
    y = y + jnp.dot(att.astype(BF16), wo_ref[SSD_WIDTH:SSD_WIDTH + ATTN_WIDTH, :],
                    preferred_element_type=F32)
    y = y + jnp.dot(yc_ref[...].astype(BF16), wo_ref[SSD_WIDTH + ATTN_WIDTH:, :],
                    preferred_element_type=F32)
    xn = x_ref[...] + gm_ref[0] * y
    xn_ref[...] = xn
    ms2 = jnp.mean(xn * xn, axis=-1, keepdims=True)
    h2 = xn * lax.rsqrt(ms2 + EPS) * a2_ref[0] + s2_ref[0]
    h2_ref[...] = h2.astype(BF16)

    hh, hl = _split3(h2)
    part = jnp.dot(jnp.concatenate([hh, hl], axis=0), wr_ref[...], preferred_element_type=F32)
    logits = (part[:tm, :LANES] + part[:tm, LANES:] + part[tm:, :LANES] + part[tm:, LANES:]) + br_ref[...]
    lanef = lane.astype(F32)
    lg = jnp.where(lane < N_GROUPS, logits, -jnp.inf)
    gmax = jnp.max(lg, axis=-1, keepdims=True)
    gsum = jnp.sum(jnp.exp(lg - gmax), axis=-1, keepdims=True)
    gidx = jnp.min(jnp.where(lg == gmax, lanef, ROUTE_BIG), axis=-1, keepdims=True)
    e_lo = N_GROUPS + PER_GROUP * gidx
    le = jnp.where((lanef >= e_lo) & (lanef < e_lo + PER_GROUP), logits, -jnp.inf)
    m1 = jnp.max(le, axis=-1, keepdims=True)
    i1 = jnp.min(jnp.where(le == m1, lanef, ROUTE_BIG), axis=-1, keepdims=True)
    le2 = jnp.where(lanef == i1, -jnp.inf, le)
    m2 = jnp.max(le2, axis=-1, keepdims=True)
    i2 = jnp.min(jnp.where(le2 == m2, lanef, ROUTE_BIG), axis=-1, keepdims=True)
    esum = jnp.sum(jnp.exp(le - m1), axis=-1, keepdims=True)
    p1 = 1.0 / esum
    p2 = jnp.exp(m2 - m1) / esum
    psel = 1.0 / gsum
    w1 = p1 / (p1 + p2) * psel
    w2 = p2 / (p1 + p2) * psel

    oh1 = (lanef == (i1 - N_GROUPS)).astype(F32)
    oh2 = (lanef == (i2 - N_GROUPS)).astype(F32)
    oh = oh1 + oh2
    cnt = jnp.sum(oh, axis=0, keepdims=True)
    before = (_iota((tm, tm), 0) > _iota((tm, tm), 1)).astype(BF16)
    rank = jnp.dot(before, oh.astype(BF16), preferred_element_type=F32)
    chunks = jnp.floor((cnt + (CHUNK - 1)) * (1.0 / CHUNK))
    below = (_iota((LANES, LANES), 0) < _iota((LANES, LANES), 1)).astype(BF16)
    seg_lo = CHUNK * jnp.dot(jnp.broadcast_to(chunks, (8, LANES)).astype(BF16), below,
                             preferred_element_type=F32)[0:1, :]
    lr1 = jnp.sum(oh1 * (rank + seg_lo), axis=-1, keepdims=True)
    lr2 = jnp.sum(oh2 * (rank + seg_lo), axis=-1, keepdims=True)
    route = jnp.where(lane == RT_W1, w1, jnp.where(lane == RT_W2, w2, jnp.where(
        lane == RT_ROW1, lr1, jnp.where(lane == RT_ROW2, lr2, 0.0))))
    rt_ref[...] = route
    rtt_ref[0] = route.T[0:8, :]
    cnt_ref[0] = jnp.broadcast_to(cnt, (8, LANES))


def _outproj(x2, ys, oa, yc, wo, fg, gm, a2, s2, wr, br, seq, tm):
    m = x2.shape[0]
    per_b = seq // tm
    nt = m // tm
    row = lambda w: pl.BlockSpec((tm, w), lambda i: (i, 0))
    bvec = pl.BlockSpec((1, 1, D_MODEL), lambda i: (i // per_b, 0, 0))
    full = lambda shape: pl.BlockSpec(shape, lambda i: (0,) * len(shape))
    return pl.pallas_call(
        _outproj_kernel,
        grid=(nt,),
        in_specs=[
            row(D_MODEL), row(SSD_WIDTH), row(ATTN_WIDTH), row(CONV_WIDTH),
            full((D_MODEL, D_MODEL)), full((1, ATTN_WIDTH)), bvec, bvec, bvec,
            full((D_MODEL, 2 * LANES)), full((1, LANES)),
        ],
        out_specs=[row(D_MODEL), row(D_MODEL), row(LANES),
                   pl.BlockSpec((1, 8, tm), lambda i: (i, 0, 0)),
                   pl.BlockSpec((1, 8, LANES), lambda i: (i, 0, 0))],
        out_shape=[jax.ShapeDtypeStruct((m, D_MODEL), F32), jax.ShapeDtypeStruct((m, D_MODEL), BF16),
                   jax.ShapeDtypeStruct((m, LANES), F32), jax.ShapeDtypeStruct((nt, 8, tm), F32),
                   jax.ShapeDtypeStruct((nt, 8, LANES), F32)],
        compiler_params=_cparams(("arbitrary",)),
        name="out_proj_router",
    )(x2, ys, oa, yc, wo, fg, gm, a2, s2, wr, br)


def _local_rows(tm):
    return -(-(2 * tm + (CHUNK - 1) * N_EXPERTS) // LANES) * LANES


def _pack_halves(x):
    w = x.shape[-1] // 2
    return pltpu.bitcast(x[:, :w], jnp.uint32) | lax.shift_right_logical(
        pltpu.bitcast(x[:, w:], jnp.uint32), jnp.uint32(16))


def _unpack_halves(p):
    hi = pltpu.bitcast(p & jnp.uint32(0xFFFF0000), F32)
    lo = pltpu.bitcast(lax.shift_left(p, jnp.uint32(16)), F32)
    return jnp.concatenate([hi, lo], axis=-1).astype(BF16)


def _wait_chunks(n, n_max, copy_of_size):
    for bit in range(n_max.bit_length()):
        @pl.when((lax.shift_right_logical(n, bit) & 1) == 1)
        def _():
            copy_of_size(CHUNK << bit).wait()


def _dispatch_kernel(lo_ref, gb_ref, n8_ref, tot_ref, ts_ref, tn_ref, h_ref, rtt_ref, xs_hbm,
                     xloc, zeros, sem, zsem, *, lr):
    t = pl.program_id(0)
    nt = pl.num_programs(0)
    slot = t % 2
    tm = h_ref.shape[0]

    def seg_copy(sl, src, dst):
        return pltpu.make_async_copy(xloc.at[sl, pl.ds(src, CHUNK), :], xs_hbm.at[pl.ds(dst, CHUNK), :],
                                     sem.at[sl])

    def wait_tile(tile, sl):
        _wait_chunks(tot_ref[tile], lr // CHUNK, lambda size: pltpu.make_async_copy(
            xloc.at[sl, pl.ds(0, size), :], xs_hbm.at[pl.ds(0, size), :], sem.at[sl]))

    @pl.when(t >= 2)
    def _():
        wait_tile(t - 2, slot)

    rows = _iota((lr, tm), 0).astype(F32)
    is1 = rows == rtt_ref[0, RT_ROW1:RT_ROW1 + 1, :]
    is2 = rows == rtt_ref[0, RT_ROW2:RT_ROW2 + 1, :]
    perm = jnp.where(is1, 1.0, jnp.where(is2, 1.0, 0.0))
    xloc[slot, :, 0:XS_FEAT] = _pack_halves(
        jnp.dot(perm.astype(BF16), h_ref[...], preferred_element_type=F32))
    w_row = jnp.sum(jnp.where(is1, rtt_ref[0, RT_W1:RT_W1 + 1, :],
                              jnp.where(is2, rtt_ref[0, RT_W2:RT_W2 + 1, :], 0.0)), axis=-1, keepdims=True)
    xloc[slot, :, XS_FEAT:XS_WIDTH] = jnp.where(
        _iota((lr, LANES), 1) == 0, pltpu.bitcast(jnp.broadcast_to(w_row, (lr, LANES)), jnp.uint32),
        jnp.uint32(0))

    def per_expert(e, c):
        idx = t * N_EXPERTS + e
        src0 = lo_ref[idx] * CHUNK
        dst0 = gb_ref[idx] * CHUNK

        def per_chunk(k, c2):
            seg_copy(slot, pl.multiple_of(src0 + k * CHUNK, CHUNK),
                     pl.multiple_of(dst0 + k * CHUNK, CHUNK)).start()
            return c2
        lax.fori_loop(0, n8_ref[idx], per_chunk, 0)
        return c
    lax.fori_loop(0, N_EXPERTS, per_expert, 0)

    @pl.when(t == nt - 1)
    def _():
        zeros[...] = jnp.zeros(zeros.shape, jnp.uint32)

        def zero_copy(dst):
            return pltpu.make_async_copy(zeros, xs_hbm.at[pl.ds(dst, CHUNK), :], zsem.at[0])

        def fill(e, c):
            dst0 = ts_ref[e] * CHUNK

            def one(k, c2):
                zero_copy(pl.multiple_of(dst0 + k * CHUNK, CHUNK)).start()
                return c2
            lax.fori_loop(0, tn_ref[e], one, 0)
            return c
        lax.fori_loop(0, N_EXPERTS + 1, fill, 0)

        def drain(e, c):
            def one(k, c2):
                zero_copy(0).wait()
                return c2
            lax.fori_loop(0, tn_ref[e], one, 0)
            return c
        lax.fori_loop(0, N_EXPERTS + 1, drain, 0)

        @pl.when(t >= 1)
        def _():
            wait_tile(t - 1, 1 - slot)
        wait_tile(t, slot)


def _dispatch(plan, h2, rtt, n_rows, tm):
    m = h2.shape[0]
    lr = _local_rows(tm)
    grid_spec = pltpu.PrefetchScalarGridSpec(
        num_scalar_prefetch=6,
        grid=(m // tm,),
        in_specs=[pl.BlockSpec((tm, D_MODEL), lambda t, *_: (t, 0)),
                  pl.BlockSpec((1, 8, tm), lambda t, *_: (t, 0, 0))],
        out_specs=pl.BlockSpec(memory_space=pl.ANY),
        scratch_shapes=[pltpu.VMEM((2, lr, XS_WIDTH), jnp.uint32),
                        pltpu.VMEM((CHUNK, XS_WIDTH), jnp.uint32),
                        pltpu.SemaphoreType.DMA((2,)), pltpu.SemaphoreType.DMA((1,))],
    )
    return pl.pallas_call(
        functools.partial(_dispatch_kernel, lr=lr),
        grid_spec=grid_spec,
        out_shape=jax.ShapeDtypeStruct((n_rows, XS_WIDTH), jnp.uint32),
        compiler_params=_cparams(("arbitrary",)),
        name="moe_dispatch",
    )(plan["lo8"], plan["gb8"], plan["n8"], plan["tot8"], plan["ts8"], plan["tn8"], h2, rtt)


def _expert_kernel(be_ref, nu_ref, x_ref, wg_ref, wu_ref, wd_ref, y_ref, wgub, wdb):
    i = pl.program_id(0)

    @pl.when(i < nu_ref[0])
    def _():
        prev_e = be_ref[jnp.maximum(i - 1, 0)]

        @pl.when((i == 0) | (be_ref[i] != prev_e))
        def _():
            wgub[:, 0:D_EXPERT] = wg_ref[0, 0].astype(BF16)
            wgub[:, D_EXPERT:2 * D_EXPERT] = wu_ref[0, 0].astype(BF16)
            wdb[...] = wd_ref[0, 0].astype(BF16)

        x = _unpack_halves(x_ref[:, 0:XS_FEAT])
        w_row = pltpu.bitcast(x_ref[:, XS_FEAT:XS_FEAT + 1], F32)
        gu = jnp.dot(x, wgub[...], preferred_element_type=F32)
        hid = _silu(gu[:, 0:D_EXPERT]) * gu[:, D_EXPERT:2 * D_EXPERT]
        y = jnp.dot(hid.astype(BF16), wdb[...], preferred_element_type=F32)
        y_ref[...] = _pack_halves((y * w_row).astype(BF16).astype(F32))

    @pl.when(i >= nu_ref[0])
    def _():
        y_ref[...] = jnp.zeros(y_ref.shape, jnp.uint32)


def _experts(block_e, n_used, xs, wg, wu, wd, layer):
    n_rows = xs.shape[0]
    wspec = lambda shape: pl.BlockSpec((1, 1) + shape, lambda i, be, nu: (layer, be[i], 0, 0))
    grid_spec = pltpu.PrefetchScalarGridSpec(
        num_scalar_prefetch=2,
        grid=(n_rows // MOE_BLOCK,),
        in_specs=[
            pl.BlockSpec((MOE_BLOCK, XS_WIDTH), lambda i, be, nu: (i, 0)),
            wspec((D_MODEL, D_EXPERT)), wspec((D_MODEL, D_EXPERT)), wspec((D_EXPERT, D_MODEL)),
        ],
        out_specs=pl.BlockSpec((MOE_BLOCK, XS_FEAT), lambda i, be, nu: (i, 0)),
        scratch_shapes=[
            pltpu.VMEM((D_MODEL, 2 * D_EXPERT), BF16),
            pltpu.VMEM((D_EXPERT, D_MODEL), BF16),
        ],
    )
    return pl.pallas_call(
        _expert_kernel,
        grid_spec=grid_spec,
        out_shape=jax.ShapeDtypeStruct((n_rows, XS_FEAT), jnp.uint32),
        compiler_params=_cparams(("arbitrary",)),
        name="moe_experts",
    )(block_e, n_used, xs, wg, wu, wd)


def _combine_kernel(lo_ref, gb_ref, n8_ref, tot_ref, x_ref, rt_ref, gf_ref, fg_ref, ys_hbm,
                    o_ref, yloc, sem, *, lr, final):
    t = pl.program_id(0)
    nt = pl.num_programs(0)
    slot = t % 2
    tm = x_ref.shape[0]

    def fetch(tile, sl):
        def per_expert(e, c):
            idx = tile * N_EXPERTS + e
            dst0 = lo_ref[idx] * CHUNK
            src0 = gb_ref[idx] * CHUNK

            def per_chunk(k, c2):
                pltpu.make_async_copy(ys_hbm.at[pl.ds(pl.multiple_of(src0 + k * CHUNK, CHUNK), CHUNK), :],
                                      yloc.at[sl, pl.ds(pl.multiple_of(dst0 + k * CHUNK, CHUNK), CHUNK), :],
                                      sem.at[sl]).start()
                return c2
            lax.fori_loop(0, n8_ref[idx], per_chunk, 0)
            return c
        lax.fori_loop(0, N_EXPERTS, per_expert, 0)

    @pl.when(t == 0)
    def _():
        yloc[...] = jnp.zeros(yloc.shape, jnp.uint32)
        fetch(0, 0)

    @pl.when(t + 1 < nt)
    def _():
        fetch(t + 1, 1 - slot)

    _wait_chunks(tot_ref[t], lr // CHUNK, lambda size: pltpu.make_async_copy(
        ys_hbm.at[pl.ds(0, size), :], yloc.at[slot, pl.ds(0, size), :], sem.at[slot]))

    ysw = _unpack_halves(yloc[slot])
    rt = rt_ref[...]
    cols = _iota((tm, lr), 1).astype(F32)
    pick = jnp.where(cols == rt[:, RT_ROW1:RT_ROW1 + 1], 1.0,
                     jnp.where(cols == rt[:, RT_ROW2:RT_ROW2 + 1], 1.0, 0.0)).astype(BF16)
    x = x_ref[...] + gf_ref[0] * jnp.dot(pick, ysw, preferred_element_type=F32)
    if final:
        ms = jnp.mean(x * x, axis=-1, keepdims=True)
        x = x * lax.rsqrt(ms + EPS) * fg_ref[...]
    o_ref[...] = x


def _combine(plan, x2, ys, rt, gf, fg, seq, tm, final):
    m = x2.shape[0]
    per_b = seq // tm
    lr = _local_rows(tm)
    grid_spec = pltpu.PrefetchScalarGridSpec(
        num_scalar_prefetch=4,
        grid=(m // tm,),
        in_specs=[
            pl.BlockSpec((tm, D_MODEL), lambda i, *_: (i, 0)),
            pl.BlockSpec((tm, LANES), lambda i, *_: (i, 0)),
            pl.BlockSpec((1, 1, D_MODEL), lambda i, *_: (i // per_b, 0, 0)),
            pl.BlockSpec((1, D_MODEL), lambda i, *_: (0, 0)),
            pl.BlockSpec(memory_space=pl.ANY),
        ],
        out_specs=pl.BlockSpec((tm, D_MODEL), lambda i, *_: (i, 0)),
        scratch_shapes=[pltpu.VMEM((2, lr, XS_FEAT), jnp.uint32), pltpu.SemaphoreType.DMA((2,))],
    )
    return pl.pallas_call(
        functools.partial(_combine_kernel, lr=lr, final=final),
        grid_spec=grid_spec,
        out_shape=jax.ShapeDtypeStruct((m, D_MODEL), F32),
        compiler_params=_cparams(("arbitrary",)),
        name="moe_combine",
    )(plan["lo8"], plan["gb8"], plan["n8"], plan["tot8"], x2, rt, gf, fg, ys)


def _moe_plan(c, n_blocks):
    i32 = jnp.int32
    blk8 = MOE_BLOCK // CHUNK
    c8 = (c + CHUNK - 1) // CHUNK
    lo8 = jnp.cumsum(c8, axis=1) - c8
    per_e = jnp.sum(c8, axis=0)
    pad8 = (per_e + blk8 - 1) // blk8 * blk8
    end8 = jnp.cumsum(pad8)
    start8 = end8 - pad8
    gb8 = start8[None, :] + jnp.cumsum(c8, axis=0) - c8
    blk_start8 = jnp.arange(n_blocks, dtype=i32) * blk8
    block_e = jnp.minimum(jnp.sum(end8[None, :] <= blk_start8[:, None], axis=1), N_EXPERTS - 1)
    return {
        "lo8": lo8.reshape(-1).astype(i32), "gb8": gb8.reshape(-1).astype(i32),
        "n8": c8.reshape(-1).astype(i32), "tot8": jnp.sum(c8, axis=1).astype(i32),
        "ts8": jnp.concatenate([start8 + per_e, end8[-1:]]).astype(i32),
        "tn8": jnp.concatenate([pad8 - per_e, n_blocks * blk8 - end8[-1:]]).astype(i32),
        "block_e": block_e.astype(i32), "n_used": (end8[-1:] // blk8).astype(i32),
    }


def _w_in_sections(w):
    d_dt = SSD_WIDTH + XBC_WIDTH
    d_q = d_dt + SSD_HEADS
    d_f = d_q + 3 * ATTN_WIDTH
    d_ga = d_f + ATTN_HEADS
    small = jnp.concatenate([w[..., d_dt:d_q], w[..., d_f:d_ga],
                             jnp.zeros(w.shape[:-1] + (LANES - SSD_HEADS - ATTN_HEADS,), w.dtype)], axis=-1)
    sections = [w[..., SSD_WIDTH:d_dt], w[..., :SSD_WIDTH], w[..., d_q:d_f], w[..., d_ga:], small]
    return [sec.astype(BF16) for sec in sections]


def kernel(x, c, ada_w, ada_b, norm_mix_g, w_in, ssd_conv_w, ssd_conv_b, ssd_dt_bias, ssd_a_log,
           ssd_d, ssd_norm_g, fox_f_bias, fox_norm_g, cm_conv_w, cm_conv_b, cm_ln_g, cm_ln_b, w_out,
           norm_ffn_g, w_router_group, b_router_group, w_router_expert, b_router_expert, w_gate,
           w_up, w_down, final_norm_g):
    bsz, seq, d = x.shape
    m = bsz * seq
    tm = min(512, seq)
    tt = min(512, seq)
    n_blocks = -(-(2 * m + (CHUNK - 1) * (m // tm) * N_EXPERTS) // MOE_BLOCK) + N_EXPERTS

    mod = _modulation(c, ada_w, ada_b)
    x2 = x.reshape(m, d)
    w_in_sections = _w_in_sections(w_in)
    for l in range(DEPTH):
        sh_m, sc_m, g_m, sh_f, sc_f, g_f = [v.reshape(bsz, 1, d) for v in jnp.split(mod[l], 6, axis=-1)]
        a_m = norm_mix_g[l][None, None, :] * (1.0 + sc_m)
        proj = _inproj(x2, a_m, sh_m, w_in_sections, l, seq, min(2 * tm, seq))
        y_ssd = _ssd(proj, ssd_conv_w[l], ssd_conv_b[l], ssd_dt_bias[l], ssd_a_log[l],
                     ssd_d[l], ssd_norm_g[l], bsz, seq, tt)
        qa, ka, va, aux, stat = _foxprep(proj, fox_f_bias[l], bsz, seq)
        o_att = _fox(qa, ka, va, aux, stat)
        y_cnv = _conformer(proj, cm_conv_w[l], cm_conv_b[l], cm_ln_g[l], cm_ln_b[l], bsz, seq, tt)

        w_r = jnp.concatenate([w_router_group[l], w_router_expert[l],
                               jnp.zeros((d, LANES - N_GROUPS - N_EXPERTS), F32)], axis=1)
        w_rh = w_r.astype(BF16)
        w_r2 = jnp.concatenate([w_rh, (w_r - w_rh.astype(F32)).astype(BF16)], axis=1)
        b_r = jnp.concatenate([b_router_group[l], b_router_expert[l],
                               jnp.zeros((LANES - N_GROUPS - N_EXPERTS,), F32)]).reshape(1, LANES)
        a_f = norm_ffn_g[l][None, None, :] * (1.0 + sc_f)
        x2, h2, rt, rtt, cnt = _outproj(x2, y_ssd, o_att, y_cnv, w_out[l].astype(BF16),
                                        fox_norm_g[l].reshape(1, -1), g_m, a_f, sh_f, w_r2, b_r,
                                        seq, tm)
        plan = _moe_plan(cnt[:, 0, :N_EXPERTS].astype(jnp.int32), n_blocks)
        xs = _dispatch(plan, h2, rtt, n_blocks * MOE_BLOCK, tm)
        ys = _experts(plan["block_e"], plan["n_used"], xs, w_gate, w_up, w_down, l)
        x2 = _combine(plan, x2, ys, rt, g_f, final_norm_g.reshape(1, d), seq, tm,
                      final=(l == DEPTH - 1))
    return x2.reshape(bsz, seq, d)
```

```python
import functools

import jax
import jax.numpy as jnp
import numpy as np
from jax import lax
from jax.experimental import pallas as pl
from jax.experimental.pallas import tpu as pltpu

F32 = jnp.float32
BF16 = jnp.bfloat16
HIGHEST = lax.Precision.HIGHEST

D_MODEL = 1024
DEPTH = 4
SSD_WIDTH = 512
SSD_HEADS = 8
SSD_HEAD_DIM = 64
SSD_STATE = 128
SSD_CONV = 4
SSD_CHUNK = 128
XBC_WIDTH = 1024
ATTN_WIDTH = 256
ATTN_HEADS = 4
ATTN_HEAD_DIM = 64
CONV_WIDTH = 256
CONV_KERNEL = 31
N_GROUPS = 4
PER_GROUP = 8
N_EXPERTS = 32
D_EXPERT = 512
MOE_BLOCK = 512
EPS = 1e-6

LANES = 128
SUBLANES = 8
COL_XBC = 0
COL_Z = 1024
COL_QKV = 1536
COL_GA = 2304
COL_GB = 2560
COL_SM = 2816
NP = 2944
SM_F = 8

VMEM_LIMIT = 56 * 1024 * 1024


def _cparams(sem):
    return pltpu.CompilerParams(dimension_semantics=sem, vmem_limit_bytes=VMEM_LIMIT)


def _sigmoid(x):
    return 1.0 / (1.0 + jnp.exp(-x))


def _silu(x):
    return x * _sigmoid(x)


def _softplus(x):
    return jnp.maximum(x, 0.0) + jnp.log1p(jnp.exp(-jnp.abs(x)))


def _iota(shape, dim):
    return lax.broadcasted_iota(jnp.int32, shape, dim)


def _split_bf16x3(x):
    hi = x.astype(BF16)
    r1 = x - hi.astype(F32)
    mid = r1.astype(BF16)
    lo = (r1 - mid.astype(F32)).astype(BF16)
    return hi, mid, lo


def _dot_onehot_rhs(x, sel):
    return sum(jnp.dot(part, sel, preferred_element_type=F32) for part in _split_bf16x3(x))


def _dot_onehot_lhs(sel, x):
    return sum(jnp.dot(sel, part, preferred_element_type=F32) for part in _split_bf16x3(x))


def _mod_kernel(c_ref, w_ref, b_ref, o_ref):
    cond = _silu(c_ref[...])
    o_ref[0] = jnp.dot(cond, w_ref[0], precision=HIGHEST, preferred_element_type=F32) + b_ref[0]


def _modulation(c, ada_w, ada_b):
    bsz = c.shape[0]
    rows = 8
    cpad = jnp.zeros((rows, D_MODEL), F32).at[:bsz].set(c)
    tn = 1536
    n6 = 6 * D_MODEL
    out = pl.pallas_call(
        _mod_kernel,
        grid=(DEPTH, n6 // tn),
        in_specs=[
            pl.BlockSpec((rows, D_MODEL), lambda l, j: (0, 0)),
            pl.BlockSpec((1, D_MODEL, tn), lambda l, j: (l, 0, j)),
            pl.BlockSpec((1, 1, tn), lambda l, j: (l, 0, j)),
        ],
        out_specs=pl.BlockSpec((1, rows, tn), lambda l, j: (l, 0, j)),
        out_shape=jax.ShapeDtypeStruct((DEPTH, rows, n6), F32),
        compiler_params=_cparams(("arbitrary", "arbitrary")),
        name="adaln_mod",
    )(cpad, ada_w, ada_b.reshape(DEPTH, 1, n6))
    return out[:, :bsz]


def _inproj_kernel(x_ref, a_ref, s_ref, *refs):
    w_refs, o_ref = refs[:-1], refs[-1]
    x = x_ref[...]
    ms = jnp.mean(x * x, axis=-1, keepdims=True)
    h = (x * lax.rsqrt(ms + EPS) * a_ref[0] + s_ref[0]).astype(BF16)
    col = 0
    for w_ref in w_refs:
        width = w_ref.shape[-1]
        o_ref[:, col:col + width] = jnp.dot(h, w_ref[0], preferred_element_type=F32)
        col += width


def _inproj(x2, a, s, w_sections, layer, seq, tm):
    m = x2.shape[0]
    per_b = seq // tm
    assert sum(w.shape[-1] for w in w_sections) == NP
    return pl.pallas_call(
        _inproj_kernel,
        grid=(m // tm,),
        in_specs=[
            pl.BlockSpec((tm, D_MODEL), lambda i: (i, 0)),
            pl.BlockSpec((1, 1, D_MODEL), lambda i: (i // per_b, 0, 0)),
            pl.BlockSpec((1, 1, D_MODEL), lambda i: (i // per_b, 0, 0)),
        ] + [pl.BlockSpec((1, D_MODEL, w.shape[-1]), lambda i: (layer, 0, 0)) for w in w_sections],
        out_specs=pl.BlockSpec((tm, NP), lambda i: (i, 0)),
        out_shape=jax.ShapeDtypeStruct((m, NP), F32),
        compiler_params=_cparams(("arbitrary",)),
        name="in_proj",
    )(x2, a, s, *w_sections)


def _ssd_kernel(z_ref, xbc_ref, sm_ref, cw_ref, cb_ref, dtb_ref, alog_ref, dtbt_ref,
                alogt_ref, e_ref, dx_ref, ng_ref, y_ref, xpad, xc, prev, *, tt):
    t = pl.program_id(1)

    @pl.when(t == 0)
    def _():
        xpad[0:8, :] = jnp.zeros((8, XBC_WIDTH), F32)
        prev[...] = jnp.zeros(prev.shape, F32)

    xpad[8:8 + tt, :] = xbc_ref[...]
    acc = jnp.broadcast_to(cb_ref[...], (tt, XBC_WIDTH))
    for k in range(SSD_CONV):
        off = 8 - (SSD_CONV - 1) + k
        acc = acc + cw_ref[k:k + 1, :] * xpad[off:off + tt, :]
    xc[...] = _silu(acc)
    xpad[0:8, :] = xpad[tt:tt + 8, :]

    cl = SSD_CHUNK
    row = _iota((cl, cl), 0)
    col = _iota((cl, cl), 1)
    causal = row >= col
    tril = jnp.where(causal, 1.0, 0.0).astype(BF16)
    triu = jnp.where(row <= col, 1.0, 0.0).astype(BF16)
    lo = col < SSD_HEAD_DIM
    lane1 = _iota((1, LANES), 1)
    a_row = jnp.where(lane1 < SSD_HEADS, -jnp.exp(alog_ref[...]), 0.0)
    a_col = -jnp.exp(alogt_ref[...])
    expand = e_ref[...]

    def chunk(c, carry):
        r0 = c * cl
        xs = xc[pl.ds(r0, cl), 0:SSD_WIDTH]
        bmat = xc[pl.ds(r0, cl), SSD_WIDTH:SSD_WIDTH + 2 * SSD_STATE]
        cmat = xc[pl.ds(r0, cl), SSD_WIDTH + 2 * SSD_STATE:XBC_WIDTH]
        sm = sm_ref[pl.ds(r0, cl), :]
        dt = _softplus(sm + dtb_ref[...])
        da = dt * a_row
        acs = _dot_onehot_lhs(tril, da)
        dtt = _softplus(sm.T[0:SSD_HEADS, :] + dtbt_ref[...])
        acst = _dot_onehot_rhs(dtt * a_col, triu)
        dt_x = _dot_onehot_rhs(dt, expand)
        acs_x = _dot_onehot_rhs(acs, expand)
        last = acs_x[cl - 1:cl, :]
        eacs_x = jnp.exp(acs_x)
        dte_x = jnp.exp(last - acs_x)
        cd_x = jnp.exp(last)
        xdt = xs * dt_x
        xdte = (xdt * dte_x).astype(BF16)
        zc = z_ref[pl.ds(r0, cl), :]
        for g in range(2):
            bg = bmat[:, g * SSD_STATE:(g + 1) * SSD_STATE]
            cg = cmat[:, g * SSD_STATE:(g + 1) * SSD_STATE].astype(BF16)
            bgt = bg.T.astype(BF16)
            cbm = jnp.dot(cg, bgt, preferred_element_type=F32)
            gs = slice(g * 2 * LANES, (g + 1) * 2 * LANES)
            prev_g = prev[:, gs]
            yo_g = jnp.dot(cg, prev_g.astype(BF16), preferred_element_type=F32) * eacs_x[:, gs]
            prev[:, gs] = prev_g * cd_x[:, gs] + jnp.dot(bgt, xdte[:, gs], preferred_element_type=F32)
            pair_out = []
            for j in range(2):
                p = 2 * g + j
                sl = slice(p * LANES, (p + 1) * LANES)
                xp = xdt[:, sl]
                gms, xms = [], []
                for half in range(2):
                    h = 2 * p + half
                    seg = acs[:, h:h + 1] - acst[h:h + 1, :]
                    dec = jnp.exp(jnp.where(causal, seg, -jnp.inf))
                    gms.append((cbm * dec).astype(BF16))
                    own = lo if half == 0 else jnp.logical_not(lo)
                    xms.append(jnp.where(own, xp, 0.0).astype(BF16))
                yd = jnp.dot(jnp.concatenate(gms, axis=1), jnp.concatenate(xms, axis=0),
                             preferred_element_type=F32)
                pair_out.append(yd + yo_g[:, j * LANES:(j + 1) * LANES] + xs[:, sl] * dx_ref[:, sl])
            yg =jnp.concatenate(pair_out, axis=-1) * _silu(zc[:, gs])
            ms = jnp.mean(yg * yg, axis=-1, keepdims=True)
            y_ref[pl.ds(r0, cl), gs] = yg * lax.rsqrt(ms + EPS) * ng_ref[:, gs]
        return carry

    for c in range(tt // cl):
        chunk(c, 0)


def _ssd(proj, cw, cb, dtb, alog, dx, ng, bsz, seq, tt):
    m = bsz * seq
    nt = seq // tt
    pad = LANES - SSD_HEADS
    dtb_row = jnp.pad(dtb, (0, pad)).reshape(1, LANES)
    alog_row = jnp.pad(alog, (0, pad)).reshape(1, LANES)
    dtb_col = jnp.broadcast_to(dtb[:, None], (SSD_HEADS, SSD_CHUNK))
    alog_col = jnp.broadcast_to(alog[:, None], (SSD_HEADS, SSD_CHUNK))
    expand = np.zeros((LANES, SSD_WIDTH), np.float32)
    for h in range(SSD_HEADS):
        expand[h, h * SSD_HEAD_DIM:(h + 1) * SSD_HEAD_DIM] = 1.0
    dx_row = jnp.repeat(dx, SSD_HEAD_DIM).reshape(1, SSD_WIDTH)
    full = lambda shape: pl.BlockSpec(shape, lambda b, t: (0,) * len(shape))
    return pl.pallas_call(
        functools.partial(_ssd_kernel, tt=tt),
        grid=(bsz, nt),
        in_specs=[
            pl.BlockSpec((tt, SSD_WIDTH), lambda b, t: (b * nt + t, COL_Z // SSD_WIDTH)),
            pl.BlockSpec((tt, XBC_WIDTH), lambda b, t: (b * nt + t, COL_XBC // XBC_WIDTH + 0)),
            pl.BlockSpec((tt, LANES), lambda b, t: (b * nt + t, COL_SM // LANES)),
            full((SSD_CONV, XBC_WIDTH)),
            full((1, XBC_WIDTH)),
            full((1, LANES)),
            full((1, LANES)),
            full((SSD_HEADS, SSD_CHUNK)),
            full((SSD_HEADS, SSD_CHUNK)),
            full((LANES, SSD_WIDTH)),
            full((1, SSD_WIDTH)),
            full((1, SSD_WIDTH)),
        ],
        out_specs=pl.BlockSpec((tt, SSD_WIDTH), lambda b, t: (b * nt + t, 0)),
        out_shape=jax.ShapeDtypeStruct((m, SSD_WIDTH), F32),
        scratch_shapes=[
            pltpu.VMEM((tt + 8, XBC_WIDTH), F32),
            pltpu.VMEM((tt, XBC_WIDTH), F32),
            pltpu.VMEM((SSD_STATE, SSD_WIDTH), F32),
        ],
        compiler_params=_cparams(("arbitrary", "arbitrary")),
        name="ssd_scan",
    )(proj, proj, proj, cw, cb.reshape(1, XBC_WIDTH), dtb_row, alog_row, dtb_col, alog_col,
      jnp.asarray(expand, dtype=BF16), dx_row, ng.reshape(1, SSD_WIDTH))


LOG2E = 1.4426950408889634
ATT_BLOCK = 256
ATT_TQ = 512
PRUNE_LOG2 = 128.0
FAST_GAP_LOG2 = 60.0
SHIFT_MARGIN_LOG2 = 1.0
AUX_CUMEND = 0
AUX_KMAX = 1
ST_QNORM, ST_CUM, ST_SELF = 0, 1, 2


def _foxprep_kernel(qkv_ref, sm_ref, fb_ref, q_out, k_out, v_out, aux_out, stat_out, carry, *, tt):
    t = pl.program_id(1)

    @pl.when(t == 0)
    def _():
        carry[...] = jnp.zeros(carry.shape, F32)
        aux_out[...] = jnp.zeros(aux_out.shape, F32)

    logit = sm_ref[...] + fb_ref[...]
    logf = -_softplus(-logit)
    row = _iota((tt, tt), 0)
    col = _iota((tt, tt), 1)
    tril = jnp.where(row >= col, 1.0, 0.0).astype(BF16)
    cum = _dot_onehot_lhs(tril, logf) + carry[...]
    carry[...] = cum[tt - 1:tt, :]

    lane = _iota((tt, LANES), 1)
    lane1 = _iota((1, LANES), 1)
    scale = ATTN_HEAD_DIM ** -0.5 * LOG2E
    for h in range(ATTN_HEADS):
        pair, half = h // 2, h % 2
        own = (lane < ATTN_HEAD_DIM) if half == 0 else (lane >= ATTN_HEAD_DIM)
        a0 = ATTN_HEAD_DIM * (1 - half)
        cs = jnp.broadcast_to(cum[:, SM_F + h:SM_F + h + 1], (tt, LANES)) * LOG2E
        hi = cs.astype(BF16).astype(F32)
        r1 = cs - hi
        mid = r1.astype(BF16).astype(F32)
        low = r1 - mid
        qp = qkv_ref[:, pair * LANES:(pair + 1) * LANES]
        kp = qkv_ref[:, ATTN_WIDTH + pair * LANES:ATTN_WIDTH + (pair + 1) * LANES]
        vp = qkv_ref[:, 2 * ATTN_WIDTH + pair * LANES:2 * ATTN_WIDTH + (pair + 1) * LANES]
        qa = jnp.where(lane == a0, hi, jnp.where(lane == a0 + 1, mid, jnp.where(
            lane == a0 + 2, low, jnp.where((lane >= a0 + 3) & (lane < a0 + 6), 1.0, 0.0))))
        ka = jnp.where(lane == a0 + 3, -hi, jnp.where(lane == a0 + 4, -mid, jnp.where(
            lane == a0 + 5, -low, jnp.where((lane >= a0) & (lane < a0 + 9), 1.0, 0.0))))
        qb = jnp.where(own, qp * scale, qa).astype(BF16)
        q_out[0, h] = qb
        kb = jnp.where(own, kp, ka).astype(BF16)
        k_out[0, h] = kb
        qf = jnp.where(own, qb.astype(F32), 0.0)
        qn = jnp.sqrt(jnp.sum(qf * qf, axis=-1, keepdims=True))
        s_self = jnp.sum(qf * kb.astype(F32), axis=-1, keepdims=True)
        stat = jnp.where(lane == ST_QNORM, qn, jnp.where(lane == ST_CUM, hi + mid + low, jnp.where(
            lane == ST_SELF, s_self, 0.0)))
        stat_out[0, h] = stat.T[0:8, :]
        v_out[0, h] = jnp.where(own, vp, jnp.where(lane == a0, 1.0, 0.0)).astype(BF16)
        kf = jnp.where(own, kb.astype(F32), 0.0)
        kn2_rows = jnp.sum(kf * kf, axis=-1, keepdims=True)
        for blk in range(tt // ATT_BLOCK):
            end = (blk + 1) * ATT_BLOCK
            kn2 = jnp.max(kn2_rows[end - ATT_BLOCK:end, :], axis=0, keepdims=True)
            here = lane1 == t * (tt // ATT_BLOCK) + blk
            aux_out[0, h, AUX_CUMEND:AUX_CUMEND + 1, :] = jnp.where(
                here, cs[end - 1:end, :], aux_out[0, h, AUX_CUMEND:AUX_CUMEND + 1, :])
            aux_out[0, h, AUX_KMAX:AUX_KMAX + 1, :] = jnp.where(
                here, jnp.sqrt(kn2), aux_out[0, h, AUX_KMAX:AUX_KMAX + 1, :])


def _foxprep(proj, fb, bsz, seq):
    tt = min(ATT_TQ, seq)
    nt = seq // tt
    assert tt % ATT_BLOCK == 0 and seq // ATT_BLOCK <= LANES
    fb_row = jnp.zeros((1, LANES), F32).at[0, SM_F:SM_F + ATTN_HEADS].set(fb)
    shp = jax.ShapeDtypeStruct((bsz, ATTN_HEADS, seq, LANES), BF16)
    ospec = pl.BlockSpec((1, ATTN_HEADS, tt, LANES), lambda b, t: (b, 0, t, 0))
    return pl.pallas_call(
        functools.partial(_foxprep_kernel, tt=tt),
        grid=(bsz, nt),
        in_specs=[
            pl.BlockSpec((tt, 3 * ATTN_WIDTH), lambda b, t: (b * nt + t, COL_QKV // (3 * ATTN_WIDTH))),
            pl.BlockSpec((tt, LANES), lambda b, t: (b * nt + t, COL_SM // LANES)),
            pl.BlockSpec((1, LANES), lambda b, t: (0, 0)),
        ],
        out_specs=[ospec, ospec, ospec,
                   pl.BlockSpec((1, ATTN_HEADS, 8, LANES), lambda b, t: (b, 0, 0, 0)),
                   pl.BlockSpec((1, ATTN_HEADS, 8, tt), lambda b, t: (b, 0, 0, t))],
        out_shape=[shp, shp, shp, jax.ShapeDtypeStruct((bsz, ATTN_HEADS, 8, LANES), F32),
                   jax.ShapeDtypeStruct((bsz, ATTN_HEADS, 8, seq), F32)],
        scratch_shapes=[pltpu.VMEM((1, LANES), F32)],
        compiler_params=_cparams(("arbitrary", "arbitrary")),
        name="fox_prep",
    )(proj, proj, fb_row)


def _fox_kernel(q_ref, k_ref, v_ref, aux_ref, stat_ref, o_ref, *, tq, kb):
    i = pl.program_id(2)
    ib = i * (tq // kb)
    nt = (((1,), (1,)), ((), ()))
    lane = _iota((tq, LANES), 1)
    lane1 = _iota((1, LANES), 1)
    qs = (q_ref[0, 0], q_ref[0, 1])

    def step(hh, r0, width, m, acc, mask):
        kblk = k_ref[0, hh, pl.ds(r0, width), :]
        vblk = v_ref[0, hh, pl.ds(r0, width), :]
        s = lax.dot_general(qs[hh], kblk, nt, preferred_element_type=F32)
        if mask is not None:
            s = jnp.where(mask, s, -jnp.inf)
        m_new = jnp.maximum(m, jnp.max(s, axis=-1, keepdims=True))
        p = jnp.exp2(s - m_new)
        alpha = jnp.exp2(m - m_new)
        acc = acc * alpha + jnp.dot(p.astype(BF16), vblk, preferred_element_type=F32)
        return m_new, acc

    d0 = pl.multiple_of(i * tq, tq)
    diag = _iota((tq, tq), 0) >= _iota((tq, tq), 1)
    ib_f = ib.astype(F32)
    first = []
    gaps = []
    q_shift = []
    for hh in range(2):
        a0 = ATTN_HEAD_DIM * (1 - hh)
        st = stat_ref[0, hh]
        qn = st[ST_QNORM:ST_QNORM + 1, :]
        cum_t = st[ST_CUM:ST_CUM + 1, :]
        s_self = st[ST_SELF:ST_SELF + 1, :]
        aux = aux_ref[0, hh]
        cend = aux[AUX_CUMEND:AUX_CUMEND + 1, :]
        kmax = jnp.max(jnp.where(lane1 < ib + tq // kb, aux[AUX_KMAX:AUX_KMAX + 1, :], 0.0), axis=-1,
                       keepdims=True)
        bound = qn * kmax
        slack = jnp.max(bound + cum_t - s_self, axis=-1, keepdims=True)
        live = (lane1 < ib) & (slack - cend > -PRUNE_LOG2)
        first.append(jnp.min(jnp.where(live, lane1.astype(F32), ib_f)))
        parts = _split_bf16x3(bound + SHIFT_MARGIN_LOG2)
        gaps.append(jnp.max(sum(part.astype(F32) for part in parts) - s_self))
        terms = jnp.concatenate([-part.astype(F32) for part in parts] + [jnp.zeros((13, tq), F32)],
                                axis=0).astype(BF16)
        place = jnp.where(_iota((16, LANES), 1) == _iota((16, LANES), 0) + (a0 + 6), 1.0,
                          0.0).astype(BF16)
        q_shift.append(qs[hh] + lax.dot_general(terms, place, (((0,), (0,)), ((), ())),
                                                preferred_element_type=F32).astype(BF16))
    j_first = jnp.minimum(first[0], first[1]).astype(jnp.int32)

    def fast(_):
        def blocks(c, r0, width, mask=None):
            out = []
            for hh in range(2):
                s = lax.dot_general(q_shift[hh], k_ref[0, hh, pl.ds(r0, width), :], nt,
                                    preferred_element_type=F32)
                if mask is not None:
                    s = jnp.where(mask, s, -jnp.inf)
                out.append(c[hh] + jnp.dot(jnp.exp2(s).astype(BF16), v_ref[0, hh, pl.ds(r0, width), :],
                                           preferred_element_type=F32))
            return tuple(out)

        zero = jnp.zeros((tq, LANES), F32)
        c = blocks((zero, zero), d0, tq, diag)
        j_lo = j_first - ((ib - j_first) & 1)
        n = ib - j_lo
        c = lax.cond((n & 2) == 2, lambda c: blocks(c, pl.multiple_of(j_lo * kb, kb), 2 * kb),
                     lambda c: c, c)
        j0 = j_lo + (n & 2)

        def body(jj, c):
            return blocks(c, pl.multiple_of((j0 + 4 * jj) * kb, kb), 4 * kb)
        return lax.fori_loop(0, lax.shift_right_logical(n, 2), body, c)

    def online(_):
        m_init = jnp.full((tq, 1), -jnp.inf, F32)
        acc_init = jnp.zeros((tq, LANES), F32)
        carry = step(0, d0, tq, m_init, acc_init, diag) + step(1, d0, tq, m_init, acc_init, diag)

        def body(j, c):
            r0 = pl.multiple_of(j * kb, kb)
            m0, a0_, m1, a1_ = c
            m0, a0_ = step(0, r0, kb, m0, a0_, None)
            m1, a1_ = step(1, r0, kb, m1, a1_, None)
            return m0, a0_, m1, a1_
        _, a0_, _, a1_ = lax.fori_loop(j_first, ib, body, carry)
        return a0_, a1_

    acc0, acc1 = lax.cond(jnp.maximum(gaps[0], gaps[1]) <= FAST_GAP_LOG2, fast, online, None)
    den0 = jnp.sum(jnp.where(lane == ATTN_HEAD_DIM, acc0, 0.0), axis=-1, keepdims=True)
    den1 = jnp.sum(jnp.where(lane == 0, acc1, 0.0), axis=-1, keepdims=True)
    o_ref[...] = jnp.where(lane < ATTN_HEAD_DIM, acc0 / den0, acc1 / den1)


def _fox(qa, ka, va, aux, stat):
    bsz, nh, seq, _ = qa.shape
    tq = min(ATT_TQ, seq)
    assert (tq // ATT_BLOCK) % 2 == 0
    nq = seq // tq
    kv_spec = pl.BlockSpec((1, 2, seq, LANES), lambda b, p, i: (b, p, 0, 0))
    return pl.pallas_call(
        functools.partial(_fox_kernel, tq=tq, kb=ATT_BLOCK),
        grid=(bsz, nh // 2, nq),
        in_specs=[pl.BlockSpec((1, 2, tq, LANES), lambda b, p, i: (b, p, i, 0)), kv_spec, kv_spec,
                  pl.BlockSpec((1, 2, 8, LANES), lambda b, p, i: (b, p, 0, 0)),
                  pl.BlockSpec((1, 2, 8, tq), lambda b, p, i: (b, p, 0, i))],
        out_specs=pl.BlockSpec((tq, LANES), lambda b, p, i: (b * nq + i, p)),
        out_shape=jax.ShapeDtypeStruct((bsz * seq, ATTN_WIDTH), F32),
        compiler_params=_cparams(("arbitrary", "arbitrary", "arbitrary")),
        name="fox_attn",
    )(qa, ka, va, aux, stat)


CONF_HALO = 32
CONF_ROWS = 64


def _conf_kernel(ga_ref, gb_ref, w_ref, b_ref, lg_ref, lb_ref, y_ref, upad, ush, *, tt):
    t = pl.program_id(1)

    @pl.when(t == 0)
    def _():
        upad[0:CONF_HALO, :] = jnp.zeros((CONF_HALO, CONV_WIDTH), F32)

    upad[CONF_HALO:CONF_HALO + tt, :] = ga_ref[...] * _sigmoid(gb_ref[...])
    base = CONF_HALO - (CONV_KERNEL - 1)
    for ph in range(1, SUBLANES):
        ush[ph - 1] = upad[ph:ph + ush.shape[1], :]
    for r in range(tt // CONF_ROWS):
        r0 = r * CONF_ROWS
        acc = jnp.broadcast_to(b_ref[...], (CONF_ROWS, CONV_WIDTH))
        for k in range(CONV_KERNEL):
            ph = (base + k) % SUBLANES
            lo = r0 + base + k - ph
            src = upad[lo:lo + CONF_ROWS, :] if ph == 0 else ush[ph - 1, lo:lo + CONF_ROWS, :]
            acc = acc + w_ref[k:k + 1, :] * src
        mu = jnp.mean(acc, axis=-1, keepdims=True)
        cen = acc - mu
        var = jnp.mean(cen * cen, axis=-1, keepdims=True)
        y = cen * lax.rsqrt(var + EPS) * lg_ref[...] + lb_ref[...]
        y_ref[r0:r0 + CONF_ROWS, :] = _silu(y)
    upad[0:CONF_HALO, :] = upad[tt:tt + CONF_HALO, :]


def _conformer(proj, w, b, lg, lb, bsz, seq, tt):
    m = bsz * seq
    nt = seq // tt
    full = lambda shape: pl.BlockSpec(shape, lambda bb, t: (0,) * len(shape))
    return pl.pallas_call(
        functools.partial(_conf_kernel, tt=tt),
        grid=(bsz, nt),
        in_specs=[
            pl.BlockSpec((tt, CONV_WIDTH), lambda bb, t: (bb * nt + t, COL_GA // CONV_WIDTH)),
            pl.BlockSpec((tt, CONV_WIDTH), lambda bb, t: (bb * nt + t, COL_GB // CONV_WIDTH)),
            full((CONV_KERNEL, CONV_WIDTH)),
            full((1, CONV_WIDTH)),
            full((1, CONV_WIDTH)),
            full((1, CONV_WIDTH)),
        ],
        out_specs=pl.BlockSpec((tt, CONV_WIDTH), lambda bb, t: (bb * nt + t, 0)),
        out_shape=jax.ShapeDtypeStruct((m, CONV_WIDTH), F32),
        scratch_shapes=[pltpu.VMEM((tt + CONF_HALO, CONV_WIDTH), F32),
                        pltpu.VMEM((SUBLANES - 1, tt + CONF_HALO - SUBLANES, CONV_WIDTH), F32)],
        compiler_params=_cparams(("arbitrary", "arbitrary")),
        name="conformer_conv",
    )(proj, proj, w, b.reshape(1, -1), lg.reshape(1, -1), lb.reshape(1, -1))


ROUTE_BIG = 1e9
CHUNK = 8
XS_FEAT = D_MODEL // 2
XS_WIDTH = XS_FEAT + LANES
RT_W1, RT_W2, RT_ROW1, RT_ROW2 = 0, 1, 2, 3


def _split3(x):
    hi = x.astype(BF16)
    lo = (x - hi.astype(F32)).astype(BF16)
    return hi, lo


def _outproj_kernel(x_ref, ys_ref, oa_ref, yc_ref, wo_ref, fg_ref, gm_ref, a2_ref, s2_ref,
                    wr_ref, br_ref, xn_ref, h2_ref, rt_ref, rtt_ref, cnt_ref):
    tm = x_ref.shape[0]
    lane = _iota((tm, LANES), 1)
    att = oa_ref[...]
    ms = jnp.mean(att * att, axis=-1, keepdims=True)
    att = att * lax.rsqrt(ms + EPS) * fg_ref[...]
    y = jnp.dot(ys_ref[...].astype(BF16), wo_ref[0:SSD_WIDTH, :], preferred_element_type=F32)
    y = y + jnp.dot(att.astype(BF16), wo_ref[SSD_WIDTH:SSD_WIDTH + ATTN_WIDTH, :],
                    preferred_element_type=F32)
    y = y + jnp.dot(yc_ref[...].astype(BF16), wo_ref[SSD_WIDTH + ATTN_WIDTH:, :],
                    preferred_element_type=F32)
    xn = x_ref[...] + gm_ref[0] * y
    xn_ref[...] = xn
    ms2 = jnp.mean(xn * xn, axis=-1, keepdims=True)
    h2 = xn * lax.rsqrt(ms2 + EPS) * a2_ref[0] + s2_ref[0]
    h2_ref[...] = h2.astype(BF16)

    hh, hl = _split3(h2)
    part = jnp.dot(jnp.concatenate([hh, hl], axis=0), wr_ref[...], preferred_element_type=F32)
    logits = (part[:tm, :LANES] + part[:tm, LANES:] + part[tm:, :LANES] + part[tm:, LANES:]) + br_ref[...]
    lanef = lane.astype(F32)
    lg = jnp.where(lane < N_GROUPS, logits, -jnp.inf)
    gmax = jnp.max(lg, axis=-1, keepdims=True)
    gsum = jnp.sum(jnp.exp(lg - gmax), axis=-1, keepdims=True)
    gidx = jnp.min(jnp.where(lg == gmax, lanef, ROUTE_BIG), axis=-1, keepdims=True)
    e_lo = N_GROUPS + PER_GROUP * gidx
    le = jnp.where((lanef >= e_lo) & (lanef < e_lo + PER_GROUP), logits, -jnp.inf)
    m1 = jnp.max(le, axis=-1, keepdims=True)
    i1 = jnp.min(jnp.where(le == m1, lanef, ROUTE_BIG), axis=-1, keepdims=True)
    le2 = jnp.where(lanef == i1, -jnp.inf, le)
    m2 = jnp.max(le2, axis=-1, keepdims=True)
    i2 = jnp.min(jnp.where(le2 == m2, lanef, ROUTE_BIG), axis=-1, keepdims=True)
    esum = jnp.sum(jnp.exp(le - m1), axis=-1, keepdims=True)
    p1 = 1.0 / esum
    p2 = jnp.exp(m2 - m1) / esum
    psel = 1.0 / gsum
    w1 = p1 / (p1 + p2) * psel
    w2 = p2 / (p1 + p2) * psel

    oh1 = (lanef == (i1 - N_GROUPS)).astype(F32)
    oh2 = (lanef == (i2 - N_GROUPS)).astype(F32)
    oh = oh1 + oh2
    cnt = jnp.sum(oh, axis=0, keepdims=True)
    before = (_iota((tm, tm), 0) > _iota((tm, tm), 1)).astype(BF16)
    rank = jnp.dot(before, oh.astype(BF16), preferred_element_type=F32)
    chunks = jnp.floor((cnt + (CHUNK - 1)) * (1.0 / CHUNK))
    below = (_iota((LANES, LANES), 0) < _iota((LANES, LANES), 1)).astype(BF16)
    seg_lo = CHUNK * jnp.dot(jnp.broadcast_to(chunks, (8, LANES)).astype(BF16), below,
                             preferred_element_type=F32)[0:1, :]
    lr1 = jnp.sum(oh1 * (rank + seg_lo), axis=-1, keepdims=True)
    lr2 = jnp.sum(oh2 * (rank + seg_lo), axis=-1, keepdims=True)
    route = jnp.where(lane == RT_W1, w1, jnp.where(lane == RT_W2, w2, jnp.where(
        lane == RT_ROW1, lr1, jnp.where(lane == RT_ROW2, lr2, 0.0))))
    rt_ref[...] = route
    rtt_ref[0] = route.T[0:8, :]
    cnt_ref[0] = jnp.broadcast_to(cnt, (8, LANES))


def _outproj(x2, ys, oa, yc, wo, fg, gm, a2, s2, wr, br, seq, tm):
    m = x2.shape[0]
    per_b = seq // tm
    nt = m // tm
    row = lambda w: pl.BlockSpec((tm, w), lambda i: (i, 0))
    bvec = pl.BlockSpec((1, 1, D_MODEL), lambda i: (i // per_b, 0, 0))
    full = lambda shape: pl.BlockSpec(shape, lambda i: (0,) * len(shape))
    return pl.pallas_call(
        _outproj_kernel,
        grid=(nt,),
        in_specs=[
            row(D_MODEL), row(SSD_WIDTH), row(ATTN_WIDTH), row(CONV_WIDTH),
            full((D_MODEL, D_MODEL)), full((1, ATTN_WIDTH)), bvec, bvec, bvec,
            full((D_MODEL, 2 * LANES)), full((1, LANES)),
        ],
        out_specs=[row(D_MODEL), row(D_MODEL), row(LANES),
                   pl.BlockSpec((1, 8, tm), lambda i: (i, 0, 0)),
                   pl.BlockSpec((1, 8, LANES), lambda i: (i, 0, 0))],
        out_shape=[jax.ShapeDtypeStruct((m, D_MODEL), F32), jax.ShapeDtypeStruct((m, D_MODEL), BF16),
                   jax.ShapeDtypeStruct((m, LANES), F32), jax.ShapeDtypeStruct((nt, 8, tm), F32),
                   jax.ShapeDtypeStruct((nt, 8, LANES), F32)],
        compiler_params=_cparams(("arbitrary",)),
        name="out_proj_router",
    )(x2, ys, oa, yc, wo, fg, gm, a2, s2, wr, br)


def _local_rows(tm):
    return -(-(2 * tm + (CHUNK - 1) * N_EXPERTS) // LANES) * LANES


def _pack_halves(x):
    w = x.shape[-1] // 2
    return pltpu.bitcast(x[:, :w], jnp.uint32) | lax.shift_right_logical(
        pltpu.bitcast(x[:, w:], jnp.uint32), jnp.uint32(16))


def _unpack_halves(p):
    hi = pltpu.bitcast(p & jnp.uint32(0xFFFF0000), F32)
    lo = pltpu.bitcast(lax.shift_left(p, jnp.uint32(16)), F32)
    return jnp.concatenate([hi, lo], axis=-1).astype(BF16)


def _start_segment(n, copy_at):
    quads = lax.shift_right_logical(n, 2)

    def per_quad(k, c):
        copy_at(pl.multiple_of(k * (4 * CHUNK), CHUNK), 4 * CHUNK).start()
        return c
    lax.fori_loop(0, quads, per_quad, 0)

    @pl.when((n & 2) == 2)
    def _():
        copy_at(pl.multiple_of(quads * (4 * CHUNK), CHUNK), 2 * CHUNK).start()

    @pl.when((n & 1) == 1)
    def _():
        copy_at(pl.multiple_of((quads * 4 + (n & 2)) * CHUNK, CHUNK), CHUNK).start()


def _wait_chunks(n, n_max, copy_of_size):
    for bit in range(n_max.bit_length()):
        @pl.when((lax.shift_right_logical(n, bit) & 1) == 1)
        def _():
            copy_of_size(CHUNK << bit).wait()


def _dispatch_kernel(lo_ref, gb_ref, n8_ref, tot_ref, ts_ref, tn_ref, h_ref, rtt_ref, xs_hbm,
                     xloc, zeros, sem, zsem, *, lr):
    t = pl.program_id(0)
    nt = pl.num_programs(0)
    slot = t % 2
    tm = h_ref.shape[0]

    def wait_tile(tile, sl):
        _wait_chunks(tot_ref[tile], lr // CHUNK, lambda size: pltpu.make_async_copy(
            xloc.at[sl, pl.ds(0, size), :], xs_hbm.at[pl.ds(0, size), :], sem.at[sl]))

    @pl.when(t >= 2)
    def _():
        wait_tile(t - 2, slot)

    rows = _iota((lr, tm), 0).astype(F32)
    is1 = rows == rtt_ref[0, RT_ROW1:RT_ROW1 + 1, :]
    is2 = rows == rtt_ref[0, RT_ROW2:RT_ROW2 + 1, :]
    perm = jnp.where(is1, 1.0, jnp.where(is2, 1.0, 0.0))
    xloc[slot, :, 0:XS_FEAT] = _pack_halves(
        jnp.dot(perm.astype(BF16), h_ref[...], preferred_element_type=F32))
    w_row = jnp.sum(jnp.where(is1, rtt_ref[0, RT_W1:RT_W1 + 1, :],
                              jnp.where(is2, rtt_ref[0, RT_W2:RT_W2 + 1, :], 0.0)), axis=-1, keepdims=True)
    xloc[slot, :, XS_FEAT:XS_WIDTH] = jnp.where(
        _iota((lr, LANES), 1) == 0, pltpu.bitcast(jnp.broadcast_to(w_row, (lr, LANES)), jnp.uint32),
        jnp.uint32(0))

    def per_expert(e, c):
        idx = t * N_EXPERTS + e
        src0 = lo_ref[idx] * CHUNK
        dst0 = gb_ref[idx] * CHUNK
        _start_segment(n8_ref[idx], lambda first, rows: pltpu.make_async_copy(
            xloc.at[slot, pl.ds(pl.multiple_of(src0 + first, CHUNK), rows), :],
            xs_hbm.at[pl.ds(pl.multiple_of(dst0 + first, CHUNK), rows), :], sem.at[slot]))
        return c
    lax.fori_loop(0, N_EXPERTS, per_expert, 0)

    @pl.when(t == nt - 1)
    def _():
        zeros[...] = jnp.zeros(zeros.shape, jnp.uint32)

        def zero_copy(dst):
            return pltpu.make_async_copy(zeros, xs_hbm.at[pl.ds(dst, CHUNK), :], zsem.at[0])

        def fill(e, c):
            dst0 = ts_ref[e] * CHUNK

            def one(k, c2):
                zero_copy(pl.multiple_of(dst0 + k * CHUNK, CHUNK)).start()
                return c2
            lax.fori_loop(0, tn_ref[e], one, 0)
            return c
        lax.fori_loop(0, N_EXPERTS + 1, fill, 0)

        def drain(e, c):
            def one(k, c2):
                zero_copy(0).wait()
                return c2
            lax.fori_loop(0, tn_ref[e], one, 0)
            return c
        lax.fori_loop(0, N_EXPERTS + 1, drain, 0)

        @pl.when(t >= 1)
        def _():
            wait_tile(t - 1, 1 - slot)
        wait_tile(t, slot)


def _dispatch(plan, h2, rtt, n_rows, tm):
    m = h2.shape[0]
    lr = _local_rows(tm)
    grid_spec = pltpu.PrefetchScalarGridSpec(
        num_scalar_prefetch=6,
        grid=(m // tm,),
        in_specs=[pl.BlockSpec((tm, D_MODEL), lambda t, *_: (t, 0)),
                  pl.BlockSpec((1, 8, tm), lambda t, *_: (t, 0, 0))],
        out_specs=pl.BlockSpec(memory_space=pl.ANY),
        scratch_shapes=[pltpu.VMEM((2, lr, XS_WIDTH), jnp.uint32),
                        pltpu.VMEM((CHUNK, XS_WIDTH), jnp.uint32),
                        pltpu.SemaphoreType.DMA((2,)), pltpu.SemaphoreType.DMA((1,))],
    )
    return pl.pallas_call(
        functools.partial(_dispatch_kernel, lr=lr),
        grid_spec=grid_spec,
        out_shape=jax.ShapeDtypeStruct((n_rows, XS_WIDTH), jnp.uint32),
        compiler_params=_cparams(("arbitrary",)),
        name="moe_dispatch",
    )(plan["lo8"], plan["gb8"], plan["n8"], plan["tot8"], plan["ts8"], plan["tn8"], h2, rtt)


def _expert_kernel(be_ref, nu_ref, x_ref, wg_ref, wu_ref, wd_ref, y_ref, wgub, wdb):
    i = pl.program_id(0)

    @pl.when(i < nu_ref[0])
    def _():
        prev_e = be_ref[jnp.maximum(i - 1, 0)]

        @pl.when((i == 0) | (be_ref[i] != prev_e))
        def _():
            wgub[:, 0:D_EXPERT] = wg_ref[0, 0].astype(BF16)
            wgub[:, D_EXPERT:2 * D_EXPERT] = wu_ref[0, 0].astype(BF16)
            wdb[...] = wd_ref[0, 0].astype(BF16)

        x = _unpack_halves(x_ref[:, 0:XS_FEAT])
        w_row = pltpu.bitcast(x_ref[:, XS_FEAT:XS_FEAT + 1], F32)
        gu = jnp.dot(x, wgub[...], preferred_element_type=F32)
        hid = _silu(gu[:, 0:D_EXPERT]) * gu[:, D_EXPERT:2 * D_EXPERT]
        y = jnp.dot(hid.astype(BF16), wdb[...], preferred_element_type=F32)
        y_ref[...] = _pack_halves((y * w_row).astype(BF16).astype(F32))

    @pl.when(i >= nu_ref[0])
    def _():
        y_ref[...] = jnp.zeros(y_ref.shape, jnp.uint32)


def _experts(block_e, n_used, xs, wg, wu, wd, layer):
    n_rows = xs.shape[0]
    wspec = lambda shape: pl.BlockSpec((1, 1) + shape, lambda i, be, nu: (layer, be[i], 0, 0))
    grid_spec = pltpu.PrefetchScalarGridSpec(
        num_scalar_prefetch=2,
        grid=(n_rows // MOE_BLOCK,),
        in_specs=[
            pl.BlockSpec((MOE_BLOCK, XS_WIDTH), lambda i, be, nu: (i, 0)),
            wspec((D_MODEL, D_EXPERT)), wspec((D_MODEL, D_EXPERT)), wspec((D_EXPERT, D_MODEL)),
        ],
        out_specs=pl.BlockSpec((MOE_BLOCK, XS_FEAT), lambda i, be, nu: (i, 0)),
        scratch_shapes=[
            pltpu.VMEM((D_MODEL, 2 * D_EXPERT), BF16),
            pltpu.VMEM((D_EXPERT, D_MODEL), BF16),
        ],
    )
    return pl.pallas_call(
        _expert_kernel,
        grid_spec=grid_spec,
        out_shape=jax.ShapeDtypeStruct((n_rows, XS_FEAT), jnp.uint32),
        compiler_params=_cparams(("arbitrary",)),
        name="moe_experts",
    )(block_e, n_used, xs, wg, wu, wd)


def _combine_kernel(lo_ref, gb_ref, n8_ref, tot_ref, x_ref, rt_ref, gf_ref, fg_ref, ys_hbm,
                    o_ref, yloc, sem, *, lr, final):
    t = pl.program_id(0)
    nt = pl.num_programs(0)
    slot = t % 2
    tm = x_ref.shape[0]

    def fetch(tile, sl):
        def per_expert(e, c):
            idx = tile * N_EXPERTS + e
            dst0 = lo_ref[idx] * CHUNK
            src0 = gb_ref[idx] * CHUNK
            _start_segment(n8_ref[idx], lambda first, rows: pltpu.make_async_copy(
                ys_hbm.at[pl.ds(pl.multiple_of(src0 + first, CHUNK), rows), :],
                yloc.at[sl, pl.ds(pl.multiple_of(dst0 + first, CHUNK), rows), :], sem.at[sl]))
            return c
        lax.fori_loop(0, N_EXPERTS, per_expert, 0)

    @pl.when(t == 0)
    def _():
        yloc[...] = jnp.zeros(yloc.shape, jnp.uint32)
        fetch(0, 0)

    @pl.when(t + 1 < nt)
    def _():
        fetch(t + 1, 1 - slot)

    _wait_chunks(tot_ref[t], lr // CHUNK, lambda size: pltpu.make_async_copy(
        ys_hbm.at[pl.ds(0, size), :], yloc.at[slot, pl.ds(0, size), :], sem.at[slot]))

    ysw = _unpack_halves(yloc[slot])
    rt = rt_ref[...]
    cols = _iota((tm, lr), 1).astype(F32)
    pick = jnp.where(cols == rt[:, RT_ROW1:RT_ROW1 + 1], 1.0,
                     jnp.where(cols == rt[:, RT_ROW2:RT_ROW2 + 1], 1.0, 0.0)).astype(BF16)
    x = x_ref[...] + gf_ref[0] * jnp.dot(pick, ysw, preferred_element_type=F32)
    if final:
        ms = jnp.mean(x * x, axis=-1, keepdims=True)
        x = x * lax.rsqrt(ms + EPS) * fg_ref[...]
    o_ref[...] = x


def _combine(plan, x2, ys, rt, gf, fg, seq, tm, final):
    m = x2.shape[0]
    per_b = seq // tm
    lr = _local_rows(tm)
    grid_spec = pltpu.PrefetchScalarGridSpec(
        num_scalar_prefetch=4,
        grid=(m // tm,),
        in_specs=[
            pl.BlockSpec((tm, D_MODEL), lambda i, *_: (i, 0)),
            pl.BlockSpec((tm, LANES), lambda i, *_: (i, 0)),
            pl.BlockSpec((1, 1, D_MODEL), lambda i, *_: (i // per_b, 0, 0)),
            pl.BlockSpec((1, D_MODEL), lambda i, *_: (0, 0)),
            pl.BlockSpec(memory_space=pl.ANY),
        ],
        out_specs=pl.BlockSpec((tm, D_MODEL), lambda i, *_: (i, 0)),
        scratch_shapes=[pltpu.VMEM((2, lr, XS_FEAT), jnp.uint32), pltpu.SemaphoreType.DMA((2,))],
    )
    return pl.pallas_call(
        functools.partial(_combine_kernel, lr=lr, final=final),
        grid_spec=grid_spec,
        out_shape=jax.ShapeDtypeStruct((m, D_MODEL), F32),
        compiler_params=_cparams(("arbitrary",)),
        name="moe_combine",
    )(plan["lo8"], plan["gb8"], plan["n8"], plan["tot8"], x2, rt, gf, fg, ys)


def _moe_plan(c, n_blocks):
    i32 = jnp.int32
    blk8 = MOE_BLOCK // CHUNK
    c8 = (c + CHUNK - 1) // CHUNK
    lo8 = jnp.cumsum(c8, axis=1) - c8
    per_e = jnp.sum(c8, axis=0)
    pad8 = (per_e + blk8 - 1) // blk8 * blk8
    end8 = jnp.cumsum(pad8)
    start8 = end8 - pad8
    gb8 = start8[None, :] + jnp.cumsum(c8, axis=0) - c8
    blk_start8 = jnp.arange(n_blocks, dtype=i32) * blk8
    block_e = jnp.minimum(jnp.sum(end8[None, :] <= blk_start8[:, None], axis=1), N_EXPERTS - 1)
    return {
        "lo8": lo8.reshape(-1).astype(i32), "gb8": gb8.reshape(-1).astype(i32),
        "n8": c8.reshape(-1).astype(i32), "tot8": jnp.sum(c8, axis=1).astype(i32),
        "ts8": jnp.concatenate([start8 + per_e, end8[-1:]]).astype(i32),
        "tn8": jnp.concatenate([pad8 - per_e, n_blocks * blk8 - end8[-1:]]).astype(i32),
        "block_e": block_e.astype(i32), "n_used": (end8[-1:] // blk8).astype(i32),
    }


def _w_in_sections(w):
    d_dt = SSD_WIDTH + XBC_WIDTH
    d_q = d_dt + SSD_HEADS
    d_f = d_q + 3 * ATTN_WIDTH
    d_ga = d_f + ATTN_HEADS
    small = jnp.concatenate([w[..., d_dt:d_q], w[..., d_f:d_ga],
                             jnp.zeros(w.shape[:-1] + (LANES - SSD_HEADS - ATTN_HEADS,), w.dtype)], axis=-1)
    sections = [w[..., SSD_WIDTH:d_dt], w[..., :SSD_WIDTH], w[..., d_q:d_f], w[..., d_ga:], small]
    return [sec.astype(BF16) for sec in sections]


def kernel(x, c, ada_w, ada_b, norm_mix_g, w_in, ssd_conv_w, ssd_conv_b, ssd_dt_bias, ssd_a_log,
           ssd_d, ssd_norm_g, fox_f_bias, fox_norm_g, cm_conv_w, cm_conv_b, cm_ln_g, cm_ln_b, w_out,
           norm_ffn_g, w_router_group, b_router_group, w_router_expert, b_router_expert, w_gate,
           w_up, w_down, final_norm_g):
    bsz, seq, d = x.shape
    m = bsz * seq
    tm = min(512, seq)
    tt = min(512, seq)
    n_blocks = -(-(2 * m + (CHUNK - 1) * (m // tm) * N_EXPERTS) // MOE_BLOCK) + N_EXPERTS

    mod = _modulation(c, ada_w, ada_b)
    x2 = x.reshape(m, d)
    w_in_sections = _w_in_sections(w_in)
    for l in range(DEPTH):
        sh_m, sc_m, g_m, sh_f, sc_f, g_f = [v.reshape(bsz, 1, d) for v in jnp.split(mod[l], 6, axis=-1)]
        a_m = norm_mix_g[l][None, None, :] * (1.0 + sc_m)
        proj = _inproj(x2, a_m, sh_m, w_in_sections, l, seq, min(2 * tm, seq))
        y_ssd = _ssd(proj, ssd_conv_w[l], ssd_conv_b[l], ssd_dt_bias[l], ssd_a_log[l],
                     ssd_d[l], ssd_norm_g[l], bsz, seq, tt)
        qa, ka, va, aux, stat = _foxprep(proj, fox_f_bias[l], bsz, seq)
        o_att = _fox(qa, ka, va, aux, stat)
        y_cnv = _conformer(proj, cm_conv_w[l], cm_conv_b[l], cm_ln_g[l], cm_ln_b[l], bsz, seq, tt)

        w_r = jnp.concatenate([w_router_group[l], w_router_expert[l],
                               jnp.zeros((d, LANES - N_GROUPS - N_EXPERTS), F32)], axis=1)
        w_rh = w_r.astype(BF16)
        w_r2 = jnp.concatenate([w_rh, (w_r - w_rh.astype(F32)).astype(BF16)], axis=1)
        b_r = jnp.concatenate([b_router_group[l], b_router_expert[l],
                               jnp.zeros((LANES - N_GROUPS - N_EXPERTS,), F32)]).reshape(1, LANES)
        a_f = norm_ffn_g[l][None, None, :] * (1.0 + sc_f)
        x2, h2, rt, rtt, cnt = _outproj(x2, y_ssd, o_att, y_cnv, w_out[l].astype(BF16),
                                        fox_norm_g[l].reshape(1, -1), g_m, a_f, sh_f, w_r2, b_r,
                                        seq, tm)
        plan = _moe_plan(cnt[:, 0, :N_EXPERTS].astype(jnp.int32), n_blocks)
        xs = _dispatch(plan, h2, rtt, n_blocks * MOE_BLOCK, tm)
        ys = _experts(plan["block_e"], plan["n_used"], xs, w_gate, w_up, w_down, l)
        x2 = _combine(plan, x2, ys, rt, g_f, final_norm_g.reshape(1, d), seq, tm,
                      final=(l == DEPTH - 1))
    return x2.reshape(bsz, seq, d)
```

```python
import functools

import jax
import jax.numpy as jnp
import numpy as np
from jax import lax
from jax.experimental import pallas as pl
from jax.experimental.pallas import tpu as pltpu

F32 = jnp.float32
BF16 = jnp.bfloat16
HIGHEST = lax.Precision.HIGHEST

D_MODEL = 1024
DEPTH = 4
SSD_WIDTH = 512
SSD_HEADS = 8
SSD_HEAD_DIM = 64
SSD_STATE = 128
SSD_CONV = 4
SSD_CHUNK = 128
XBC_WIDTH = 1024
ATTN_WIDTH = 256
ATTN_HEADS = 4
ATTN_HEAD_DIM = 64
CONV_WIDTH = 256
CONV_KERNEL = 31
N_GROUPS = 4
PER_GROUP = 8
N_EXPERTS = 32
D_EXPERT = 512
MOE_BLOCK = 512
EPS = 1e-6

LANES = 128
SUBLANES = 8
BF16_SUBLANES = 16
SSD_HALO = SUBLANES
COL_XBC = 0
COL_Z = 1024
COL_QKV = 1536
COL_GA = 2304
COL_GB = 2560
COL_SM = 2816
NP = 2944
SM_F = 8

VMEM_LIMIT = 56 * 1024 * 1024


def _cparams(sem):
    return pltpu.CompilerParams(dimension_semantics=sem, vmem_limit_bytes=VMEM_LIMIT)


def _sigmoid(x):
    return 1.0 / (1.0 + jnp.exp(-x))


def _silu(x):
    return x * _sigmoid(x)


def _softplus(x):
    return jnp.maximum(x, 0.0) + jnp.log1p(jnp.exp(-jnp.abs(x)))


def _iota(shape, dim):
    return lax.broadcasted_iota(jnp.int32, shape, dim)


def _split_bf16x3(x):
    hi = x.astype(BF16)
    r1 = x - hi.astype(F32)
    mid = r1.astype(BF16)
    lo = (r1 - mid.astype(F32)).astype(BF16)
    return hi, mid, lo


def _dot_onehot_rhs(x, sel):
    return sum(jnp.dot(part, sel, preferred_element_type=F32) for part in _split_bf16x3(x))


def _dot_onehot_lhs(sel, x):
    return sum(jnp.dot(sel, part, preferred_element_type=F32) for part in _split_bf16x3(x))


def _mod_kernel(c_ref, w_ref, b_ref, o_ref):
    cond = _silu(c_ref[...])
    o_ref[0] = jnp.dot(cond, w_ref[0], precision=HIGHEST, preferred_element_type=F32) + b_ref[0]


def _modulation(c, ada_w, ada_b):
    bsz = c.shape[0]
    rows = SUBLANES
    assert bsz <= rows
    cpad = jnp.zeros((rows, D_MODEL), F32).at[:bsz].set(c)
    tn = 1536
    n6 = 6 * D_MODEL
    out = pl.pallas_call(
        _mod_kernel,
        grid=(DEPTH, n6 // tn),
        in_specs=[
            pl.BlockSpec((rows, D_MODEL), lambda l, j: (0, 0)),
            pl.BlockSpec((1, D_MODEL, tn), lambda l, j: (l, 0, j)),
            pl.BlockSpec((1, 1, tn), lambda l, j: (l, 0, j)),
        ],
        out_specs=pl.BlockSpec((1, rows, tn), lambda l, j: (l, 0, j)),
        out_shape=jax.ShapeDtypeStruct((DEPTH, rows, n6), F32),
        compiler_params=_cparams(("arbitrary", "arbitrary")),
        name="adaln_mod",
    )(cpad, ada_w, ada_b.reshape(DEPTH, 1, n6))
    return out[:, :bsz]


def _inproj_kernel(x_ref, a_ref, s_ref, *refs):
    w_refs, o_ref = refs[:-1], refs[-1]
    x = x_ref[...]
    ms = jnp.mean(x * x, axis=-1, keepdims=True)
    h = (x * lax.rsqrt(ms + EPS) * a_ref[0] + s_ref[0]).astype(BF16)
    col = 0
    for w_ref in w_refs:
        width = w_ref.shape[-1]
        o_ref[:, col:col + width] = jnp.dot(h, w_ref[0], preferred_element_type=F32)
        col += width


def _inproj(x2, a, s, w_sections, layer, seq, tm):
    m = x2.shape[0]
    per_b = seq // tm
    assert sum(w.shape[-1] for w in w_sections) == NP
    return pl.pallas_call(
        _inproj_kernel,
        grid=(m // tm,),
        in_specs=[
            pl.BlockSpec((tm, D_MODEL), lambda i: (i, 0)),
            pl.BlockSpec((1, 1, D_MODEL), lambda i: (i // per_b, 0, 0)),
            pl.BlockSpec((1, 1, D_MODEL), lambda i: (i // per_b, 0, 0)),
        ] + [pl.BlockSpec((1, D_MODEL, w.shape[-1]), lambda i: (layer, 0, 0)) for w in w_sections],
        out_specs=pl.BlockSpec((tm, NP), lambda i: (i, 0)),
        out_shape=jax.ShapeDtypeStruct((m, NP), F32),
        compiler_params=_cparams(("arbitrary",)),
        name="in_proj",
    )(x2, a, s, *w_sections)


def _ssd_kernel(z_ref, xbc_ref, sm_ref, cw_ref, cb_ref, dtb_ref, alog_ref, dtbt_ref,
                alogt_ref, e_ref, dx_ref, ng_ref, y_ref, xpad, xc, prev, *, tt):
    t = pl.program_id(1)

    @pl.when(t == 0)
    def _():
        xpad[0:SSD_HALO, :] = jnp.zeros((SSD_HALO, XBC_WIDTH), F32)
        prev[...] = jnp.zeros(prev.shape, F32)

    xpad[SSD_HALO:SSD_HALO + tt, :] = xbc_ref[...]
    acc = jnp.broadcast_to(cb_ref[...], (tt, XBC_WIDTH))
    for k in range(SSD_CONV):
        off = SSD_HALO - (SSD_CONV - 1) + k
        acc = acc + cw_ref[k:k + 1, :] * xpad[off:off + tt, :]
    xc[...] = _silu(acc)
    xpad[0:SSD_HALO, :] = xpad[tt:tt + SSD_HALO, :]

    cl = SSD_CHUNK
    row = _iota((cl, cl), 0)
    col = _iota((cl, cl), 1)
    causal = row >= col
    tril = jnp.where(causal, 1.0, 0.0).astype(BF16)
    triu = jnp.where(row <= col, 1.0, 0.0).astype(BF16)
    lo = col < SSD_HEAD_DIM
    lane1 = _iota((1, LANES), 1)
    a_row = jnp.where(lane1 < SSD_HEADS, -jnp.exp(alog_ref[...]), 0.0)
    a_col = -jnp.exp(alogt_ref[...])
    expand = e_ref[...]

    def chunk(c, carry):
        r0 = c * cl
        xs = xc[pl.ds(r0, cl), 0:SSD_WIDTH]
        bmat = xc[pl.ds(r0, cl), SSD_WIDTH:SSD_WIDTH + 2 * SSD_STATE]
        cmat = xc[pl.ds(r0, cl), SSD_WIDTH + 2 * SSD_STATE:XBC_WIDTH]
        sm = sm_ref[pl.ds(r0, cl), :]
        dt = _softplus(sm + dtb_ref[...])
        da = dt * a_row
        acs = _dot_onehot_lhs(tril, da)
        dtt = _softplus(sm.T[0:SSD_HEADS, :] + dtbt_ref[...])
        acst = _dot_onehot_rhs(dtt * a_col, triu)
        dt_x = _dot_onehot_rhs(dt, expand)
        acs_x = _dot_onehot_rhs(acs, expand)
        last = acs_x[cl - 1:cl, :]
        eacs_x = jnp.exp(acs_x)
        dte_x = jnp.exp(last - acs_x)
        cd_x = jnp.exp(last)
        xdt = xs * dt_x
        xdte = (xdt * dte_x).astype(BF16)
        zc = z_ref[pl.ds(r0, cl), :]
        for g in range(2):
            bg = bmat[:, g * SSD_STATE:(g + 1) * SSD_STATE]
            cg = cmat[:, g * SSD_STATE:(g + 1) * SSD_STATE].astype(BF16)
            bgt = bg.T.astype(BF16)
            cbm = jnp.dot(cg, bgt, preferred_element_type=F32)
            gs = slice(g * 2 * LANES, (g + 1) * 2 * LANES)
            prev_g = prev[:, gs]
            yo_g = jnp.dot(cg, prev_g.astype(BF16), preferred_element_type=F32) * eacs_x[:, gs]
            prev[:, gs] = prev_g * cd_x[:, gs] + jnp.dot(bgt, xdte[:, gs], preferred_element_type=F32)
            pair_out = []
            for j in range(2):
                p = 2 * g + j
                sl = slice(p * LANES, (p + 1) * LANES)
                xp = xdt[:, sl]
                gms, xms = [], []
                for half in range(2):
                    h = 2 * p + half
                    seg = acs[:, h:h + 1] - acst[h:h + 1, :]
                    dec = jnp.exp(jnp.where(causal, seg, -jnp.inf))
                    gms.append((cbm * dec).astype(BF16))
                    own = lo if half == 0 else jnp.logical_not(lo)
                    xms.append(jnp.where(own, xp, 0.0).astype(BF16))
                yd = jnp.dot(jnp.concatenate(gms, axis=1), jnp.concatenate(xms, axis=0),
                             preferred_element_type=F32)
                pair_out.append(yd + yo_g[:, j * LANES:(j + 1) * LANES] + xs[:, sl] * dx_ref[:, sl])
            yg =jnp.concatenate(pair_out, axis=-1) * _silu(zc[:, gs])
            ms = jnp.mean(yg * yg, axis=-1, keepdims=True)
            y_ref[pl.ds(r0, cl), gs] = yg * lax.rsqrt(ms + EPS) * ng_ref[:, gs]
        return carry

    for c in range(tt // cl):
        chunk(c, 0)


def _ssd(proj, cw, cb, dtb, alog, dx, ng, bsz, seq, tt):
    m = bsz * seq
    nt = seq // tt
    pad = LANES - SSD_HEADS
    dtb_row = jnp.pad(dtb, (0, pad)).reshape(1, LANES)
    alog_row = jnp.pad(alog, (0, pad)).reshape(1, LANES)
    dtb_col = jnp.broadcast_to(dtb[:, None], (SSD_HEADS, SSD_CHUNK))
    alog_col = jnp.broadcast_to(alog[:, None], (SSD_HEADS, SSD_CHUNK))
    expand = np.zeros((LANES, SSD_WIDTH), np.float32)
    for h in range(SSD_HEADS):
        expand[h, h * SSD_HEAD_DIM:(h + 1) * SSD_HEAD_DIM] = 1.0
    dx_row = jnp.repeat(dx, SSD_HEAD_DIM).reshape(1, SSD_WIDTH)
    full = lambda shape: pl.BlockSpec(shape, lambda b, t: (0,) * len(shape))
    return pl.pallas_call(
        functools.partial(_ssd_kernel, tt=tt),
        grid=(bsz, nt),
        in_specs=[
            pl.BlockSpec((tt, SSD_WIDTH), lambda b, t: (b * nt + t, COL_Z // SSD_WIDTH)),
            pl.BlockSpec((tt, XBC_WIDTH), lambda b, t: (b * nt + t, COL_XBC // XBC_WIDTH + 0)),
            pl.BlockSpec((tt, LANES), lambda b, t: (b * nt + t, COL_SM // LANES)),
            full((SSD_CONV, XBC_WIDTH)),
            full((1, XBC_WIDTH)),
            full((1, LANES)),
            full((1, LANES)),
            full((SSD_HEADS, SSD_CHUNK)),
            full((SSD_HEADS, SSD_CHUNK)),
            full((LANES, SSD_WIDTH)),
            full((1, SSD_WIDTH)),
            full((1, SSD_WIDTH)),
        ],
        out_specs=pl.BlockSpec((tt, SSD_WIDTH), lambda b, t: (b * nt + t, 0)),
        out_shape=jax.ShapeDtypeStruct((m, SSD_WIDTH), F32),
        scratch_shapes=[
            pltpu.VMEM((tt + SSD_HALO, XBC_WIDTH), F32),
            pltpu.VMEM((tt, XBC_WIDTH), F32),
            pltpu.VMEM((SSD_STATE, SSD_WIDTH), F32),
        ],
        compiler_params=_cparams(("arbitrary", "arbitrary")),
        name="ssd_scan",
    )(proj, proj, proj, cw, cb.reshape(1, XBC_WIDTH), dtb_row, alog_row, dtb_col, alog_col,
      jnp.asarray(expand, dtype=BF16), dx_row, ng.reshape(1, SSD_WIDTH))


LOG2E = 1.4426950408889634
ATT_BLOCK = 256
ATT_TQ = 512
PRUNE_LOG2 = 128.0
FAST_GAP_LOG2 = 60.0
SHIFT_MARGIN_LOG2 = 1.0
AUX_CUMEND = 0
AUX_KMAX = 1
ST_QNORM, ST_CUM, ST_SELF = 0, 1, 2


def _foxprep_kernel(qkv_ref, sm_ref, fb_ref, q_out, k_out, v_out, aux_out, stat_out, carry, *, tt):
    t = pl.program_id(1)

    @pl.when(t == 0)
    def _():
        carry[...] = jnp.zeros(carry.shape, F32)
        aux_out[...] = jnp.zeros(aux_out.shape, F32)

    logit = sm_ref[...] + fb_ref[...]
    logf = -_softplus(-logit)
    row = _iota((tt, tt), 0)
    col = _iota((tt, tt), 1)
    tril = jnp.where(row >= col, 1.0, 0.0).astype(BF16)
    cum = _dot_onehot_lhs(tril, logf) + carry[...]
    carry[...] = cum[tt - 1:tt, :]

    lane = _iota((tt, LANES), 1)
    lane1 = _iota((1, LANES), 1)
    scale = ATTN_HEAD_DIM ** -0.5 * LOG2E
    for h in range(ATTN_HEADS):
        pair, half = h // 2, h % 2
        own = (lane < ATTN_HEAD_DIM) if half == 0 else (lane >= ATTN_HEAD_DIM)
        a0 = ATTN_HEAD_DIM * (1 - half)
        cs = jnp.broadcast_to(cum[:, SM_F + h:SM_F + h + 1], (tt, LANES)) * LOG2E
        hi = cs.astype(BF16).astype(F32)
        r1 = cs - hi
        mid = r1.astype(BF16).astype(F32)
        low = r1 - mid
        qp = qkv_ref[:, pair * LANES:(pair + 1) * LANES]
        kp = qkv_ref[:, ATTN_WIDTH + pair * LANES:ATTN_WIDTH + (pair + 1) * LANES]
        vp = qkv_ref[:, 2 * ATTN_WIDTH + pair * LANES:2 * ATTN_WIDTH + (pair + 1) * LANES]
        qa = jnp.where(lane == a0, hi, jnp.where(lane == a0 + 1, mid, jnp.where(
            lane == a0 + 2, low, jnp.where((lane >= a0 + 3) & (lane < a0 + 6), 1.0, 0.0))))
        ka = jnp.where(lane == a0 + 3, -hi, jnp.where(lane == a0 + 4, -mid, jnp.where(
            lane == a0 + 5, -low, jnp.where((lane >= a0) & (lane < a0 + 9), 1.0, 0.0))))
        qb = jnp.where(own, qp * scale, qa).astype(BF16)
        q_out[0, h] = qb
        kb = jnp.where(own, kp, ka).astype(BF16)
        k_out[0, h] = kb
        qf = jnp.where(own, qb.astype(F32), 0.0)
        qn = jnp.sqrt(jnp.sum(qf * qf, axis=-1, keepdims=True))
        s_self = jnp.sum(qf * kb.astype(F32), axis=-1, keepdims=True)
        stat = jnp.where(lane == ST_QNORM, qn, jnp.where(lane == ST_CUM, hi + mid + low, jnp.where(
            lane == ST_SELF, s_self, 0.0)))
        stat_out[0, h] = stat.T[0:SUBLANES, :]
        v_out[0, h] = jnp.where(own, vp, jnp.where(lane == a0, 1.0, 0.0)).astype(BF16)
        kf = jnp.where(own, kb.astype(F32), 0.0)
        kn2_rows = jnp.sum(kf * kf, axis=-1, keepdims=True)
        for blk in range(tt // ATT_BLOCK):
            end = (blk + 1) * ATT_BLOCK
            kn2 = jnp.max(kn2_rows[end - ATT_BLOCK:end, :], axis=0, keepdims=True)
            here = lane1 == t * (tt // ATT_BLOCK) + blk
            aux_out[0, h, AUX_CUMEND:AUX_CUMEND + 1, :] = jnp.where(
                here, cs[end - 1:end, :], aux_out[0, h, AUX_CUMEND:AUX_CUMEND + 1, :])
            aux_out[0, h, AUX_KMAX:AUX_KMAX + 1, :] = jnp.where(
                here, jnp.sqrt(kn2), aux_out[0, h, AUX_KMAX:AUX_KMAX + 1, :])


def _foxprep(proj, fb, bsz, seq):
    tt = ATT_BLOCK
    nt = seq // tt
    assert tt % ATT_BLOCK == 0 and seq // ATT_BLOCK <= LANES
    fb_row = jnp.zeros((1, LANES), F32).at[0, SM_F:SM_F + ATTN_HEADS].set(fb)
    shp = jax.ShapeDtypeStruct((bsz, ATTN_HEADS, seq, LANES), BF16)
    ospec = pl.BlockSpec((1, ATTN_HEADS, tt, LANES), lambda b, t: (b, 0, t, 0))
    return pl.pallas_call(
        functools.partial(_foxprep_kernel, tt=tt),
        grid=(bsz, nt),
        in_specs=[
            pl.BlockSpec((tt, 3 * ATTN_WIDTH), lambda b, t: (b * nt + t, COL_QKV // (3 * ATTN_WIDTH))),
            pl.BlockSpec((tt, LANES), lambda b, t: (b * nt + t, COL_SM // LANES)),
            pl.BlockSpec((1, LANES), lambda b, t: (0, 0)),
        ],
        out_specs=[ospec, ospec, ospec,
                   pl.BlockSpec((1, ATTN_HEADS, SUBLANES,LANES), lambda b, t: (b, 0, 0, 0)),
                   pl.BlockSpec((1, ATTN_HEADS, SUBLANES,tt), lambda b, t: (b, 0, 0, t))],
        out_shape=[shp, shp, shp, jax.ShapeDtypeStruct((bsz, ATTN_HEADS, SUBLANES,LANES), F32),
                   jax.ShapeDtypeStruct((bsz, ATTN_HEADS, SUBLANES,seq), F32)],
        scratch_shapes=[pltpu.VMEM((1, LANES), F32)],
        compiler_params=_cparams(("arbitrary", "arbitrary")),
        name="fox_prep",
    )(proj, proj, fb_row)


def _fox_kernel(q_ref, k_ref, v_ref, aux_ref, stat_ref, o_ref, *, tq, kb):
    i = pl.program_id(2)
    ib = i * (tq // kb)
    nt = (((1,), (1,)), ((), ()))
    lane = _iota((tq, LANES), 1)
    lane1 = _iota((1, LANES), 1)
    qs = (q_ref[0, 0], q_ref[0, 1])

    def step(hh, r0, width, m, acc, mask):
        kblk = k_ref[0, hh, pl.ds(r0, width), :]
        vblk = v_ref[0, hh, pl.ds(r0, width), :]
        s = lax.dot_general(qs[hh], kblk, nt, preferred_element_type=F32)
        if mask is not None:
            s = jnp.where(mask, s, -jnp.inf)
        m_new = jnp.maximum(m, jnp.max(s, axis=-1, keepdims=True))
        p = jnp.exp2(s - m_new)
        alpha = jnp.exp2(m - m_new)
        acc = acc * alpha + jnp.dot(p.astype(BF16), vblk, preferred_element_type=F32)
        return m_new, acc

    d0 = pl.multiple_of(i * tq, tq)
    diag = _iota((tq, tq), 0) >= _iota((tq, tq), 1)
    ib_f = ib.astype(F32)
    first = []
    gaps = []
    q_shift = []
    for hh in range(2):
        a0 = ATTN_HEAD_DIM * (1 - hh)
        st = stat_ref[0, hh]
        qn = st[ST_QNORM:ST_QNORM + 1, :]
        cum_t = st[ST_CUM:ST_CUM + 1, :]
        s_self = st[ST_SELF:ST_SELF + 1, :]
        aux = aux_ref[0, hh]
        cend = aux[AUX_CUMEND:AUX_CUMEND + 1, :]
        kmax = jnp.max(jnp.where(lane1 < ib + tq // kb, aux[AUX_KMAX:AUX_KMAX + 1, :], 0.0), axis=-1,
                       keepdims=True)
        bound = qn * kmax
        slack = jnp.max(bound + cum_t - s_self, axis=-1, keepdims=True)
        live = (lane1 < ib) & (slack - cend > -PRUNE_LOG2)
        first.append(jnp.min(jnp.where(live, lane1.astype(F32), ib_f)))
        parts = _split_bf16x3(bound + SHIFT_MARGIN_LOG2)
        gaps.append(jnp.max(sum(part.astype(F32) for part in parts) - s_self))
        rows_t = BF16_SUBLANES
        terms = jnp.concatenate([-part.astype(F32) for part in parts]
                                + [jnp.zeros((rows_t - len(parts), tq), F32)], axis=0).astype(BF16)
        place = jnp.where(_iota((rows_t, LANES), 1) == _iota((rows_t, LANES), 0) + (a0 + 6), 1.0,
                          0.0).astype(BF16)
        q_shift.append(qs[hh] + lax.dot_general(terms, place, (((0,), (0,)), ((), ())),
                                                preferred_element_type=F32).astype(BF16))
    j_first = jnp.minimum(first[0], first[1]).astype(jnp.int32)

    def fast(_):
        def blocks(c, r0, width, mask=None):
            out = []
            for hh in range(2):
                s = lax.dot_general(q_shift[hh], k_ref[0, hh, pl.ds(r0, width), :], nt,
                                    preferred_element_type=F32)
                if mask is not None:
                    s = jnp.where(mask, s, -jnp.inf)
                out.append(c[hh] + jnp.dot(jnp.exp2(s).astype(BF16), v_ref[0, hh, pl.ds(r0, width), :],
                                           preferred_element_type=F32))
            return tuple(out)

        zero = jnp.zeros((tq, LANES), F32)
        c = blocks((zero, zero), d0, tq, diag)
        j_lo = j_first - ((ib - j_first) & 1)
        n = ib - j_lo
        c = lax.cond((n & 2) == 2, lambda c: blocks(c, pl.multiple_of(j_lo * kb, kb), 2 * kb),
                     lambda c: c, c)
        j0 = j_lo + (n & 2)

        def body(jj, c):
            return blocks(c, pl.multiple_of((j0 + 4 * jj) * kb, kb), 4 * kb)
        return lax.fori_loop(0, lax.shift_right_logical(n, 2), body, c)

    def online(_):
        m_init = jnp.full((tq, 1), -jnp.inf, F32)
        acc_init = jnp.zeros((tq, LANES), F32)
        carry = step(0, d0, tq, m_init, acc_init, diag) + step(1, d0, tq, m_init, acc_init, diag)

        def body(j, c):
            r0 = pl.multiple_of(j * kb, kb)
            m0, a0_, m1, a1_ = c
            m0, a0_ = step(0, r0, kb, m0, a0_, None)
            m1, a1_ = step(1, r0, kb, m1, a1_, None)
            return m0, a0_, m1, a1_
        _, a0_, _, a1_ = lax.fori_loop(j_first, ib, body, carry)
        return a0_, a1_

    acc0, acc1 = lax.cond(jnp.maximum(gaps[0], gaps[1]) <= FAST_GAP_LOG2, fast, online, None)
    den0 = jnp.sum(jnp.where(lane == ATTN_HEAD_DIM, acc0, 0.0), axis=-1, keepdims=True)
    den1 = jnp.sum(jnp.where(lane == 0, acc1, 0.0), axis=-1, keepdims=True)
    o_ref[...] = jnp.where(lane < ATTN_HEAD_DIM, acc0 / den0, acc1 / den1)


def _fox(qa, ka, va, aux, stat):
    bsz, nh, seq, _ = qa.shape
    tq = min(ATT_TQ, seq)
    assert (tq // ATT_BLOCK) % 2 == 0
    nq = seq // tq
    kv_spec = pl.BlockSpec((1, 2, seq, LANES), lambda b, p, i: (b, p, 0, 0))
    return pl.pallas_call(
        functools.partial(_fox_kernel, tq=tq, kb=ATT_BLOCK),
        grid=(bsz, nh // 2, nq),
        in_specs=[pl.BlockSpec((1, 2, tq, LANES), lambda b, p, i: (b, p, i, 0)), kv_spec, kv_spec,
                  pl.BlockSpec((1, 2, SUBLANES,LANES), lambda b, p, i: (b, p, 0, 0)),
                  pl.BlockSpec((1, 2, SUBLANES,tq), lambda b, p, i: (b, p, 0, i))],
        out_specs=pl.BlockSpec((tq, LANES), lambda b, p, i: (b * nq + i, p)),
        out_shape=jax.ShapeDtypeStruct((bsz * seq, ATTN_WIDTH), F32),
        compiler_params=_cparams(("arbitrary", "arbitrary", "arbitrary")),
        name="fox_attn",
    )(qa, ka, va, aux, stat)


CONF_HALO = 32
CONF_ROWS = 64


def _conf_kernel(ga_ref, gb_ref, w_ref, b_ref, lg_ref, lb_ref, y_ref, upad, ush, *, tt):
    t = pl.program_id(1)

    @pl.when(t == 0)
    def _():
        upad[0:CONF_HALO, :] = jnp.zeros((CONF_HALO, CONV_WIDTH), F32)

    upad[CONF_HALO:CONF_HALO + tt, :] = ga_ref[...] * _sigmoid(gb_ref[...])
    base = CONF_HALO - (CONV_KERNEL - 1)
    for ph in range(1, SUBLANES):
        ush[ph - 1] = upad[ph:ph + ush.shape[1], :]
    for r in range(tt // CONF_ROWS):
        r0 = r * CONF_ROWS
        acc = jnp.broadcast_to(b_ref[...], (CONF_ROWS, CONV_WIDTH))
        for k in range(CONV_KERNEL):
            ph = (base + k) % SUBLANES
            lo = r0 + base + k - ph
            src = upad[lo:lo + CONF_ROWS, :] if ph == 0 else ush[ph - 1, lo:lo + CONF_ROWS, :]
            acc = acc + w_ref[k:k + 1, :] * src
        mu = jnp.mean(acc, axis=-1, keepdims=True)
        cen = acc - mu
        var = jnp.mean(cen * cen, axis=-1, keepdims=True)
        y = cen * lax.rsqrt(var + EPS) * lg_ref[...] + lb_ref[...]
        y_ref[r0:r0 + CONF_ROWS, :] = _silu(y)
    upad[0:CONF_HALO, :] = upad[tt:tt + CONF_HALO, :]


def _conformer(proj, w, b, lg, lb, bsz, seq, tt):
    m = bsz * seq
    nt = seq // tt
    full = lambda shape: pl.BlockSpec(shape, lambda bb, t: (0,) * len(shape))
    return pl.pallas_call(
        functools.partial(_conf_kernel, tt=tt),
        grid=(bsz, nt),
        in_specs=[
            pl.BlockSpec((tt, CONV_WIDTH), lambda bb, t: (bb * nt + t, COL_GA // CONV_WIDTH)),
            pl.BlockSpec((tt, CONV_WIDTH), lambda bb, t: (bb * nt + t, COL_GB // CONV_WIDTH)),
            full((CONV_KERNEL, CONV_WIDTH)),
            full((1, CONV_WIDTH)),
            full((1, CONV_WIDTH)),
            full((1, CONV_WIDTH)),
        ],
        out_specs=pl.BlockSpec((tt, CONV_WIDTH), lambda bb, t: (bb * nt + t, 0)),
        out_shape=jax.ShapeDtypeStruct((m, CONV_WIDTH), F32),
        scratch_shapes=[pltpu.VMEM((tt + CONF_HALO, CONV_WIDTH), F32),
                        pltpu.VMEM((SUBLANES - 1, tt + CONF_HALO - SUBLANES, CONV_WIDTH), F32)],
        compiler_params=_cparams(("arbitrary", "arbitrary")),
        name="conformer_conv",
    )(proj, proj, w, b.reshape(1, -1), lg.reshape(1, -1), lb.reshape(1, -1))


ROUTE_BIG = 1e9
CHUNK = 8
XS_FEAT = D_MODEL // 2
XS_WIDTH = XS_FEAT + LANES
RT_W1, RT_W2, RT_ROW1, RT_ROW2 = 0, 1, 2, 3


def _split3(x):
    hi = x.astype(BF16)
    lo = (x - hi.astype(F32)).astype(BF16)
    return hi, lo


def _outproj_kernel(x_ref, ys_ref, oa_ref, yc_ref, wo_ref, fg_ref, gm_ref, a2_ref, s2_ref,
                    wr_ref, br_ref, xn_ref, h2_ref, rt_ref, rtt_ref, cnt_ref):
    tm = x_ref.shape[0]
    lane = _iota((tm, LANES), 1)
    att = oa_ref[...]
    ms = jnp.mean(att * att, axis=-1, keepdims=True)
    att = att * lax.rsqrt(ms + EPS) * fg_ref[...]
    y = jnp.dot(ys_ref[...].astype(BF16), wo_ref[0:SSD_WIDTH, :], preferred_element_type=F32)
    y = y + jnp.dot(att.astype(BF16), wo_ref[SSD_WIDTH:SSD_WIDTH + ATTN_WIDTH, :],
                    preferred_element_type=F32)
    y = y + jnp.dot(yc_ref[...].astype(BF16), wo_ref[SSD_WIDTH + ATTN_WIDTH:, :],
                    preferred_element_type=F32)
    xn = x_ref[...] + gm_ref[0] * y
    xn_ref[...] = xn
    ms2 = jnp.mean(xn * xn, axis=-1, keepdims=True)
    h2 = xn * lax.rsqrt(ms2 + EPS) * a2_ref[0] + s2_ref[0]
    h2_ref[...] = h2.astype(BF16)

    hh, hl = _split3(h2)
    part = jnp.dot(jnp.concatenate([hh, hl], axis=0), wr_ref[...], preferred_element_type=F32)
    logits = (part[:tm, :LANES] + part[:tm, LANES:] + part[tm:, :LANES] + part[tm:, LANES:]) + br_ref[...]
    lanef = lane.astype(F32)
    lg = jnp.where(lane < N_GROUPS, logits, -jnp.inf)
    gmax = jnp.max(lg, axis=-1, keepdims=True)
    gsum = jnp.sum(jnp.exp(lg - gmax), axis=-1, keepdims=True)
    gidx = jnp.min(jnp.where(lg == gmax, lanef, ROUTE_BIG), axis=-1, keepdims=True)
    e_lo = N_GROUPS + PER_GROUP * gidx
    le = jnp.where((lanef >= e_lo) & (lanef < e_lo + PER_GROUP), logits, -jnp.inf)
    m1 = jnp.max(le, axis=-1, keepdims=True)
    i1 = jnp.min(jnp.where(le == m1, lanef, ROUTE_BIG), axis=-1, keepdims=True)
    le2 = jnp.where(lanef == i1, -jnp.inf, le)
    m2 = jnp.max(le2, axis=-1, keepdims=True)
    i2 = jnp.min(jnp.where(le2 == m2, lanef, ROUTE_BIG), axis=-1, keepdims=True)
    esum = jnp.sum(jnp.exp(le - m1), axis=-1, keepdims=True)
    p1 = 1.0 / esum
    p2 = jnp.exp(m2 - m1) / esum
    psel = 1.0 / gsum
    w1 = p1 / (p1 + p2) * psel
    w2 = p2 / (p1 + p2) * psel

    oh1 = (lanef == (i1 - N_GROUPS)).astype(F32)
    oh2 = (lanef == (i2 - N_GROUPS)).astype(F32)
    oh = oh1 + oh2
    cnt = jnp.sum(oh, axis=0, keepdims=True)
    before = (_iota((tm, tm), 0) > _iota((tm, tm), 1)).astype(BF16)
    rank = jnp.dot(before, oh.astype(BF16), preferred_element_type=F32)
    chunks = jnp.floor((cnt + (CHUNK - 1)) * (1.0 / CHUNK))
    below = (_iota((LANES, LANES), 0) < _iota((LANES, LANES), 1)).astype(BF16)
    seg_lo = CHUNK * jnp.dot(jnp.broadcast_to(chunks, (SUBLANES, LANES)).astype(BF16), below,
                             preferred_element_type=F32)[0:1, :]
    lr1 = jnp.sum(oh1 * (rank + seg_lo), axis=-1, keepdims=True)
    lr2 = jnp.sum(oh2 * (rank + seg_lo), axis=-1, keepdims=True)
    route = jnp.where(lane == RT_W1, w1, jnp.where(lane == RT_W2, w2, jnp.where(
        lane == RT_ROW1, lr1, jnp.where(lane == RT_ROW2, lr2, 0.0))))
    rt_ref[...] = route
    rtt_ref[0] = route.T[0:SUBLANES, :]
    cnt_ref[0] = jnp.broadcast_to(cnt, (SUBLANES, LANES))


def _outproj(x2, ys, oa, yc, wo, fg, gm, a2, s2, wr, br, seq, tm):
    m = x2.shape[0]
    per_b = seq // tm
    nt = m // tm
    row = lambda w: pl.BlockSpec((tm, w), lambda i: (i, 0))
    bvec = pl.BlockSpec((1, 1, D_MODEL), lambda i: (i // per_b, 0, 0))
    full = lambda shape: pl.BlockSpec(shape, lambda i: (0,) * len(shape))
    return pl.pallas_call(
        _outproj_kernel,
        grid=(nt,),
        in_specs=[
            row(D_MODEL), row(SSD_WIDTH), row(ATTN_WIDTH), row(CONV_WIDTH),
            full((D_MODEL, D_MODEL)), full((1, ATTN_WIDTH)), bvec, bvec, bvec,
            full((D_MODEL, 2 * LANES)), full((1, LANES)),
        ],
        out_specs=[row(D_MODEL), row(D_MODEL), row(LANES),
                   pl.BlockSpec((1, SUBLANES, tm), lambda i: (i, 0, 0)),
                   pl.BlockSpec((1, SUBLANES, LANES), lambda i: (i, 0, 0))],
        out_shape=[jax.ShapeDtypeStruct((m, D_MODEL), F32), jax.ShapeDtypeStruct((m, D_MODEL), BF16),
                   jax.ShapeDtypeStruct((m, LANES), F32), jax.ShapeDtypeStruct((nt, SUBLANES,tm), F32),
                   jax.ShapeDtypeStruct((nt, SUBLANES,LANES), F32)],
        compiler_params=_cparams(("arbitrary",)),
        name="out_proj_router",
    )(x2, ys, oa, yc, wo, fg, gm, a2, s2, wr, br)


def _local_rows(tm):
    return -(-(2 * tm + (CHUNK - 1) * N_EXPERTS) // LANES) * LANES


def _pack_halves(x):
    w = x.shape[-1] // 2
    return pltpu.bitcast(x[:, :w], jnp.uint32) | lax.shift_right_logical(
        pltpu.bitcast(x[:, w:], jnp.uint32), jnp.uint32(16))


def _unpack_halves(p):
    hi = pltpu.bitcast(p & jnp.uint32(0xFFFF0000), F32)
    lo = pltpu.bitcast(lax.shift_left(p, jnp.uint32(16)), F32)
    return jnp.concatenate([hi, lo], axis=-1).astype(BF16)


def _start_segment(n, copy_at):
    quads = lax.shift_right_logical(n, 2)

    def per_quad(k, c):
        copy_at(pl.multiple_of(k * (4 * CHUNK), CHUNK), 4 * CHUNK).start()
        return c
    lax.fori_loop(0, quads, per_quad, 0)

    @pl.when((n & 2) == 2)
    def _():
        copy_at(pl.multiple_of(quads * (4 * CHUNK), CHUNK), 2 * CHUNK).start()

    @pl.when((n & 1) == 1)
    def _():
        copy_at(pl.multiple_of((quads * 4 + (n & 2)) * CHUNK, CHUNK), CHUNK).start()


def _wait_chunks(n, n_max, copy_of_size):
    for bit in range(n_max.bit_length()):
        @pl.when((lax.shift_right_logical(n, bit) & 1) == 1)
        def _():
            copy_of_size(CHUNK << bit).wait()


def _dispatch_kernel(lo_ref, gb_ref, n8_ref, tot_ref, ts_ref, tn_ref, h_ref, rtt_ref, xs_hbm,
                     xloc, zeros, sem, zsem, *, lr):
    t = pl.program_id(0)
    nt = pl.num_programs(0)
    slot = t % 2
    tm = h_ref.shape[0]

    def wait_tile(tile, sl):
        _wait_chunks(tot_ref[tile], lr // CHUNK, lambda size: pltpu.make_async_copy(
            xloc.at[sl, pl.ds(0, size), :], xs_hbm.at[pl.ds(0, size), :], sem.at[sl]))

    @pl.when(t >= 2)
    def _():
        wait_tile(t - 2, slot)

    rows = _iota((lr, tm), 0).astype(F32)
    is1 = rows == rtt_ref[0, RT_ROW1:RT_ROW1 + 1, :]
    is2 = rows == rtt_ref[0, RT_ROW2:RT_ROW2 + 1, :]
    perm = jnp.where(is1, 1.0, jnp.where(is2, 1.0, 0.0))
    xloc[slot, :, 0:XS_FEAT] = _pack_halves(
        jnp.dot(perm.astype(BF16), h_ref[...], preferred_element_type=F32))
    w_row = jnp.sum(jnp.where(is1, rtt_ref[0, RT_W1:RT_W1 + 1, :],
                              jnp.where(is2, rtt_ref[0, RT_W2:RT_W2 + 1, :], 0.0)), axis=-1, keepdims=True)
    xloc[slot, :, XS_FEAT:XS_WIDTH] = jnp.where(
        _iota((lr, LANES), 1) == 0, pltpu.bitcast(jnp.broadcast_to(w_row, (lr, LANES)), jnp.uint32),
        jnp.uint32(0))

    def per_expert(e, c):
        idx = t * N_EXPERTS + e
        src0 = lo_ref[idx] * CHUNK
        dst0 = gb_ref[idx] * CHUNK
        _start_segment(n8_ref[idx], lambda first, rows: pltpu.make_async_copy(
            xloc.at[slot, pl.ds(pl.multiple_of(src0 + first, CHUNK), rows), :],
            xs_hbm.at[pl.ds(pl.multiple_of(dst0 + first, CHUNK), rows), :], sem.at[slot]))
        return c
    lax.fori_loop(0, N_EXPERTS, per_expert, 0)

    @pl.when(t == nt - 1)
    def _():
        zeros[...] = jnp.zeros(zeros.shape, jnp.uint32)

        def zero_copy(dst):
            return pltpu.make_async_copy(zeros, xs_hbm.at[pl.ds(dst, CHUNK), :], zsem.at[0])

        def fill(e, c):
            dst0 = ts_ref[e] * CHUNK

            def one(k, c2):
                zero_copy(pl.multiple_of(dst0 + k * CHUNK, CHUNK)).start()
                return c2
            lax.fori_loop(0, tn_ref[e], one, 0)
            return c
        lax.fori_loop(0, N_EXPERTS + 1, fill, 0)

        def drain(e, c):
            def one(k, c2):
                zero_copy(0).wait()
                return c2
            lax.fori_loop(0, tn_ref[e], one, 0)
            return c
        lax.fori_loop(0, N_EXPERTS + 1, drain, 0)

        @pl.when(t >= 1)
        def _():
            wait_tile(t - 1, 1 - slot)
        wait_tile(t, slot)


def _dispatch(plan, h2, rtt, n_rows, tm):
    m = h2.shape[0]
    lr = _local_rows(tm)
    grid_spec = pltpu.PrefetchScalarGridSpec(
        num_scalar_prefetch=6,
        grid=(m // tm,),
        in_specs=[pl.BlockSpec((tm, D_MODEL), lambda t, *_: (t, 0)),
                  pl.BlockSpec((1, SUBLANES, tm), lambda t, *_: (t, 0, 0))],
        out_specs=pl.BlockSpec(memory_space=pl.ANY),
        scratch_shapes=[pltpu.VMEM((2, lr, XS_WIDTH), jnp.uint32),
                        pltpu.VMEM((CHUNK, XS_WIDTH), jnp.uint32),
                        pltpu.SemaphoreType.DMA((2,)), pltpu.SemaphoreType.DMA((1,))],
    )
    return pl.pallas_call(
        functools.partial(_dispatch_kernel, lr=lr),
        grid_spec=grid_spec,
        out_shape=jax.ShapeDtypeStruct((n_rows, XS_WIDTH), jnp.uint32),
        compiler_params=_cparams(("arbitrary",)),
        name="moe_dispatch",
    )(plan["lo8"], plan["gb8"], plan["n8"], plan["tot8"], plan["ts8"], plan["tn8"], h2, rtt)


def _expert_kernel(be_ref, nu_ref, x_ref, wg_ref, wu_ref, wd_ref, y_ref, wgub, wdb):
    i = pl.program_id(0)

    @pl.when(i < nu_ref[0])
    def _():
        prev_e = be_ref[jnp.maximum(i - 1, 0)]

        @pl.when((i == 0) | (be_ref[i] != prev_e))
        def _():
            wgub[:, 0:D_EXPERT] = wg_ref[0, 0].astype(BF16)
            wgub[:, D_EXPERT:2 * D_EXPERT] = wu_ref[0, 0].astype(BF16)
            wdb[...] = wd_ref[0, 0].astype(BF16)

        x = _unpack_halves(x_ref[:, 0:XS_FEAT])
        w_row = pltpu.bitcast(x_ref[:, XS_FEAT:XS_FEAT + 1], F32)
        gu = jnp.dot(x, wgub[...], preferred_element_type=F32)
        hid = _silu(gu[:, 0:D_EXPERT]) * gu[:, D_EXPERT:2 * D_EXPERT]
        y = jnp.dot(hid.astype(BF16), wdb[...], preferred_element_type=F32)
        y_ref[...] = _pack_halves((y * w_row).astype(BF16).astype(F32))

    @pl.when(i >= nu_ref[0])
    def _():
        y_ref[...] = jnp.zeros(y_ref.shape, jnp.uint32)


def _experts(block_e, n_used, xs, wg, wu, wd, layer):
    n_rows = xs.shape[0]
    wspec = lambda shape: pl.BlockSpec((1, 1) + shape, lambda i, be, nu: (layer, be[i], 0, 0))
    grid_spec = pltpu.PrefetchScalarGridSpec(
        num_scalar_prefetch=2,
        grid=(n_rows // MOE_BLOCK,),
        in_specs=[
            pl.BlockSpec((MOE_BLOCK, XS_WIDTH), lambda i, be, nu: (i, 0)),
            wspec((D_MODEL, D_EXPERT)), wspec((D_MODEL, D_EXPERT)), wspec((D_EXPERT, D_MODEL)),
        ],
        out_specs=pl.BlockSpec((MOE_BLOCK, XS_FEAT), lambda i, be, nu: (i, 0)),
        scratch_shapes=[
            pltpu.VMEM((D_MODEL, 2 * D_EXPERT), BF16),
            pltpu.VMEM((D_EXPERT, D_MODEL), BF16),
        ],
    )
    return pl.pallas_call(
        _expert_kernel,
        grid_spec=grid_spec,
        out_shape=jax.ShapeDtypeStruct((n_rows, XS_FEAT), jnp.uint32),
        compiler_params=_cparams(("arbitrary",)),
        name="moe_experts",
    )(block_e, n_used, xs, wg, wu, wd)


def _combine_kernel(lo_ref, gb_ref, n8_ref, tot_ref, x_ref, rt_ref, gf_ref, fg_ref, ys_hbm,
                    o_ref, yloc, sem, *, lr, final):
    t = pl.program_id(0)
    nt = pl.num_programs(0)
    slot = t % 2
    tm = x_ref.shape[0]

    def fetch(tile, sl):
        def per_expert(e, c):
            idx = tile * N_EXPERTS + e
            dst0 = lo_ref[idx] * CHUNK
            src0 = gb_ref[idx] * CHUNK
            _start_segment(n8_ref[idx], lambda first, rows: pltpu.make_async_copy(
                ys_hbm.at[pl.ds(pl.multiple_of(src0 + first, CHUNK), rows), :],
                yloc.at[sl, pl.ds(pl.multiple_of(dst0 + first, CHUNK), rows), :], sem.at[sl]))
            return c
        lax.fori_loop(0, N_EXPERTS, per_expert, 0)

    @pl.when(t == 0)
    def _():
        yloc[...] = jnp.zeros(yloc.shape, jnp.uint32)
        fetch(0, 0)

    @pl.when(t + 1 < nt)
    def _():
        fetch(t + 1, 1 - slot)

    _wait_chunks(tot_ref[t], lr // CHUNK, lambda size: pltpu.make_async_copy(
        ys_hbm.at[pl.ds(0, size), :], yloc.at[slot, pl.ds(0, size), :], sem.at[slot]))

    ysw = _unpack_halves(yloc[slot])
    rt = rt_ref[...]
    cols = _iota((tm, lr), 1).astype(F32)
    pick = jnp.where(cols == rt[:, RT_ROW1:RT_ROW1 + 1], 1.0,
                     jnp.where(cols == rt[:, RT_ROW2:RT_ROW2 + 1], 1.0, 0.0)).astype(BF16)
    x = x_ref[...] + gf_ref[0] * jnp.dot(pick, ysw, preferred_element_type=F32)
    if final:
        ms = jnp.mean(x * x, axis=-1, keepdims=True)
        x = x * lax.rsqrt(ms + EPS) * fg_ref[...]
    o_ref[...] = x


def _combine(plan, x2, ys, rt, gf, fg, seq, tm, final):
    m = x2.shape[0]
    per_b = seq // tm
    lr = _local_rows(tm)
    grid_spec = pltpu.PrefetchScalarGridSpec(
        num_scalar_prefetch=4,
        grid=(m // tm,),
        in_specs=[
            pl.BlockSpec((tm, D_MODEL), lambda i, *_: (i, 0)),
            pl.BlockSpec((tm, LANES), lambda i, *_: (i, 0)),
            pl.BlockSpec((1, 1, D_MODEL), lambda i, *_: (i // per_b, 0, 0)),
            pl.BlockSpec((1, D_MODEL), lambda i, *_: (0, 0)),
            pl.BlockSpec(memory_space=pl.ANY),
        ],
        out_specs=pl.BlockSpec((tm, D_MODEL), lambda i, *_: (i, 0)),
        scratch_shapes=[pltpu.VMEM((2, lr, XS_FEAT), jnp.uint32), pltpu.SemaphoreType.DMA((2,))],
    )
    return pl.pallas_call(
        functools.partial(_combine_kernel, lr=lr, final=final),
        grid_spec=grid_spec,
        out_shape=jax.ShapeDtypeStruct((m, D_MODEL), F32),
        compiler_params=_cparams(("arbitrary",)),
        name="moe_combine",
    )(plan["lo8"], plan["gb8"], plan["n8"], plan["tot8"], x2, rt, gf, fg, ys)


def _moe_plan(c, n_blocks):
    i32 = jnp.int32
    blk8 = MOE_BLOCK // CHUNK
    c8 = (c + CHUNK - 1) // CHUNK
    lo8 = jnp.cumsum(c8, axis=1) - c8
    per_e = jnp.sum(c8, axis=0)
    pad8 = (per_e + blk8 - 1) // blk8 * blk8
    end8 = jnp.cumsum(pad8)
    start8 = end8 - pad8
    gb8 = start8[None, :] + jnp.cumsum(c8, axis=0) - c8
    blk_start8 = jnp.arange(n_blocks, dtype=i32) * blk8
    block_e = jnp.minimum(jnp.sum(end8[None, :] <= blk_start8[:, None], axis=1), N_EXPERTS - 1)
    return {
        "lo8": lo8.reshape(-1).astype(i32), "gb8": gb8.reshape(-1).astype(i32),
        "n8": c8.reshape(-1).astype(i32), "tot8": jnp.sum(c8, axis=1).astype(i32),
        "ts8": jnp.concatenate([start8 + per_e, end8[-1:]]).astype(i32),
        "tn8": jnp.concatenate([pad8 - per_e, n_blocks * blk8 - end8[-1:]]).astype(i32),
        "block_e": block_e.astype(i32), "n_used": (end8[-1:] // blk8).astype(i32),
    }


def _w_in_sections(w):
    d_dt = SSD_WIDTH + XBC_WIDTH
    d_q = d_dt + SSD_HEADS
    d_f = d_q + 3 * ATTN_WIDTH
    d_ga = d_f + ATTN_HEADS
    small = jnp.concatenate([w[..., d_dt:d_q], w[..., d_f:d_ga],
                             jnp.zeros(w.shape[:-1] + (LANES - SSD_HEADS - ATTN_HEADS,), w.dtype)], axis=-1)
    sections = [w[..., SSD_WIDTH:d_dt], w[..., :SSD_WIDTH], w[..., d_q:d_f], w[..., d_ga:], small]
    return [sec.astype(BF16) for sec in sections]


def kernel(x, c, ada_w, ada_b, norm_mix_g, w_in, ssd_conv_w, ssd_conv_b, ssd_dt_bias, ssd_a_log,
           ssd_d, ssd_norm_g, fox_f_bias, fox_norm_g, cm_conv_w, cm_conv_b, cm_ln_g, cm_ln_b, w_out,
           norm_ffn_g, w_router_group, b_router_group, w_router_expert, b_router_expert, w_gate,
           w_up, w_down, final_norm_g):
    bsz, seq, d = x.shape
    m = bsz * seq
    tm = min(512, seq)
    tt = min(512, seq)
    n_blocks = -(-(2 * m + (CHUNK - 1) * (m // tm) * N_EXPERTS) // MOE_BLOCK) + N_EXPERTS

    mod = _modulation(c, ada_w, ada_b)
    x2 = x.reshape(m, d)
    w_in_sections = _w_in_sections(w_in)
    for l in range(DEPTH):
        sh_m, sc_m, g_m, sh_f, sc_f, g_f = [v.reshape(bsz, 1, d) for v in jnp.split(mod[l], 6, axis=-1)]
        a_m = norm_mix_g[l][None, None, :] * (1.0 + sc_m)
        proj = _inproj(x2, a_m, sh_m, w_in_sections, l, seq, min(2 * tm, seq))
        y_ssd = _ssd(proj, ssd_conv_w[l], ssd_conv_b[l], ssd_dt_bias[l], ssd_a_log[l],
                     ssd_d[l], ssd_norm_g[l], bsz, seq, tt)
        qa, ka, va, aux, stat = _foxprep(proj, fox_f_bias[l], bsz, seq)
        o_att = _fox(qa, ka, va, aux, stat)
        y_cnv = _conformer(proj, cm_conv_w[l], cm_conv_b[l], cm_ln_g[l], cm_ln_b[l], bsz, seq, tt)

        w_r = jnp.concatenate([w_router_group[l], w_router_expert[l],
                               jnp.zeros((d, LANES - N_GROUPS - N_EXPERTS), F32)], axis=1)
        w_rh = w_r.astype(BF16)
        w_r2 = jnp.concatenate([w_rh, (w_r - w_rh.astype(F32)).astype(BF16)], axis=1)
        b_r = jnp.concatenate([b_router_group[l], b_router_expert[l],
                               jnp.zeros((LANES - N_GROUPS - N_EXPERTS,), F32)]).reshape(1, LANES)
        a_f = norm_ffn_g[l][None, None, :] * (1.0 + sc_f)
        x2, h2, rt, rtt, cnt = _outproj(x2, y_ssd, o_att, y_cnv, w_out[l].astype(BF16),
                                        fox_norm_g[l].reshape(1, -1), g_m, a_f, sh_f, w_r2, b_r,
                                        seq, tm)
        plan = _moe_plan(cnt[:, 0, :N_EXPERTS].astype(jnp.int32), n_blocks)
        xs = _dispatch(plan, h2, rtt, n_blocks * MOE_BLOCK, tm)
        ys = _experts(plan["block_e"], plan["n_used"], xs, w_gate, w_up, w_down, l)
        x2 = _combine(plan, x2, ys, rt, g_f, final_norm_g.reshape(1, d), seq, tm,
                      final=(l == DEPTH - 1))
    return x2.reshape(bsz, seq, d)
```

```python
import functools

import jax
import jax.numpy as jnp
import numpy as np
from jax import lax
from jax.experimental import pallas as pl
from jax.experimental.pallas import tpu as pltpu

F32 = jnp.float32
BF16 = jnp.bfloat16
HIGHEST = lax.Precision.HIGHEST

D_MODEL = 1024
DEPTH = 4
SSD_WIDTH = 512
SSD_HEADS = 8
SSD_HEAD_DIM = 64
SSD_STATE = 128
SSD_CONV = 4
SSD_CHUNK = 128
XBC_WIDTH = 1024
ATTN_WIDTH = 256
ATTN_HEADS = 4
ATTN_HEAD_DIM = 64
CONV_WIDTH = 256
CONV_KERNEL = 31
N_GROUPS = 4
PER_GROUP = 8
N_EXPERTS = 32
D_EXPERT = 512
MOE_BLOCK = 512
EPS = 1e-6

LANES = 128
SUBLANES = 8
BF16_SUBLANES = 16
SSD_HALO = SUBLANES
COL_XBC = 0
COL_Z = 1024
COL_QKV = 1536
COL_GA = 2304
COL_GB = 2560
COL_SM = 2816
NP = 2944
SM_F = 8

VMEM_LIMIT = 56 * 1024 * 1024


def _cparams(sem):
    return pltpu.CompilerParams(dimension_semantics=sem, vmem_limit_bytes=VMEM_LIMIT)


def _sigmoid(x):
    return 1.0 / (1.0 + jnp.exp(-x))


def _silu(x):
    return x * _sigmoid(x)


def _softplus(x):
    return jnp.maximum(x, 0.0) + jnp.log1p(jnp.exp(-jnp.abs(x)))


def _iota(shape, dim):
    return lax.broadcasted_iota(jnp.int32, shape, dim)


def _split_bf16x3(x):
    hi = x.astype(BF16)
    r1 = x - hi.astype(F32)
    mid = r1.astype(BF16)
    lo = (r1 - mid.astype(F32)).astype(BF16)
    return hi, mid, lo


def _dot_onehot_rhs(x, sel):
    return sum(jnp.dot(part, sel, preferred_element_type=F32) for part in _split_bf16x3(x))


def _dot_onehot_lhs(sel, x):
    return sum(jnp.dot(sel, part, preferred_element_type=F32) for part in _split_bf16x3(x))


def _mod_kernel(c_ref, w_ref, b_ref, o_ref):
    cond = _silu(c_ref[...])
    o_ref[0] = jnp.dot(cond, w_ref[0], precision=HIGHEST, preferred_element_type=F32) + b_ref[0]


def _modulation(c, ada_w, ada_b):
    bsz = c.shape[0]
    rows = SUBLANES
    assert bsz <= rows
    cpad = jnp.zeros((rows, D_MODEL), F32).at[:bsz].set(c)
    tn = 1536
    n6 = 6 * D_MODEL
    out = pl.pallas_call(
        _mod_kernel,
        grid=(DEPTH, n6 // tn),
        in_specs=[
            pl.BlockSpec((rows, D_MODEL), lambda l, j: (0, 0)),
            pl.BlockSpec((1, D_MODEL, tn), lambda l, j: (l, 0, j)),
            pl.BlockSpec((1, 1, tn), lambda l, j: (l, 0, j)),
        ],
        out_specs=pl.BlockSpec((1, rows, tn), lambda l, j: (l, 0, j)),
        out_shape=jax.ShapeDtypeStruct((DEPTH, rows, n6), F32),
        compiler_params=_cparams(("arbitrary", "arbitrary")),
        name="adaln_mod",
    )(cpad, ada_w, ada_b.reshape(DEPTH, 1, n6))
    return out[:, :bsz]


def _inproj_kernel(x_ref, a_ref, s_ref, *refs):
    w_refs, o_ref = refs[:-1], refs[-1]
    x = x_ref[...]
    ms = jnp.mean(x * x, axis=-1, keepdims=True)
    h = (x * lax.rsqrt(ms + EPS) * a_ref[0] + s_ref[0]).astype(BF16)
    col = 0
    for w_ref in w_refs:
        width = w_ref.shape[-1]
        o_ref[:, col:col + width] = jnp.dot(h, w_ref[0], preferred_element_type=F32)
        col += width


def _inproj(x2, a, s, w_sections, layer, seq, tm):
    m = x2.shape[0]
    per_b = seq // tm
    assert sum(w.shape[-1] for w in w_sections) == NP
    return pl.pallas_call(
        _inproj_kernel,
        grid=(m // tm,),
        in_specs=[
            pl.BlockSpec((tm, D_MODEL), lambda i: (i, 0)),
            pl.BlockSpec((1, 1, D_MODEL), lambda i: (i // per_b, 0, 0)),
            pl.BlockSpec((1, 1, D_MODEL), lambda i: (i // per_b, 0, 0)),
        ] + [pl.BlockSpec((1, D_MODEL, w.shape[-1]), lambda i: (layer, 0, 0)) for w in w_sections],
        out_specs=pl.BlockSpec((tm, NP), lambda i: (i, 0)),
        out_shape=jax.ShapeDtypeStruct((m, NP), F32),
        compiler_params=_cparams(("arbitrary",)),
        name="in_proj",
    )(x2, a, s, *w_sections)


def _ssd_kernel(z_ref, xbc_ref, sm_ref, cw_ref, cb_ref, dtb_ref, alog_ref, dtbt_ref,
                alogt_ref, e_ref, dx_ref, ng_ref, y_ref, xpad, xc, prev, *, tt):
    t = pl.program_id(1)

    @pl.when(t == 0)
    def _():
        xpad[0:SSD_HALO, :] = jnp.zeros((SSD_HALO, XBC_WIDTH), F32)
        prev[...] = jnp.zeros(prev.shape, F32)

    xpad[SSD_HALO:SSD_HALO + tt, :] = xbc_ref[...]
    acc = jnp.broadcast_to(cb_ref[...], (tt, XBC_WIDTH))
    for k in range(SSD_CONV):
        off = SSD_HALO - (SSD_CONV - 1) + k
        acc = acc + cw_ref[k:k + 1, :] * xpad[off:off + tt, :]
    xc[...] = _silu(acc)
    xpad[0:SSD_HALO, :] = xpad[tt:tt + SSD_HALO, :]

    cl = SSD_CHUNK
    row = _iota((cl, cl), 0)
    col = _iota((cl, cl), 1)
    causal = row >= col
    tril = jnp.where(causal, 1.0, 0.0).astype(BF16)
    triu = jnp.where(row <= col, 1.0, 0.0).astype(BF16)
    lo = col < SSD_HEAD_DIM
    lane1 = _iota((1, LANES), 1)
    a_row = jnp.where(lane1 < SSD_HEADS, -jnp.exp(alog_ref[...]), 0.0)
    a_col = -jnp.exp(alogt_ref[...])
    expand = e_ref[...]

    def chunk(c, carry):
        r0 = c * cl
        xs = xc[pl.ds(r0, cl), 0:SSD_WIDTH]
        bmat = xc[pl.ds(r0, cl), SSD_WIDTH:SSD_WIDTH + 2 * SSD_STATE]
        cmat = xc[pl.ds(r0, cl), SSD_WIDTH + 2 * SSD_STATE:XBC_WIDTH]
        sm = sm_ref[pl.ds(r0, cl), :]
        dt = _softplus(sm + dtb_ref[...])
        da = dt * a_row
        acs = _dot_onehot_lhs(tril, da)
        dtt = _softplus(sm.T[0:SSD_HEADS, :] + dtbt_ref[...])
        acst = _dot_onehot_rhs(dtt * a_col, triu)
        dt_x = _dot_onehot_rhs(dt, expand)
        acs_x = _dot_onehot_rhs(acs, expand)
        last = acs_x[cl - 1:cl, :]
        eacs_x = jnp.exp(acs_x)
        dte_x = jnp.exp(last - acs_x)
        cd_x = jnp.exp(last)
        xdt = xs * dt_x
        xdte = (xdt * dte_x).astype(BF16)
        zc = z_ref[pl.ds(r0, cl), :]
        for g in range(2):
            bg = bmat[:, g * SSD_STATE:(g + 1) * SSD_STATE]
            cg = cmat[:, g * SSD_STATE:(g + 1) * SSD_STATE].astype(BF16)
            bgt = bg.T.astype(BF16)
            cbm = jnp.dot(cg, bgt, preferred_element_type=F32)
            gs = slice(g * 2 * LANES, (g + 1) * 2 * LANES)
            prev_g = prev[:, gs]
            yo_g = jnp.dot(cg, prev_g.astype(BF16), preferred_element_type=F32) * eacs_x[:, gs]
            prev[:, gs] = prev_g * cd_x[:, gs] + jnp.dot(bgt, xdte[:, gs], preferred_element_type=F32)
            pair_out = []
            for j in range(2):
                p = 2 * g + j
                sl = slice(p * LANES, (p + 1) * LANES)
                xp = xdt[:, sl]
                gms, xms = [], []
                for half in range(2):
                    h = 2 * p + half
                    seg = acs[:, h:h + 1] - acst[h:h + 1, :]
                    dec = jnp.exp(jnp.where(causal, seg, -jnp.inf))
                    gms.append((cbm * dec).astype(BF16))
                    own = lo if half == 0 else jnp.logical_not(lo)
                    xms.append(jnp.where(own, xp, 0.0).astype(BF16))
                yd = jnp.dot(jnp.concatenate(gms, axis=1), jnp.concatenate(xms, axis=0),
                             preferred_element_type=F32)
                pair_out.append(yd + yo_g[:, j * LANES:(j + 1) * LANES] + xs[:, sl] * dx_ref[:, sl])
            yg =jnp.concatenate(pair_out, axis=-1) * _silu(zc[:, gs])
            ms = jnp.mean(yg * yg, axis=-1, keepdims=True)
            y_ref[pl.ds(r0, cl), gs] = yg * lax.rsqrt(ms + EPS) * ng_ref[:, gs]
        return carry

    for c in range(tt // cl):
        chunk(c, 0)


def _ssd(proj, cw, cb, dtb, alog, dx, ng, bsz, seq, tt):
    m = bsz * seq
    nt = seq // tt
    pad = LANES - SSD_HEADS
    dtb_row = jnp.pad(dtb, (0, pad)).reshape(1, LANES)
    alog_row = jnp.pad(alog, (0, pad)).reshape(1, LANES)
    dtb_col = jnp.broadcast_to(dtb[:, None], (SSD_HEADS, SSD_CHUNK))
    alog_col = jnp.broadcast_to(alog[:, None], (SSD_HEADS, SSD_CHUNK))
    expand = np.zeros((LANES, SSD_WIDTH), np.float32)
    for h in range(SSD_HEADS):
        expand[h, h * SSD_HEAD_DIM:(h + 1) * SSD_HEAD_DIM] = 1.0
    dx_row = jnp.repeat(dx, SSD_HEAD_DIM).reshape(1, SSD_WIDTH)
    full = lambda shape: pl.BlockSpec(shape, lambda b, t: (0,) * len(shape))
    return pl.pallas_call(
        functools.partial(_ssd_kernel, tt=tt),
        grid=(bsz, nt),
        in_specs=[
            pl.BlockSpec((tt, SSD_WIDTH), lambda b, t: (b * nt + t, COL_Z // SSD_WIDTH)),
            pl.BlockSpec((tt, XBC_WIDTH), lambda b, t: (b * nt + t, COL_XBC // XBC_WIDTH + 0)),
            pl.BlockSpec((tt, LANES), lambda b, t: (b * nt + t, COL_SM // LANES)),
            full((SSD_CONV, XBC_WIDTH)),
            full((1, XBC_WIDTH)),
            full((1, LANES)),
            full((1, LANES)),
            full((SSD_HEADS, SSD_CHUNK)),
            full((SSD_HEADS, SSD_CHUNK)),
            full((LANES, SSD_WIDTH)),
            full((1, SSD_WIDTH)),
            full((1, SSD_WIDTH)),
        ],
        out_specs=pl.BlockSpec((tt, SSD_WIDTH), lambda b, t: (b * nt + t, 0)),
        out_shape=jax.ShapeDtypeStruct((m, SSD_WIDTH), F32),
        scratch_shapes=[
            pltpu.VMEM((tt + SSD_HALO, XBC_WIDTH), F32),
            pltpu.VMEM((tt, XBC_WIDTH), F32),
            pltpu.VMEM((SSD_STATE, SSD_WIDTH), F32),
        ],
        compiler_params=_cparams(("arbitrary", "arbitrary")),
        name="ssd_scan",
    )(proj, proj, proj, cw, cb.reshape(1, XBC_WIDTH), dtb_row, alog_row, dtb_col, alog_col,
      jnp.asarray(expand, dtype=BF16), dx_row, ng.reshape(1, SSD_WIDTH))


LOG2E = 1.4426950408889634
ATT_BLOCK = 256
ATT_TQ = 512
PRUNE_LOG2 = 128.0
FAST_GAP_LOG2 = 60.0
SHIFT_MARGIN_LOG2 = 1.0
AUX_CUMEND = 0
AUX_KMAX = 1
ST_QNORM, ST_CUM, ST_SELF = 0, 1, 2


def _foxprep_kernel(qkv_ref, sm_ref, fb_ref, q_out, k_out, v_out, aux_out, stat_out, carry, *, tt):
    t = pl.program_id(1)

    @pl.when(t == 0)
    def _():
        carry[...] = jnp.zeros(carry.shape, F32)
        aux_out[...] = jnp.zeros(aux_out.shape, F32)

    logit = sm_ref[...] + fb_ref[...]
    logf = -_softplus(-logit)
    row = _iota((tt, tt), 0)
    col = _iota((tt, tt), 1)
    tril = jnp.where(row >= col, 1.0, 0.0).astype(BF16)
    cum = _dot_onehot_lhs(tril, logf) + carry[...]
    carry[...] = cum[tt - 1:tt, :]

    lane = _iota((tt, LANES), 1)
    lane1 = _iota((1, LANES), 1)
    scale = ATTN_HEAD_DIM ** -0.5 * LOG2E
    for h in range(ATTN_HEADS):
        pair, half = h // 2, h % 2
        own = (lane < ATTN_HEAD_DIM) if half == 0 else (lane >= ATTN_HEAD_DIM)
        a0 = ATTN_HEAD_DIM * (1 - half)
        cs = jnp.broadcast_to(cum[:, SM_F + h:SM_F + h + 1], (tt, LANES)) * LOG2E
        hi = cs.astype(BF16).astype(F32)
        r1 = cs - hi
        mid = r1.astype(BF16).astype(F32)
        low = r1 - mid
        qp = qkv_ref[:, pair * LANES:(pair + 1) * LANES]
        kp = qkv_ref[:, ATTN_WIDTH + pair * LANES:ATTN_WIDTH + (pair + 1) * LANES]
        vp = qkv_ref[:, 2 * ATTN_WIDTH + pair * LANES:2 * ATTN_WIDTH + (pair + 1) * LANES]
        qa = jnp.where(lane == a0, hi, jnp.where(lane == a0 + 1, mid, jnp.where(
            lane == a0 + 2, low, jnp.where((lane >= a0 + 3) & (lane < a0 + 6), 1.0, 0.0))))
        ka = jnp.where(lane == a0 + 3, -hi, jnp.where(lane == a0 + 4, -mid, jnp.where(
            lane == a0 + 5, -low, jnp.where((lane >= a0) & (lane < a0 + 9), 1.0, 0.0))))
        qb = jnp.where(own, qp * scale, qa).astype(BF16)
        q_out[0, h] = qb
        kb = jnp.where(own, kp, ka).astype(BF16)
        k_out[0, h] = kb
        qf = jnp.where(own, qb.astype(F32), 0.0)
        qn = jnp.sqrt(jnp.sum(qf * qf, axis=-1, keepdims=True))
        s_self = jnp.sum(qf * kb.astype(F32), axis=-1, keepdims=True)
        stat = jnp.where(lane == ST_QNORM, qn, jnp.where(lane == ST_CUM, hi + mid + low, jnp.where(
            lane == ST_SELF, s_self, 0.0)))
        stat_out[0, h] = stat.T[0:SUBLANES, :]
        v_out[0, h] = jnp.where(own, vp, jnp.where(lane == a0, 1.0, 0.0)).astype(BF16)
        kf = jnp.where(own, kb.astype(F32), 0.0)
        kn2_rows = jnp.sum(kf * kf, axis=-1, keepdims=True)
        for blk in range(tt // ATT_BLOCK):
            end = (blk + 1) * ATT_BLOCK
            kn2 = jnp.max(kn2_rows[end - ATT_BLOCK:end, :], axis=0, keepdims=True)
            here = lane1 == t * (tt // ATT_BLOCK) + blk
            aux_out[0, h, AUX_CUMEND:AUX_CUMEND + 1, :] = jnp.where(
                here, cs[end - 1:end, :], aux_out[0, h, AUX_CUMEND:AUX_CUMEND + 1, :])
            aux_out[0, h, AUX_KMAX:AUX_KMAX + 1, :] = jnp.where(
                here, jnp.sqrt(kn2), aux_out[0, h, AUX_KMAX:AUX_KMAX + 1, :])


def _foxprep(proj, fb, bsz, seq):
    tt = ATT_BLOCK
    nt = seq // tt
    assert tt % ATT_BLOCK == 0 and seq // ATT_BLOCK <= LANES
    fb_row = jnp.zeros((1, LANES), F32).at[0, SM_F:SM_F + ATTN_HEADS].set(fb)
    shp = jax.ShapeDtypeStruct((bsz, ATTN_HEADS, seq, LANES), BF16)
    ospec = pl.BlockSpec((1, ATTN_HEADS, tt, LANES), lambda b, t: (b, 0, t, 0))
    return pl.pallas_call(
        functools.partial(_foxprep_kernel, tt=tt),
        grid=(bsz, nt),
        in_specs=[
            pl.BlockSpec((tt, 3 * ATTN_WIDTH), lambda b, t: (b * nt + t, COL_QKV // (3 * ATTN_WIDTH))),
            pl.BlockSpec((tt, LANES), lambda b, t: (b * nt + t, COL_SM // LANES)),
            pl.BlockSpec((1, LANES), lambda b, t: (0, 0)),
        ],
        out_specs=[ospec, ospec, ospec,
                   pl.BlockSpec((1, ATTN_HEADS, SUBLANES,LANES), lambda b, t: (b, 0, 0, 0)),
                   pl.BlockSpec((1, ATTN_HEADS, SUBLANES,tt), lambda b, t: (b, 0, 0, t))],
        out_shape=[shp, shp, shp, jax.ShapeDtypeStruct((bsz, ATTN_HEADS, SUBLANES,LANES), F32),
                   jax.ShapeDtypeStruct((bsz, ATTN_HEADS, SUBLANES,seq), F32)],
        scratch_shapes=[pltpu.VMEM((1, LANES), F32)],
        compiler_params=_cparams(("arbitrary", "arbitrary")),
        name="fox_prep",
    )(proj, proj, fb_row)


def _fox_kernel(q_ref, k_ref, v_ref, aux_ref, stat_ref, o_ref, *, tq, kb):
    i = pl.program_id(2)
    ib = i * (tq // kb)
    nt = (((1,), (1,)), ((), ()))
    lane = _iota((tq, LANES), 1)
    lane1 = _iota((1, LANES), 1)
    qs = (q_ref[0, 0], q_ref[0, 1])

    def step(hh, r0, width, m, acc, mask):
        kblk = k_ref[0, hh, pl.ds(r0, width), :]
        vblk = v_ref[0, hh, pl.ds(r0, width), :]
        s = lax.dot_general(qs[hh], kblk, nt, preferred_element_type=F32)
        if mask is not None:
            s = jnp.where(mask, s, -jnp.inf)
        m_new = jnp.maximum(m, jnp.max(s, axis=-1, keepdims=True))
        p = jnp.exp2(s - m_new)
        alpha = jnp.exp2(m - m_new)
        acc = acc * alpha + jnp.dot(p.astype(BF16), vblk, preferred_element_type=F32)
        return m_new, acc

    d0 = pl.multiple_of(i * tq, tq)
    diag = _iota((tq, tq), 0) >= _iota((tq, tq), 1)
    ib_f = ib.astype(F32)
    first = []
    gaps = []
    q_shift = []
    for hh in range(2):
        a0 = ATTN_HEAD_DIM * (1 - hh)
        st = stat_ref[0, hh]
        qn = st[ST_QNORM:ST_QNORM + 1, :]
        cum_t = st[ST_CUM:ST_CUM + 1, :]
        s_self = st[ST_SELF:ST_SELF + 1, :]
        aux = aux_ref[0, hh]
        cend = aux[AUX_CUMEND:AUX_CUMEND + 1, :]
        kmax = jnp.max(jnp.where(lane1 < ib + tq // kb, aux[AUX_KMAX:AUX_KMAX + 1, :], 0.0), axis=-1,
                       keepdims=True)
        bound = qn * kmax
        slack = jnp.max(bound + cum_t - s_self, axis=-1, keepdims=True)
        live = (lane1 < ib) & (slack - cend > -PRUNE_LOG2)
        first.append(jnp.min(jnp.where(live, lane1.astype(F32), ib_f)))
        parts = _split_bf16x3(bound + SHIFT_MARGIN_LOG2)
        gaps.append(jnp.max(sum(part.astype(F32) for part in parts) - s_self))
        rows_t = BF16_SUBLANES
        terms = jnp.concatenate([-part.astype(F32) for part in parts]
                                + [jnp.zeros((rows_t - len(parts), tq), F32)], axis=0).astype(BF16)
        place = jnp.where(_iota((rows_t, LANES), 1) == _iota((rows_t, LANES), 0) + (a0 + 6), 1.0,
                          0.0).astype(BF16)
        q_shift.append(qs[hh] + lax.dot_general(terms, place, (((0,), (0,)), ((), ())),
                                                preferred_element_type=F32).astype(BF16))
    j_first = jnp.minimum(first[0], first[1]).astype(jnp.int32)

    def fast(_):
        def blocks(c, r0, width, mask=None):
            out = []
            for hh in range(2):
                s = lax.dot_general(q_shift[hh], k_ref[0, hh, pl.ds(r0, width), :], nt,
                                    preferred_element_type=F32)
                if mask is not None:
                    s = jnp.where(mask, s, -jnp.inf)
                out.append(c[hh] + jnp.dot(jnp.exp2(s).astype(BF16), v_ref[0, hh, pl.ds(r0, width), :],
                                           preferred_element_type=F32))
            return tuple(out)

        zero = jnp.zeros((tq, LANES), F32)
        c = blocks((zero, zero), d0, tq, diag)
        j_lo = j_first - ((ib - j_first) & 1)
        n = ib - j_lo
        c = lax.cond((n & 2) == 2, lambda c: blocks(c, pl.multiple_of(j_lo * kb, kb), 2 * kb),
                     lambda c: c, c)
        j0 = j_lo + (n & 2)

        def body(jj, c):
            return blocks(c, pl.multiple_of((j0 + 4 * jj) * kb, kb), 4 * kb)
        return lax.fori_loop(0, lax.shift_right_logical(n, 2), body, c)

    def online(_):
        m_init = jnp.full((tq, 1), -jnp.inf, F32)
        acc_init = jnp.zeros((tq, LANES), F32)
        carry = step(0, d0, tq, m_init, acc_init, diag) + step(1, d0, tq, m_init, acc_init, diag)

        def body(j, c):
            r0 = pl.multiple_of(j * kb, kb)
            m0, a0_, m1, a1_ = c
            m0, a0_ = step(0, r0, kb, m0, a0_, None)
            m1, a1_ = step(1, r0, kb, m1, a1_, None)
            return m0, a0_, m1, a1_
        _, a0_, _, a1_ = lax.fori_loop(j_first, ib, body, carry)
        return a0_, a1_

    acc0, acc1 = lax.cond(jnp.maximum(gaps[0], gaps[1]) <= FAST_GAP_LOG2, fast, online, None)
    den0 = jnp.sum(jnp.where(lane == ATTN_HEAD_DIM, acc0, 0.0), axis=-1, keepdims=True)
    den1 = jnp.sum(jnp.where(lane == 0, acc1, 0.0), axis=-1, keepdims=True)
    o_ref[...] = jnp.where(lane < ATTN_HEAD_DIM, acc0 / den0, acc1 / den1)


def _fox(qa, ka, va, aux, stat):
    bsz, nh, seq, _ = qa.shape
    tq = min(ATT_TQ, seq)
    assert (tq // ATT_BLOCK) % 2 == 0
    nq = seq // tq
    kv_spec = pl.BlockSpec((1, 2, seq, LANES), lambda b, p, i: (b, p, 0, 0))
    return pl.pallas_call(
        functools.partial(_fox_kernel, tq=tq, kb=ATT_BLOCK),
        grid=(bsz, nh // 2, nq),
        in_specs=[pl.BlockSpec((1, 2, tq, LANES), lambda b, p, i: (b, p, i, 0)), kv_spec, kv_spec,
                  pl.BlockSpec((1, 2, SUBLANES,LANES), lambda b, p, i: (b, p, 0, 0)),
                  pl.BlockSpec((1, 2, SUBLANES,tq), lambda b, p, i: (b, p, 0, i))],
        out_specs=pl.BlockSpec((tq, LANES), lambda b, p, i: (b * nq + i, p)),
        out_shape=jax.ShapeDtypeStruct((bsz * seq, ATTN_WIDTH), F32),
        compiler_params=_cparams(("arbitrary", "arbitrary", "arbitrary")),
        name="fox_attn",
    )(qa, ka, va, aux, stat)


CONF_HALO = 32
CONF_ROWS = 64


def _conf_kernel(ga_ref, gb_ref, w_ref, b_ref, lg_ref, lb_ref, y_ref, upad, ush, *, tt):
    t = pl.program_id(1)

    @pl.when(t == 0)
    def _():
        upad[0:CONF_HALO, :] = jnp.zeros((CONF_HALO, CONV_WIDTH), F32)

    upad[CONF_HALO:CONF_HALO + tt, :] = ga_ref[...] * _sigmoid(gb_ref[...])
    base = CONF_HALO - (CONV_KERNEL - 1)
    for ph in range(1, SUBLANES):
        ush[ph - 1] = upad[ph:ph + ush.shape[1], :]
    for r in range(tt // CONF_ROWS):
        r0 = r * CONF_ROWS
        acc = jnp.broadcast_to(b_ref[...], (CONF_ROWS, CONV_WIDTH))
        for k in range(CONV_KERNEL):
            ph = (base + k) % SUBLANES
            lo = r0 + base + k - ph
            src = upad[lo:lo + CONF_ROWS, :] if ph == 0 else ush[ph - 1, lo:lo + CONF_ROWS, :]
            acc = acc + w_ref[k:k + 1, :] * src
        mu = jnp.mean(acc, axis=-1, keepdims=True)
        cen = acc - mu
        var = jnp.mean(cen * cen, axis=-1, keepdims=True)
        y = cen * lax.rsqrt(var + EPS) * lg_ref[...] + lb_ref[...]
        y_ref[r0:r0 + CONF_ROWS, :] = _silu(y)
    upad[0:CONF_HALO, :] = upad[tt:tt + CONF_HALO, :]


def _conformer(proj, w, b, lg, lb, bsz, seq, tt):
    m = bsz * seq
    nt = seq // tt
    full = lambda shape: pl.BlockSpec(shape, lambda bb, t: (0,) * len(shape))
    return pl.pallas_call(
        functools.partial(_conf_kernel, tt=tt),
        grid=(bsz, nt),
        in_specs=[
            pl.BlockSpec((tt, CONV_WIDTH), lambda bb, t: (bb * nt + t, COL_GA // CONV_WIDTH)),
            pl.BlockSpec((tt, CONV_WIDTH), lambda bb, t: (bb * nt + t, COL_GB // CONV_WIDTH)),
            full((CONV_KERNEL, CONV_WIDTH)),
            full((1, CONV_WIDTH)),
            full((1, CONV_WIDTH)),
            full((1, CONV_WIDTH)),
        ],
        out_specs=pl.BlockSpec((tt, CONV_WIDTH), lambda bb, t: (bb * nt + t, 0)),
        out_shape=jax.ShapeDtypeStruct((m, CONV_WIDTH), F32),
        scratch_shapes=[pltpu.VMEM((tt + CONF_HALO, CONV_WIDTH), F32),
                        pltpu.VMEM((SUBLANES - 1, tt + CONF_HALO - SUBLANES, CONV_WIDTH), F32)],
        compiler_params=_cparams(("arbitrary", "arbitrary")),
        name="conformer_conv",
    )(proj, proj, w, b.reshape(1, -1), lg.reshape(1, -1), lb.reshape(1, -1))


ROUTE_BIG = 1e9
CHUNK = 8
DMA_QUEUES = 2
XS_FEAT = D_MODEL // 2
XS_WIDTH = XS_FEAT + LANES
RT_W1, RT_W2, RT_ROW1, RT_ROW2 = 0, 1, 2, 3


def _split3(x):
    hi = x.astype(BF16)
    lo = (x - hi.astype(F32)).astype(BF16)
    return hi, lo


def _outproj_kernel(x_ref, ys_ref, oa_ref, yc_ref, wo_ref, fg_ref, gm_ref, a2_ref, s2_ref,
                    wr_ref, br_ref, xn_ref, h2_ref, rt_ref, rtt_ref, cnt_ref):
    tm = x_ref.shape[0]
    lane = _iota((tm, LANES), 1)
    att = oa_ref[...]
    ms = jnp.mean(att * att, axis=-1, keepdims=True)
    att = att * lax.rsqrt(ms + EPS) * fg_ref[...]
    y = jnp.dot(ys_ref[...].astype(BF16), wo_ref[0:SSD_WIDTH, :], preferred_element_type=F32)
    y = y + jnp.dot(att.astype(BF16), wo_ref[SSD_WIDTH:SSD_WIDTH + ATTN_WIDTH, :],
                    preferred_element_type=F32)
    y = y + jnp.dot(yc_ref[...].astype(BF16), wo_ref[SSD_WIDTH + ATTN_WIDTH:, :],
                    preferred_element_type=F32)
    xn = x_ref[...] + gm_ref[0] * y
    xn_ref[...] = xn
    ms2 = jnp.mean(xn * xn, axis=-1, keepdims=True)
    h2 = xn * lax.rsqrt(ms2 + EPS) * a2_ref[0] + s2_ref[0]
    h2_ref[...] = h2.astype(BF16)

    hh, hl = _split3(h2)
    part = jnp.dot(jnp.concatenate([hh, hl], axis=0), wr_ref[...], preferred_element_type=F32)
    logits = (part[:tm, :LANES] + part[:tm, LANES:] + part[tm:, :LANES] + part[tm:, LANES:]) + br_ref[...]
    lanef = lane.astype(F32)
    lg = jnp.where(lane < N_GROUPS, logits, -jnp.inf)
    gmax = jnp.max(lg, axis=-1, keepdims=True)
    gsum = jnp.sum(jnp.exp(lg - gmax), axis=-1, keepdims=True)
    gidx = jnp.min(jnp.where(lg == gmax, lanef, ROUTE_BIG), axis=-1, keepdims=True)
    e_lo = N_GROUPS + PER_GROUP * gidx
    le = jnp.where((lanef >= e_lo) & (lanef < e_lo + PER_GROUP), logits, -jnp.inf)
    m1 = jnp.max(le, axis=-1, keepdims=True)
    i1 = jnp.min(jnp.where(le == m1, lanef, ROUTE_BIG), axis=-1, keepdims=True)
    le2 = jnp.where(lanef == i1, -jnp.inf, le)
    m2 = jnp.max(le2, axis=-1, keepdims=True)
    i2 = jnp.min(jnp.where(le2 == m2, lanef, ROUTE_BIG), axis=-1, keepdims=True)
    esum = jnp.sum(jnp.exp(le - m1), axis=-1, keepdims=True)
    p1 = 1.0 / esum
    p2 = jnp.exp(m2 - m1) / esum
    psel = 1.0 / gsum
    w1 = p1 / (p1 + p2) * psel
    w2 = p2 / (p1 + p2) * psel

    oh1 = (lanef == (i1 - N_GROUPS)).astype(F32)
    oh2 = (lanef == (i2 - N_GROUPS)).astype(F32)
    oh = oh1 + oh2
    cnt = jnp.sum(oh, axis=0, keepdims=True)
    before = (_iota((tm, tm), 0) > _iota((tm, tm), 1)).astype(BF16)
    rank = jnp.dot(before, oh.astype(BF16), preferred_element_type=F32)
    chunks = jnp.floor((cnt + (CHUNK - 1)) * (1.0 / CHUNK))
    below = (_iota((LANES, LANES), 0) < _iota((LANES, LANES), 1)).astype(BF16)
    seg_lo = CHUNK * jnp.dot(jnp.broadcast_to(chunks, (SUBLANES, LANES)).astype(BF16), below,
                             preferred_element_type=F32)[0:1, :]
    lr1 = jnp.sum(oh1 * (rank + seg_lo), axis=-1, keepdims=True)
    lr2 = jnp.sum(oh2 * (rank + seg_lo), axis=-1, keepdims=True)
    route = jnp.where(lane == RT_W1, w1, jnp.where(lane == RT_W2, w2, jnp.where(
        lane == RT_ROW1, lr1, jnp.where(lane == RT_ROW2, lr2, 0.0))))
    rt_ref[...] = route
    rtt_ref[0] = route.T[0:SUBLANES, :]
    cnt_ref[0] = jnp.broadcast_to(cnt, (SUBLANES, LANES))


def _outproj(x2, ys, oa, yc, wo, fg, gm, a2, s2, wr, br, seq, tm):
    m = x2.shape[0]
    per_b = seq // tm
    nt = m // tm
    row = lambda w: pl.BlockSpec((tm, w), lambda i: (i, 0))
    bvec = pl.BlockSpec((1, 1, D_MODEL), lambda i: (i // per_b, 0, 0))
    full = lambda shape: pl.BlockSpec(shape, lambda i: (0,) * len(shape))
    return pl.pallas_call(
        _outproj_kernel,
        grid=(nt,),
        in_specs=[
            row(D_MODEL), row(SSD_WIDTH), row(ATTN_WIDTH), row(CONV_WIDTH),
            full((D_MODEL, D_MODEL)), full((1, ATTN_WIDTH)), bvec, bvec, bvec,
            full((D_MODEL, 2 * LANES)), full((1, LANES)),
        ],
        out_specs=[row(D_MODEL), row(D_MODEL), row(LANES),
                   pl.BlockSpec((1, SUBLANES, tm), lambda i: (i, 0, 0)),
                   pl.BlockSpec((1, SUBLANES, LANES), lambda i: (i, 0, 0))],
        out_shape=[jax.ShapeDtypeStruct((m, D_MODEL), F32), jax.ShapeDtypeStruct((m, D_MODEL), BF16),
                   jax.ShapeDtypeStruct((m, LANES), F32), jax.ShapeDtypeStruct((nt, SUBLANES,tm), F32),
                   jax.ShapeDtypeStruct((nt, SUBLANES,LANES), F32)],
        compiler_params=_cparams(("arbitrary",)),
        name="out_proj_router",
    )(x2, ys, oa, yc, wo, fg, gm, a2, s2, wr, br)


def _local_rows(tm):
    return -(-(2 * tm + (CHUNK - 1) * N_EXPERTS) // LANES) * LANES


def _pack_halves(x):
    w = x.shape[-1] // 2
    return pltpu.bitcast(x[:, :w], jnp.uint32) | lax.shift_right_logical(
        pltpu.bitcast(x[:, w:], jnp.uint32), jnp.uint32(16))


def _unpack_halves(p):
    hi = pltpu.bitcast(p & jnp.uint32(0xFFFF0000), F32)
    lo = pltpu.bitcast(lax.shift_left(p, jnp.uint32(16)), F32)
    return jnp.concatenate([hi, lo], axis=-1).astype(BF16)


def _start_segment(n, copy_at, priority):
    quads = lax.shift_right_logical(n, 2)

    def per_quad(k, c):
        copy_at(pl.multiple_of(k * (4 * CHUNK), CHUNK), 4 * CHUNK).start(priority=priority)
        return c
    lax.fori_loop(0, quads, per_quad, 0)

    @pl.when((n & 2) == 2)
    def _():
        copy_at(pl.multiple_of(quads * (4 * CHUNK), CHUNK), 2 * CHUNK).start(priority=priority)

    @pl.when((n & 1) == 1)
    def _():
        copy_at(pl.multiple_of((quads * 4 + (n & 2)) * CHUNK, CHUNK), CHUNK).start(priority=priority)


def _for_expert_segments(start_one):
    assert N_EXPERTS % DMA_QUEUES == 0

    def per_group(k, c):
        for queue in range(DMA_QUEUES):
            start_one(k * DMA_QUEUES + queue, queue)
        return c
    lax.fori_loop(0, N_EXPERTS // DMA_QUEUES, per_group, 0)


def _wait_chunks(n, n_max, copy_of_size):
    for bit in range(n_max.bit_length()):
        @pl.when((lax.shift_right_logical(n, bit) & 1) == 1)
        def _():
            copy_of_size(CHUNK << bit).wait()


def _dispatch_kernel(lo_ref, gb_ref, n8_ref, tot_ref, ts_ref, tn_ref, h_ref, rtt_ref, xs_hbm,
                     xloc, zeros, sem, zsem, *, lr):
    t = pl.program_id(0)
    nt = pl.num_programs(0)
    slot = t % 2
    tm = h_ref.shape[0]

    def wait_tile(tile, sl):
        _wait_chunks(tot_ref[tile], lr // CHUNK, lambda size: pltpu.make_async_copy(
            xloc.at[sl, pl.ds(0, size), :], xs_hbm.at[pl.ds(0, size), :], sem.at[sl]))

    @pl.when(t >= 2)
    def _():
        wait_tile(t - 2, slot)

    rows = _iota((lr, tm), 0).astype(F32)
    is1 = rows == rtt_ref[0, RT_ROW1:RT_ROW1 + 1, :]
    is2 = rows == rtt_ref[0, RT_ROW2:RT_ROW2 + 1, :]
    perm = jnp.where(is1, 1.0, jnp.where(is2, 1.0, 0.0))
    xloc[slot, :, 0:XS_FEAT] = _pack_halves(
        jnp.dot(perm.astype(BF16), h_ref[...], preferred_element_type=F32))
    w_row = jnp.sum(jnp.where(is1, rtt_ref[0, RT_W1:RT_W1 + 1, :],
                              jnp.where(is2, rtt_ref[0, RT_W2:RT_W2 + 1, :], 0.0)), axis=-1, keepdims=True)
    xloc[slot, :, XS_FEAT:XS_WIDTH] = jnp.where(
        _iota((lr, LANES), 1) == 0, pltpu.bitcast(jnp.broadcast_to(w_row, (lr, LANES)), jnp.uint32),
        jnp.uint32(0))

    def per_expert(e, queue):
        idx = t * N_EXPERTS + e
        src0 = lo_ref[idx] * CHUNK
        dst0 = gb_ref[idx] * CHUNK
        _start_segment(n8_ref[idx], lambda first, rows: pltpu.make_async_copy(
            xloc.at[slot, pl.ds(pl.multiple_of(src0 + first, CHUNK), rows), :],
            xs_hbm.at[pl.ds(pl.multiple_of(dst0 + first, CHUNK), rows), :], sem.at[slot]), queue)
    _for_expert_segments(per_expert)

    @pl.when(t == nt - 1)
    def _():
        zeros[...] = jnp.zeros(zeros.shape, jnp.uint32)

        def zero_copy(dst):
            return pltpu.make_async_copy(zeros, xs_hbm.at[pl.ds(dst, CHUNK), :], zsem.at[0])

        def fill(e, c):
            dst0 = ts_ref[e] * CHUNK

            def one(k, c2):
                zero_copy(pl.multiple_of(dst0 + k * CHUNK, CHUNK)).start()
                return c2
            lax.fori_loop(0, tn_ref[e], one, 0)
            return c
        lax.fori_loop(0, N_EXPERTS + 1, fill, 0)

        def drain(e, c):
            def one(k, c2):
                zero_copy(0).wait()
                return c2
            lax.fori_loop(0, tn_ref[e], one, 0)
            return c
        lax.fori_loop(0, N_EXPERTS + 1, drain, 0)

        @pl.when(t >= 1)
        def _():
            wait_tile(t - 1, 1 - slot)
        wait_tile(t, slot)


def _dispatch(plan, h2, rtt, n_rows, tm):
    m = h2.shape[0]
    lr = _local_rows(tm)
    grid_spec = pltpu.PrefetchScalarGridSpec(
        num_scalar_prefetch=6,
        grid=(m // tm,),
        in_specs=[pl.BlockSpec((tm, D_MODEL), lambda t, *_: (t, 0)),
                  pl.BlockSpec((1, SUBLANES, tm), lambda t, *_: (t, 0, 0))],
        out_specs=pl.BlockSpec(memory_space=pl.ANY),
        scratch_shapes=[pltpu.VMEM((2, lr, XS_WIDTH), jnp.uint32),
                        pltpu.VMEM((CHUNK, XS_WIDTH), jnp.uint32),
                        pltpu.SemaphoreType.DMA((2,)), pltpu.SemaphoreType.DMA((1,))],
    )
    return pl.pallas_call(
        functools.partial(_dispatch_kernel, lr=lr),
        grid_spec=grid_spec,
        out_shape=jax.ShapeDtypeStruct((n_rows, XS_WIDTH), jnp.uint32),
        compiler_params=_cparams(("arbitrary",)),
        name="moe_dispatch",
    )(plan["lo8"], plan["gb8"], plan["n8"], plan["tot8"], plan["ts8"], plan["tn8"], h2, rtt)


def _expert_kernel(be_ref, nu_ref, x_ref, wg_ref, wu_ref, wd_ref, y_ref, wgub, wdb):
    i = pl.program_id(0)

    @pl.when(i < nu_ref[0])
    def _():
        prev_e = be_ref[jnp.maximum(i - 1, 0)]

        @pl.when((i == 0) | (be_ref[i] != prev_e))
        def _():
            wgub[:, 0:D_EXPERT] = wg_ref[0, 0].astype(BF16)
            wgub[:, D_EXPERT:2 * D_EXPERT] = wu_ref[0, 0].astype(BF16)
            wdb[...] = wd_ref[0, 0].astype(BF16)

        x = _unpack_halves(x_ref[:, 0:XS_FEAT])
        w_row = pltpu.bitcast(x_ref[:, XS_FEAT:XS_FEAT + 1], F32)
        gu = jnp.dot(x, wgub[...], preferred_element_type=F32)
        hid = _silu(gu[:, 0:D_EXPERT]) * gu[:, D_EXPERT:2 * D_EXPERT]
        y = jnp.dot(hid.astype(BF16), wdb[...], preferred_element_type=F32)
        y_ref[...] = _pack_halves((y * w_row).astype(BF16).astype(F32))

    @pl.when(i >= nu_ref[0])
    def _():
        y_ref[...] = jnp.zeros(y_ref.shape, jnp.uint32)


def _experts(block_e, n_used, xs, wg, wu, wd, layer):
    n_rows = xs.shape[0]
    wspec = lambda shape: pl.BlockSpec((1, 1) + shape, lambda i, be, nu: (layer, be[i], 0, 0))
    grid_spec = pltpu.PrefetchScalarGridSpec(
        num_scalar_prefetch=2,
        grid=(n_rows // MOE_BLOCK,),
        in_specs=[
            pl.BlockSpec((MOE_BLOCK, XS_WIDTH), lambda i, be, nu: (i, 0)),
            wspec((D_MODEL, D_EXPERT)), wspec((D_MODEL, D_EXPERT)), wspec((D_EXPERT, D_MODEL)),
        ],
        out_specs=pl.BlockSpec((MOE_BLOCK, XS_FEAT), lambda i, be, nu: (i, 0)),
        scratch_shapes=[
            pltpu.VMEM((D_MODEL, 2 * D_EXPERT), BF16),
            pltpu.VMEM((D_EXPERT, D_MODEL), BF16),
        ],
    )
    return pl.pallas_call(
        _expert_kernel,
        grid_spec=grid_spec,
        out_shape=jax.ShapeDtypeStruct((n_rows, XS_FEAT), jnp.uint32),
        compiler_params=_cparams(("arbitrary",)),
        name="moe_experts",
    )(block_e, n_used, xs, wg, wu, wd)


def _combine_kernel(lo_ref, gb_ref, n8_ref, tot_ref, x_ref, rt_ref, gf_ref, fg_ref, ys_hbm,
                    o_ref, yloc, sem, *, lr, final):
    t = pl.program_id(0)
    nt = pl.num_programs(0)
    slot = t % 2
    tm = x_ref.shape[0]

    def fetch(tile, sl):
        def per_expert(e, queue):
            idx = tile * N_EXPERTS + e
            dst0 = lo_ref[idx] * CHUNK
            src0 = gb_ref[idx] * CHUNK
            _start_segment(n8_ref[idx], lambda first, rows: pltpu.make_async_copy(
                ys_hbm.at[pl.ds(pl.multiple_of(src0 + first, CHUNK), rows), :],
                yloc.at[sl, pl.ds(pl.multiple_of(dst0 + first, CHUNK), rows), :], sem.at[sl]), queue)
        _for_expert_segments(per_expert)

    @pl.when(t == 0)
    def _():
        yloc[...] = jnp.zeros(yloc.shape, jnp.uint32)
        fetch(0, 0)

    @pl.when(t + 1 < nt)
    def _():
        fetch(t + 1, 1 - slot)

    _wait_chunks(tot_ref[t], lr // CHUNK, lambda size: pltpu.make_async_copy(
        ys_hbm.at[pl.ds(0, size), :], yloc.at[slot, pl.ds(0, size), :], sem.at[slot]))

    ysw = _unpack_halves(yloc[slot])
    rt = rt_ref[...]
    cols = _iota((tm, lr), 1).astype(F32)
    pick = jnp.where(cols == rt[:, RT_ROW1:RT_ROW1 + 1], 1.0,
                     jnp.where(cols == rt[:, RT_ROW2:RT_ROW2 + 1], 1.0, 0.0)).astype(BF16)
    x = x_ref[...] + gf_ref[0] * jnp.dot(pick, ysw, preferred_element_type=F32)
    if final:
        ms = jnp.mean(x * x, axis=-1, keepdims=True)
        x = x * lax.rsqrt(ms + EPS) * fg_ref[...]
    o_ref[...] = x


def _combine(plan, x2, ys, rt, gf, fg, seq, tm, final):
    m = x2.shape[0]
    per_b = seq // tm
    lr = _local_rows(tm)
    grid_spec = pltpu.PrefetchScalarGridSpec(
        num_scalar_prefetch=4,
        grid=(m // tm,),
        in_specs=[
            pl.BlockSpec((tm, D_MODEL), lambda i, *_: (i, 0)),
            pl.BlockSpec((tm, LANES), lambda i, *_: (i, 0)),
            pl.BlockSpec((1, 1, D_MODEL), lambda i, *_: (i // per_b, 0, 0)),
            pl.BlockSpec((1, D_MODEL), lambda i, *_: (0, 0)),
            pl.BlockSpec(memory_space=pl.ANY),
        ],
        out_specs=pl.BlockSpec((tm, D_MODEL), lambda i, *_: (i, 0)),
        scratch_shapes=[pltpu.VMEM((2, lr, XS_FEAT), jnp.uint32), pltpu.SemaphoreType.DMA((2,))],
    )
    return pl.pallas_call(
        functools.partial(_combine_kernel, lr=lr, final=final),
        grid_spec=grid_spec,
        out_shape=jax.ShapeDtypeStruct((m, D_MODEL), F32),
        compiler_params=_cparams(("arbitrary",)),
        name="moe_combine",
    )(plan["lo8"], plan["gb8"], plan["n8"], plan["tot8"], x2, rt, gf, fg, ys)


def _moe_plan(c, n_blocks):
    i32 = jnp.int32
    blk8 = MOE_BLOCK // CHUNK
    c8 = (c + CHUNK - 1) // CHUNK
    lo8 = jnp.cumsum(c8, axis=1) - c8
    per_e = jnp.sum(c8, axis=0)
    pad8 = (per_e + blk8 - 1) // blk8 * blk8
    end8 = jnp.cumsum(pad8)
    start8 = end8 - pad8
    gb8 = start8[None, :] + jnp.cumsum(c8, axis=0) - c8
    blk_start8 = jnp.arange(n_blocks, dtype=i32) * blk8
    block_e = jnp.minimum(jnp.sum(end8[None, :] <= blk_start8[:, None], axis=1), N_EXPERTS - 1)
    return {
        "lo8": lo8.reshape(-1).astype(i32), "gb8": gb8.reshape(-1).astype(i32),
        "n8": c8.reshape(-1).astype(i32), "tot8": jnp.sum(c8, axis=1).astype(i32),
        "ts8": jnp.concatenate([start8 + per_e, end8[-1:]]).astype(i32),
        "tn8": jnp.concatenate([pad8 - per_e, n_blocks * blk8 - end8[-1:]]).astype(i32),
        "block_e": block_e.astype(i32), "n_used": (end8[-1:] // blk8).astype(i32),
    }


def _w_in_sections(w):
    d_dt = SSD_WIDTH + XBC_WIDTH
    d_q = d_dt + SSD_HEADS
    d_f = d_q + 3 * ATTN_WIDTH
    d_ga = d_f + ATTN_HEADS
    small = jnp.concatenate([w[..., d_dt:d_q], w[..., d_f:d_ga],
                             jnp.zeros(w.shape[:-1] + (LANES - SSD_HEADS - ATTN_HEADS,), w.dtype)], axis=-1)
    sections = [w[..., SSD_WIDTH:d_dt], w[..., :SSD_WIDTH], w[..., d_q:d_f], w[..., d_ga:], small]
    return [sec.astype(BF16) for sec in sections]


def kernel(x, c, ada_w, ada_b, norm_mix_g, w_in, ssd_conv_w, ssd_conv_b, ssd_dt_bias, ssd_a_log,
           ssd_d, ssd_norm_g, fox_f_bias, fox_norm_g, cm_conv_w, cm_conv_b, cm_ln_g, cm_ln_b, w_out,
           norm_ffn_g, w_router_group, b_router_group, w_router_expert, b_router_expert, w_gate,
           w_up, w_down, final_norm_g):
    bsz, seq, d = x.shape
    m = bsz * seq
    tm = min(512, seq)
    tt = min(512, seq)
    n_blocks = -(-(2 * m + (CHUNK - 1) * (m // tm) * N_EXPERTS) // MOE_BLOCK) + N_EXPERTS

    mod = _modulation(c, ada_w, ada_b)
    x2 = x.reshape(m, d)
    w_in_sections = _w_in_sections(w_in)
    for l in range(DEPTH):
        sh_m, sc_m, g_m, sh_f, sc_f, g_f = [v.reshape(bsz, 1, d) for v in jnp.split(mod[l], 6, axis=-1)]
        a_m = norm_mix_g[l][None, None, :] * (1.0 + sc_m)
        proj = _inproj(x2, a_m, sh_m, w_in_sections, l, seq, min(2 * tm, seq))
        y_ssd = _ssd(proj, ssd_conv_w[l], ssd_conv_b[l], ssd_dt_bias[l], ssd_a_log[l],
                     ssd_d[l], ssd_norm_g[l], bsz, seq, tt)
        qa, ka, va, aux, stat = _foxprep(proj, fox_f_bias[l], bsz, seq)
        o_att = _fox(qa, ka, va, aux, stat)
        y_cnv = _conformer(proj, cm_conv_w[l], cm_conv_b[l], cm_ln_g[l], cm_ln_b[l], bsz, seq, tt)

        w_r = jnp.concatenate([w_router_group[l], w_router_expert[l],
                               jnp.zeros((d, LANES - N_GROUPS - N_EXPERTS), F32)], axis=1)
        w_rh = w_r.astype(BF16)
        w_r2 = jnp.concatenate([w_rh, (w_r - w_rh.astype(F32)).astype(BF16)], axis=1)
        b_r = jnp.concatenate([b_router_group[l], b_router_expert[l],
                               jnp.zeros((LANES - N_GROUPS - N_EXPERTS,), F32)]).reshape(1, LANES)
        a_f = norm_ffn_g[l][None, None, :] * (1.0 + sc_f)
        x2, h2, rt, rtt, cnt = _outproj(x2, y_ssd, o_att, y_cnv, w_out[l].astype(BF16),
                                        fox_norm_g[l].reshape(1, -1), g_m, a_f, sh_f, w_r2, b_r,
                                        seq, tm)
        plan = _moe_plan(cnt[:, 0, :N_EXPERTS].astype(jnp.int32), n_blocks)
        xs = _dispatch(plan, h2, rtt, n_blocks * MOE_BLOCK, tm)
        ys = _experts(plan["block_e"], plan["n_used"], xs, w_gate, w_up, w_down, l)
        x2 = _combine(plan, x2, ys, rt, g_f, final_norm_g.reshape(1, d), seq, tm,
                      final=(l == DEPTH - 1))
    return x2.reshape(bsz, seq, d)
```
